```python
import jax, jax.numpy as jnp
from jax import lax
import numpy as np

D_MODEL = 2048
BATCH = 4
SEQ = 2048
DEPTH = 2
DEC_BATCH = 128
DEC_SEQ = 1
PAST_LEN = 8192
PAGE_SIZE = 128

BRANCH_W = D_MODEL // 2
HEAD_DIM = 64
N_HEADS = BRANCH_W // HEAD_DIM
N_KV_HEADS = 4
GQA_GROUP = N_HEADS // N_KV_HEADS
KV_W = N_KV_HEADS * HEAD_DIM
WINDOW = 128
D_CONV = BRANCH_W
CONV_WIDTH = 3
D_GMLP = BRANCH_W
CHUNK = 128
N_SPATIAL_GROUPS = 8
SPATIAL_GROUP_W = D_GMLP // N_SPATIAL_GROUPS
N_BRANCHES = 3
D_FF = -(-8 * D_MODEL // (3 * 256)) * 256
COL_SIZES = (BRANCH_W, KV_W, KV_W, D_CONV, D_CONV, D_CONV, D_GMLP, D_GMLP, N_BRANCHES * D_MODEL)
IN_COLS = sum(COL_SIZES)
EPS = 1e-6
NEG_INF = -1e30

kernel_name = "hybrid_swa_shortconv_gmlp_gated_decoder_step"


def rms_norm(x, g):
    xf = x.astype(jnp.float32)
    y = xf * lax.rsqrt(jnp.mean(xf * xf, axis=-1, keepdims=True) + EPS)
    return (y * g.astype(jnp.float32)).astype(x.dtype)


def alibi_slopes():
    return jnp.exp2(-8.0 * jnp.arange(1, N_HEADS + 1, dtype=jnp.float32) / N_HEADS)


def split_cols(z):
    idx = np.cumsum(np.array(COL_SIZES))[:-1].tolist()
    return jnp.split(z, idx, axis=-1)


def window_attend(q, k, v, q_pos, k_pos, sinks):
    Bn, N, Tq = q.shape[:3]
    qg = q.reshape(Bn, N, Tq, N_KV_HEADS, GQA_GROUP, HEAD_DIM)
    s = jnp.einsum("bnqkgd,bnskd->bnkgqs", qg, k, preferred_element_type=jnp.float32) * (HEAD_DIM ** -0.5)
    dist = q_pos[:, :, None] - k_pos[:, None, :]
    valid = (k_pos[:, None, :] >= 0) & (dist >= 0) & (dist < WINDOW)
    slopes = alibi_slopes().reshape(N_KV_HEADS, GQA_GROUP)
    bias = -slopes[None, :, :, None, None] * dist.astype(jnp.float32)[:, None, None]
    s = jnp.where(valid[:, None, None], s + bias, NEG_INF)
    sink = sinks.astype(jnp.float32).reshape(N_KV_HEADS, GQA_GROUP)[None, None, :, :, None, None]
    sink = jnp.broadcast_to(sink, s.shape[:-1] + (1,))
    p = jax.nn.softmax(jnp.concatenate([s, sink], axis=-1), axis=-1)[..., :-1]
    o = jnp.einsum("bnkgqs,bnskd->bnqkgd", p.astype(v.dtype), v)
    return o.reshape(Bn, N, Tq, BRANCH_W)


def prompt_window_attention(q, k, v, sinks):
    Bn, T = q.shape[:2]
    nb = T // WINDOW
    qb = q.reshape(Bn, nb, WINDOW, N_HEADS, HEAD_DIM)

    def band(xb):
        prev = jnp.concatenate([jnp.zeros_like(xb[:, :1]), xb[:, :-1]], axis=1)
        return jnp.concatenate([prev, xb], axis=2)

    kb = band(k.reshape(Bn, nb, WINDOW, N_KV_HEADS, HEAD_DIM))
    vb = band(v.reshape(Bn, nb, WINDOW, N_KV_HEADS, HEAD_DIM))
    pos = jnp.arange(T, dtype=jnp.int32).reshape(nb, WINDOW)
    k_pos = jnp.concatenate([pos - WINDOW, pos], axis=1)
    return window_attend(qb, kb, vb, pos, k_pos, sinks).reshape(Bn, T, BRANCH_W)


def sample_window_attention(q, k, v, k_buf, v_buf, sinks):
    Bn, T = q.shape[:2]
    n_buf = k_buf.shape[1]
    k_all = jnp.concatenate([k_buf, k], axis=1)[:, None]
    v_all = jnp.concatenate([v_buf, v], axis=1)[:, None]
    q_pos = PAST_LEN + jnp.arange(T, dtype=jnp.int32)
    k_pos = jnp.concatenate([PAST_LEN - n_buf + jnp.arange(n_buf, dtype=jnp.int32), q_pos])
    o = window_attend(q[:, None], k_all, v_all, q_pos[None], k_pos[None], sinks)
    return o.reshape(Bn, T, BRANCH_W)


def short_conv(z_pad, w, T):
    return sum(w[j] * z_pad[:, j:j + T] for j in range(CONV_WIDTH))


def spatial_mix(v, w_s, b_s):
    T = v.shape[2]
    mask = jnp.tril(jnp.ones((T, T), dtype=bool))
    w = jnp.where(mask[None], w_s[:, :T, :T], jnp.zeros((), w_s.dtype))
    return jnp.einsum("gpq,bnqgc->bnpgc", w, v) + b_s[:, :T].T[None, None, :, :, None]


def layer(x, lw, is_prompt, k_buf=None, v_buf=None, conv_buf=None):
    (norm_mix, w_in, b_gate, q_norm, k_norm, sinks, conv_w, v_norm,
     w_spatial, b_spatial, w_branch, w_out, norm_ffn, w_gate_up, w_down) = lw
    Bn, T, _ = x.shape
    xn = rms_norm(x, norm_mix)
    q, k, v, bg, cg, h, u, vg, g = split_cols(xn @ w_in)

    q = rms_norm(q.reshape(Bn, T, N_HEADS, HEAD_DIM), q_norm)
    k = rms_norm(k.reshape(Bn, T, N_KV_HEADS, HEAD_DIM), k_norm)
    v = v.reshape(Bn, T, N_KV_HEADS, HEAD_DIM)
    if is_prompt:
        o_a = prompt_window_attention(q, k, v, sinks)
        keep = min(WINDOW, T)
        new_k, new_v = k[:, T - keep:], v[:, T - keep:]
    else:
        o_a = sample_window_attention(q, k, v, k_buf, v_buf, sinks)
        new_k, new_v = k, v

    zc = cg * h
    if is_prompt:
        z_pad = jnp.concatenate([jnp.zeros((Bn, CONV_WIDTH - 1, D_CONV), zc.dtype), zc], axis=1)
    else:
        z_pad = jnp.concatenate([conv_buf, zc], axis=1)
    o_b = bg * short_conv(z_pad, conv_w, T)
    new_conv = z_pad[:, -(CONV_WIDTH - 1):]

    u = jax.nn.gelu(u)
    vg = rms_norm(jax.nn.gelu(vg), v_norm)
    n_chunks = T // CHUNK if is_prompt else 1
    vc = vg.reshape(Bn, n_chunks, T // n_chunks, N_SPATIAL_GROUPS, SPATIAL_GROUP_W)
    o_c = u * spatial_mix(vc, w_spatial, b_spatial).reshape(Bn, T, D_GMLP)

    branches = jnp.stack([o_a, o_b, o_c], axis=2)
    proj = jnp.einsum("btic,icd->btid", branches, w_branch)
    gates = jax.nn.sigmoid(g.reshape(Bn, T, N_BRANCHES, D_MODEL) + b_gate)
    x = x + jnp.sum(gates * proj, axis=2) @ w_out

    gate, up = jnp.split(rms_norm(x, norm_ffn) @ w_gate_up, 2, axis=-1)
    x = x + (jax.nn.silu(gate) * up) @ w_down
    return x, new_k, new_v, new_conv, vg


def setup_inputs(seed: int = 0) -> dict:
    key = jax.random.key(seed)
    ks = jax.random.split(key, 24)
    f32 = jnp.float32
    n = lambda k, shape, s: jax.random.normal(k, shape, f32) * s
    w_buf = min(WINDOW, PAST_LEN)
    return {
        "x_prompt": n(ks[0], (BATCH, SEQ, D_MODEL), 1.0),
        "x_sample": n(ks[1], (DEC_BATCH, DEC_SEQ, D_MODEL), 1.0),
        "cache_k": n(ks[2], (DEPTH, DEC_BATCH, w_buf, N_KV_HEADS, HEAD_DIM), 1.0),
        "cache_v": n(ks[3], (DEPTH, DEC_BATCH, w_buf, N_KV_HEADS, HEAD_DIM), 1.0),
        "state_conv": n(ks[4], (DEPTH, DEC_BATCH, CONV_WIDTH - 1, D_CONV), 1.0),
        "norm_mix": 1.0 + n(ks[5], (DEPTH, D_MODEL), 0.05),
        "w_in": n(ks[6], (DEPTH, D_MODEL, IN_COLS), D_MODEL ** -0.5),
        "b_gate": n(ks[7], (DEPTH, N_BRANCHES, D_MODEL), 0.1),
        "q_norm": 1.0 + n(ks[8], (DEPTH, HEAD_DIM), 0.05),
        "k_norm": 1.0 + n(ks[9], (DEPTH, HEAD_DIM), 0.05),
        "sinks": n(ks[10], (DEPTH, N_HEADS), 0.5),
        "conv_w": n(ks[11], (DEPTH, CONV_WIDTH, D_CONV), CONV_WIDTH ** -0.5),
        "v_norm": 1.0 + n(ks[12], (DEPTH, D_GMLP), 0.05),
        "w_spatial": n(ks[13], (DEPTH, N_SPATIAL_GROUPS, CHUNK, CHUNK), CHUNK ** -0.5),
        "b_spatial": 1.0 + n(ks[14], (DEPTH, N_SPATIAL_GROUPS, CHUNK), 0.1),
        "w_branch": n(ks[15], (DEPTH, N_BRANCHES, BRANCH_W, D_MODEL), BRANCH_W ** -0.5),
        "w_out": n(ks[16], (DEPTH, D_MODEL, D_MODEL), D_MODEL ** -0.5),
        "norm_ffn": 1.0 + n(ks[17], (DEPTH, D_MODEL), 0.05),
        "w_gate_up": n(ks[18], (DEPTH, D_MODEL, 2 * D_FF), D_MODEL ** -0.5),
        "w_down": n(ks[19], (DEPTH, D_FF, D_MODEL), D_FF ** -0.5),
    }


def reference(x_prompt, x_sample, cache_k, cache_v, state_conv, norm_mix, w_in, b_gate, q_norm, k_norm,
              sinks, conv_w, v_norm, w_spatial, b_spatial, w_branch, w_out, norm_ffn, w_gate_up, w_down):
    weights = (norm_mix, w_in, b_gate, q_norm, k_norm, sinks, conv_w, v_norm,
               w_spatial, b_spatial, w_branch, w_out, norm_ffn, w_gate_up, w_down)
    xp, xs = x_prompt, x_sample
    kp, vp, cp, ksm, vsm, csm, gsm = [], [], [], [], [], [], []
    for l in range(DEPTH):
        lw = tuple(a[l] for a in weights)
        xp, k1, v1, c1, _ = layer(xp, lw, True)
        xs, k2, v2, c2, g2 = layer(xs, lw, False, cache_k[l], cache_v[l], state_conv[l])
        kp.append(k1); vp.append(v1); cp.append(c1)
        ksm.append(k2); vsm.append(v2); csm.append(c2); gsm.append(g2)
    return (xp, xs, jnp.stack(kp), jnp.stack(vp), jnp.stack(cp),
            jnp.stack(ksm), jnp.stack(vsm), jnp.stack(csm), jnp.stack(gsm))
```

```python
import functools

import jax
import jax.numpy as jnp
import numpy as np
from jax import lax
from jax.experimental import pallas as pl
from jax.experimental.pallas import tpu as pltpu

D_MODEL = 2048
BATCH = 4
SEQ = 2048
DEPTH = 2
DEC_BATCH = 128
PAST_LEN = 8192
BRANCH_W = 1024
HEAD_DIM = 64
N_HEADS = 16
N_KV_HEADS = 4
GQA_GROUP = 4
KV_W = 256
WINDOW = 128
D_CONV = 1024
CONV_WIDTH = 3
D_GMLP = 1024
CHUNK = 128
N_SPATIAL_GROUPS = 8
SPATIAL_GROUP_W = 128
N_BRANCHES = 3
D_FF = 5632
EPS = 1e-6
NEG_INF = -1e30

N_PROMPT = BATCH * SEQ
N_ROWS = N_PROMPT + DEC_BATCH
SAMPLE_BLOCK = N_PROMPT // DEC_BATCH

OFF_Q, OFF_K, OFF_V, OFF_BG, OFF_CG, OFF_H, OFF_U, OFF_VG, OFF_G = (
    0, 1024, 1280, 1536, 2560, 3584, 4608, 5632, 6656)

TM_SEQ = 512
TM_ROW = 640
TN = 512
VMEM_LIMIT = 56 * 1024 * 1024

F32 = jnp.float32
BF16 = jnp.bfloat16


def _params(sem):
    return pltpu.CompilerParams(dimension_semantics=sem, vmem_limit_bytes=VMEM_LIMIT)


def _rms_rows(x, g):
    ms = jnp.mean(x * x, axis=-1, keepdims=True)
    return x * lax.rsqrt(ms + EPS) * g


def _dot(a, b):
    return jnp.dot(a, b, preferred_element_type=F32)


def _dot_nt(a, b):
    return lax.dot_general(a, b, (((1,), (1,)), ((), ())), preferred_element_type=F32)


def _gelu(x):
    return 0.5 * x * (1.0 + jnp.tanh(np.sqrt(2.0 / np.pi).astype(np.float32) * (x + 0.044715 * (x * x * x))))


def _qkv_kernel(x_ref, g_ref, w_ref, qn_ref, kn_ref, pq_ref, pk_ref, q_ref, k_ref, v_ref):
    xn = _rms_rows(x_ref[...], g_ref[...]).astype(BF16)
    z = _dot(xn, w_ref[...])
    q = z[:, :BRANCH_W]
    k = z[:, BRANCH_W:BRANCH_W + KV_W]
    q_ms = _dot((q * q).astype(BF16), pq_ref[...]) * (1.0 / HEAD_DIM)
    k_ms = _dot((k * k).astype(BF16), pk_ref[...]) * (1.0 / HEAD_DIM)
    q_ref[...] = (q * lax.rsqrt(q_ms + EPS) * qn_ref[...] * (HEAD_DIM ** -0.5)).astype(BF16)
    k_ref[...] = k * lax.rsqrt(k_ms + EPS) * kn_ref[...]
    v_ref[...] = z[:, BRANCH_W + KV_W:]


def _qkv(x_all, g, w_qkv, qn, kn, pq, pk, *, row_block0, n_rows, tm):
    nt = n_rows // tm
    const = lambda i: (0, 0)
    rows = lambda i: (i, 0)
    return pl.pallas_call(
        _qkv_kernel,
        grid=(nt,),
        in_specs=[
            pl.BlockSpec((tm, D_MODEL), lambda i: (row_block0 + i, 0)),
            pl.BlockSpec((1, D_MODEL), const),
            pl.BlockSpec((D_MODEL, BRANCH_W + 2 * KV_W), const),
            pl.BlockSpec((1, BRANCH_W), const),
            pl.BlockSpec((1, KV_W), const),
            pl.BlockSpec((BRANCH_W, BRANCH_W), const),
            pl.BlockSpec((KV_W, KV_W), const),
        ],
        out_specs=[
            pl.BlockSpec((tm, BRANCH_W), rows),
            pl.BlockSpec((tm, KV_W), rows),
            pl.BlockSpec((tm, KV_W), rows),
        ],
        out_shape=[
            jax.ShapeDtypeStruct((n_rows, BRANCH_W), BF16),
            jax.ShapeDtypeStruct((n_rows, KV_W), F32),
            jax.ShapeDtypeStruct((n_rows, KV_W), F32),
        ],
        compiler_params=_params(("arbitrary",)),
        name="qkv",
    )(x_all, g, w_qkv, qn, kn, pq, pk)


def _attn_prompt_kernel(sink_ref, q_ref, kp_ref, ko_ref, vp_ref, vo_ref, bias_ref, o_ref):
    kk = jnp.concatenate([kp_ref[...], ko_ref[...]], axis=0).astype(BF16)
    vv = jnp.concatenate([vp_ref[...], vo_ref[...]], axis=0).astype(BF16)
    for g in range(N_KV_HEADS):
        kg = kk[:, g * HEAD_DIM:(g + 1) * HEAD_DIM]
        vg = vv[:, g * HEAD_DIM:(g + 1) * HEAD_DIM]
        heads = range(g * GQA_GROUP, (g + 1) * GQA_GROUP)
        qg = jnp.concatenate([q_ref[:, h * HEAD_DIM:(h + 1) * HEAD_DIM] for h in heads], axis=0)
        s_all = _dot_nt(qg, kg)
        ps, dens = [], []
        for i, h in enumerate(heads):
            s = s_all[i * WINDOW:(i + 1) * WINDOW] + bias_ref[h]
            sink = sink_ref[h]
            m = jnp.maximum(jnp.max(s, axis=-1, keepdims=True), sink)
            p = jnp.exp(s - m)
            dens.append(jnp.sum(p, axis=-1, keepdims=True) + jnp.exp(sink - m))
            ps.append(p.astype(BF16))
        o_all = _dot(jnp.concatenate(ps, axis=0), vg)
        for i, h in enumerate(heads):
            o = o_all[i * WINDOW:(i + 1) * WINDOW] / dens[i]
            o_ref[:, h * HEAD_DIM:(h + 1) * HEAD_DIM] = o.astype(BF16)


def _attn_prompt(sinks, q, k, v, bias):
    nb = SEQ // WINDOW
    own = lambda b, j: (b * nb + j, 0)
    prev = lambda b, j: (b * nb + jnp.maximum(j - 1, 0), 0)
    return pl.pallas_call(
        _attn_prompt_kernel,
        grid=(BATCH, nb),
        in_specs=[
            pl.BlockSpec(memory_space=pltpu.SMEM),
            pl.BlockSpec((WINDOW, BRANCH_W), own),
            pl.BlockSpec((WINDOW, KV_W), prev),
            pl.BlockSpec((WINDOW, KV_W), own),
            pl.BlockSpec((WINDOW, KV_W), prev),
            pl.BlockSpec((WINDOW, KV_W), own),
            pl.BlockSpec((None, N_HEADS, WINDOW, 2 * WINDOW), lambda b, j: (jnp.minimum(j, 1), 0, 0, 0)),
        ],
        out_specs=pl.BlockSpec((WINDOW, BRANCH_W), own),
        out_shape=jax.ShapeDtypeStruct((N_PROMPT, BRANCH_W), BF16),
        compiler_params=_params(("arbitrary", "arbitrary")),
        name="attn_prompt",
    )(sinks, q, k, k, v, v, bias)


def _prompt_bias():
    slopes = jnp.exp2(-8.0 * jnp.arange(1, N_HEADS + 1, dtype=F32) / N_HEADS)
    qi = jnp.arange(WINDOW, dtype=jnp.int32)[:, None]
    ki = jnp.arange(2 * WINDOW, dtype=jnp.int32)[None, :] - WINDOW
    dist = qi - ki
    valid = (dist >= 0) & (dist < WINDOW)
    bias = -slopes[:, None, None] * dist.astype(F32)[None]
    with_prev = jnp.where(valid[None], bias, NEG_INF)
    first = jnp.where((valid & (ki >= 0))[None], bias, NEG_INF)
    return jnp.stack([first, with_prev])


SAMPLE_BT = 8


def _attn_sample_kernel(q_ref, kn_ref, vn_ref, ck_ref, cv_ref, bias_ref, sink_ref, mask_ref,
                        rep_ref, o_ref):
    mask = mask_ref[...]
    bias = bias_ref[...]
    sink = sink_ref[...]
    for b in range(SAMPLE_BT):
        qe = _dot(q_ref[b], rep_ref[...]) * mask
        s = _dot_nt(qe.astype(BF16), ck_ref[b].astype(BF16)) + bias
        s_new = jnp.sum(qe * kn_ref[b], axis=-1, keepdims=True)
        m = jnp.maximum(jnp.maximum(jnp.max(s, axis=-1, keepdims=True), s_new), sink)
        p = jnp.exp(s - m)
        p_new = jnp.exp(s_new - m)
        den = jnp.sum(p, axis=-1, keepdims=True) + p_new + jnp.exp(sink - m)
        of = _dot(p.astype(BF16), cv_ref[b].astype(BF16)) + p_new * vn_ref[b]
        of = of * mask / den
        o = (of[:, 0:64] + of[:, 64:128]) + (of[:, 128:192] + of[:, 192:256])
        o_ref[b] = o.astype(BF16)


def _attn_sample(q3, k_new, v_new, ck, cv, bias, sinks_col, mask, rep):
    nsteps = DEC_BATCH // SAMPLE_BT
    b3 = lambda i: (i, 0, 0)
    c2 = lambda i: (0, 0)
    return pl.pallas_call(
        _attn_sample_kernel,
        grid=(nsteps,),
        in_specs=[
            pl.BlockSpec((SAMPLE_BT, N_HEADS, HEAD_DIM), b3),
            pl.BlockSpec((SAMPLE_BT, 1, KV_W), b3),
            pl.BlockSpec((SAMPLE_BT, 1, KV_W), b3),
            pl.BlockSpec((SAMPLE_BT, WINDOW, KV_W), b3),
            pl.BlockSpec((SAMPLE_BT, WINDOW, KV_W), b3),
            pl.BlockSpec((N_HEADS, WINDOW), c2),
            pl.BlockSpec((N_HEADS, 1), c2),
            pl.BlockSpec((N_HEADS, KV_W), c2),
            pl.BlockSpec((HEAD_DIM, KV_W), c2),
        ],
        out_specs=pl.BlockSpec((SAMPLE_BT, N_HEADS, HEAD_DIM), b3),
        out_shape=jax.ShapeDtypeStruct((DEC_BATCH, N_HEADS, HEAD_DIM), BF16),
        compiler_params=_params(("arbitrary",)),
        name="attn_sample",
    )(q3, k_new, v_new, ck, cv, bias, sinks_col, mask, rep)


def _sample_bias(n_buf):
    slopes = jnp.exp2(-8.0 * jnp.arange(1, N_HEADS + 1, dtype=F32) / N_HEADS)
    dist = n_buf - jnp.arange(n_buf, dtype=jnp.int32)
    bias = -slopes[:, None] * dist.astype(F32)[None, :]
    return jnp.where((dist < WINDOW)[None, :], bias, NEG_INF)


def _conv_prompt_kernel(x_ref, g_ref, w_ref, cw_ref, ob_ref, nc_ref, zbuf, carry):
    t = pl.program_id(1)
    tm = x_ref.shape[0]
    xn = _rms_rows(x_ref[...], g_ref[...]).astype(BF16)

    @pl.when(t == 0)
    def _():
        zbuf[0:8, :] = jnp.zeros((8, D_CONV), F32)

    @pl.when(t > 0)
    def _():
        zbuf[0:8, :] = carry[...]

    for c in range(D_CONV // TN):
        cols = slice(c * TN, (c + 1) * TN)
        w_cols = lambda seg: w_ref[:, seg * D_CONV + c * TN:seg * D_CONV + (c + 1) * TN]
        zc = _dot(xn, w_cols(1)) * _dot(xn, w_cols(2))
        zbuf[8:8 + tm, cols] = zc
        cw = cw_ref[:, cols]
        y = cw[0:1] * zbuf[6:6 + tm, cols] + cw[1:2] * zbuf[7:7 + tm, cols] + cw[2:3] * zc
        ob_ref[:, cols] = (_dot(xn, w_cols(0)) * y).astype(BF16)
    carry[...] = zbuf[tm:tm + 8, :]
    nc_ref[...] = zbuf[tm + 6:tm + 8, :]


def _conv_prompt(x_all, g, w_bch, conv_w):
    nt = SEQ // TM_SEQ
    const = lambda b, t: (0, 0)
    return pl.pallas_call(
        _conv_prompt_kernel,
        grid=(BATCH, nt),
        in_specs=[
            pl.BlockSpec((TM_SEQ, D_MODEL), lambda b, t: (b * nt + t, 0)),
            pl.BlockSpec((1, D_MODEL), const),
            pl.BlockSpec((D_MODEL, 3 * D_CONV), const),
            pl.BlockSpec((CONV_WIDTH, D_CONV), const),
        ],
        out_specs=[
            pl.BlockSpec((TM_SEQ, D_CONV), lambda b, t: (b * nt + t, 0)),
            pl.BlockSpec((None, CONV_WIDTH - 1, D_CONV), lambda b, t: (b, 0, 0)),
        ],
        out_shape=[
            jax.ShapeDtypeStruct((N_PROMPT, D_CONV), BF16),
            jax.ShapeDtypeStruct((BATCH, CONV_WIDTH - 1, D_CONV), F32),
        ],
        scratch_shapes=[
            pltpu.VMEM((TM_SEQ + 8, D_CONV), F32),
            pltpu.VMEM((8, D_CONV), F32),
        ],
        compiler_params=_params(("arbitrary", "arbitrary")),
        name="conv_prompt",
    )(x_all, g, w_bch, conv_w)


def _conv_sample_kernel(x_ref, g_ref, wbg_ref, wcg_ref, wh_ref, cw_ref, cb0_ref, cb1_ref,
                        ob_ref, zc_ref):
    xn = _rms_rows(x_ref[...], g_ref[...]).astype(BF16)
    zc = _dot(xn, wcg_ref[...]) * _dot(xn, wh_ref[...])
    cw = cw_ref[...]
    y = cw[0:1] * cb0_ref[...] + cw[1:2] * cb1_ref[...] + cw[2:3] * zc
    ob_ref[...] = (_dot(xn, wbg_ref[...]) * y).astype(BF16)
    zc_ref[...] = zc


def _conv_sample(x_all, g, w_bch, conv_w, cb0, cb1):
    nc = D_CONV // TN
    col = lambda c: (0, c)
    return pl.pallas_call(
        _conv_sample_kernel,
        grid=(nc,),
        in_specs=[
            pl.BlockSpec((DEC_BATCH, D_MODEL), lambda c: (SAMPLE_BLOCK, 0)),
            pl.BlockSpec((1, D_MODEL), lambda c: (0, 0)),
            pl.BlockSpec((D_MODEL, TN), col),
            pl.BlockSpec((D_MODEL, TN), lambda c: (0, nc + c)),
            pl.BlockSpec((D_MODEL, TN), lambda c: (0, 2 * nc + c)),
            pl.BlockSpec((CONV_WIDTH, TN), col),
            pl.BlockSpec((DEC_BATCH, TN), col),
            pl.BlockSpec((DEC_BATCH, TN), col),
        ],
        out_specs=[pl.BlockSpec((DEC_BATCH, TN), col), pl.BlockSpec((DEC_BATCH, TN), col)],
        out_shape=[
            jax.ShapeDtypeStruct((DEC_BATCH, D_CONV), BF16),
            jax.ShapeDtypeStruct((DEC_BATCH, D_CONV), F32),
        ],
        compiler_params=_params(("arbitrary",)),
        name="conv_sample",
    )(x_all, g, w_bch, w_bch, w_bch, conv_w, cb0, cb1)


def _gmlp_prompt_kernel(x_ref, g_ref, wu_ref, wv_ref, vn_ref, ws_ref, bs_ref, oc_ref, u_scr, v_scr):
    tm = x_ref.shape[0]
    xn = _rms_rows(x_ref[...], g_ref[...]).astype(BF16)
    u_scr[...] = _gelu(_dot(xn, wu_ref[...]))
    v_scr[...] = _rms_rows(_gelu(_dot(xn, wv_ref[...])), vn_ref[...]).astype(BF16)
    row = lax.broadcasted_iota(jnp.int32, (CHUNK, CHUNK), 0)
    col = lax.broadcasted_iota(jnp.int32, (CHUNK, CHUNK), 1)
    for grp in range(N_SPATIAL_GROUPS):
        w = jnp.where(col <= row, ws_ref[grp], 0.0).astype(BF16)
        cols = slice(grp * SPATIAL_GROUP_W, (grp + 1) * SPATIAL_GROUP_W)
        for ch in range(tm // CHUNK):
            rows = slice(ch * CHUNK, (ch + 1) * CHUNK)
            mix = _dot(w, v_scr[rows, cols]) + bs_ref[grp]
            oc_ref[rows, cols] = (u_scr[rows, cols] * mix).astype(BF16)


def _gmlp_prompt(x_all, g, w_u, w_v, v_norm, w_s, b_s_wide):
    nt = N_PROMPT // TM_SEQ
    c2 = lambda i: (0, 0)
    c3 = lambda i: (0, 0, 0)
    return pl.pallas_call(
        _gmlp_prompt_kernel,
        grid=(nt,),
        in_specs=[
            pl.BlockSpec((TM_SEQ, D_MODEL), lambda i: (i, 0)),
            pl.BlockSpec((1, D_MODEL), c2),
            pl.BlockSpec((D_MODEL, D_GMLP), c2),
            pl.BlockSpec((D_MODEL, D_GMLP), c2),
            pl.BlockSpec((1, D_GMLP), c2),
            pl.BlockSpec((N_SPATIAL_GROUPS, CHUNK, CHUNK), c3),
            pl.BlockSpec((N_SPATIAL_GROUPS, CHUNK, SPATIAL_GROUP_W), c3),
        ],
        out_specs=pl.BlockSpec((TM_SEQ, D_GMLP), lambda i: (i, 0)),
        out_shape=jax.ShapeDtypeStruct((N_PROMPT, D_GMLP), BF16),
        scratch_shapes=[pltpu.VMEM((TM_SEQ, D_GMLP), F32), pltpu.VMEM((TM_SEQ, D_GMLP), BF16)],
        compiler_params=_params(("arbitrary",)),
        name="gmlp_prompt",
    )(x_all, g, w_u, w_v, v_norm, w_s, b_s_wide)


def _gmlp_sample_kernel(x_ref, g_ref, wu_ref, wv_ref, vn_ref, ws0_ref, bs0_ref, oc_ref, vg_ref):
    xn = _rms_rows(x_ref[...], g_ref[...]).astype(BF16)
    u = _gelu(_dot(xn, wu_ref[...]))
    vg = _rms_rows(_gelu(_dot(xn, wv_ref[...])), vn_ref[...])
    vg_ref[...] = vg
    oc_ref[...] = (u * (ws0_ref[...] * vg + bs0_ref[...])).astype(BF16)


def _gmlp_sample(x_all, g, w_u, w_v, v_norm, ws0, bs0):
    c2 = lambda i: (0, 0)
    return pl.pallas_call(
        _gmlp_sample_kernel,
        grid=(1,),
        in_specs=[
            pl.BlockSpec((DEC_BATCH, D_MODEL), lambda i: (SAMPLE_BLOCK, 0)),
            pl.BlockSpec((1, D_MODEL), c2),
            pl.BlockSpec((D_MODEL, D_GMLP), c2),
            pl.BlockSpec((D_MODEL, D_GMLP), c2),
            pl.BlockSpec((1, D_GMLP), c2),
            pl.BlockSpec((1, D_GMLP), c2),
            pl.BlockSpec((1, D_GMLP), c2),
        ],
        out_specs=[pl.BlockSpec((DEC_BATCH, D_GMLP), c2), pl.BlockSpec((DEC_BATCH, D_GMLP), c2)],
        out_shape=[
            jax.ShapeDtypeStruct((DEC_BATCH, D_GMLP), BF16),
            jax.ShapeDtypeStruct((DEC_BATCH, D_GMLP), F32),
        ],
        compiler_params=_params(("arbitrary",)),
        name="gmlp_sample",
    )(x_all, g, w_u, w_v, v_norm, ws0, bs0)


def _merge_kernel(x_ref, g_ref, wg0_ref, wg1_ref, wg2_ref, bg_ref, oa_ref, ob_ref, oc_ref, wb_ref,
                  m_ref, xn_scr):
    @pl.when(pl.program_id(1) == 0)
    def _():
        xn_scr[...] = _rms_rows(x_ref[...], g_ref[...]).astype(BF16)

    xn = xn_scr[...]
    bg = bg_ref[...]
    acc = None
    for i, (wg_ref, o_ref) in enumerate(((wg0_ref, oa_ref), (wg1_ref, ob_ref), (wg2_ref, oc_ref))):
        gate = jax.nn.sigmoid(_dot(xn, wg_ref[...]) + bg[i:i + 1])
        term = gate * _dot(o_ref[...], wb_ref[i])
        acc = term if acc is None else acc + term
    m_ref[...] = acc.astype(BF16)


def _merge(x_all, g, w_g, b_gate, o_a, o_b, o_c, w_branch):
    nt = N_ROWS // TM_ROW
    nn = D_MODEL // TN
    rows = lambda i, j: (i, 0)
    return pl.pallas_call(
        _merge_kernel,
        grid=(nt, nn),
        in_specs=[
            pl.BlockSpec((TM_ROW, D_MODEL), rows),
            pl.BlockSpec((1, D_MODEL), lambda i, j: (0, 0)),
            pl.BlockSpec((D_MODEL, TN), lambda i, j: (0, j)),
            pl.BlockSpec((D_MODEL, TN), lambda i, j: (0, nn + j)),
            pl.BlockSpec((D_MODEL, TN), lambda i, j: (0, 2 * nn + j)),
            pl.BlockSpec((N_BRANCHES, TN), lambda i, j: (0, j)),
            pl.BlockSpec((TM_ROW, BRANCH_W), rows),
            pl.BlockSpec((TM_ROW, BRANCH_W), rows),
            pl.BlockSpec((TM_ROW, BRANCH_W), rows),
            pl.BlockSpec((N_BRANCHES, BRANCH_W, TN), lambda i, j: (0, 0, j)),
        ],
        out_specs=pl.BlockSpec((TM_ROW, TN), lambda i, j: (i, j)),
        out_shape=jax.ShapeDtypeStruct((N_ROWS, D_MODEL), BF16),
        scratch_shapes=[pltpu.VMEM((TM_ROW, D_MODEL), BF16)],
        compiler_params=_params(("arbitrary", "arbitrary")),
        name="merge",
    )(x_all, g, w_g, w_g, w_g, b_gate, o_a, o_b, o_c, w_branch)


def _residual_matmul_kernel(a_ref, w_ref, x_ref, o_ref):
    o_ref[...] = x_ref[...] + _dot(a_ref[...], w_ref[...])


def _residual_matmul(a, w, x_all, name):
    k = a.shape[1]
    nt = N_ROWS // TM_ROW
    nn = D_MODEL // TN
    return pl.pallas_call(
        _residual_matmul_kernel,
        grid=(nt, nn),
        in_specs=[
            pl.BlockSpec((TM_ROW, k), lambda i, j: (i, 0)),
            pl.BlockSpec((k, TN), lambda i, j: (0, j)),
            pl.BlockSpec((TM_ROW, TN), lambda i, j: (i, j)),
        ],
        out_specs=pl.BlockSpec((TM_ROW, TN), lambda i, j: (i, j)),
        out_shape=jax.ShapeDtypeStruct((N_ROWS, D_MODEL), F32),
        compiler_params=_params(("arbitrary", "arbitrary")),
        name=name,
    )(a, w, x_all)


def _ffn_up_kernel(x_ref, g_ref, wgate_ref, wup_ref, h_ref, xn_scr):
    @pl.when(pl.program_id(1) == 0)
    def _():
        xn_scr[...] = _rms_rows(x_ref[...], g_ref[...]).astype(BF16)

    xn = xn_scr[...]
    gate = _dot(xn, wgate_ref[...])
    h_ref[...] = (gate * jax.nn.sigmoid(gate) * _dot(xn, wup_ref[...])).astype(BF16)


def _ffn_up(x_all, g, w_gate_up):
    nt = N_ROWS // TM_ROW
    nn = D_FF // TN
    return pl.pallas_call(
        _ffn_up_kernel,
        grid=(nt, nn),
        in_specs=[
            pl.BlockSpec((TM_ROW, D_MODEL), lambda i, j: (i, 0)),
            pl.BlockSpec((1, D_MODEL), lambda i, j: (0, 0)),
            pl.BlockSpec((D_MODEL, TN), lambda i, j: (0, j)),
            pl.BlockSpec((D_MODEL, TN), lambda i, j: (0, nn + j)),
        ],
        out_specs=pl.BlockSpec((TM_ROW, TN), lambda i, j: (i, j)),
        out_shape=jax.ShapeDtypeStruct((N_ROWS, D_FF), BF16),
        scratch_shapes=[pltpu.VMEM((TM_ROW, D_MODEL), BF16)],
        compiler_params=_params(("arbitrary", "arbitrary")),
        name="ffn_up",
    )(x_all, g, w_gate_up, w_gate_up)


def _block_diag_ones(width):
    head = np.arange(width) // HEAD_DIM
    return jnp.asarray(head[:, None] == head[None, :], dtype=BF16)


def _layer(x_all, l, cache_k, cache_v, state_conv, prm, const):
    (norm_mix, w_in, b_gate, q_norm, k_norm, sinks, conv_w, v_norm, w_spatial, b_spatial,
     w_branch, w_out, norm_ffn, w_gate_up, w_down) = (p[l] for p in prm)
    w_in_l = w_in
    w_qkv = w_in_l[:, OFF_Q:OFF_BG].astype(BF16)
    w_bch = w_in_l[:, OFF_BG:OFF_U].astype(BF16)
    w_u = w_in_l[:, OFF_U:OFF_VG].astype(BF16)
    w_v = w_in_l[:, OFF_VG:OFF_G].astype(BF16)
    w_g = w_in_l[:, OFF_G:].astype(BF16)
    g_mix = norm_mix.reshape(1, D_MODEL)
    qn = jnp.tile(q_norm, N_HEADS).reshape(1, BRANCH_W)
    kn = jnp.tile(k_norm, N_KV_HEADS).reshape(1, KV_W)

    qkv = functools.partial(_qkv, x_all, g_mix, w_qkv, qn, kn, const["pq"], const["pk"])
    q_p, k_p, v_p = qkv(row_block0=0, n_rows=N_PROMPT, tm=TM_SEQ)
    q_s, k_s, v_s = qkv(row_block0=SAMPLE_BLOCK, n_rows=DEC_BATCH, tm=DEC_BATCH)
    oa_p = _attn_prompt(sinks, q_p, k_p, v_p, const["bias_p"])
    n_buf = cache_k.shape[2]
    oa_s = _attn_sample(
        q_s.reshape(DEC_BATCH, N_HEADS, HEAD_DIM),
        k_s.reshape(DEC_BATCH, 1, KV_W), v_s.reshape(DEC_BATCH, 1, KV_W),
        cache_k[l].reshape(DEC_BATCH, n_buf, KV_W), cache_v[l].reshape(DEC_BATCH, n_buf, KV_W),
        const["bias_s"], sinks.reshape(N_HEADS, 1), const["mask_s"], const["rep_s"],
    ).reshape(DEC_BATCH, BRANCH_W)

    ob_p, nc_p = _conv_prompt(x_all, g_mix, w_bch, conv_w)
    ob_s, zc_s = _conv_sample(x_all, g_mix, w_bch, conv_w, state_conv[l, :, 0], state_conv[l, :, 1])

    vn = v_norm.reshape(1, D_GMLP)
    bs_wide = jnp.broadcast_to(b_spatial[:, :, None], (N_SPATIAL_GROUPS, CHUNK, SPATIAL_GROUP_W))
    oc_p = _gmlp_prompt(x_all, g_mix, w_u, w_v, vn, w_spatial, bs_wide)
    ws0 = jnp.repeat(w_spatial[:, 0, 0], SPATIAL_GROUP_W).reshape(1, D_GMLP)
    bs0 = jnp.repeat(b_spatial[:, 0], SPATIAL_GROUP_W).reshape(1, D_GMLP)
    oc_s, vg_s = _gmlp_sample(x_all, g_mix, w_u, w_v, vn, ws0, bs0)

    o_a = jnp.concatenate([oa_p, oa_s], axis=0)
    o_b = jnp.concatenate([ob_p, ob_s], axis=0)
    o_c = jnp.concatenate([oc_p, oc_s], axis=0)
    m = _merge(x_all, g_mix, w_g, b_gate, o_a, o_b, o_c, w_branch.astype(BF16))
    x1 = _residual_matmul(m, w_out.astype(BF16), x_all, "out_proj")
    h = _ffn_up(x1, norm_ffn.reshape(1, D_MODEL), w_gate_up.astype(BF16))
    x2 = _residual_matmul(h, w_down.astype(BF16), x1, "ffn_down")

    keep = min(WINDOW, SEQ)
    new_k_p = k_p.reshape(BATCH, SEQ, N_KV_HEADS, HEAD_DIM)[:, SEQ - keep:]
    new_v_p = v_p.reshape(BATCH, SEQ, N_KV_HEADS, HEAD_DIM)[:, SEQ - keep:]
    new_k_s = k_s.reshape(DEC_BATCH, 1, N_KV_HEADS, HEAD_DIM)
    new_v_s = v_s.reshape(DEC_BATCH, 1, N_KV_HEADS, HEAD_DIM)
    new_conv_s = jnp.stack([state_conv[l, :, 1], zc_s], axis=1)
    return x2, (new_k_p, new_v_p, nc_p, new_k_s, new_v_s, new_conv_s, vg_s.reshape(DEC_BATCH, 1, D_GMLP))


def kernel(x_prompt, x_sample, cache_k, cache_v, state_conv, norm_mix, w_in, b_gate, q_norm, k_norm,
           sinks, conv_w, v_norm, w_spatial, b_spatial, w_branch, w_out, norm_ffn, w_gate_up, w_down):
    prm = (norm_mix, w_in, b_gate, q_norm, k_norm, sinks, conv_w, v_norm, w_spatial, b_spatial,
           w_branch, w_out, norm_ffn, w_gate_up, w_down)
    kv_of_col = np.arange(KV_W) // HEAD_DIM
    kv_of_head = np.arange(N_HEADS) // GQA_GROUP
    const = {
        "pq": _block_diag_ones(BRANCH_W),
        "pk": _block_diag_ones(KV_W),
        "bias_p": _prompt_bias(),
        "bias_s": _sample_bias(cache_k.shape[2]),
        "mask_s": jnp.asarray(kv_of_head[:, None] == kv_of_col[None, :], dtype=F32),
        "rep_s": jnp.asarray(np.tile(np.eye(HEAD_DIM), (1, N_KV_HEADS)), dtype=BF16),
    }
    x_all = jnp.concatenate([x_prompt.reshape(N_PROMPT, D_MODEL), x_sample.reshape(DEC_BATCH, D_MODEL)], axis=0)
    per_layer = []
    for l in range(DEPTH):
        x_all, outs = _layer(x_all, l, cache_k, cache_v, state_conv, prm, const)
        per_layer.append(outs)
    stacked = [jnp.stack([per_layer[l][i] for l in range(DEPTH)]) for i in range(7)]
    y_prompt = x_all[:N_PROMPT].reshape(BATCH, SEQ, D_MODEL)
    y_sample = x_all[N_PROMPT:].reshape(DEC_BATCH, 1, D_MODEL)
    return (y_prompt, y_sample, *stacked)
```

```python
import jax
import jax.numpy as jnp
import numpy as np
from jax import lax
from jax.experimental import pallas as pl
from jax.experimental.pallas import tpu as pltpu

D_MODEL = 2048
BATCH = 4
SEQ = 2048
DEPTH = 2
DEC_BATCH = 128
BRANCH_W = 1024
HEAD_DIM = 64
N_HEADS = 16
N_KV_HEADS = 4
GQA_GROUP = 4
KV_W = 256
WINDOW = 128
D_CONV = 1024
CONV_WIDTH = 3
D_GMLP = 1024
CHUNK = 128
N_SPATIAL_GROUPS = 8
SPATIAL_GROUP_W = 128
N_BRANCHES = 3
D_FF = 5632
EPS = 1e-6
NEG_INF = -1e30

N_PROMPT = BATCH * SEQ
N_ROWS = N_PROMPT + DEC_BATCH
SAMPLE_BLOCK = N_PROMPT // DEC_BATCH

WIDE = 1536
OFF_G = 6656

TM_SEQ = 512
TM_ROW = 640
TN = 512
VMEM_LIMIT = 56 * 1024 * 1024

F32 = jnp.float32
BF16 = jnp.bfloat16


def _params(sem):
    return pltpu.CompilerParams(dimension_semantics=sem, vmem_limit_bytes=VMEM_LIMIT)


def _rms_rows(x, g):
    ms = jnp.mean(x * x, axis=-1, keepdims=True)
    return x * lax.rsqrt(ms + EPS) * g


def _dot(a, b):
    return jnp.dot(a, b, preferred_element_type=F32)


def _dot_nt(a, b):
    return lax.dot_general(a, b, (((1,), (1,)), ((), ())), preferred_element_type=F32)


def _gelu(x):
    return 0.5 * x * (1.0 + jnp.tanh(np.sqrt(2.0 / np.pi).astype(np.float32) * (x + 0.044715 * (x * x * x))))


def _layer_block(shape, l, *idx):
    return pl.BlockSpec((None, *shape), lambda *_: (l, *idx))


def _qkv_kernel(x_ref, g_ref, w_ref, qn_ref, kn_ref, pq_ref, pk_ref, q_ref, k_ref, v_ref, *last_refs):
    xn = _rms_rows(x_ref[...], g_ref[...]).astype(BF16)
    z = _dot(xn, w_ref[...])
    q = z[:, :BRANCH_W]
    k = z[:, BRANCH_W:BRANCH_W + KV_W]
    v = z[:, BRANCH_W + KV_W:]
    q_ms = _dot((q * q).astype(BF16), pq_ref[...]) * (1.0 / HEAD_DIM)
    k_ms = _dot((k * k).astype(BF16), pk_ref[...]) * (1.0 / HEAD_DIM)
    q_ref[...] = (q * lax.rsqrt(q_ms + EPS) * qn_ref[...] * (HEAD_DIM ** -0.5)).astype(BF16)
    kn = k * lax.rsqrt(k_ms + EPS) * kn_ref[...]
    k_ref[...] = kn
    v_ref[...] = v
    if last_refs:
        klast_ref, vlast_ref = last_refs
        tm = x_ref.shape[0]
        klast_ref[...] = kn[tm - WINDOW:]
        vlast_ref[...] = v[tm - WINDOW:]


def _qkv(x_all, g, w_in, l, qn, kn, pq, pk, *, row_block0, n_rows, tm, tiles_per_seq=None):
    nt = n_rows // tm
    const = lambda i: (0, 0)
    rows = lambda i: (i, 0)
    out_specs = [
        pl.BlockSpec((tm, BRANCH_W), rows),
        pl.BlockSpec((tm, KV_W), rows),
        pl.BlockSpec((tm, KV_W), rows),
    ]
    out_shape = [
        jax.ShapeDtypeStruct((n_rows, BRANCH_W), BF16),
        jax.ShapeDtypeStruct((n_rows, KV_W), F32),
        jax.ShapeDtypeStruct((n_rows, KV_W), F32),
    ]
    if tiles_per_seq is not None:
        n_seq = nt // tiles_per_seq
        out_specs += [pl.BlockSpec((WINDOW, KV_W), lambda i: (i // tiles_per_seq, 0))] * 2
        out_shape += [jax.ShapeDtypeStruct((n_seq * WINDOW, KV_W), F32)] * 2
    return pl.pallas_call(
        _qkv_kernel,
        grid=(nt,),
        in_specs=[
            pl.BlockSpec((tm, D_MODEL), lambda i: (row_block0 + i, 0)),
            _layer_block((1, D_MODEL), l, 0, 0),
            _layer_block((D_MODEL, WIDE), l, 0, 0),
            pl.BlockSpec((1, BRANCH_W), const),
            pl.BlockSpec((1, KV_W), const),
            pl.BlockSpec((BRANCH_W, BRANCH_W), const),
            pl.BlockSpec((KV_W, KV_W), const),
        ],
        out_specs=out_specs,
        out_shape=out_shape,
        compiler_params=_params(("arbitrary",)),
        name="qkv",
    )(x_all, g, w_in, qn, kn, pq, pk)


def _attn_prompt_kernel(sink_ref, q_ref, kp_ref, ko_ref, vp_ref, vo_ref, bias_ref, o_ref):
    kk = jnp.concatenate([kp_ref[...], ko_ref[...]], axis=0).astype(BF16)
    vv = jnp.concatenate([vp_ref[...], vo_ref[...]], axis=0).astype(BF16)
    for g in range(N_KV_HEADS):
        kg = kk[:, g * HEAD_DIM:(g + 1) * HEAD_DIM]
        vg = vv[:, g * HEAD_DIM:(g + 1) * HEAD_DIM]
        heads = range(g * GQA_GROUP, (g + 1) * GQA_GROUP)
        qg = jnp.concatenate([q_ref[:, h * HEAD_DIM:(h + 1) * HEAD_DIM] for h in heads], axis=0)
        s_all = _dot_nt(qg, kg)
        ps, dens = [], []
        for i, h in enumerate(heads):
            s = s_all[i * WINDOW:(i + 1) * WINDOW] + bias_ref[h]
            sink = sink_ref[h]
            m = jnp.maximum(jnp.max(s, axis=-1, keepdims=True), sink)
            p = jnp.exp(s - m)
            dens.append(jnp.sum(p, axis=-1, keepdims=True) + jnp.exp(sink - m))
            ps.append(p.astype(BF16))
        o_all = _dot(jnp.concatenate(ps, axis=0), vg)
        for i, h in enumerate(heads):
            o = o_all[i * WINDOW:(i + 1) * WINDOW] / dens[i]
            o_ref[:, h * HEAD_DIM:(h + 1) * HEAD_DIM] = o.astype(BF16)


def _attn_prompt(sinks, q, k, v, bias):
    nb = SEQ // WINDOW
    own = lambda b, j: (b * nb + j, 0)
    prev = lambda b, j: (b * nb + jnp.maximum(j - 1, 0), 0)
    return pl.pallas_call(
        _attn_prompt_kernel,
        grid=(BATCH, nb),
        in_specs=[
            pl.BlockSpec(memory_space=pltpu.SMEM),
            pl.BlockSpec((WINDOW, BRANCH_W), own),
            pl.BlockSpec((WINDOW, KV_W), prev),
            pl.BlockSpec((WINDOW, KV_W), own),
            pl.BlockSpec((WINDOW, KV_W), prev),
            pl.BlockSpec((WINDOW, KV_W), own),
            pl.BlockSpec((None, N_HEADS, WINDOW, 2 * WINDOW), lambda b, j: (jnp.minimum(j, 1), 0, 0, 0)),
        ],
        out_specs=pl.BlockSpec((WINDOW, BRANCH_W), own),
        out_shape=jax.ShapeDtypeStruct((N_ROWS, BRANCH_W), BF16),
        compiler_params=_params(("arbitrary", "arbitrary")),
        name="attn_prompt",
    )(sinks, q, k, k, v, v, bias)


def _prompt_bias():
    slopes = jnp.exp2(-8.0 * jnp.arange(1, N_HEADS + 1, dtype=F32) / N_HEADS)
    qi = jnp.arange(WINDOW, dtype=jnp.int32)[:, None]
    ki = jnp.arange(2 * WINDOW, dtype=jnp.int32)[None, :] - WINDOW
    dist = qi - ki
    valid = (dist >= 0) & (dist < WINDOW)
    bias = -slopes[:, None, None] * dist.astype(F32)[None]
    with_prev = jnp.where(valid[None], bias, NEG_INF)
    first = jnp.where((valid & (ki >= 0))[None], bias, NEG_INF)
    return jnp.stack([first, with_prev])


SAMPLE_BT = 8


def _attn_sample_kernel(q_ref, kn_ref, vn_ref, ck_ref, cv_ref, bias_ref, sink_ref, mask_ref,
                        rep_ref, o_ref):
    mask = mask_ref[...]
    bias = bias_ref[...]
    sink = sink_ref[...]
    for b in range(SAMPLE_BT):
        qe = _dot(q_ref[b], rep_ref[...]) * mask
        s = _dot_nt(qe.astype(BF16), ck_ref[b].astype(BF16)) + bias
        s_new = jnp.sum(qe * kn_ref[b], axis=-1, keepdims=True)
        m = jnp.maximum(jnp.maximum(jnp.max(s, axis=-1, keepdims=True), s_new), sink)
        p = jnp.exp(s - m)
        p_new = jnp.exp(s_new - m)
        den = jnp.sum(p, axis=-1, keepdims=True) + p_new + jnp.exp(sink - m)
        of = _dot(p.astype(BF16), cv_ref[b].astype(BF16)) + p_new * vn_ref[b]
        of = of * mask / den
        o = (of[:, 0:64] + of[:, 64:128]) + (of[:, 128:192] + of[:, 192:256])
        o_ref[b] = o.astype(BF16)


def _attn_sample(q3, k_new, v_new, ck, cv, bias, sinks_col, mask, rep):
    nsteps = DEC_BATCH // SAMPLE_BT
    b3 = lambda i: (i, 0, 0)
    c2 = lambda i: (0, 0)
    return pl.pallas_call(
        _attn_sample_kernel,
        grid=(nsteps,),
        in_specs=[
            pl.BlockSpec((SAMPLE_BT, N_HEADS, HEAD_DIM), b3),
            pl.BlockSpec((SAMPLE_BT, 1, KV_W), b3),
            pl.BlockSpec((SAMPLE_BT, 1, KV_W), b3),
            pl.BlockSpec((SAMPLE_BT, WINDOW, KV_W), b3),
            pl.BlockSpec((SAMPLE_BT, WINDOW, KV_W), b3),
            pl.BlockSpec((N_HEADS, WINDOW), c2),
            pl.BlockSpec((N_HEADS, 1), c2),
            pl.BlockSpec((N_HEADS, KV_W), c2),
            pl.BlockSpec((HEAD_DIM, KV_W), c2),
        ],
        out_specs=pl.BlockSpec((SAMPLE_BT, N_HEADS, HEAD_DIM), b3),
        out_shape=jax.ShapeDtypeStruct((DEC_BATCH, N_HEADS, HEAD_DIM), BF16),
        compiler_params=_params(("arbitrary",)),
        name="attn_sample",
    )(q3, k_new, v_new, ck, cv, bias, sinks_col, mask, rep)


def _sample_bias(n_buf):
    slopes = jnp.exp2(-8.0 * jnp.arange(1, N_HEADS + 1, dtype=F32) / N_HEADS)
    dist = n_buf - jnp.arange(n_buf, dtype=jnp.int32)
    bias = -slopes[:, None] * dist.astype(F32)[None, :]
    return jnp.where((dist < WINDOW)[None, :], bias, NEG_INF)


def _bch_cols(wa_ref, wb_ref, seg, c):
    r = seg * D_CONV + c * TN
    return wa_ref[:, r:r + TN] if r < WIDE else wb_ref[:, r - WIDE:r - WIDE + TN]


def _conv_prompt_kernel(x_ref, g_ref, wa_ref, wb_ref, cw_ref, ob_ref, nc_ref, zbuf, carry):
    t = pl.program_id(1)
    tm = x_ref.shape[0]
    xn = _rms_rows(x_ref[...], g_ref[...]).astype(BF16)

    @pl.when(t == 0)
    def _():
        zbuf[0:8, :] = jnp.zeros((8, D_CONV), F32)

    @pl.when(t > 0)
    def _():
        zbuf[0:8, :] = carry[...]

    for c in range(D_CONV // TN):
        cols = slice(c * TN, (c + 1) * TN)
        zc = _dot(xn, _bch_cols(wa_ref, wb_ref, 1, c)) * _dot(xn, _bch_cols(wa_ref, wb_ref, 2, c))
        zbuf[8:8 + tm, cols] = zc
        cw = cw_ref[:, cols]
        y = cw[0:1] * zbuf[6:6 + tm, cols] + cw[1:2] * zbuf[7:7 + tm, cols] + cw[2:3] * zc
        ob_ref[:, cols] = (_dot(xn, _bch_cols(wa_ref, wb_ref, 0, c)) * y).astype(BF16)
    carry[...] = zbuf[tm:tm + 8, :]
    nc_ref[...] = zbuf[tm + 6:tm + 8, :]


def _conv_prompt(x_all, g, w_in, l, conv_w):
    nt = SEQ // TM_SEQ
    return pl.pallas_call(
        _conv_prompt_kernel,
        grid=(BATCH, nt),
        in_specs=[
            pl.BlockSpec((TM_SEQ, D_MODEL), lambda b, t: (b * nt + t, 0)),
            _layer_block((1, D_MODEL), l, 0, 0),
            _layer_block((D_MODEL, WIDE), l, 0, 1),
            _layer_block((D_MODEL, WIDE), l, 0, 2),
            _layer_block((CONV_WIDTH, D_CONV), l, 0, 0),
        ],
        out_specs=[
            pl.BlockSpec((TM_SEQ, D_CONV), lambda b, t: (b * nt + t, 0)),
            pl.BlockSpec((None, CONV_WIDTH - 1, D_CONV), lambda b, t: (b, 0, 0)),
        ],
        out_shape=[
            jax.ShapeDtypeStruct((N_ROWS, D_CONV), BF16),
            jax.ShapeDtypeStruct((BATCH, CONV_WIDTH - 1, D_CONV), F32),
        ],
        scratch_shapes=[
            pltpu.VMEM((TM_SEQ + 8, D_CONV), F32),
            pltpu.VMEM((8, D_CONV), F32),
        ],
        compiler_params=_params(("arbitrary", "arbitrary")),
        name="conv_prompt",
    )(x_all, g, w_in, w_in, conv_w)


def _conv_sample_kernel(x_ref, g_ref, wa_ref, wb_ref, cw_ref, cb0_ref, cb1_ref, ob_ref, zc_ref):
    xn = _rms_rows(x_ref[...], g_ref[...]).astype(BF16)
    for c in range(D_CONV // TN):
        cols = slice(c * TN, (c + 1) * TN)
        zc = _dot(xn, _bch_cols(wa_ref, wb_ref, 1, c)) * _dot(xn, _bch_cols(wa_ref, wb_ref, 2, c))
        cw = cw_ref[:, cols]
        y = cw[0:1] * cb0_ref[:, cols] + cw[1:2] * cb1_ref[:, cols] + cw[2:3] * zc
        ob_ref[:, cols] = (_dot(xn, _bch_cols(wa_ref, wb_ref, 0, c)) * y).astype(BF16)
        zc_ref[:, cols] = zc


def _conv_sample(x_all, g, w_in, l, conv_w, cb0, cb1):
    c2 = lambda i: (0, 0)
    return pl.pallas_call(
        _conv_sample_kernel,
        grid=(1,),
        in_specs=[
            pl.BlockSpec((DEC_BATCH, D_MODEL), lambda i: (SAMPLE_BLOCK, 0)),
            _layer_block((1, D_MODEL), l, 0, 0),
            _layer_block((D_MODEL, WIDE), l, 0, 1),
            _layer_block((D_MODEL, WIDE), l, 0, 2),
            _layer_block((CONV_WIDTH, D_CONV), l, 0, 0),
            pl.BlockSpec((DEC_BATCH, D_CONV), c2),
            pl.BlockSpec((DEC_BATCH, D_CONV), c2),
        ],
        out_specs=[pl.BlockSpec((DEC_BATCH, D_CONV), c2), pl.BlockSpec((DEC_BATCH, D_CONV), c2)],
        out_shape=[
            jax.ShapeDtypeStruct((DEC_BATCH, D_CONV), BF16),
            jax.ShapeDtypeStruct((DEC_BATCH, D_CONV), F32),
        ],
        compiler_params=_params(("arbitrary",)),
        name="conv_sample",
    )(x_all, g, w_in, w_in, conv_w, cb0, cb1)


HALF_V = WIDE - D_GMLP


def _gmlp_uv(xn, wa_ref, wb_ref, vn_ref):
    u = _gelu(_dot(xn, wa_ref[:, :D_GMLP]))
    v_lo = _gelu(_dot(xn, wa_ref[:, D_GMLP:]))
    v_hi = _gelu(_dot(xn, wb_ref[:, :D_GMLP - HALF_V]))
    ms = (jnp.sum(v_lo * v_lo, axis=-1, keepdims=True)
          + jnp.sum(v_hi * v_hi, axis=-1, keepdims=True)) * (1.0 / D_GMLP)
    r = lax.rsqrt(ms + EPS)
    return u, v_lo * r * vn_ref[:, :HALF_V], v_hi * r * vn_ref[:, HALF_V:]


def _gmlp_prompt_kernel(x_ref, g_ref, wa_ref, wb_ref, vn_ref, ws_ref, bs_ref, oc_ref, u_scr, v_scr):
    tm = x_ref.shape[0]
    xn = _rms_rows(x_ref[...], g_ref[...]).astype(BF16)
    u, v_lo, v_hi = _gmlp_uv(xn, wa_ref, wb_ref, vn_ref)
    u_scr[...] = u
    v_scr[:, :HALF_V] = v_lo.astype(BF16)
    v_scr[:, HALF_V:] = v_hi.astype(BF16)
    row = lax.broadcasted_iota(jnp.int32, (CHUNK, CHUNK), 0)
    col = lax.broadcasted_iota(jnp.int32, (CHUNK, CHUNK), 1)
    for grp in range(N_SPATIAL_GROUPS):
        w = jnp.where(col <= row, ws_ref[grp], 0.0).astype(BF16)
        cols = slice(grp * SPATIAL_GROUP_W, (grp + 1) * SPATIAL_GROUP_W)
        for ch in range(tm // CHUNK):
            rows = slice(ch * CHUNK, (ch + 1) * CHUNK)
            mix = _dot(w, v_scr[rows, cols]) + bs_ref[grp]
            oc_ref[rows, cols] = (u_scr[rows, cols] * mix).astype(BF16)


def _gmlp_prompt(x_all, g, w_in, l, v_norm, w_s, b_s_wide):
    nt = N_PROMPT // TM_SEQ
    return pl.pallas_call(
        _gmlp_prompt_kernel,
        grid=(nt,),
        in_specs=[
            pl.BlockSpec((TM_SEQ, D_MODEL), lambda i: (i, 0)),
            _layer_block((1, D_MODEL), l, 0, 0),
            _layer_block((D_MODEL, WIDE), l, 0, 3),
            _layer_block((D_MODEL, WIDE), l, 0, 4),
            _layer_block((1, D_GMLP), l, 0, 0),
            _layer_block((N_SPATIAL_GROUPS, CHUNK, CHUNK), l, 0, 0, 0),
            _layer_block((N_SPATIAL_GROUPS, CHUNK, SPATIAL_GROUP_W), l, 0, 0, 0),
        ],
        out_specs=pl.BlockSpec((TM_SEQ, D_GMLP), lambda i: (i, 0)),
        out_shape=jax.ShapeDtypeStruct((N_ROWS, D_GMLP), BF16),
        scratch_shapes=[pltpu.VMEM((TM_SEQ, D_GMLP), F32), pltpu.VMEM((TM_SEQ, D_GMLP), BF16)],
        compiler_params=_params(("arbitrary",)),
        name="gmlp_prompt",
    )(x_all, g, w_in, w_in, v_norm, w_s, b_s_wide)


def _gmlp_sample_kernel(x_ref, g_ref, wa_ref, wb_ref, vn_ref, ws0_ref, bs0_ref, oc_ref, vg_ref):
    xn = _rms_rows(x_ref[...], g_ref[...]).astype(BF16)
    u, v_lo, v_hi = _gmlp_uv(xn, wa_ref, wb_ref, vn_ref)
    vg_ref[:, :HALF_V] = v_lo
    vg_ref[:, HALF_V:] = v_hi
    oc_ref[...] = (u * (ws0_ref[...] * vg_ref[...] + bs0_ref[...])).astype(BF16)


def _gmlp_sample(x_all, g, w_in, l, v_norm, ws0, bs0):
    c2 = lambda i: (0, 0)
    return pl.pallas_call(
        _gmlp_sample_kernel,
        grid=(1,),
        in_specs=[
            pl.BlockSpec((DEC_BATCH, D_MODEL), lambda i: (SAMPLE_BLOCK, 0)),
            _layer_block((1, D_MODEL), l, 0, 0),
            _layer_block((D_MODEL, WIDE), l, 0, 3),
            _layer_block((D_MODEL, WIDE), l, 0, 4),
            _layer_block((1, D_GMLP), l, 0, 0),
            pl.BlockSpec((1, D_GMLP), c2),
            pl.BlockSpec((1, D_GMLP), c2),
        ],
        out_specs=[pl.BlockSpec((DEC_BATCH, D_GMLP), c2), pl.BlockSpec((DEC_BATCH, D_GMLP), c2)],
        out_shape=[
            jax.ShapeDtypeStruct((DEC_BATCH, D_GMLP), BF16),
            jax.ShapeDtypeStruct((DEC_BATCH, D_GMLP), F32),
        ],
        compiler_params=_params(("arbitrary",)),
        name="gmlp_sample",
    )(x_all, g, w_in, w_in, v_norm, ws0, bs0)


def _merge_kernel(x_ref, g_ref, wg0_ref, wg1_ref, wg2_ref, bg_ref, oa_ref, ob_ref, oc_ref, wb_ref,
                  m_ref, xn_scr):
    @pl.when(pl.program_id(1) == 0)
    def _():
        xn_scr[...] = _rms_rows(x_ref[...], g_ref[...]).astype(BF16)

    xn = xn_scr[...]
    bg = bg_ref[...]
    acc = None
    for i, (wg_ref, o_ref) in enumerate(((wg0_ref, oa_ref), (wg1_ref, ob_ref), (wg2_ref, oc_ref))):
        gate = jax.nn.sigmoid(_dot(xn, wg_ref[...]) + bg[i:i + 1])
        term = gate * _dot(o_ref[...], wb_ref[i])
        acc = term if acc is None else acc + term
    m_ref[...] = acc.astype(BF16)


def _merge(x_all, g, w_in, l, b_gate, o_a, o_b, o_c, w_branch):
    nt = N_ROWS // TM_ROW
    nn = D_MODEL // TN
    g0 = OFF_G // TN
    rows = lambda i, j: (i, 0)
    gate_w = lambda br: pl.BlockSpec((None, D_MODEL, TN), lambda i, j: (l, 0, g0 + br * nn + j))
    return pl.pallas_call(
        _merge_kernel,
        grid=(nt, nn),
        in_specs=[
            pl.BlockSpec((TM_ROW, D_MODEL), rows),
            _layer_block((1, D_MODEL), l, 0, 0),
            gate_w(0), gate_w(1), gate_w(2),
            pl.BlockSpec((None, N_BRANCHES, TN), lambda i, j: (l, 0, j)),
            pl.BlockSpec((TM_ROW, BRANCH_W), rows),
            pl.BlockSpec((TM_ROW, BRANCH_W), rows),
            pl.BlockSpec((TM_ROW, BRANCH_W), rows),
            pl.BlockSpec((None, N_BRANCHES, BRANCH_W, TN), lambda i, j: (l, 0, 0, j)),
        ],
        out_specs=pl.BlockSpec((TM_ROW, TN), lambda i, j: (i, j)),
        out_shape=jax.ShapeDtypeStruct((N_ROWS, D_MODEL), BF16),
        scratch_shapes=[pltpu.VMEM((TM_ROW, D_MODEL), BF16)],
        compiler_params=_params(("arbitrary", "arbitrary")),
        name="merge",
    )(x_all, g, w_in, w_in, w_in, b_gate, o_a, o_b, o_c, w_branch)


OUT_SPLIT = 2


def _out_proj_kernel(m_ref, w_ref, x_ref, g_ref, x1_ref, xn_ref):
    sub = m_ref.shape[0] // OUT_SPLIT
    for s in range(OUT_SPLIT):
        rows = slice(s * sub, (s + 1) * sub)
        x1 = x_ref[rows, :] + _dot(m_ref[rows, :], w_ref[...])
        x1_ref[rows, :] = x1
        xn_ref[rows, :] = _rms_rows(x1, g_ref[...]).astype(BF16)


def _out_proj(m, w_out, l, x_all, g_ffn):
    nt = N_ROWS // TM_ROW
    rows = lambda i: (i, 0)
    return pl.pallas_call(
        _out_proj_kernel,
        grid=(nt,),
        in_specs=[
            pl.BlockSpec((TM_ROW, D_MODEL), rows),
            _layer_block((D_MODEL, D_MODEL), l, 0, 0),
            pl.BlockSpec((TM_ROW, D_MODEL), rows),
            _layer_block((1, D_MODEL), l, 0, 0),
        ],
        out_specs=[pl.BlockSpec((TM_ROW, D_MODEL), rows), pl.BlockSpec((TM_ROW, D_MODEL), rows)],
        out_shape=[
            jax.ShapeDtypeStruct((N_ROWS, D_MODEL), F32),
            jax.ShapeDtypeStruct((N_ROWS, D_MODEL), BF16),
        ],
        compiler_params=_params(("arbitrary",)),
        name="out_proj",
    )(m, w_out, x_all, g_ffn)


def _ffn_up_kernel(xn_ref, wgate_ref, wup_ref, h_ref):
    xn = xn_ref[...]
    gate = _dot(xn, wgate_ref[...])
    h_ref[...] = (gate * jax.nn.sigmoid(gate) * _dot(xn, wup_ref[...])).astype(BF16)


def _ffn_up(xn, w_gate_up, l):
    nt = N_ROWS // TM_ROW
    nn = D_FF // TN
    return pl.pallas_call(
        _ffn_up_kernel,
        grid=(nt, nn),
        in_specs=[
            pl.BlockSpec((TM_ROW, D_MODEL), lambda i, j: (i, 0)),
            pl.BlockSpec((None, D_MODEL, TN), lambda i, j: (l, 0, j)),
            pl.BlockSpec((None, D_MODEL, TN), lambda i, j: (l, 0, nn + j)),
        ],
        out_specs=pl.BlockSpec((TM_ROW, TN), lambda i, j: (i, j)),
        out_shape=jax.ShapeDtypeStruct((N_ROWS, D_FF), BF16),
        compiler_params=_params(("arbitrary", "arbitrary")),
        name="ffn_up",
    )(xn, w_gate_up, w_gate_up)


def _ffn_down_kernel(h_ref, w_ref, x_ref, o_ref):
    o_ref[...] = x_ref[...] + _dot(h_ref[...], w_ref[...])


def _ffn_down(h, w_down, l, x1, *, row_block0, n_rows, tm):
    nt = n_rows // tm
    nn = D_MODEL // TN
    return pl.pallas_call(
        _ffn_down_kernel,
        grid=(nt, nn),
        in_specs=[
            pl.BlockSpec((tm, D_FF), lambda i, j: (row_block0 + i, 0)),
            pl.BlockSpec((None, D_FF, TN), lambda i, j: (l, 0, j)),
            pl.BlockSpec((tm, TN), lambda i, j: (row_block0 + i, j)),
        ],
        out_specs=pl.BlockSpec((tm, TN), lambda i, j: (i, j)),
        out_shape=jax.ShapeDtypeStruct((n_rows, D_MODEL), F32),
        compiler_params=_params(("arbitrary", "arbitrary")),
        name="ffn_down",
    )(h, w_down, x1)


def _block_diag_ones(width):
    head = np.arange(width) // HEAD_DIM
    return jnp.asarray(head[:, None] == head[None, :], dtype=BF16)


def _fill_sample_rows(o_prompt, o_sample):
    return lax.dynamic_update_slice(o_prompt, o_sample, (N_PROMPT, 0))


def _layer(x_all, l, last, cache_k, cache_v, state_conv, p, const):
    g_mix = p["norm_mix"]
    w_in = p["w_in"]
    qn = jnp.tile(p["q_norm"][l], N_HEADS).reshape(1, BRANCH_W)
    kn = jnp.tile(p["k_norm"][l], N_KV_HEADS).reshape(1, KV_W)
    sinks = p["sinks"][l]

    q_p, k_p, v_p, klast_p, vlast_p = _qkv(
        x_all, g_mix, w_in, l, qn, kn, const["pq"], const["pk"],
        row_block0=0, n_rows=N_PROMPT, tm=TM_SEQ, tiles_per_seq=SEQ // TM_SEQ)
    q_s, k_s, v_s = _qkv(
        x_all, g_mix, w_in, l, qn, kn, const["pq"], const["pk"],
        row_block0=SAMPLE_BLOCK, n_rows=DEC_BATCH, tm=DEC_BATCH)
    oa_p = _attn_prompt(sinks, q_p, k_p, v_p, const["bias_p"])
    n_buf = cache_k.shape[2]
    oa_s = _attn_sample(
        q_s.reshape(DEC_BATCH, N_HEADS, HEAD_DIM),
        k_s.reshape(DEC_BATCH, 1, KV_W), v_s.reshape(DEC_BATCH, 1, KV_W),
        cache_k[l].reshape(DEC_BATCH, n_buf, KV_W), cache_v[l].reshape(DEC_BATCH, n_buf, KV_W),
        const["bias_s"], sinks.reshape(N_HEADS, 1), const["mask_s"], const["rep_s"],
    ).reshape(DEC_BATCH, BRANCH_W)

    ob_p, nc_p = _conv_prompt(x_all, g_mix, w_in, l, p["conv_w"])
    ob_s, zc_s = _conv_sample(x_all, g_mix, w_in, l, p["conv_w"], state_conv[l, :, 0], state_conv[l, :, 1])

    w_s = p["w_spatial"]
    b_s = p["b_spatial"]
    bs_wide = jnp.broadcast_to(b_s[:, :, :, None], (DEPTH, N_SPATIAL_GROUPS, CHUNK, SPATIAL_GROUP_W))
    oc_p = _gmlp_prompt(x_all, g_mix, w_in, l, p["v_norm"], w_s, bs_wide)
    ws0 = jnp.repeat(w_s[l, :, 0, 0], SPATIAL_GROUP_W).reshape(1, D_GMLP)
    bs0 = jnp.repeat(b_s[l, :, 0], SPATIAL_GROUP_W).reshape(1, D_GMLP)
    oc_s, vg_s = _gmlp_sample(x_all, g_mix, w_in, l, p["v_norm"], ws0, bs0)

    o_a = _fill_sample_rows(oa_p, oa_s)
    o_b = _fill_sample_rows(ob_p, ob_s)
    o_c = _fill_sample_rows(oc_p, oc_s)
    m = _merge(x_all, g_mix, w_in, l, p["b_gate"], o_a, o_b, o_c, p["w_branch"])
    x1, xn2 = _out_proj(m, p["w_out"], l, x_all, p["norm_ffn"])
    h = _ffn_up(xn2, p["w_gate_up"], l)
    if last:
        x2 = (_ffn_down(h, p["w_down"], l, x1, row_block0=0, n_rows=N_PROMPT, tm=TM_SEQ),
              _ffn_down(h, p["w_down"], l, x1, row_block0=SAMPLE_BLOCK, n_rows=DEC_BATCH, tm=DEC_BATCH))
    else:
        x2 = _ffn_down(h, p["w_down"], l, x1, row_block0=0, n_rows=N_ROWS, tm=TM_ROW)

    new_k_p = klast_p.reshape(BATCH, WINDOW, N_KV_HEADS, HEAD_DIM)
    new_v_p = vlast_p.reshape(BATCH, WINDOW, N_KV_HEADS, HEAD_DIM)
    new_k_s = k_s.reshape(DEC_BATCH, 1, N_KV_HEADS, HEAD_DIM)
    new_v_s = v_s.reshape(DEC_BATCH, 1, N_KV_HEADS, HEAD_DIM)
    new_conv_s = jnp.stack([state_conv[l, :, 1], zc_s], axis=1)
    return x2, (new_k_p, new_v_p, nc_p, new_k_s, new_v_s, new_conv_s, vg_s.reshape(DEC_BATCH, 1, D_GMLP))


def kernel(x_prompt, x_sample, cache_k, cache_v, state_conv, norm_mix, w_in, b_gate, q_norm, k_norm,
           sinks, conv_w, v_norm, w_spatial, b_spatial, w_branch, w_out, norm_ffn, w_gate_up, w_down):
    assert min(WINDOW, SEQ) == WINDOW and SEQ % TM_SEQ == 0 and TM_SEQ >= WINDOW
    p = {
        "norm_mix": norm_mix.reshape(DEPTH, 1, D_MODEL),
        "norm_ffn": norm_ffn.reshape(DEPTH, 1, D_MODEL),
        "v_norm": v_norm.reshape(DEPTH, 1, D_GMLP),
        "w_in": w_in.astype(BF16),
        "w_branch": w_branch.astype(BF16),
        "w_out": w_out.astype(BF16),
        "w_gate_up": w_gate_up.astype(BF16),
        "w_down": w_down.astype(BF16),
        "b_gate": b_gate, "q_norm": q_norm, "k_norm": k_norm, "sinks": sinks, "conv_w": conv_w,
        "w_spatial": w_spatial, "b_spatial": b_spatial,
    }
    kv_of_col = np.arange(KV_W) // HEAD_DIM
    kv_of_head = np.arange(N_HEADS) // GQA_GROUP
    const = {
        "pq": _block_diag_ones(BRANCH_W),
        "pk": _block_diag_ones(KV_W),
        "bias_p": _prompt_bias(),
        "bias_s": _sample_bias(cache_k.shape[2]),
        "mask_s": jnp.asarray(kv_of_head[:, None] == kv_of_col[None, :], dtype=F32),
        "rep_s": jnp.asarray(np.tile(np.eye(HEAD_DIM), (1, N_KV_HEADS)), dtype=BF16),
    }
    x_all = jnp.concatenate([x_prompt.reshape(N_PROMPT, D_MODEL), x_sample.reshape(DEC_BATCH, D_MODEL)], axis=0)
    per_layer = []
    for l in range(DEPTH):
        x_all, outs = _layer(x_all, l, l == DEPTH - 1, cache_k, cache_v, state_conv, p, const)
        per_layer.append(outs)
    stacked = [jnp.stack([per_layer[l][i] for l in range(DEPTH)]) for i in range(7)]
    y_prompt, y_sample = x_all
    return (y_prompt.reshape(BATCH, SEQ, D_MODEL), y_sample.reshape(DEC_BATCH, 1, D_MODEL), *stacked)
```

```python
import jax
import jax.numpy as jnp
import numpy as np
from jax import lax
from jax.experimental import pallas as pl
from jax.experimental.pallas import tpu as pltpu

D_MODEL = 2048
BATCH = 4
SEQ = 2048
DEPTH = 2
DEC_BATCH = 128
BRANCH_W = 1024
HEAD_DIM = 64
N_HEADS = 16
N_KV_HEADS = 4
GQA_GROUP = 4
KV_W = 256
WINDOW = 128
D_CONV = 1024
CONV_WIDTH = 3
D_GMLP = 1024
CHUNK = 128
N_SPATIAL_GROUPS = 8
SPATIAL_GROUP_W = 128
N_BRANCHES = 3
D_FF = 5632
EPS = 1e-6
NEG_INF = -1e30

N_PROMPT = BATCH * SEQ
N_ROWS = N_PROMPT + DEC_BATCH
SAMPLE_BLOCK = N_PROMPT // DEC_BATCH

WIDE = 1536
OFF_G = 6656

TM_SEQ = 512
TM_ROW = 640
TN = 512
VMEM_LIMIT = 56 * 1024 * 1024

F32 = jnp.float32
BF16 = jnp.bfloat16


def _params(sem):
    return pltpu.CompilerParams(dimension_semantics=sem, vmem_limit_bytes=VMEM_LIMIT)


def _rms_rows(x, g):
    ms = jnp.mean(x * x, axis=-1, keepdims=True)
    return x * lax.rsqrt(ms + EPS) * g


def _dot(a, b):
    return jnp.dot(a, b, preferred_element_type=F32)


def _dot_nt(a, b):
    return lax.dot_general(a, b, (((1,), (1,)), ((), ())), preferred_element_type=F32)


def _gelu(x):
    return 0.5 * x * (1.0 + jnp.tanh(np.sqrt(2.0 / np.pi).astype(np.float32) * (x + 0.044715 * (x * x * x))))


def _layer_block(shape, l, *idx):
    return pl.BlockSpec((None, *shape), lambda *_: (l, *idx))


def _norm_kernel(x_ref, g_ref, o_ref):
    o_ref[...] = _rms_rows(x_ref[...], g_ref[...]).astype(BF16)


def _norm(x, g, l, *, tm, out_rows):
    rows = lambda i: (i, 0)
    return pl.pallas_call(
        _norm_kernel,
        grid=(x.shape[0] // tm,),
        in_specs=[pl.BlockSpec((tm, D_MODEL), rows), _layer_block((1, D_MODEL), l, 0, 0)],
        out_specs=pl.BlockSpec((tm, D_MODEL), rows),
        out_shape=jax.ShapeDtypeStruct((out_rows, D_MODEL), BF16),
        compiler_params=_params(("arbitrary",)),
        name="norm",
    )(x, g)


def _qkv_kernel(xn_ref, w_ref, qn_ref, kn_ref, pq_ref, pk_ref, q_ref, k_ref, v_ref, *last_refs):
    z = _dot(xn_ref[...], w_ref[...])
    q = z[:, :BRANCH_W]
    k = z[:, BRANCH_W:BRANCH_W + KV_W]
    v = z[:, BRANCH_W + KV_W:]
    q_ms = _dot((q * q).astype(BF16), pq_ref[...]) * (1.0 / HEAD_DIM)
    k_ms = _dot((k * k).astype(BF16), pk_ref[...]) * (1.0 / HEAD_DIM)
    q_ref[...] = (q * lax.rsqrt(q_ms + EPS) * qn_ref[...] * (HEAD_DIM ** -0.5)).astype(BF16)
    kn = k * lax.rsqrt(k_ms + EPS) * kn_ref[...]
    k_ref[...] = kn
    v_ref[...] = v
    if last_refs:
        klast_ref, vlast_ref = last_refs
        tm = xn_ref.shape[0]
        klast_ref[...] = kn[tm - WINDOW:]
        vlast_ref[...] = v[tm - WINDOW:]


def _qkv(xn_all, w_in, l, qn, kn, pq, pk, *, row_block0, n_rows, tm, tiles_per_seq=None):
    nt = n_rows // tm
    const = lambda i: (0, 0)
    rows = lambda i: (i, 0)
    out_specs = [
        pl.BlockSpec((tm, BRANCH_W), rows),
        pl.BlockSpec((tm, KV_W), rows),
        pl.BlockSpec((tm, KV_W), rows),
    ]
    out_shape = [
        jax.ShapeDtypeStruct((n_rows, BRANCH_W), BF16),
        jax.ShapeDtypeStruct((n_rows, KV_W), F32),
        jax.ShapeDtypeStruct((n_rows, KV_W), F32),
    ]
    if tiles_per_seq is not None:
        n_seq = nt // tiles_per_seq
        out_specs += [pl.BlockSpec((WINDOW, KV_W), lambda i: (i // tiles_per_seq, 0))] * 2
        out_shape += [jax.ShapeDtypeStruct((n_seq * WINDOW, KV_W), F32)] * 2
    return pl.pallas_call(
        _qkv_kernel,
        grid=(nt,),
        in_specs=[
            pl.BlockSpec((tm, D_MODEL), lambda i: (row_block0 + i, 0)),
            _layer_block((D_MODEL, WIDE), l, 0, 0),
            pl.BlockSpec((1, BRANCH_W), const),
            pl.BlockSpec((1, KV_W), const),
            pl.BlockSpec((BRANCH_W, BRANCH_W), const),
            pl.BlockSpec((KV_W, KV_W), const),
        ],
        out_specs=out_specs,
        out_shape=out_shape,
        compiler_params=_params(("arbitrary",)),
        name="qkv",
    )(xn_all, w_in, qn, kn, pq, pk)


def _attn_prompt_kernel(sink_ref, q_ref, kp_ref, ko_ref, vp_ref, vo_ref, bias_ref, o_ref, s_scr, p_scr):
    kk = jnp.concatenate([kp_ref[...], ko_ref[...]], axis=0).astype(BF16)
    vv = jnp.concatenate([vp_ref[...], vo_ref[...]], axis=0).astype(BF16)
    group_rows = GQA_GROUP * WINDOW
    for g in range(N_KV_HEADS):
        kg = kk[:, g * HEAD_DIM:(g + 1) * HEAD_DIM]
        heads = range(g * GQA_GROUP, (g + 1) * GQA_GROUP)
        qg = jnp.concatenate([q_ref[:, h * HEAD_DIM:(h + 1) * HEAD_DIM] for h in heads], axis=0)
        s_scr[g * group_rows:(g + 1) * group_rows, :] = _dot_nt(qg, kg)
    sink_terms = []
    for h in range(N_HEADS):
        rows = slice(h * WINDOW, (h + 1) * WINDOW)
        s = s_scr[rows, :] + bias_ref[h]
        sink = sink_ref[h]
        m = jnp.maximum(jnp.max(s, axis=-1, keepdims=True), sink)
        p_scr[rows, :] = jnp.exp(s - m).astype(BF16)
        sink_terms.append(jnp.exp(sink - m))
    ones = jnp.ones((2 * WINDOW, HEAD_DIM), BF16)
    for g in range(N_KV_HEADS):
        p = p_scr[g * group_rows:(g + 1) * group_rows, :]
        o_all = _dot(p, vv[:, g * HEAD_DIM:(g + 1) * HEAD_DIM])
        den_all = _dot(p, ones)
        for i in range(GQA_GROUP):
            h = g * GQA_GROUP + i
            rows = slice(i * WINDOW, (i + 1) * WINDOW)
            o = o_all[rows] / (den_all[rows] + sink_terms[h])
            o_ref[:, h * HEAD_DIM:(h + 1) * HEAD_DIM] = o.astype(BF16)


def _attn_prompt(sinks, q, k, v, bias):
    nb = SEQ // WINDOW
    own = lambda b, j: (b * nb + j, 0)
    prev = lambda b, j: (b * nb + jnp.maximum(j - 1, 0), 0)
    return pl.pallas_call(
        _attn_prompt_kernel,
        grid=(BATCH, nb),
        in_specs=[
            pl.BlockSpec(memory_space=pltpu.SMEM),
            pl.BlockSpec((WINDOW, BRANCH_W), own),
            pl.BlockSpec((WINDOW, KV_W), prev),
            pl.BlockSpec((WINDOW, KV_W), own),
            pl.BlockSpec((WINDOW, KV_W), prev),
            pl.BlockSpec((WINDOW, KV_W), own),
            pl.BlockSpec((None, N_HEADS, WINDOW, 2 * WINDOW), lambda b, j: (jnp.minimum(j, 1), 0, 0, 0)),
        ],
        out_specs=pl.BlockSpec((WINDOW, BRANCH_W), own),
        out_shape=jax.ShapeDtypeStruct((N_ROWS, BRANCH_W), BF16),
        scratch_shapes=[
            pltpu.VMEM((N_HEADS * WINDOW, 2 * WINDOW), F32),
            pltpu.VMEM((N_HEADS * WINDOW, 2 * WINDOW), BF16),
        ],
        compiler_params=_params(("arbitrary", "arbitrary")),
        name="attn_prompt",
    )(sinks, q, k, k, v, v, bias)


def _prompt_bias():
    slopes = jnp.exp2(-8.0 * jnp.arange(1, N_HEADS + 1, dtype=F32) / N_HEADS)
    qi = jnp.arange(WINDOW, dtype=jnp.int32)[:, None]
    ki = jnp.arange(2 * WINDOW, dtype=jnp.int32)[None, :] - WINDOW
    dist = qi - ki
    valid = (dist >= 0) & (dist < WINDOW)
    bias = -slopes[:, None, None] * dist.astype(F32)[None]
    with_prev = jnp.where(valid[None], bias, NEG_INF)
    first = jnp.where((valid & (ki >= 0))[None], bias, NEG_INF)
    return jnp.stack([first, with_prev])


SAMPLE_BT = 16


def _attn_sample_kernel(q_ref, kn_ref, vn_ref, ck_ref, cv_ref, bias_ref, sink_ref, mask_ref,
                        rep_ref, o_ref):
    bt = q_ref.shape[0]
    mask = mask_ref[...][None]
    sink = sink_ref[...][None]
    qe = _dot(q_ref[...].reshape(bt * N_HEADS, HEAD_DIM), rep_ref[...])
    qe = qe.reshape(bt, N_HEADS, KV_W) * mask
    qe_bf = qe.astype(BF16)
    s = jnp.stack([_dot_nt(qe_bf[b], ck_ref[b].astype(BF16)) for b in range(bt)])
    s = s + bias_ref[...][None]
    s_new = jnp.sum(qe * kn_ref[...], axis=-1, keepdims=True)
    m = jnp.maximum(jnp.maximum(jnp.max(s, axis=-1, keepdims=True), s_new), sink)
    p = jnp.exp(s - m)
    p_new = jnp.exp(s_new - m)
    den = jnp.sum(p, axis=-1, keepdims=True) + p_new + jnp.exp(sink - m)
    p_bf = p.astype(BF16)
    of = jnp.stack([_dot(p_bf[b], cv_ref[b].astype(BF16)) for b in range(bt)])
    of = (of + p_new * vn_ref[...]) * mask / den
    o = (of[..., 0:64] + of[..., 64:128]) + (of[..., 128:192] + of[..., 192:256])
    o_ref[...] = o.astype(BF16)


def _attn_sample(q3, k_new, v_new, ck, cv, l, bias, sinks_col, mask, rep):
    nsteps = DEC_BATCH // SAMPLE_BT
    b3 = lambda i: (i, 0, 0)
    c2 = lambda i: (0, 0)
    cache = pl.BlockSpec((None, SAMPLE_BT, WINDOW, KV_W), lambda i: (l, i, 0, 0))
    return pl.pallas_call(
        _attn_sample_kernel,
        grid=(nsteps,),
        in_specs=[
            pl.BlockSpec((SAMPLE_BT, N_HEADS, HEAD_DIM), b3),
            pl.BlockSpec((SAMPLE_BT, 1, KV_W), b3),
            pl.BlockSpec((SAMPLE_BT, 1, KV_W), b3),
            cache,
            cache,
            pl.BlockSpec((N_HEADS, WINDOW), c2),
            pl.BlockSpec((N_HEADS, 1), c2),
            pl.BlockSpec((N_HEADS, KV_W), c2),
            pl.BlockSpec((HEAD_DIM, KV_W), c2),
        ],
        out_specs=pl.BlockSpec((SAMPLE_BT, N_HEADS, HEAD_DIM), b3),
        out_shape=jax.ShapeDtypeStruct((DEC_BATCH, N_HEADS, HEAD_DIM), BF16),
        compiler_params=_params(("arbitrary",)),
        name="attn_sample",
    )(q3, k_new, v_new, ck, cv, bias, sinks_col, mask, rep)


def _sample_bias(n_buf):
    slopes = jnp.exp2(-8.0 * jnp.arange(1, N_HEADS + 1, dtype=F32) / N_HEADS)
    dist = n_buf - jnp.arange(n_buf, dtype=jnp.int32)
    bias = -slopes[:, None] * dist.astype(F32)[None, :]
    return jnp.where((dist < WINDOW)[None, :], bias, NEG_INF)


def _bch_cols(wa_ref, wb_ref, seg, c):
    r = seg * D_CONV + c * TN
    return wa_ref[:, r:r + TN] if r < WIDE else wb_ref[:, r - WIDE:r - WIDE + TN]


def _conv_prompt_kernel(xn_ref, wa_ref, wb_ref, cw_ref, ob_ref, nc_ref, zbuf, carry):
    t = pl.program_id(1)
    tm = xn_ref.shape[0]
    xn = xn_ref[...]

    @pl.when(t == 0)
    def _():
        zbuf[0:8, :] = jnp.zeros((8, D_CONV), F32)

    @pl.when(t > 0)
    def _():
        zbuf[0:8, :] = carry[...]

    for c in range(D_CONV // TN):
        cols = slice(c * TN, (c + 1) * TN)
        zc = _dot(xn, _bch_cols(wa_ref, wb_ref, 1, c)) * _dot(xn, _bch_cols(wa_ref, wb_ref, 2, c))
        zbuf[8:8 + tm, cols] = zc
        cw = cw_ref[:, cols]
        y = cw[0:1] * zbuf[6:6 + tm, cols] + cw[1:2] * zbuf[7:7 + tm, cols] + cw[2:3] * zc
        ob_ref[:, cols] = (_dot(xn, _bch_cols(wa_ref, wb_ref, 0, c)) * y).astype(BF16)
    carry[...] = zbuf[tm:tm + 8, :]
    nc_ref[...] = zbuf[tm + 6:tm + 8, :]


def _conv_prompt(xn_all, w_in, l, conv_w):
    nt = SEQ // TM_SEQ
    return pl.pallas_call(
        _conv_prompt_kernel,
        grid=(BATCH, nt),
        in_specs=[
            pl.BlockSpec((TM_SEQ, D_MODEL), lambda b, t: (b * nt + t, 0)),
            _layer_block((D_MODEL, WIDE), l, 0, 1),
            _layer_block((D_MODEL, WIDE), l, 0, 2),
            _layer_block((CONV_WIDTH, D_CONV), l, 0, 0),
        ],
        out_specs=[
            pl.BlockSpec((TM_SEQ, D_CONV), lambda b, t: (b * nt + t, 0)),
            pl.BlockSpec((None, CONV_WIDTH - 1, D_CONV), lambda b, t: (b, 0, 0)),
        ],
        out_shape=[
            jax.ShapeDtypeStruct((N_ROWS, D_CONV), BF16),
            jax.ShapeDtypeStruct((BATCH, CONV_WIDTH - 1, D_CONV), F32),
        ],
        scratch_shapes=[
            pltpu.VMEM((TM_SEQ + 8, D_CONV), F32),
            pltpu.VMEM((8, D_CONV), F32),
        ],
        compiler_params=_params(("arbitrary", "arbitrary")),
        name="conv_prompt",
    )(xn_all, w_in, w_in, conv_w)


def _conv_sample_kernel(xn_ref, wa_ref, wb_ref, cw_ref, cb0_ref, cb1_ref, ob_ref, zc_ref):
    xn = xn_ref[...]
    for c in range(D_CONV // TN):
        cols = slice(c * TN, (c + 1) * TN)
        zc = _dot(xn, _bch_cols(wa_ref, wb_ref, 1, c)) * _dot(xn, _bch_cols(wa_ref, wb_ref, 2, c))
        cw = cw_ref[:, cols]
        y = cw[0:1] * cb0_ref[:, cols] + cw[1:2] * cb1_ref[:, cols] + cw[2:3] * zc
        ob_ref[:, cols] = (_dot(xn, _bch_cols(wa_ref, wb_ref, 0, c)) * y).astype(BF16)
        zc_ref[:, cols] = zc


def _conv_sample(xn_all, w_in, l, conv_w, cb0, cb1):
    c2 = lambda i: (0, 0)
    return pl.pallas_call(
        _conv_sample_kernel,
        grid=(1,),
        in_specs=[
            pl.BlockSpec((DEC_BATCH, D_MODEL), lambda i: (SAMPLE_BLOCK, 0)),
            _layer_block((D_MODEL, WIDE), l, 0, 1),
            _layer_block((D_MODEL, WIDE), l, 0, 2),
            _layer_block((CONV_WIDTH, D_CONV), l, 0, 0),
            pl.BlockSpec((DEC_BATCH, D_CONV), c2),
            pl.BlockSpec((DEC_BATCH, D_CONV), c2),
        ],
        out_specs=[pl.BlockSpec((DEC_BATCH, D_CONV), c2), pl.BlockSpec((DEC_BATCH, D_CONV), c2)],
        out_shape=[
            jax.ShapeDtypeStruct((DEC_BATCH, D_CONV), BF16),
            jax.ShapeDtypeStruct((DEC_BATCH, D_CONV), F32),
        ],
        compiler_params=_params(("arbitrary",)),
        name="conv_sample",
    )(xn_all, w_in, w_in, conv_w, cb0, cb1)


HALF_V = WIDE - D_GMLP


def _gmlp_uv(xn, wa_ref, wb_ref, vn_ref):
    u = _gelu(_dot(xn, wa_ref[:, :D_GMLP]))
    v_lo = _gelu(_dot(xn, wa_ref[:, D_GMLP:]))
    v_hi = _gelu(_dot(xn, wb_ref[:, :D_GMLP - HALF_V]))
    ms = (jnp.sum(v_lo * v_lo, axis=-1, keepdims=True)
          + jnp.sum(v_hi * v_hi, axis=-1, keepdims=True)) * (1.0 / D_GMLP)
    r = lax.rsqrt(ms + EPS)
    return u, v_lo * r * vn_ref[:, :HALF_V], v_hi * r * vn_ref[:, HALF_V:]


def _gmlp_prompt_kernel(xn_ref, wa_ref, wb_ref, vn_ref, ws_ref, bs_ref, oc_ref, u_scr, v_scr):
    tm = xn_ref.shape[0]
    u, v_lo, v_hi = _gmlp_uv(xn_ref[...], wa_ref, wb_ref, vn_ref)
    u_scr[...] = u
    v_scr[:, :HALF_V] = v_lo.astype(BF16)
    v_scr[:, HALF_V:] = v_hi.astype(BF16)
    row = lax.broadcasted_iota(jnp.int32, (CHUNK, CHUNK), 0)
    col = lax.broadcasted_iota(jnp.int32, (CHUNK, CHUNK), 1)
    for grp in range(N_SPATIAL_GROUPS):
        w = jnp.where(col <= row, ws_ref[grp], 0.0).astype(BF16)
        cols = slice(grp * SPATIAL_GROUP_W, (grp + 1) * SPATIAL_GROUP_W)
        for ch in range(tm // CHUNK):
            rows = slice(ch * CHUNK, (ch + 1) * CHUNK)
            mix = _dot(w, v_scr[rows, cols]) + bs_ref[grp]
            oc_ref[rows, cols] = (u_scr[rows, cols] * mix).astype(BF16)


def _gmlp_prompt(xn_all, w_in, l, v_norm, w_s, b_s_wide):
    nt = N_PROMPT // TM_SEQ
    return pl.pallas_call(
        _gmlp_prompt_kernel,
        grid=(nt,),
        in_specs=[
            pl.BlockSpec((TM_SEQ, D_MODEL), lambda i: (i, 0)),
            _layer_block((D_MODEL, WIDE), l, 0, 3),
            _layer_block((D_MODEL, WIDE), l, 0, 4),
            _layer_block((1, D_GMLP), l, 0, 0),
            _layer_block((N_SPATIAL_GROUPS, CHUNK, CHUNK), l, 0, 0, 0),
            _layer_block((N_SPATIAL_GROUPS, CHUNK, SPATIAL_GROUP_W), l, 0, 0, 0),
        ],
        out_specs=pl.BlockSpec((TM_SEQ, D_GMLP), lambda i: (i, 0)),
        out_shape=jax.ShapeDtypeStruct((N_ROWS, D_GMLP), BF16),
        scratch_shapes=[pltpu.VMEM((TM_SEQ, D_GMLP), F32), pltpu.VMEM((TM_SEQ, D_GMLP), BF16)],
        compiler_params=_params(("arbitrary",)),
        name="gmlp_prompt",
    )(xn_all, w_in, w_in, v_norm, w_s, b_s_wide)


def _gmlp_sample_kernel(xn_ref, wa_ref, wb_ref, vn_ref, ws0_ref, bs0_ref, oc_ref, vg_ref):
    u, v_lo, v_hi = _gmlp_uv(xn_ref[...], wa_ref, wb_ref, vn_ref)
    vg_ref[:, :HALF_V] = v_lo
    vg_ref[:, HALF_V:] = v_hi
    oc_ref[...] = (u * (ws0_ref[...] * vg_ref[...] + bs0_ref[...])).astype(BF16)


def _gmlp_sample(xn_all, w_in, l, v_norm, ws0, bs0):
    c2 = lambda i: (0, 0)
    return pl.pallas_call(
        _gmlp_sample_kernel,
        grid=(1,),
        in_specs=[
            pl.BlockSpec((DEC_BATCH, D_MODEL), lambda i: (SAMPLE_BLOCK, 0)),
            _layer_block((D_MODEL, WIDE), l, 0, 3),
            _layer_block((D_MODEL, WIDE), l, 0, 4),
            _layer_block((1, D_GMLP), l, 0, 0),
            pl.BlockSpec((1, D_GMLP), c2),
            pl.BlockSpec((1, D_GMLP), c2),
        ],
        out_specs=[pl.BlockSpec((DEC_BATCH, D_GMLP), c2), pl.BlockSpec((DEC_BATCH, D_GMLP), c2)],
        out_shape=[
            jax.ShapeDtypeStruct((DEC_BATCH, D_GMLP), BF16),
            jax.ShapeDtypeStruct((DEC_BATCH, D_GMLP), F32),
        ],
        compiler_params=_params(("arbitrary",)),
        name="gmlp_sample",
    )(xn_all, w_in, w_in, v_norm, ws0, bs0)


def _merge_kernel(xn_ref, wg0_ref, wg1_ref, wg2_ref, bg_ref, oa_ref, ob_ref, oc_ref, wb_ref, m_ref):
    xn = xn_ref[...]
    bg = bg_ref[...]
    acc = None
    for i, (wg_ref, o_ref) in enumerate(((wg0_ref, oa_ref), (wg1_ref, ob_ref), (wg2_ref, oc_ref))):
        gate = jax.nn.sigmoid(_dot(xn, wg_ref[...]) + bg[i:i + 1])
        term = gate * _dot(o_ref[...], wb_ref[i])
        acc = term if acc is None else acc + term
    m_ref[...] = acc.astype(BF16)


def _merge(xn_all, w_in, l, b_gate, o_a, o_b, o_c, w_branch):
    nt = N_ROWS // TM_ROW
    nn = D_MODEL // TN
    g0 = OFF_G // TN
    rows = lambda i, j: (i, 0)
    gate_w = lambda br: pl.BlockSpec((None, D_MODEL, TN), lambda i, j: (l, 0, g0 + br * nn + j))
    return pl.pallas_call(
        _merge_kernel,
        grid=(nt, nn),
        in_specs=[
            pl.BlockSpec((TM_ROW, D_MODEL), rows),
            gate_w(0), gate_w(1), gate_w(2),
            pl.BlockSpec((None, N_BRANCHES, TN), lambda i, j: (l, 0, j)),
            pl.BlockSpec((TM_ROW, BRANCH_W), rows),
            pl.BlockSpec((TM_ROW, BRANCH_W), rows),
            pl.BlockSpec((TM_ROW, BRANCH_W), rows),
            pl.BlockSpec((None, N_BRANCHES, BRANCH_W, TN), lambda i, j: (l, 0, 0, j)),
        ],
        out_specs=pl.BlockSpec((TM_ROW, TN), lambda i, j: (i, j)),
        out_shape=jax.ShapeDtypeStruct((N_ROWS, D_MODEL), BF16),
        compiler_params=_params(("arbitrary", "arbitrary")),
        name="merge",
    )(xn_all, w_in, w_in, w_in, b_gate, o_a, o_b, o_c, w_branch)


OUT_SPLIT = 2


def _out_proj_kernel(m_ref, w_ref, x_ref, g_ref, x1_ref, xn_ref):
    sub = m_ref.shape[0] // OUT_SPLIT
    for s in range(OUT_SPLIT):
        rows = slice(s * sub, (s + 1) * sub)
        x1 = x_ref[rows, :] + _dot(m_ref[rows, :], w_ref[...])
        x1_ref[rows, :] = x1
        xn_ref[rows, :] = _rms_rows(x1, g_ref[...]).astype(BF16)


def _out_proj(m, w_out, l, x, g_ffn, *, m_row_block0, tm, out_rows):
    rows = lambda i: (i, 0)
    return pl.pallas_call(
        _out_proj_kernel,
        grid=(x.shape[0] // tm,),
        in_specs=[
            pl.BlockSpec((tm, D_MODEL), lambda i: (m_row_block0 + i, 0)),
            _layer_block((D_MODEL, D_MODEL), l, 0, 0),
            pl.BlockSpec((tm, D_MODEL), rows),
            _layer_block((1, D_MODEL), l, 0, 0),
        ],
        out_specs=[pl.BlockSpec((tm, D_MODEL), rows), pl.BlockSpec((tm, D_MODEL), rows)],
        out_shape=[
            jax.ShapeDtypeStruct((out_rows, D_MODEL), F32),
            jax.ShapeDtypeStruct((out_rows, D_MODEL), BF16),
        ],
        compiler_params=_params(("arbitrary",)),
        name="out_proj",
    )(m, w_out, x, g_ffn)


def _ffn_up_kernel(xn_ref, wgate_ref, wup_ref, h_ref, wgate_scr, wup_scr):
    @pl.when(pl.program_id(1) == 0)
    def _():
        wgate_scr[...] = wgate_ref[...].astype(BF16)
        wup_scr[...] = wup_ref[...].astype(BF16)

    sub = xn_ref.shape[0] // OUT_SPLIT
    for s in range(OUT_SPLIT):
        rows = slice(s * sub, (s + 1) * sub)
        xn = xn_ref[rows, :]
        gate = _dot(xn, wgate_scr[...])
        h_ref[rows, :] = (gate * jax.nn.sigmoid(gate) * _dot(xn, wup_scr[...])).astype(BF16)


def _ffn_up(xn, w_gate_up, l):
    nt = N_ROWS // TM_ROW
    nn = D_FF // TN
    return pl.pallas_call(
        _ffn_up_kernel,
        grid=(nn, nt),
        in_specs=[
            pl.BlockSpec((TM_ROW, D_MODEL), lambda j, i: (i, 0)),
            pl.BlockSpec((None, D_MODEL, TN), lambda j, i: (l, 0, j)),
            pl.BlockSpec((None, D_MODEL, TN), lambda j, i: (l, 0, nn + j)),
        ],
        out_specs=pl.BlockSpec((TM_ROW, TN), lambda j, i: (i, j)),
        out_shape=jax.ShapeDtypeStruct((N_ROWS, D_FF), BF16),
        scratch_shapes=[pltpu.VMEM((D_MODEL, TN), BF16), pltpu.VMEM((D_MODEL, TN), BF16)],
        compiler_params=_params(("arbitrary", "arbitrary")),
        name="ffn_up",
    )(xn, w_gate_up, w_gate_up)


def _ffn_down_kernel(h_ref, w_ref, x_ref, o_ref, w_scr):
    @pl.when(pl.program_id(1) == 0)
    def _():
        w_scr[...] = w_ref[...].astype(BF16)

    o_ref[...] = x_ref[...] + _dot(h_ref[...], w_scr[...])


def _ffn_down(h, w_down, l, x1, *, row_block0, n_rows, tm):
    nt = n_rows // tm
    nn = D_MODEL // TN
    return pl.pallas_call(
        _ffn_down_kernel,
        grid=(nn, nt),
        in_specs=[
            pl.BlockSpec((tm, D_FF), lambda j, i: (row_block0 + i, 0)),
            pl.BlockSpec((None, D_FF, TN), lambda j, i: (l, 0, j)),
            pl.BlockSpec((tm, TN), lambda j, i: (row_block0 + i, j)),
        ],
        out_specs=pl.BlockSpec((tm, TN), lambda j, i: (i, j)),
        out_shape=jax.ShapeDtypeStruct((n_rows, D_MODEL), F32),
        scratch_shapes=[pltpu.VMEM((D_FF, TN), BF16)],
        compiler_params=_params(("arbitrary", "arbitrary")),
        name="ffn_down",
    )(h, w_down, x1)


def _block_diag_ones(width):
    head = np.arange(width) // HEAD_DIM
    return jnp.asarray(head[:, None] == head[None, :], dtype=BF16)


def _fill_sample_rows(o_prompt, o_sample):
    return lax.dynamic_update_slice(o_prompt, o_sample, (N_PROMPT, 0))


def _layer(x, l, cache_k, cache_v, state_conv, p, const):
    first, last = l == 0, l == DEPTH - 1
    if first:
        x_p, x_s = x
        xn = _fill_sample_rows(_norm(x_p, p["norm_mix"], l, tm=TM_SEQ, out_rows=N_ROWS),
                               _norm(x_s, p["norm_mix"], l, tm=DEC_BATCH, out_rows=DEC_BATCH))
    else:
        xn = _norm(x, p["norm_mix"], l, tm=TM_ROW, out_rows=N_ROWS)
    w_in = p["w_in"]
    qn = jnp.tile(p["q_norm"][l], N_HEADS).reshape(1, BRANCH_W)
    kn = jnp.tile(p["k_norm"][l], N_KV_HEADS).reshape(1, KV_W)
    sinks = p["sinks"][l]

    q_p, k_p, v_p, klast_p, vlast_p = _qkv(
        xn, w_in, l, qn, kn, const["pq"], const["pk"],
        row_block0=0, n_rows=N_PROMPT, tm=TM_SEQ, tiles_per_seq=SEQ // TM_SEQ)
    q_s, k_s, v_s = _qkv(
        xn, w_in, l, qn, kn, const["pq"], const["pk"],
        row_block0=SAMPLE_BLOCK, n_rows=DEC_BATCH, tm=DEC_BATCH)
    oa_p = _attn_prompt(sinks, q_p, k_p, v_p, const["bias_p"])
    oa_s = _attn_sample(
        q_s.reshape(DEC_BATCH, N_HEADS, HEAD_DIM),
        k_s.reshape(DEC_BATCH, 1, KV_W), v_s.reshape(DEC_BATCH, 1, KV_W),
        cache_k, cache_v, l,
        const["bias_s"], sinks.reshape(N_HEADS, 1), const["mask_s"], const["rep_s"],
    ).reshape(DEC_BATCH, BRANCH_W)

    ob_p, nc_p = _conv_prompt(xn, w_in, l, p["conv_w"])
    ob_s, zc_s = _conv_sample(xn, w_in, l, p["conv_w"], state_conv[l, :, 0], state_conv[l, :, 1])

    w_s = p["w_spatial"]
    b_s = p["b_spatial"]
    bs_wide = jnp.broadcast_to(b_s[:, :, :, None], (DEPTH, N_SPATIAL_GROUPS, CHUNK, SPATIAL_GROUP_W))
    oc_p = _gmlp_prompt(xn, w_in, l, p["v_norm"], w_s, bs_wide)
    ws0 = jnp.repeat(w_s[l, :, 0, 0], SPATIAL_GROUP_W).reshape(1, D_GMLP)
    bs0 = jnp.repeat(b_s[l, :, 0], SPATIAL_GROUP_W).reshape(1, D_GMLP)
    oc_s, vg_s = _gmlp_sample(xn, w_in, l, p["v_norm"], ws0, bs0)

    o_a = _fill_sample_rows(oa_p, oa_s)
    o_b = _fill_sample_rows(ob_p, ob_s)
    o_c = _fill_sample_rows(oc_p, oc_s)
    m = _merge(xn, w_in, l, p["b_gate"], o_a, o_b, o_c, p["w_branch"])
    if first:
        x1_p, xn2_p = _out_proj(m, p["w_out"], l, x_p, p["norm_ffn"], m_row_block0=0, tm=TM_SEQ, out_rows=N_ROWS)
        x1_s, xn2_s = _out_proj(m, p["w_out"], l, x_s, p["norm_ffn"], m_row_block0=SAMPLE_BLOCK,
                                tm=DEC_BATCH, out_rows=DEC_BATCH)
        x1, xn2 = _fill_sample_rows(x1_p, x1_s), _fill_sample_rows(xn2_p, xn2_s)
    else:
        x1, xn2 = _out_proj(m, p["w_out"], l, x, p["norm_ffn"], m_row_block0=0, tm=TM_ROW, out_rows=N_ROWS)
    h = _ffn_up(xn2, p["w_gate_up"], l)
    if last:
        x2 = (_ffn_down(h, p["w_down"], l, x1, row_block0=0, n_rows=N_PROMPT, tm=TM_SEQ),
              _ffn_down(h, p["w_down"], l, x1, row_block0=SAMPLE_BLOCK, n_rows=DEC_BATCH, tm=DEC_BATCH))
    else:
        x2 = _ffn_down(h, p["w_down"], l, x1, row_block0=0, n_rows=N_ROWS, tm=TM_ROW)

    new_k_p = klast_p.reshape(BATCH, WINDOW, N_KV_HEADS, HEAD_DIM)
    new_v_p = vlast_p.reshape(BATCH, WINDOW, N_KV_HEADS, HEAD_DIM)
    new_k_s = k_s.reshape(DEC_BATCH, 1, N_KV_HEADS, HEAD_DIM)
    new_v_s = v_s.reshape(DEC_BATCH, 1, N_KV_HEADS, HEAD_DIM)
    new_conv_s = jnp.stack([state_conv[l, :, 1], zc_s], axis=1)
    return x2, (new_k_p, new_v_p, nc_p, new_k_s, new_v_s, new_conv_s, vg_s.reshape(DEC_BATCH, 1, D_GMLP))


def kernel(x_prompt, x_sample, cache_k, cache_v, state_conv, norm_mix, w_in, b_gate, q_norm, k_norm,
           sinks, conv_w, v_norm, w_spatial, b_spatial, w_branch, w_out, norm_ffn, w_gate_up, w_down):
    assert min(WINDOW, SEQ) == WINDOW and SEQ % TM_SEQ == 0 and TM_SEQ >= WINDOW
    n_buf = cache_k.shape[2]
    assert n_buf == WINDOW
    cache_k = cache_k.reshape(DEPTH, DEC_BATCH, n_buf, KV_W)
    cache_v = cache_v.reshape(DEPTH, DEC_BATCH, n_buf, KV_W)
    p = {
        "norm_mix": norm_mix.reshape(DEPTH, 1, D_MODEL),
        "norm_ffn": norm_ffn.reshape(DEPTH, 1, D_MODEL),
        "v_norm": v_norm.reshape(DEPTH, 1, D_GMLP),
        "w_in": w_in.astype(BF16),
        "w_branch": w_branch.astype(BF16),
        "w_out": w_out.astype(BF16),
        "w_gate_up": w_gate_up,
        "w_down": w_down,
        "b_gate": b_gate, "q_norm": q_norm, "k_norm": k_norm, "sinks": sinks, "conv_w": conv_w,
        "w_spatial": w_spatial, "b_spatial": b_spatial,
    }
    kv_of_col = np.arange(KV_W) // HEAD_DIM
    kv_of_head = np.arange(N_HEADS) // GQA_GROUP
    const = {
        "pq": _block_diag_ones(BRANCH_W),
        "pk": _block_diag_ones(KV_W),
        "bias_p": _prompt_bias(),
        "bias_s": _sample_bias(n_buf),
        "mask_s": jnp.asarray(kv_of_head[:, None] == kv_of_col[None, :], dtype=F32),
        "rep_s": jnp.asarray(np.tile(np.eye(HEAD_DIM), (1, N_KV_HEADS)), dtype=BF16),
    }
    x = (x_prompt.reshape(N_PROMPT, D_MODEL), x_sample.reshape(DEC_BATCH, D_MODEL))
    per_layer = []
    for l in range(DEPTH):
        x, outs = _layer(x, l, cache_k, cache_v, state_conv, p, const)
        per_layer.append(outs)
    stacked = [jnp.stack([per_layer[l][i] for l in range(DEPTH)]) for i in range(7)]
    y_prompt, y_sample = x
    return (y_prompt.reshape(BATCH, SEQ, D_MODEL), y_sample.reshape(DEC_BATCH, 1, D_MODEL), *stacked)
```

```python
import jax
import jax.numpy as jnp
import numpy as np
from jax import lax
from jax.experimental import pallas as pl
from jax.experimental.pallas import tpu as pltpu

D_MODEL = 2048
BATCH = 4
SEQ = 2048
DEPTH = 2
DEC_BATCH = 128
BRANCH_W = 1024
HEAD_DIM = 64
N_HEADS = 16
N_KV_HEADS = 4
GQA_GROUP = 4
KV_W = 256
WINDOW = 128
D_CONV = 1024
CONV_WIDTH = 3
D_GMLP = 1024
CHUNK = 128
N_SPATIAL_GROUPS = 8
SPATIAL_GROUP_W = 128
N_BRANCHES = 3
D_FF = 5632
EPS = 1e-6
NEG_INF = -1e30

N_PROMPT = BATCH * SEQ
N_ROWS = N_PROMPT + DEC_BATCH
SAMPLE_BLOCK = N_PROMPT // DEC_BATCH

WIDE = 1536
OFF_G = 6656

TM_SEQ = 512
TM_ROW = 640
TN = 512
VMEM_LIMIT = 56 * 1024 * 1024

F32 = jnp.float32
BF16 = jnp.bfloat16


def _params(sem):
    return pltpu.CompilerParams(dimension_semantics=sem, vmem_limit_bytes=VMEM_LIMIT)


def _rms_rows(x, g):
    ms = jnp.mean(x * x, axis=-1, keepdims=True)
    return x * lax.rsqrt(ms + EPS) * g


def _dot(a, b):
    return jnp.dot(a, b, preferred_element_type=F32)


def _dot_nt(a, b):
    return lax.dot_general(a, b, (((1,), (1,)), ((), ())), preferred_element_type=F32)


def _gelu(x):
    return 0.5 * x * (1.0 + jnp.tanh(np.sqrt(2.0 / np.pi).astype(np.float32) * (x + 0.044715 * (x * x * x))))


def _layer_block(shape, l, *idx):
    return pl.BlockSpec((None, *shape), lambda *_: (l, *idx))


def _norm_kernel(x_ref, g_ref, o_ref):
    o_ref[...] = _rms_rows(x_ref[...], g_ref[...]).astype(BF16)


def _norm(x, g, l, *, tm, out_rows):
    rows = lambda i: (i, 0)
    return pl.pallas_call(
        _norm_kernel,
        grid=(x.shape[0] // tm,),
        in_specs=[pl.BlockSpec((tm, D_MODEL), rows), _layer_block((1, D_MODEL), l, 0, 0)],
        out_specs=pl.BlockSpec((tm, D_MODEL), rows),
        out_shape=jax.ShapeDtypeStruct((out_rows, D_MODEL), BF16),
        compiler_params=_params(("arbitrary",)),
        name="norm",
    )(x, g)


def _qkv_kernel(xn_ref, w_ref, qn_ref, kn_ref, pq_ref, pk_ref, q_ref, k_ref, v_ref, *last_refs):
    z = _dot(xn_ref[...], w_ref[...])
    q = z[:, :BRANCH_W]
    k = z[:, BRANCH_W:BRANCH_W + KV_W]
    v = z[:, BRANCH_W + KV_W:]
    q_ms = _dot((q * q).astype(BF16), pq_ref[...]) * (1.0 / HEAD_DIM)
    k_ms = _dot((k * k).astype(BF16), pk_ref[...]) * (1.0 / HEAD_DIM)
    q_ref[...] = (q * lax.rsqrt(q_ms + EPS) * qn_ref[...] * (HEAD_DIM ** -0.5)).astype(BF16)
    kn = k * lax.rsqrt(k_ms + EPS) * kn_ref[...]
    k_ref[...] = kn
    v_ref[...] = v
    if last_refs:
        klast_ref, vlast_ref = last_refs
        tm = xn_ref.shape[0]
        klast_ref[...] = kn[tm - WINDOW:]
        vlast_ref[...] = v[tm - WINDOW:]


def _qkv(xn_all, w_in, l, qn, kn, pq, pk, *, row_block0, n_rows, tm, tiles_per_seq=None):
    nt = n_rows // tm
    const = lambda i: (0, 0)
    rows = lambda i: (i, 0)
    out_specs = [
        pl.BlockSpec((tm, BRANCH_W), rows),
        pl.BlockSpec((tm, KV_W), rows),
        pl.BlockSpec((tm, KV_W), rows),
    ]
    out_shape = [
        jax.ShapeDtypeStruct((n_rows, BRANCH_W), BF16),
        jax.ShapeDtypeStruct((n_rows, KV_W), F32),
        jax.ShapeDtypeStruct((n_rows, KV_W), F32),
    ]
    if tiles_per_seq is not None:
        n_seq = nt // tiles_per_seq
        out_specs += [pl.BlockSpec((WINDOW, KV_W), lambda i: (i // tiles_per_seq, 0))] * 2
        out_shape += [jax.ShapeDtypeStruct((n_seq * WINDOW, KV_W), F32)] * 2
    return pl.pallas_call(
        _qkv_kernel,
        grid=(nt,),
        in_specs=[
            pl.BlockSpec((tm, D_MODEL), lambda i: (row_block0 + i, 0)),
            _layer_block((D_MODEL, WIDE), l, 0, 0),
            pl.BlockSpec((1, BRANCH_W), const),
            pl.BlockSpec((1, KV_W), const),
            pl.BlockSpec((BRANCH_W, BRANCH_W), const),
            pl.BlockSpec((KV_W, KV_W), const),
        ],
        out_specs=out_specs,
        out_shape=out_shape,
        compiler_params=_params(("arbitrary",)),
        name="qkv",
    )(xn_all, w_in, qn, kn, pq, pk)


def _attn_prompt_kernel(sink_ref, q_ref, kp_ref, ko_ref, vp_ref, vo_ref, bias_ref, o_ref, s_scr, p_scr):
    kk = jnp.concatenate([kp_ref[...], ko_ref[...]], axis=0).astype(BF16)
    vv = jnp.concatenate([vp_ref[...], vo_ref[...]], axis=0).astype(BF16)
    group_rows = GQA_GROUP * WINDOW
    for g in range(N_KV_HEADS):
        kg = kk[:, g * HEAD_DIM:(g + 1) * HEAD_DIM]
        heads = range(g * GQA_GROUP, (g + 1) * GQA_GROUP)
        qg = jnp.concatenate([q_ref[:, h * HEAD_DIM:(h + 1) * HEAD_DIM] for h in heads], axis=0)
        s_scr[g * group_rows:(g + 1) * group_rows, :] = _dot_nt(qg, kg)
    sink_terms = []
    for h in range(N_HEADS):
        rows = slice(h * WINDOW, (h + 1) * WINDOW)
        s = s_scr[rows, :] + bias_ref[h]
        sink = sink_ref[h]
        m = jnp.maximum(jnp.max(s, axis=-1, keepdims=True), sink)
        p_scr[rows, :] = jnp.exp(s - m).astype(BF16)
        sink_terms.append(jnp.exp(sink - m))
    ones = jnp.ones((2 * WINDOW, HEAD_DIM), BF16)
    for g in range(N_KV_HEADS):
        p = p_scr[g * group_rows:(g + 1) * group_rows, :]
        o_all = _dot(p, vv[:, g * HEAD_DIM:(g + 1) * HEAD_DIM])
        den_all = _dot(p, ones)
        for i in range(GQA_GROUP):
            h = g * GQA_GROUP + i
            rows = slice(i * WINDOW, (i + 1) * WINDOW)
            o = o_all[rows] / (den_all[rows] + sink_terms[h])
            o_ref[:, h * HEAD_DIM:(h + 1) * HEAD_DIM] = o.astype(BF16)


def _attn_prompt(sinks, q, k, v, bias):
    nb = SEQ // WINDOW
    own = lambda b, j: (b * nb + j, 0)
    prev = lambda b, j: (b * nb + jnp.maximum(j - 1, 0), 0)
    return pl.pallas_call(
        _attn_prompt_kernel,
        grid=(BATCH, nb),
        in_specs=[
            pl.BlockSpec(memory_space=pltpu.SMEM),
            pl.BlockSpec((WINDOW, BRANCH_W), own),
            pl.BlockSpec((WINDOW, KV_W), prev),
            pl.BlockSpec((WINDOW, KV_W), own),
            pl.BlockSpec((WINDOW, KV_W), prev),
            pl.BlockSpec((WINDOW, KV_W), own),
            pl.BlockSpec((None, N_HEADS, WINDOW, 2 * WINDOW), lambda b, j: (jnp.minimum(j, 1), 0, 0, 0)),
        ],
        out_specs=pl.BlockSpec((WINDOW, BRANCH_W), own),
        out_shape=jax.ShapeDtypeStruct((N_ROWS, BRANCH_W), BF16),
        scratch_shapes=[
            pltpu.VMEM((N_HEADS * WINDOW, 2 * WINDOW), F32),
            pltpu.VMEM((N_HEADS * WINDOW, 2 * WINDOW), BF16),
        ],
        compiler_params=_params(("arbitrary", "arbitrary")),
        name="attn_prompt",
    )(sinks, q, k, k, v, v, bias)


def _prompt_bias():
    slopes = jnp.exp2(-8.0 * jnp.arange(1, N_HEADS + 1, dtype=F32) / N_HEADS)
    qi = jnp.arange(WINDOW, dtype=jnp.int32)[:, None]
    ki = jnp.arange(2 * WINDOW, dtype=jnp.int32)[None, :] - WINDOW
    dist = qi - ki
    valid = (dist >= 0) & (dist < WINDOW)
    bias = -slopes[:, None, None] * dist.astype(F32)[None]
    with_prev = jnp.where(valid[None], bias, NEG_INF)
    first = jnp.where((valid & (ki >= 0))[None], bias, NEG_INF)
    return jnp.stack([first, with_prev])


SAMPLE_BT = 16


def _attn_sample_kernel(q_ref, kn_ref, vn_ref, ck_ref, cv_ref, bias_ref, sink_ref, mask_ref,
                        rep_ref, o_ref):
    bt = q_ref.shape[0]
    mask = mask_ref[...][None]
    sink = sink_ref[...][None]
    qe = _dot(q_ref[...].reshape(bt * N_HEADS, HEAD_DIM), rep_ref[...])
    qe = qe.reshape(bt, N_HEADS, KV_W) * mask
    qe_bf = qe.astype(BF16)
    s = jnp.stack([_dot_nt(qe_bf[b], ck_ref[b].astype(BF16)) for b in range(bt)])
    s = s + bias_ref[...][None]
    s_new = jnp.sum(qe * kn_ref[...], axis=-1, keepdims=True)
    m = jnp.maximum(jnp.maximum(jnp.max(s, axis=-1, keepdims=True), s_new), sink)
    p = jnp.exp(s - m)
    p_new = jnp.exp(s_new - m)
    den = jnp.sum(p, axis=-1, keepdims=True) + p_new + jnp.exp(sink - m)
    p_bf = p.astype(BF16)
    of = jnp.stack([_dot(p_bf[b], cv_ref[b].astype(BF16)) for b in range(bt)])
    of = (of + p_new * vn_ref[...]) * mask / den
    o = (of[..., 0:64] + of[..., 64:128]) + (of[..., 128:192] + of[..., 192:256])
    o_ref[...] = o.astype(BF16)


def _attn_sample(q3, k_new, v_new, ck, cv, l, bias, sinks_col, mask, rep):
    nsteps = DEC_BATCH // SAMPLE_BT
    b3 = lambda i: (i, 0, 0)
    c2 = lambda i: (0, 0)
    cache = pl.BlockSpec((None, SAMPLE_BT, WINDOW, KV_W), lambda i: (l, i, 0, 0))
    return pl.pallas_call(
        _attn_sample_kernel,
        grid=(nsteps,),
        in_specs=[
            pl.BlockSpec((SAMPLE_BT, N_HEADS, HEAD_DIM), b3),
            pl.BlockSpec((SAMPLE_BT, 1, KV_W), b3),
            pl.BlockSpec((SAMPLE_BT, 1, KV_W), b3),
            cache,
            cache,
            pl.BlockSpec((N_HEADS, WINDOW), c2),
            pl.BlockSpec((N_HEADS, 1), c2),
            pl.BlockSpec((N_HEADS, KV_W), c2),
            pl.BlockSpec((HEAD_DIM, KV_W), c2),
        ],
        out_specs=pl.BlockSpec((SAMPLE_BT, N_HEADS, HEAD_DIM), b3),
        out_shape=jax.ShapeDtypeStruct((DEC_BATCH, N_HEADS, HEAD_DIM), BF16),
        compiler_params=_params(("arbitrary",)),
        name="attn_sample",
    )(q3, k_new, v_new, ck, cv, bias, sinks_col, mask, rep)


def _sample_bias(n_buf):
    slopes = jnp.exp2(-8.0 * jnp.arange(1, N_HEADS + 1, dtype=F32) / N_HEADS)
    dist = n_buf - jnp.arange(n_buf, dtype=jnp.int32)
    bias = -slopes[:, None] * dist.astype(F32)[None, :]
    return jnp.where((dist < WINDOW)[None, :], bias, NEG_INF)


def _bch_cols(wa_ref, wb_ref, seg, c):
    r = seg * D_CONV + c * TN
    return wa_ref[:, r:r + TN] if r < WIDE else wb_ref[:, r - WIDE:r - WIDE + TN]


def _conv_prompt_kernel(xn_ref, wa_ref, wb_ref, cw_ref, ob_ref, nc_ref, zbuf, carry):
    t = pl.program_id(1)
    tm = xn_ref.shape[0]
    xn = xn_ref[...]

    @pl.when(t == 0)
    def _():
        zbuf[0:8, :] = jnp.zeros((8, D_CONV), F32)

    @pl.when(t > 0)
    def _():
        zbuf[0:8, :] = carry[...]

    for c in range(D_CONV // TN):
        cols = slice(c * TN, (c + 1) * TN)
        zc = _dot(xn, _bch_cols(wa_ref, wb_ref, 1, c)) * _dot(xn, _bch_cols(wa_ref, wb_ref, 2, c))
        zbuf[8:8 + tm, cols] = zc
        cw = cw_ref[:, cols]
        y = cw[0:1] * zbuf[6:6 + tm, cols] + cw[1:2] * zbuf[7:7 + tm, cols] + cw[2:3] * zc
        ob_ref[:, cols] = (_dot(xn, _bch_cols(wa_ref, wb_ref, 0, c)) * y).astype(BF16)
    carry[...] = zbuf[tm:tm + 8, :]
    nc_ref[...] = zbuf[tm + 6:tm + 8, :]


def _conv_prompt(xn_all, w_in, l, conv_w):
    nt = SEQ // TM_SEQ
    return pl.pallas_call(
        _conv_prompt_kernel,
        grid=(BATCH, nt),
        in_specs=[
            pl.BlockSpec((TM_SEQ, D_MODEL), lambda b, t: (b * nt + t, 0)),
            _layer_block((D_MODEL, WIDE), l, 0, 1),
            _layer_block((D_MODEL, WIDE), l, 0, 2),
            _layer_block((CONV_WIDTH, D_CONV), l, 0, 0),
        ],
        out_specs=[
            pl.BlockSpec((TM_SEQ, D_CONV), lambda b, t: (b * nt + t, 0)),
            pl.BlockSpec((None, CONV_WIDTH - 1, D_CONV), lambda b, t: (b, 0, 0)),
        ],
        out_shape=[
            jax.ShapeDtypeStruct((N_ROWS, D_CONV), BF16),
            jax.ShapeDtypeStruct((BATCH, CONV_WIDTH - 1, D_CONV), F32),
        ],
        scratch_shapes=[
            pltpu.VMEM((TM_SEQ + 8, D_CONV), F32),
            pltpu.VMEM((8, D_CONV), F32),
        ],
        compiler_params=_params(("arbitrary", "arbitrary")),
        name="conv_prompt",
    )(xn_all, w_in, w_in, conv_w)


def _conv_sample_kernel(xn_ref, wa_ref, wb_ref, cw_ref, cb0_ref, cb1_ref, ob_ref, zc_ref):
    xn = xn_ref[...]
    for c in range(D_CONV // TN):
        cols = slice(c * TN, (c + 1) * TN)
        zc = _dot(xn, _bch_cols(wa_ref, wb_ref, 1, c)) * _dot(xn, _bch_cols(wa_ref, wb_ref, 2, c))
        cw = cw_ref[:, cols]
        y = cw[0:1] * cb0_ref[:, cols] + cw[1:2] * cb1_ref[:, cols] + cw[2:3] * zc
        ob_ref[:, cols] = (_dot(xn, _bch_cols(wa_ref, wb_ref, 0, c)) * y).astype(BF16)
        zc_ref[:, cols] = zc


def _conv_sample(xn_all, w_in, l, conv_w, cb0, cb1):
    c2 = lambda i: (0, 0)
    return pl.pallas_call(
        _conv_sample_kernel,
        grid=(1,),
        in_specs=[
            pl.BlockSpec((DEC_BATCH, D_MODEL), lambda i: (SAMPLE_BLOCK, 0)),
            _layer_block((D_MODEL, WIDE), l, 0, 1),
            _layer_block((D_MODEL, WIDE), l, 0, 2),
            _layer_block((CONV_WIDTH, D_CONV), l, 0, 0),
            pl.BlockSpec((DEC_BATCH, D_CONV), c2),
            pl.BlockSpec((DEC_BATCH, D_CONV), c2),
        ],
        out_specs=[pl.BlockSpec((DEC_BATCH, D_CONV), c2), pl.BlockSpec((DEC_BATCH, D_CONV), c2)],
        out_shape=[
            jax.ShapeDtypeStruct((DEC_BATCH, D_CONV), BF16),
            jax.ShapeDtypeStruct((DEC_BATCH, D_CONV), F32),
        ],
        compiler_params=_params(("arbitrary",)),
        name="conv_sample",
    )(xn_all, w_in, w_in, conv_w, cb0, cb1)


HALF_V = WIDE - D_GMLP


def _gmlp_uv(xn, wa_ref, wb_ref, vn_ref):
    u = _gelu(_dot(xn, wa_ref[:, :D_GMLP]))
    v_lo = _gelu(_dot(xn, wa_ref[:, D_GMLP:]))
    v_hi = _gelu(_dot(xn, wb_ref[:, :D_GMLP - HALF_V]))
    ms = (jnp.sum(v_lo * v_lo, axis=-1, keepdims=True)
          + jnp.sum(v_hi * v_hi, axis=-1, keepdims=True)) * (1.0 / D_GMLP)
    r = lax.rsqrt(ms + EPS)
    return u, v_lo * r * vn_ref[:, :HALF_V], v_hi * r * vn_ref[:, HALF_V:]


def _gmlp_prompt_kernel(xn_ref, wa_ref, wb_ref, vn_ref, ws_ref, bs_ref, oc_ref, u_scr, v_scr):
    tm = xn_ref.shape[0]
    u, v_lo, v_hi = _gmlp_uv(xn_ref[...], wa_ref, wb_ref, vn_ref)
    u_scr[...] = u
    v_scr[:, :HALF_V] = v_lo.astype(BF16)
    v_scr[:, HALF_V:] = v_hi.astype(BF16)
    row = lax.broadcasted_iota(jnp.int32, (CHUNK, CHUNK), 0)
    col = lax.broadcasted_iota(jnp.int32, (CHUNK, CHUNK), 1)
    for grp in range(N_SPATIAL_GROUPS):
        w = jnp.where(col <= row, ws_ref[grp], 0.0).astype(BF16)
        cols = slice(grp * SPATIAL_GROUP_W, (grp + 1) * SPATIAL_GROUP_W)
        for ch in range(tm // CHUNK):
            rows = slice(ch * CHUNK, (ch + 1) * CHUNK)
            mix = _dot(w, v_scr[rows, cols]) + bs_ref[grp]
            oc_ref[rows, cols] = (u_scr[rows, cols] * mix).astype(BF16)


def _gmlp_prompt(xn_all, w_in, l, v_norm, w_s, b_s_wide):
    nt = N_PROMPT // TM_SEQ
    return pl.pallas_call(
        _gmlp_prompt_kernel,
        grid=(nt,),
        in_specs=[
            pl.BlockSpec((TM_SEQ, D_MODEL), lambda i: (i, 0)),
            _layer_block((D_MODEL, WIDE), l, 0, 3),
            _layer_block((D_MODEL, WIDE), l, 0, 4),
            _layer_block((1, D_GMLP), l, 0, 0),
            _layer_block((N_SPATIAL_GROUPS, CHUNK, CHUNK), l, 0, 0, 0),
            _layer_block((N_SPATIAL_GROUPS, CHUNK, SPATIAL_GROUP_W), l, 0, 0, 0),
        ],
        out_specs=pl.BlockSpec((TM_SEQ, D_GMLP), lambda i: (i, 0)),
        out_shape=jax.ShapeDtypeStruct((N_ROWS, D_GMLP), BF16),
        scratch_shapes=[pltpu.VMEM((TM_SEQ, D_GMLP), F32), pltpu.VMEM((TM_SEQ, D_GMLP), BF16)],
        compiler_params=_params(("arbitrary",)),
        name="gmlp_prompt",
    )(xn_all, w_in, w_in, v_norm, w_s, b_s_wide)


def _gmlp_sample_kernel(xn_ref, wa_ref, wb_ref, vn_ref, ws0_ref, bs0_ref, oc_ref, vg_ref):
    u, v_lo, v_hi = _gmlp_uv(xn_ref[...], wa_ref, wb_ref, vn_ref)
    vg_ref[:, :HALF_V] = v_lo
    vg_ref[:, HALF_V:] = v_hi
    oc_ref[...] = (u * (ws0_ref[...] * vg_ref[...] + bs0_ref[...])).astype(BF16)


def _gmlp_sample(xn_all, w_in, l, v_norm, ws0, bs0):
    c2 = lambda i: (0, 0)
    return pl.pallas_call(
        _gmlp_sample_kernel,
        grid=(1,),
        in_specs=[
            pl.BlockSpec((DEC_BATCH, D_MODEL), lambda i: (SAMPLE_BLOCK, 0)),
            _layer_block((D_MODEL, WIDE), l, 0, 3),
            _layer_block((D_MODEL, WIDE), l, 0, 4),
            _layer_block((1, D_GMLP), l, 0, 0),
            pl.BlockSpec((1, D_GMLP), c2),
            pl.BlockSpec((1, D_GMLP), c2),
        ],
        out_specs=[pl.BlockSpec((DEC_BATCH, D_GMLP), c2), pl.BlockSpec((DEC_BATCH, D_GMLP), c2)],
        out_shape=[
            jax.ShapeDtypeStruct((DEC_BATCH, D_GMLP), BF16),
            jax.ShapeDtypeStruct((DEC_BATCH, D_GMLP), F32),
        ],
        compiler_params=_params(("arbitrary",)),
        name="gmlp_sample",
    )(xn_all, w_in, w_in, v_norm, ws0, bs0)


def _merge_kernel(xn_ref, wg0_ref, wg1_ref, wg2_ref, bg_ref, oa_ref, ob_ref, oc_ref, wb_ref, m_ref):
    xn = xn_ref[...]
    bg = bg_ref[...]
    acc = None
    for i, (wg_ref, o_ref) in enumerate(((wg0_ref, oa_ref), (wg1_ref, ob_ref), (wg2_ref, oc_ref))):
        gate = jax.nn.sigmoid(_dot(xn, wg_ref[...]) + bg[i:i + 1])
        term = gate * _dot(o_ref[...], wb_ref[i])
        acc = term if acc is None else acc + term
    m_ref[...] = acc.astype(BF16)


def _merge(xn_all, w_in, l, b_gate, o_a, o_b, o_c, w_branch):
    nt = N_ROWS // TM_ROW
    nn = D_MODEL // TN
    g0 = OFF_G // TN
    rows = lambda i, j: (i, 0)
    gate_w = lambda br: pl.BlockSpec((None, D_MODEL, TN), lambda i, j: (l, 0, g0 + br * nn + j))
    return pl.pallas_call(
        _merge_kernel,
        grid=(nt, nn),
        in_specs=[
            pl.BlockSpec((TM_ROW, D_MODEL), rows),
            gate_w(0), gate_w(1), gate_w(2),
            pl.BlockSpec((None, N_BRANCHES, TN), lambda i, j: (l, 0, j)),
            pl.BlockSpec((TM_ROW, BRANCH_W), rows),
            pl.BlockSpec((TM_ROW, BRANCH_W), rows),
            pl.BlockSpec((TM_ROW, BRANCH_W), rows),
            pl.BlockSpec((None, N_BRANCHES, BRANCH_W, TN), lambda i, j: (l, 0, 0, j)),
        ],
        out_specs=pl.BlockSpec((TM_ROW, TN), lambda i, j: (i, j)),
        out_shape=jax.ShapeDtypeStruct((N_ROWS, D_MODEL), BF16),
        compiler_params=_params(("arbitrary", "arbitrary")),
        name="merge",
    )(xn_all, w_in, w_in, w_in, b_gate, o_a, o_b, o_c, w_branch)


OUT_SPLIT = 2


def _out_proj_kernel(m_ref, w_ref, x_ref, g_ref, x1_ref, xn_ref):
    sub = m_ref.shape[0] // OUT_SPLIT
    for s in range(OUT_SPLIT):
        rows = slice(s * sub, (s + 1) * sub)
        x1 = x_ref[rows, :] + _dot(m_ref[rows, :], w_ref[...])
        x1_ref[rows, :] = x1
        xn_ref[rows, :] = _rms_rows(x1, g_ref[...]).astype(BF16)


def _out_proj(m, w_out, l, x, g_ffn, *, m_row_block0, tm, out_rows):
    rows = lambda i: (i, 0)
    return pl.pallas_call(
        _out_proj_kernel,
        grid=(x.shape[0] // tm,),
        in_specs=[
            pl.BlockSpec((tm, D_MODEL), lambda i: (m_row_block0 + i, 0)),
            _layer_block((D_MODEL, D_MODEL), l, 0, 0),
            pl.BlockSpec((tm, D_MODEL), rows),
            _layer_block((1, D_MODEL), l, 0, 0),
        ],
        out_specs=[pl.BlockSpec((tm, D_MODEL), rows), pl.BlockSpec((tm, D_MODEL), rows)],
        out_shape=[
            jax.ShapeDtypeStruct((out_rows, D_MODEL), F32),
            jax.ShapeDtypeStruct((out_rows, D_MODEL), BF16),
        ],
        compiler_params=_params(("arbitrary",)),
        name="out_proj",
    )(m, w_out, x, g_ffn)


def _row_halves(tm, width, row_block0=0):
    half = lambda s: pl.BlockSpec((tm // 2, width), lambda j, i: (2 * (row_block0 + i) + s, 0))
    return [half(0), half(1)]


def _ffn_up_kernel(xa_ref, xb_ref, wgate_ref, wup_ref, h_ref, wgate_scr, wup_scr):
    @pl.when(pl.program_id(1) == 0)
    def _():
        wgate_scr[...] = wgate_ref[...].astype(BF16)
        wup_scr[...] = wup_ref[...].astype(BF16)

    sub = xa_ref.shape[0]
    for s, xn_ref in enumerate((xa_ref, xb_ref)):
        xn = xn_ref[...]
        gate = _dot(xn, wgate_scr[...])
        h_ref[s * sub:(s + 1) * sub, :] = (
            gate * jax.nn.sigmoid(gate) * _dot(xn, wup_scr[...])).astype(BF16)


def _ffn_up(xn, w_gate_up, l):
    nt = N_ROWS // TM_ROW
    nn = D_FF // TN
    return pl.pallas_call(
        _ffn_up_kernel,
        grid=(nn, nt),
        in_specs=[
            *_row_halves(TM_ROW, D_MODEL),
            pl.BlockSpec((None, D_MODEL, TN), lambda j, i: (l, 0, j)),
            pl.BlockSpec((None, D_MODEL, TN), lambda j, i: (l, 0, nn + j)),
        ],
        out_specs=pl.BlockSpec((TM_ROW, TN), lambda j, i: (i, j)),
        out_shape=jax.ShapeDtypeStruct((N_ROWS, D_FF), BF16),
        scratch_shapes=[pltpu.VMEM((D_MODEL, TN), BF16), pltpu.VMEM((D_MODEL, TN), BF16)],
        compiler_params=_params(("arbitrary", "arbitrary")),
        name="ffn_up",
    )(xn, xn, w_gate_up, w_gate_up)


def _ffn_down_kernel(ha_ref, hb_ref, w_ref, x_ref, o_ref, w_scr):
    @pl.when(pl.program_id(1) == 0)
    def _():
        w_scr[...] = w_ref[...].astype(BF16)

    sub = ha_ref.shape[0]
    for s, h_ref in enumerate((ha_ref, hb_ref)):
        rows = slice(s * sub, (s + 1) * sub)
        o_ref[rows, :] = x_ref[rows, :] + _dot(h_ref[...], w_scr[...])


def _ffn_down(h, w_down, l, x1, *, row_block0, n_rows, tm):
    nt = n_rows // tm
    nn = D_MODEL // TN
    return pl.pallas_call(
        _ffn_down_kernel,
        grid=(nn, nt),
        in_specs=[
            *_row_halves(tm, D_FF, row_block0),
            pl.BlockSpec((None, D_FF, TN), lambda j, i: (l, 0, j)),
            pl.BlockSpec((tm, TN), lambda j, i: (row_block0 + i, j)),
        ],
        out_specs=pl.BlockSpec((tm, TN), lambda j, i: (i, j)),
        out_shape=jax.ShapeDtypeStruct((n_rows, D_MODEL), F32),
        scratch_shapes=[pltpu.VMEM((D_FF, TN), BF16)],
        compiler_params=_params(("arbitrary", "arbitrary")),
        name="ffn_down",
    )(h, h, w_down, x1)


def _block_diag_ones(width):
    head = np.arange(width) // HEAD_DIM
    return jnp.asarray(head[:, None] == head[None, :], dtype=BF16)


def _fill_sample_rows(o_prompt, o_sample):
    return lax.dynamic_update_slice(o_prompt, o_sample, (N_PROMPT, 0))


def _layer(x, l, cache_k, cache_v, state_conv, p, const):
    first, last = l == 0, l == DEPTH - 1
    if first:
        x_p, x_s = x
        xn = _fill_sample_rows(_norm(x_p, p["norm_mix"], l, tm=TM_SEQ, out_rows=N_ROWS),
                               _norm(x_s, p["norm_mix"], l, tm=DEC_BATCH, out_rows=DEC_BATCH))
    else:
        xn = _norm(x, p["norm_mix"], l, tm=TM_ROW, out_rows=N_ROWS)
    w_in = p["w_in"]
    qn = jnp.tile(p["q_norm"][l], N_HEADS).reshape(1, BRANCH_W)
    kn = jnp.tile(p["k_norm"][l], N_KV_HEADS).reshape(1, KV_W)
    sinks = p["sinks"][l]

    q_p, k_p, v_p, klast_p, vlast_p = _qkv(
        xn, w_in, l, qn, kn, const["pq"], const["pk"],
        row_block0=0, n_rows=N_PROMPT, tm=TM_SEQ, tiles_per_seq=SEQ // TM_SEQ)
    q_s, k_s, v_s = _qkv(
        xn, w_in, l, qn, kn, const["pq"], const["pk"],
        row_block0=SAMPLE_BLOCK, n_rows=DEC_BATCH, tm=DEC_BATCH)
    oa_p = _attn_prompt(sinks, q_p, k_p, v_p, const["bias_p"])
    oa_s = _attn_sample(
        q_s.reshape(DEC_BATCH, N_HEADS, HEAD_DIM),
        k_s.reshape(DEC_BATCH, 1, KV_W), v_s.reshape(DEC_BATCH, 1, KV_W),
        cache_k, cache_v, l,
        const["bias_s"], sinks.reshape(N_HEADS, 1), const["mask_s"], const["rep_s"],
    ).reshape(DEC_BATCH, BRANCH_W)

    ob_p, nc_p = _conv_prompt(xn, w_in, l, p["conv_w"])
    ob_s, zc_s = _conv_sample(xn, w_in, l, p["conv_w"], state_conv[l, :, 0], state_conv[l, :, 1])

    w_s = p["w_spatial"]
    b_s = p["b_spatial"]
    bs_wide = jnp.broadcast_to(b_s[:, :, :, None], (DEPTH, N_SPATIAL_GROUPS, CHUNK, SPATIAL_GROUP_W))
    oc_p = _gmlp_prompt(xn, w_in, l, p["v_norm"], w_s, bs_wide)
    ws0 = jnp.repeat(w_s[l, :, 0, 0], SPATIAL_GROUP_W).reshape(1, D_GMLP)
    bs0 = jnp.repeat(b_s[l, :, 0], SPATIAL_GROUP_W).reshape(1, D_GMLP)
    oc_s, vg_s = _gmlp_sample(xn, w_in, l, p["v_norm"], ws0, bs0)

    o_a = _fill_sample_rows(oa_p, oa_s)
    o_b = _fill_sample_rows(ob_p, ob_s)
    o_c = _fill_sample_rows(oc_p, oc_s)
    m = _merge(xn, w_in, l, p["b_gate"], o_a, o_b, o_c, p["w_branch"])
    if first:
        x1_p, xn2_p = _out_proj(m, p["w_out"], l, x_p, p["norm_ffn"], m_row_block0=0, tm=TM_SEQ, out_rows=N_ROWS)
        x1_s, xn2_s = _out_proj(m, p["w_out"], l, x_s, p["norm_ffn"], m_row_block0=SAMPLE_BLOCK,
                                tm=DEC_BATCH, out_rows=DEC_BATCH)
        x1, xn2 = _fill_sample_rows(x1_p, x1_s), _fill_sample_rows(xn2_p, xn2_s)
    else:
        x1, xn2 = _out_proj(m, p["w_out"], l, x, p["norm_ffn"], m_row_block0=0, tm=TM_ROW, out_rows=N_ROWS)
    h = _ffn_up(xn2, p["w_gate_up"], l)
    if last:
        x2 = (_ffn_down(h, p["w_down"], l, x1, row_block0=0, n_rows=N_PROMPT, tm=TM_SEQ),
              _ffn_down(h, p["w_down"], l, x1, row_block0=SAMPLE_BLOCK, n_rows=DEC_BATCH, tm=DEC_BATCH))
    else:
        x2 = _ffn_down(h, p["w_down"], l, x1, row_block0=0, n_rows=N_ROWS, tm=TM_ROW)

    new_k_p = klast_p.reshape(BATCH, WINDOW, N_KV_HEADS, HEAD_DIM)
    new_v_p = vlast_p.reshape(BATCH, WINDOW, N_KV_HEADS, HEAD_DIM)
    new_k_s = k_s.reshape(DEC_BATCH, 1, N_KV_HEADS, HEAD_DIM)
    new_v_s = v_s.reshape(DEC_BATCH, 1, N_KV_HEADS, HEAD_DIM)
    new_conv_s = jnp.stack([state_conv[l, :, 1], zc_s], axis=1)
    return x2, (new_k_p, new_v_p, nc_p, new_k_s, new_v_s, new_conv_s, vg_s.reshape(DEC_BATCH, 1, D_GMLP))


def kernel(x_prompt, x_sample, cache_k, cache_v, state_conv, norm_mix, w_in, b_gate, q_norm, k_norm,
           sinks, conv_w, v_norm, w_spatial, b_spatial, w_branch, w_out, norm_ffn, w_gate_up, w_down):
    assert min(WINDOW, SEQ) == WINDOW and SEQ % TM_SEQ == 0 and TM_SEQ >= WINDOW
    n_buf = cache_k.shape[2]
    assert n_buf == WINDOW
    cache_k = cache_k.reshape(DEPTH, DEC_BATCH, n_buf, KV_W)
    cache_v = cache_v.reshape(DEPTH, DEC_BATCH, n_buf, KV_W)
    p = {
        "norm_mix": norm_mix.reshape(DEPTH, 1, D_MODEL),
        "norm_ffn": norm_ffn.reshape(DEPTH, 1, D_MODEL),
        "v_norm": v_norm.reshape(DEPTH, 1, D_GMLP),
        "w_in": w_in.astype(BF16),
        "w_branch": w_branch.astype(BF16),
        "w_out": w_out.astype(BF16),
        "w_gate_up": w_gate_up,
        "w_down": w_down,
        "b_gate": b_gate, "q_norm": q_norm, "k_norm": k_norm, "sinks": sinks, "conv_w": conv_w,
        "w_spatial": w_spatial, "b_spatial": b_spatial,
    }
    kv_of_col = np.arange(KV_W) // HEAD_DIM
    kv_of_head = np.arange(N_HEADS) // GQA_GROUP
    const = {
        "pq": _block_diag_ones(BRANCH_W),
        "pk": _block_diag_ones(KV_W),
        "bias_p": _prompt_bias(),
        "bias_s": _sample_bias(n_buf),
        "mask_s": jnp.asarray(kv_of_head[:, None] == kv_of_col[None, :], dtype=F32),
        "rep_s": jnp.asarray(np.tile(np.eye(HEAD_DIM), (1, N_KV_HEADS)), dtype=BF16),
    }
    x = (x_prompt.reshape(N_PROMPT, D_MODEL), x_sample.reshape(DEC_BATCH, D_MODEL))
    per_layer = []
    for l in range(DEPTH):
        x, outs = _layer(x, l, cache_k, cache_v, state_conv, p, const)
        per_layer.append(outs)
    stacked = [jnp.stack([per_layer[l][i] for l in range(DEPTH)]) for i in range(7)]
    y_prompt, y_sample = x
    return (y_prompt.reshape(BATCH, SEQ, D_MODEL), y_sample.reshape(DEC_BATCH, 1, D_MODEL), *stacked)
```

```python
import jax
import jax.numpy as jnp
import numpy as np
from jax import lax
from jax.experimental import pallas as pl
from jax.experimental.pallas import tpu as pltpu

D_MODEL = 2048
BATCH = 4
SEQ = 2048
DEPTH = 2
DEC_BATCH = 128
BRANCH_W = 1024
HEAD_DIM = 64
N_HEADS = 16
N_KV_HEADS = 4
GQA_GROUP = 4
KV_W = 256
WINDOW = 128
D_CONV = 1024
CONV_WIDTH = 3
D_GMLP = 1024
CHUNK = 128
N_SPATIAL_GROUPS = 8
SPATIAL_GROUP_W = 128
N_BRANCHES = 3
D_FF = 5632
EPS = 1e-6
NEG_INF = -1e30

N_PROMPT = BATCH * SEQ
N_ROWS = N_PROMPT + DEC_BATCH
SAMPLE_BLOCK = N_PROMPT // DEC_BATCH

WIDE = 1536
OFF_G = 6656

TM_SEQ = 512
TM_ROW = 640
TN = 512
VMEM_LIMIT = 56 * 1024 * 1024

F32 = jnp.float32
BF16 = jnp.bfloat16


def _params(sem):
    return pltpu.CompilerParams(dimension_semantics=sem, vmem_limit_bytes=VMEM_LIMIT)


def _rms_rows(x, g):
    ms = jnp.mean(x * x, axis=-1, keepdims=True)
    return x * lax.rsqrt(ms + EPS) * g


def _dot(a, b):
    return jnp.dot(a, b, preferred_element_type=F32)


def _dot_nt(a, b):
    return lax.dot_general(a, b, (((1,), (1,)), ((), ())), preferred_element_type=F32)


def _gelu(x):
    return 0.5 * x * (1.0 + jnp.tanh(np.sqrt(2.0 / np.pi).astype(np.float32) * (x + 0.044715 * (x * x * x))))


def _layer_block(shape, l, *idx):
    return pl.BlockSpec((None, *shape), lambda *_: (l, *idx))


def _norm_kernel(x_ref, g_ref, o_ref):
    o_ref[...] = _rms_rows(x_ref[...], g_ref[...]).astype(BF16)


def _norm(x, g, l, *, tm, out_rows):
    rows = lambda i: (i, 0)
    return pl.pallas_call(
        _norm_kernel,
        grid=(x.shape[0] // tm,),
        in_specs=[pl.BlockSpec((tm, D_MODEL), rows), _layer_block((1, D_MODEL), l, 0, 0)],
        out_specs=pl.BlockSpec((tm, D_MODEL), rows),
        out_shape=jax.ShapeDtypeStruct((out_rows, D_MODEL), BF16),
        compiler_params=_params(("arbitrary",)),
        name="norm",
    )(x, g)


def _qkv_kernel(xn_ref, w_ref, qn_ref, kn_ref, pq_ref, pk_ref, q_ref, k_ref, v_ref, *last_refs):
    z = _dot(xn_ref[...], w_ref[...])
    q = z[:, :BRANCH_W]
    k = z[:, BRANCH_W:BRANCH_W + KV_W]
    v = z[:, BRANCH_W + KV_W:]
    q_ms = _dot((q * q).astype(BF16), pq_ref[...]) * (1.0 / HEAD_DIM)
    k_ms = _dot((k * k).astype(BF16), pk_ref[...]) * (1.0 / HEAD_DIM)
    q_ref[...] = (q * lax.rsqrt(q_ms + EPS) * qn_ref[...] * (HEAD_DIM ** -0.5)).astype(BF16)
    kn = k * lax.rsqrt(k_ms + EPS) * kn_ref[...]
    k_ref[...] = kn
    v_ref[...] = v
    if last_refs:
        klast_ref, vlast_ref = last_refs
        tm = xn_ref.shape[0]
        klast_ref[...] = kn[tm - WINDOW:]
        vlast_ref[...] = v[tm - WINDOW:]


def _qkv(xn_all, w_in, l, qn, kn, pq, pk, *, row_block0, n_rows, tm, tiles_per_seq=None):
    nt = n_rows // tm
    const = lambda i: (0, 0)
    rows = lambda i: (i, 0)
    out_specs = [
        pl.BlockSpec((tm, BRANCH_W), rows),
        pl.BlockSpec((tm, KV_W), rows),
        pl.BlockSpec((tm, KV_W), rows),
    ]
    out_shape = [
        jax.ShapeDtypeStruct((n_rows, BRANCH_W), BF16),
        jax.ShapeDtypeStruct((n_rows, KV_W), F32),
        jax.ShapeDtypeStruct((n_rows, KV_W), F32),
    ]
    if tiles_per_seq is not None:
        n_seq = nt // tiles_per_seq
        out_specs += [pl.BlockSpec((WINDOW, KV_W), lambda i: (i // tiles_per_seq, 0))] * 2
        out_shape += [jax.ShapeDtypeStruct((n_seq * WINDOW, KV_W), F32)] * 2
    return pl.pallas_call(
        _qkv_kernel,
        grid=(nt,),
        in_specs=[
            pl.BlockSpec((tm, D_MODEL), lambda i: (row_block0 + i, 0)),
            _layer_block((D_MODEL, WIDE), l, 0, 0),
            pl.BlockSpec((1, BRANCH_W), const),
            pl.BlockSpec((1, KV_W), const),
            pl.BlockSpec((BRANCH_W, BRANCH_W), const),
            pl.BlockSpec((KV_W, KV_W), const),
        ],
        out_specs=out_specs,
        out_shape=out_shape,
        compiler_params=_params(("arbitrary",)),
        name="qkv",
    )(xn_all, w_in, qn, kn, pq, pk)


def _attn_prompt_kernel(sink_ref, q_ref, kp_ref, ko_ref, vp_ref, vo_ref, bias_ref, o_ref, s_scr, p_scr):
    kk = jnp.concatenate([kp_ref[...], ko_ref[...]], axis=0).astype(BF16)
    vv = jnp.concatenate([vp_ref[...], vo_ref[...]], axis=0).astype(BF16)
    group_rows = GQA_GROUP * WINDOW
    for g in range(N_KV_HEADS):
        kg = kk[:, g * HEAD_DIM:(g + 1) * HEAD_DIM]
        heads = range(g * GQA_GROUP, (g + 1) * GQA_GROUP)
        qg = jnp.concatenate([q_ref[:, h * HEAD_DIM:(h + 1) * HEAD_DIM] for h in heads], axis=0)
        s_scr[g * group_rows:(g + 1) * group_rows, :] = _dot_nt(qg, kg)
    sink_terms = []
    for h in range(N_HEADS):
        rows = slice(h * WINDOW, (h + 1) * WINDOW)
        s = s_scr[rows, :] + bias_ref[h]
        sink = sink_ref[h]
        m = jnp.maximum(jnp.max(s, axis=-1, keepdims=True), sink)
        p_scr[rows, :] = jnp.exp(s - m).astype(BF16)
        sink_terms.append(jnp.exp(sink - m))
    ones = jnp.ones((2 * WINDOW, HEAD_DIM), BF16)
    for g in range(N_KV_HEADS):
        p = p_scr[g * group_rows:(g + 1) * group_rows, :]
        o_all = _dot(p, vv[:, g * HEAD_DIM:(g + 1) * HEAD_DIM])
        den_all = _dot(p, ones)
        for i in range(GQA_GROUP):
            h = g * GQA_GROUP + i
            rows = slice(i * WINDOW, (i + 1) * WINDOW)
            o = o_all[rows] / (den_all[rows] + sink_terms[h])
            o_ref[:, h * HEAD_DIM:(h + 1) * HEAD_DIM] = o.astype(BF16)


def _attn_prompt(sinks, q, k, v, bias):
    nb = SEQ // WINDOW
    own = lambda b, j: (b * nb + j, 0)
    prev = lambda b, j: (b * nb + jnp.maximum(j - 1, 0), 0)
    return pl.pallas_call(
        _attn_prompt_kernel,
        grid=(BATCH, nb),
        in_specs=[
            pl.BlockSpec(memory_space=pltpu.SMEM),
            pl.BlockSpec((WINDOW, BRANCH_W), own),
            pl.BlockSpec((WINDOW, KV_W), prev),
            pl.BlockSpec((WINDOW, KV_W), own),
            pl.BlockSpec((WINDOW, KV_W), prev),
            pl.BlockSpec((WINDOW, KV_W), own),
            pl.BlockSpec((None, N_HEADS, WINDOW, 2 * WINDOW), lambda b, j: (jnp.minimum(j, 1), 0, 0, 0)),
        ],
        out_specs=pl.BlockSpec((WINDOW, BRANCH_W), own),
        out_shape=jax.ShapeDtypeStruct((N_ROWS, BRANCH_W), BF16),
        scratch_shapes=[
            pltpu.VMEM((N_HEADS * WINDOW, 2 * WINDOW), F32),
            pltpu.VMEM((N_HEADS * WINDOW, 2 * WINDOW), BF16),
        ],
        compiler_params=_params(("arbitrary", "arbitrary")),
        name="attn_prompt",
    )(sinks, q, k, k, v, v, bias)


def _prompt_bias():
    slopes = jnp.exp2(-8.0 * jnp.arange(1, N_HEADS + 1, dtype=F32) / N_HEADS)
    qi = jnp.arange(WINDOW, dtype=jnp.int32)[:, None]
    ki = jnp.arange(2 * WINDOW, dtype=jnp.int32)[None, :] - WINDOW
    dist = qi - ki
    valid = (dist >= 0) & (dist < WINDOW)
    bias = -slopes[:, None, None] * dist.astype(F32)[None]
    with_prev = jnp.where(valid[None], bias, NEG_INF)
    first = jnp.where((valid & (ki >= 0))[None], bias, NEG_INF)
    return jnp.stack([first, with_prev])


SAMPLE_BT = 16


def _attn_sample_kernel(q_ref, kn_ref, vn_ref, ck_ref, cv_ref, bias_ref, sink_ref, mask_ref,
                        rep_ref, o_ref):
    bt = q_ref.shape[0]
    mask = mask_ref[...][None]
    sink = sink_ref[...][None]
    qe = _dot(q_ref[...].reshape(bt * N_HEADS, HEAD_DIM), rep_ref[...])
    qe = qe.reshape(bt, N_HEADS, KV_W) * mask
    qe_bf = qe.astype(BF16)
    s = jnp.stack([_dot_nt(qe_bf[b], ck_ref[b].astype(BF16)) for b in range(bt)])
    s = s + bias_ref[...][None]
    s_new = jnp.sum(qe * kn_ref[...], axis=-1, keepdims=True)
    m = jnp.maximum(jnp.maximum(jnp.max(s, axis=-1, keepdims=True), s_new), sink)
    p = jnp.exp(s - m)
    p_new = jnp.exp(s_new - m)
    den = jnp.sum(p, axis=-1, keepdims=True) + p_new + jnp.exp(sink - m)
    p_bf = p.astype(BF16)
    of = jnp.stack([_dot(p_bf[b], cv_ref[b].astype(BF16)) for b in range(bt)])
    of = (of + p_new * vn_ref[...]) * mask / den
    o = (of[..., 0:64] + of[..., 64:128]) + (of[..., 128:192] + of[..., 192:256])
    o_ref[...] = o.astype(BF16)


def _attn_sample(q3, k_new, v_new, ck, cv, l, bias, sinks_col, mask, rep):
    nsteps = DEC_BATCH // SAMPLE_BT
    b3 = lambda i: (i, 0, 0)
    c2 = lambda i: (0, 0)
    cache = pl.BlockSpec((None, SAMPLE_BT, WINDOW, KV_W), lambda i: (l, i, 0, 0))
    return pl.pallas_call(
        _attn_sample_kernel,
        grid=(nsteps,),
        in_specs=[
            pl.BlockSpec((SAMPLE_BT, N_HEADS, HEAD_DIM), b3),
            pl.BlockSpec((SAMPLE_BT, 1, KV_W), b3),
            pl.BlockSpec((SAMPLE_BT, 1, KV_W), b3),
            cache,
            cache,
            pl.BlockSpec((N_HEADS, WINDOW), c2),
            pl.BlockSpec((N_HEADS, 1), c2),
            pl.BlockSpec((N_HEADS, KV_W), c2),
            pl.BlockSpec((HEAD_DIM, KV_W), c2),
        ],
        out_specs=pl.BlockSpec((SAMPLE_BT, N_HEADS, HEAD_DIM), b3),
        out_shape=jax.ShapeDtypeStruct((DEC_BATCH, N_HEADS, HEAD_DIM), BF16),
        compiler_params=_params(("arbitrary",)),
        name="attn_sample",
    )(q3, k_new, v_new, ck, cv, bias, sinks_col, mask, rep)


def _sample_bias(n_buf):
    slopes = jnp.exp2(-8.0 * jnp.arange(1, N_HEADS + 1, dtype=F32) / N_HEADS)
    dist = n_buf - jnp.arange(n_buf, dtype=jnp.int32)
    bias = -slopes[:, None] * dist.astype(F32)[None, :]
    return jnp.where((dist < WINDOW)[None, :], bias, NEG_INF)


def _bch_cols(wa_ref, wb_ref, seg, c):
    r = seg * D_CONV + c * TN
    return wa_ref[:, r:r + TN] if r < WIDE else wb_ref[:, r - WIDE:r - WIDE + TN]


def _conv_prompt_kernel(xn_ref, wa_ref, wb_ref, cw_ref, ob_ref, nc_ref, zbuf, carry):
    t = pl.program_id(1)
    tm = xn_ref.shape[0]
    xn = xn_ref[...]

    @pl.when(t == 0)
    def _():
        zbuf[0:8, :] = jnp.zeros((8, D_CONV), F32)

    @pl.when(t > 0)
    def _():
        zbuf[0:8, :] = carry[...]

    for c in range(D_CONV // TN):
        cols = slice(c * TN, (c + 1) * TN)
        zc = _dot(xn, _bch_cols(wa_ref, wb_ref, 1, c)) * _dot(xn, _bch_cols(wa_ref, wb_ref, 2, c))
        zbuf[8:8 + tm, cols] = zc
        cw = cw_ref[:, cols]
        y = cw[0:1] * zbuf[6:6 + tm, cols] + cw[1:2] * zbuf[7:7 + tm, cols] + cw[2:3] * zc
        ob_ref[:, cols] = (_dot(xn, _bch_cols(wa_ref, wb_ref, 0, c)) * y).astype(BF16)
    carry[...] = zbuf[tm:tm + 8, :]
    nc_ref[...] = zbuf[tm + 6:tm + 8, :]


def _conv_prompt(xn_all, w_in, l, conv_w):
    nt = SEQ // TM_SEQ
    return pl.pallas_call(
        _conv_prompt_kernel,
        grid=(BATCH, nt),
        in_specs=[
            pl.BlockSpec((TM_SEQ, D_MODEL), lambda b, t: (b * nt + t, 0)),
            _layer_block((D_MODEL, WIDE), l, 0, 1),
            _layer_block((D_MODEL, WIDE), l, 0, 2),
            _layer_block((CONV_WIDTH, D_CONV), l, 0, 0),
        ],
        out_specs=[
            pl.BlockSpec((TM_SEQ, D_CONV), lambda b, t: (b * nt + t, 0)),
            pl.BlockSpec((None, CONV_WIDTH - 1, D_CONV), lambda b, t: (b, 0, 0)),
        ],
        out_shape=[
            jax.ShapeDtypeStruct((N_ROWS, D_CONV), BF16),
            jax.ShapeDtypeStruct((BATCH, CONV_WIDTH - 1, D_CONV), F32),
        ],
        scratch_shapes=[
            pltpu.VMEM((TM_SEQ + 8, D_CONV), F32),
            pltpu.VMEM((8, D_CONV), F32),
        ],
        compiler_params=_params(("arbitrary", "arbitrary")),
        name="conv_prompt",
    )(xn_all, w_in, w_in, conv_w)


def _conv_sample_kernel(xn_ref, wa_ref, wb_ref, cw_ref, cb0_ref, cb1_ref, ob_ref, zc_ref):
    xn = xn_ref[...]
    for c in range(D_CONV // TN):
        cols = slice(c * TN, (c + 1) * TN)
        zc = _dot(xn, _bch_cols(wa_ref, wb_ref, 1, c)) * _dot(xn, _bch_cols(wa_ref, wb_ref, 2, c))
        cw = cw_ref[:, cols]
        y = cw[0:1] * cb0_ref[:, cols] + cw[1:2] * cb1_ref[:, cols] + cw[2:3] * zc
        ob_ref[:, cols] = (_dot(xn, _bch_cols(wa_ref, wb_ref, 0, c)) * y).astype(BF16)
        zc_ref[:, cols] = zc


def _conv_sample(xn_all, w_in, l, conv_w, cb0, cb1):
    c2 = lambda i: (0, 0)
    return pl.pallas_call(
        _conv_sample_kernel,
        grid=(1,),
        in_specs=[
            pl.BlockSpec((DEC_BATCH, D_MODEL), lambda i: (SAMPLE_BLOCK, 0)),
            _layer_block((D_MODEL, WIDE), l, 0, 1),
            _layer_block((D_MODEL, WIDE), l, 0, 2),
            _layer_block((CONV_WIDTH, D_CONV), l, 0, 0),
            pl.BlockSpec((DEC_BATCH, D_CONV), c2),
            pl.BlockSpec((DEC_BATCH, D_CONV), c2),
        ],
        out_specs=[pl.BlockSpec((DEC_BATCH, D_CONV), c2), pl.BlockSpec((DEC_BATCH, D_CONV), c2)],
        out_shape=[
            jax.ShapeDtypeStruct((DEC_BATCH, D_CONV), BF16),
            jax.ShapeDtypeStruct((DEC_BATCH, D_CONV), F32),
        ],
        compiler_params=_params(("arbitrary",)),
        name="conv_sample",
    )(xn_all, w_in, w_in, conv_w, cb0, cb1)


HALF_V = WIDE - D_GMLP


def _gmlp_uv(xn, wa_ref, wb_ref, vn_ref):
    u = _gelu(_dot(xn, wa_ref[:, :D_GMLP]))
    v_lo = _gelu(_dot(xn, wa_ref[:, D_GMLP:]))
    v_hi = _gelu(_dot(xn, wb_ref[:, :D_GMLP - HALF_V]))
    ms = (jnp.sum(v_lo * v_lo, axis=-1, keepdims=True)
          + jnp.sum(v_hi * v_hi, axis=-1, keepdims=True)) * (1.0 / D_GMLP)
    r = lax.rsqrt(ms + EPS)
    return u, v_lo * r * vn_ref[:, :HALF_V], v_hi * r * vn_ref[:, HALF_V:]


def _gmlp_prompt_kernel(xn_ref, wa_ref, wb_ref, vn_ref, ws_ref, bs_ref, oc_ref, u_scr, v_scr):
    tm = xn_ref.shape[0]
    u, v_lo, v_hi = _gmlp_uv(xn_ref[...], wa_ref, wb_ref, vn_ref)
    u_scr[...] = u
    v_scr[:, :HALF_V] = v_lo.astype(BF16)
    v_scr[:, HALF_V:] = v_hi.astype(BF16)
    row = lax.broadcasted_iota(jnp.int32, (CHUNK, CHUNK), 0)
    col = lax.broadcasted_iota(jnp.int32, (CHUNK, CHUNK), 1)
    for grp in range(N_SPATIAL_GROUPS):
        w = jnp.where(col <= row, ws_ref[grp], 0.0).astype(BF16)
        cols = slice(grp * SPATIAL_GROUP_W, (grp + 1) * SPATIAL_GROUP_W)
        for ch in range(tm // CHUNK):
            rows = slice(ch * CHUNK, (ch + 1) * CHUNK)
            mix = _dot(w, v_scr[rows, cols]) + bs_ref[grp]
            oc_ref[rows, cols] = (u_scr[rows, cols] * mix).astype(BF16)


def _gmlp_prompt(xn_all, w_in, l, v_norm, w_s, b_s_wide):
    nt = N_PROMPT // TM_SEQ
    return pl.pallas_call(
        _gmlp_prompt_kernel,
        grid=(nt,),
        in_specs=[
            pl.BlockSpec((TM_SEQ, D_MODEL), lambda i: (i, 0)),
            _layer_block((D_MODEL, WIDE), l, 0, 3),
            _layer_block((D_MODEL, WIDE), l, 0, 4),
            _layer_block((1, D_GMLP), l, 0, 0),
            _layer_block((N_SPATIAL_GROUPS, CHUNK, CHUNK), l, 0, 0, 0),
            _layer_block((N_SPATIAL_GROUPS, CHUNK, SPATIAL_GROUP_W), l, 0, 0, 0),
        ],
        out_specs=pl.BlockSpec((TM_SEQ, D_GMLP), lambda i: (i, 0)),
        out_shape=jax.ShapeDtypeStruct((N_ROWS, D_GMLP), BF16),
        scratch_shapes=[pltpu.VMEM((TM_SEQ, D_GMLP), F32), pltpu.VMEM((TM_SEQ, D_GMLP), BF16)],
        compiler_params=_params(("arbitrary",)),
        name="gmlp_prompt",
    )(xn_all, w_in, w_in, v_norm, w_s, b_s_wide)


def _gmlp_sample_kernel(xn_ref, wa_ref, wb_ref, vn_ref, ws0_ref, bs0_ref, oc_ref, vg_ref):
    u, v_lo, v_hi = _gmlp_uv(xn_ref[...], wa_ref, wb_ref, vn_ref)
    vg_ref[:, :HALF_V] = v_lo
    vg_ref[:, HALF_V:] = v_hi
    oc_ref[...] = (u * (ws0_ref[...] * vg_ref[...] + bs0_ref[...])).astype(BF16)


def _gmlp_sample(xn_all, w_in, l, v_norm, ws0, bs0):
    c2 = lambda i: (0, 0)
    return pl.pallas_call(
        _gmlp_sample_kernel,
        grid=(1,),
        in_specs=[
            pl.BlockSpec((DEC_BATCH, D_MODEL), lambda i: (SAMPLE_BLOCK, 0)),
            _layer_block((D_MODEL, WIDE), l, 0, 3),
            _layer_block((D_MODEL, WIDE), l, 0, 4),
            _layer_block((1, D_GMLP), l, 0, 0),
            pl.BlockSpec((1, D_GMLP), c2),
            pl.BlockSpec((1, D_GMLP), c2),
        ],
        out_specs=[pl.BlockSpec((DEC_BATCH, D_GMLP), c2), pl.BlockSpec((DEC_BATCH, D_GMLP), c2)],
        out_shape=[
            jax.ShapeDtypeStruct((DEC_BATCH, D_GMLP), BF16),
            jax.ShapeDtypeStruct((DEC_BATCH, D_GMLP), F32),
        ],
        compiler_params=_params(("arbitrary",)),
        name="gmlp_sample",
    )(xn_all, w_in, w_in, v_norm, ws0, bs0)


def _merge_kernel(xn_ref, wg0_ref, wg1_ref, wg2_ref, bg_ref, oa_ref, ob_ref, oc_ref, wb_ref, m_ref):
    xn = xn_ref[...]
    bg = bg_ref[...]
    acc = None
    for i, (wg_ref, o_ref) in enumerate(((wg0_ref, oa_ref), (wg1_ref, ob_ref), (wg2_ref, oc_ref))):
        gate = jax.nn.sigmoid(_dot(xn, wg_ref[...]) + bg[i:i + 1])
        term = gate * _dot(o_ref[...], wb_ref[i])
        acc = term if acc is None else acc + term
    m_ref[...] = acc.astype(BF16)


TM_MERGE = 832


def _merge(xn_all, w_in, l, b_gate, o_a, o_b, o_c, w_branch):
    nt = N_ROWS // TM_MERGE
    nn = D_MODEL // TN
    g0 = OFF_G // TN
    rows = lambda i, j: (i, 0)
    gate_w = lambda br: pl.BlockSpec((None, D_MODEL, TN), lambda i, j: (l, 0, g0 + br * nn + j))
    return pl.pallas_call(
        _merge_kernel,
        grid=(nt, nn),
        in_specs=[
            pl.BlockSpec((TM_MERGE, D_MODEL), rows),
            gate_w(0), gate_w(1), gate_w(2),
            pl.BlockSpec((None, N_BRANCHES, TN), lambda i, j: (l, 0, j)),
            pl.BlockSpec((TM_MERGE, BRANCH_W), rows),
            pl.BlockSpec((TM_MERGE, BRANCH_W), rows),
            pl.BlockSpec((TM_MERGE, BRANCH_W), rows),
            pl.BlockSpec((None, N_BRANCHES, BRANCH_W, TN), lambda i, j: (l, 0, 0, j)),
        ],
        out_specs=pl.BlockSpec((TM_MERGE, TN), lambda i, j: (i, j)),
        out_shape=jax.ShapeDtypeStruct((N_ROWS, D_MODEL), BF16),
        compiler_params=_params(("arbitrary", "arbitrary")),
        name="merge",
    )(xn_all, w_in, w_in, w_in, b_gate, o_a, o_b, o_c, w_branch)


OUT_SPLIT = 2


def _out_proj_kernel(m_ref, w_ref, x_ref, g_ref, x1_ref, xn_ref):
    sub = m_ref.shape[0] // OUT_SPLIT
    for s in range(OUT_SPLIT):
        rows = slice(s * sub, (s + 1) * sub)
        x1 = x_ref[rows, :] + _dot(m_ref[rows, :], w_ref[...])
        x1_ref[rows, :] = x1
        xn_ref[rows, :] = _rms_rows(x1, g_ref[...]).astype(BF16)


def _out_proj(m, w_out, l, x, g_ffn, *, m_row_block0, tm, out_rows):
    rows = lambda i: (i, 0)
    return pl.pallas_call(
        _out_proj_kernel,
        grid=(x.shape[0] // tm,),
        in_specs=[
            pl.BlockSpec((tm, D_MODEL), lambda i: (m_row_block0 + i, 0)),
            _layer_block((D_MODEL, D_MODEL), l, 0, 0),
            pl.BlockSpec((tm, D_MODEL), rows),
            _layer_block((1, D_MODEL), l, 0, 0),
        ],
        out_specs=[pl.BlockSpec((tm, D_MODEL), rows), pl.BlockSpec((tm, D_MODEL), rows)],
        out_shape=[
            jax.ShapeDtypeStruct((out_rows, D_MODEL), F32),
            jax.ShapeDtypeStruct((out_rows, D_MODEL), BF16),
        ],
        compiler_params=_params(("arbitrary",)),
        name="out_proj",
    )(m, w_out, x, g_ffn)


TM_UP = 1664
UP_SPLIT = 4


def _ffn_up_kernel(xn_ref, wgate_ref, wup_ref, h_ref, wgate_scr, wup_scr):
    @pl.when(pl.program_id(1) == 0)
    def _():
        wgate_scr[...] = wgate_ref[...].astype(BF16)
        wup_scr[...] = wup_ref[...].astype(BF16)

    sub = xn_ref.shape[0] // UP_SPLIT
    for s in range(UP_SPLIT):
        rows = slice(s * sub, (s + 1) * sub)
        xn = xn_ref[rows, :]
        gate = _dot(xn, wgate_scr[...])
        h_ref[rows, :] = (gate * jax.nn.sigmoid(gate) * _dot(xn, wup_scr[...])).astype(BF16)


def _ffn_up(xn, w_gate_up, l):
    nt = N_ROWS // TM_UP
    nn = D_FF // TN
    return pl.pallas_call(
        _ffn_up_kernel,
        grid=(nn, nt),
        in_specs=[
            pl.BlockSpec((TM_UP, D_MODEL), lambda j, i: (i, 0)),
            pl.BlockSpec((None, D_MODEL, TN), lambda j, i: (l, 0, j)),
            pl.BlockSpec((None, D_MODEL, TN), lambda j, i: (l, 0, nn + j)),
        ],
        out_specs=pl.BlockSpec((TM_UP, TN), lambda j, i: (i, j)),
        out_shape=jax.ShapeDtypeStruct((N_ROWS, D_FF), BF16),
        scratch_shapes=[pltpu.VMEM((D_MODEL, TN), BF16), pltpu.VMEM((D_MODEL, TN), BF16)],
        compiler_params=_params(("arbitrary", "arbitrary")),
        name="ffn_up",
    )(xn, w_gate_up, w_gate_up)


def _ffn_down_kernel(h_ref, w_ref, x_ref, o_ref, w_scr):
    @pl.when(pl.program_id(1) == 0)
    def _():
        w_scr[...] = w_ref[...].astype(BF16)

    o_ref[...] = x_ref[...] + _dot(h_ref[...], w_scr[...])


def _ffn_down(h, w_down, l, x1, *, row_block0, n_rows, tm):
    nt = n_rows // tm
    nn = D_MODEL // TN
    return pl.pallas_call(
        _ffn_down_kernel,
        grid=(nn, nt),
        in_specs=[
            pl.BlockSpec((tm, D_FF), lambda j, i: (row_block0 + i, 0)),
            pl.BlockSpec((None, D_FF, TN), lambda j, i: (l, 0, j)),
            pl.BlockSpec((tm, TN), lambda j, i: (row_block0 + i, j)),
        ],
        out_specs=pl.BlockSpec((tm, TN), lambda j, i: (i, j)),
        out_shape=jax.ShapeDtypeStruct((n_rows, D_MODEL), F32),
        scratch_shapes=[pltpu.VMEM((D_FF, TN), BF16)],
        compiler_params=_params(("arbitrary", "arbitrary")),
        name="ffn_down",
    )(h, w_down, x1)


def _block_diag_ones(width):
    head = np.arange(width) // HEAD_DIM
    return jnp.asarray(head[:, None] == head[None, :], dtype=BF16)


def _fill_sample_rows(o_prompt, o_sample):
    return lax.dynamic_update_slice(o_prompt, o_sample, (N_PROMPT, 0))


def _layer(x, l, cache_k, cache_v, state_conv, p, const):
    first, last = l == 0, l == DEPTH - 1
    if first:
        x_p, x_s = x
        xn = _fill_sample_rows(_norm(x_p, p["norm_mix"], l, tm=TM_SEQ, out_rows=N_ROWS),
                               _norm(x_s, p["norm_mix"], l, tm=DEC_BATCH, out_rows=DEC_BATCH))
    else:
        xn = _norm(x, p["norm_mix"], l, tm=TM_ROW, out_rows=N_ROWS)
    w_in = p["w_in"]
    qn = jnp.tile(p["q_norm"][l], N_HEADS).reshape(1, BRANCH_W)
    kn = jnp.tile(p["k_norm"][l], N_KV_HEADS).reshape(1, KV_W)
    sinks = p["sinks"][l]

    q_p, k_p, v_p, klast_p, vlast_p = _qkv(
        xn, w_in, l, qn, kn, const["pq"], const["pk"],
        row_block0=0, n_rows=N_PROMPT, tm=TM_SEQ, tiles_per_seq=SEQ // TM_SEQ)
    q_s, k_s, v_s = _qkv(
        xn, w_in, l, qn, kn, const["pq"], const["pk"],
        row_block0=SAMPLE_BLOCK, n_rows=DEC_BATCH, tm=DEC_BATCH)
    oa_p = _attn_prompt(sinks, q_p, k_p, v_p, const["bias_p"])
    oa_s = _attn_sample(
        q_s.reshape(DEC_BATCH, N_HEADS, HEAD_DIM),
        k_s.reshape(DEC_BATCH, 1, KV_W), v_s.reshape(DEC_BATCH, 1, KV_W),
        cache_k, cache_v, l,
        const["bias_s"], sinks.reshape(N_HEADS, 1), const["mask_s"], const["rep_s"],
    ).reshape(DEC_BATCH, BRANCH_W)

    ob_p, nc_p = _conv_prompt(xn, w_in, l, p["conv_w"])
    ob_s, zc_s = _conv_sample(xn, w_in, l, p["conv_w"], state_conv[l, :, 0], state_conv[l, :, 1])

    w_s = p["w_spatial"]
    b_s = p["b_spatial"]
    bs_wide = jnp.broadcast_to(b_s[:, :, :, None], (DEPTH, N_SPATIAL_GROUPS, CHUNK, SPATIAL_GROUP_W))
    oc_p = _gmlp_prompt(xn, w_in, l, p["v_norm"], w_s, bs_wide)
    ws0 = jnp.repeat(w_s[l, :, 0, 0], SPATIAL_GROUP_W).reshape(1, D_GMLP)
    bs0 = jnp.repeat(b_s[l, :, 0], SPATIAL_GROUP_W).reshape(1, D_GMLP)
    oc_s, vg_s = _gmlp_sample(xn, w_in, l, p["v_norm"], ws0, bs0)

    o_a = _fill_sample_rows(oa_p, oa_s)
    o_b = _fill_sample_rows(ob_p, ob_s)
    o_c = _fill_sample_rows(oc_p, oc_s)
    m = _merge(xn, w_in, l, p["b_gate"], o_a, o_b, o_c, p["w_branch"])
    if first:
        x1_p, xn2_p = _out_proj(m, p["w_out"], l, x_p, p["norm_ffn"], m_row_block0=0, tm=TM_SEQ, out_rows=N_ROWS)
        x1_s, xn2_s = _out_proj(m, p["w_out"], l, x_s, p["norm_ffn"], m_row_block0=SAMPLE_BLOCK,
                                tm=DEC_BATCH, out_rows=DEC_BATCH)
        x1, xn2 = _fill_sample_rows(x1_p, x1_s), _fill_sample_rows(xn2_p, xn2_s)
    else:
        x1, xn2 = _out_proj(m, p["w_out"], l, x, p["norm_ffn"], m_row_block0=0, tm=TM_ROW, out_rows=N_ROWS)
    h = _ffn_up(xn2, p["w_gate_up"], l)
    if last:
        x2 = (_ffn_down(h, p["w_down"], l, x1, row_block0=0, n_rows=N_PROMPT, tm=TM_SEQ),
              _ffn_down(h, p["w_down"], l, x1, row_block0=SAMPLE_BLOCK, n_rows=DEC_BATCH, tm=DEC_BATCH))
    else:
        x2 = _ffn_down(h, p["w_down"], l, x1, row_block0=0, n_rows=N_ROWS, tm=TM_ROW)

    new_k_p = klast_p.reshape(BATCH, WINDOW, N_KV_HEADS, HEAD_DIM)
    new_v_p = vlast_p.reshape(BATCH, WINDOW, N_KV_HEADS, HEAD_DIM)
    new_k_s = k_s.reshape(DEC_BATCH, 1, N_KV_HEADS, HEAD_DIM)
    new_v_s = v_s.reshape(DEC_BATCH, 1, N_KV_HEADS, HEAD_DIM)
    new_conv_s = jnp.stack([state_conv[l, :, 1], zc_s], axis=1)
    return x2, (new_k_p, new_v_p, nc_p, new_k_s, new_v_s, new_conv_s, vg_s.reshape(DEC_BATCH, 1, D_GMLP))


def kernel(x_prompt, x_sample, cache_k, cache_v, state_conv, norm_mix, w_in, b_gate, q_norm, k_norm,
           sinks, conv_w, v_norm, w_spatial, b_spatial, w_branch, w_out, norm_ffn, w_gate_up, w_down):
    assert min(WINDOW, SEQ) == WINDOW and SEQ % TM_SEQ == 0 and TM_SEQ >= WINDOW
    n_buf = cache_k.shape[2]
    assert n_buf == WINDOW
    cache_k = cache_k.reshape(DEPTH, DEC_BATCH, n_buf, KV_W)
    cache_v = cache_v.reshape(DEPTH, DEC_BATCH, n_buf, KV_W)
    p = {
        "norm_mix": norm_mix.reshape(DEPTH, 1, D_MODEL),
        "norm_ffn": norm_ffn.reshape(DEPTH, 1, D_MODEL),
        "v_norm": v_norm.reshape(DEPTH, 1, D_GMLP),
        "w_in": w_in.astype(BF16),
        "w_branch": w_branch.astype(BF16),
        "w_out": w_out.astype(BF16),
        "w_gate_up": w_gate_up,
        "w_down": w_down,
        "b_gate": b_gate, "q_norm": q_norm, "k_norm": k_norm, "sinks": sinks, "conv_w": conv_w,
        "w_spatial": w_spatial, "b_spatial": b_spatial,
    }
    kv_of_col = np.arange(KV_W) // HEAD_DIM
    kv_of_head = np.arange(N_HEADS) // GQA_GROUP
    const = {
        "pq": _block_diag_ones(BRANCH_W),
        "pk": _block_diag_ones(KV_W),
        "bias_p": _prompt_bias(),
        "bias_s": _sample_bias(n_buf),
        "mask_s": jnp.asarray(kv_of_head[:, None] == kv_of_col[None, :], dtype=F32),
        "rep_s": jnp.asarray(np.tile(np.eye(HEAD_DIM), (1, N_KV_HEADS)), dtype=BF16),
    }
    x = (x_prompt.reshape(N_PROMPT, D_MODEL), x_sample.reshape(DEC_BATCH, D_MODEL))
    per_layer = []
    for l in range(DEPTH):
        x, outs = _layer(x, l, cache_k, cache_v, state_conv, p, const)
        per_layer.append(outs)
    stacked = [jnp.stack([per_layer[l][i] for l in range(DEPTH)]) for i in range(7)]
    y_prompt, y_sample = x
    return (y_prompt.reshape(BATCH, SEQ, D_MODEL), y_sample.reshape(DEC_BATCH, 1, D_MODEL), *stacked)
```

```python
import jax
import jax.numpy as jnp
import numpy as np
from jax import lax
from jax.experimental import pallas as pl
from jax.experimental.pallas import tpu as pltpu

D_MODEL = 2048
BATCH = 4
SEQ = 2048
DEPTH = 2
DEC_BATCH = 128
BRANCH_W = 1024
HEAD_DIM = 64
N_HEADS = 16
N_KV_HEADS = 4
GQA_GROUP = 4
KV_W = 256
WINDOW = 128
D_CONV = 1024
CONV_WIDTH = 3
D_GMLP = 1024
CHUNK = 128
N_SPATIAL_GROUPS = 8
SPATIAL_GROUP_W = 128
N_BRANCHES = 3
D_FF = 5632
EPS = 1e-6
NEG_INF = -1e30

N_PROMPT = BATCH * SEQ
N_ROWS = N_PROMPT + DEC_BATCH
SAMPLE_BLOCK = N_PROMPT // DEC_BATCH

WIDE = 1536
OFF_G = 6656

TM_SEQ = 512
TM_ROW = 640
TN = 512
VMEM_LIMIT = 56 * 1024 * 1024

F32 = jnp.float32
BF16 = jnp.bfloat16


def _params(sem):
    return pltpu.CompilerParams(dimension_semantics=sem, vmem_limit_bytes=VMEM_LIMIT)


def _rms_rows(x, g):
    ms = jnp.mean(x * x, axis=-1, keepdims=True)
    return x * lax.rsqrt(ms + EPS) * g


def _dot(a, b):
    return jnp.dot(a, b, preferred_element_type=F32)


def _dot_nt(a, b):
    return lax.dot_general(a, b, (((1,), (1,)), ((), ())), preferred_element_type=F32)


def _gelu(x):
    return 0.5 * x * (1.0 + jnp.tanh(np.sqrt(2.0 / np.pi).astype(np.float32) * (x + 0.044715 * (x * x * x))))


def _layer_block(shape, l, *idx):
    return pl.BlockSpec((None, *shape), lambda *_: (l, *idx))


def _norm_kernel(x_ref, g_ref, o_ref):
    o_ref[...] = _rms_rows(x_ref[...], g_ref[...]).astype(BF16)


def _norm(x, g, l, *, tm, out_rows):
    rows = lambda i: (i, 0)
    return pl.pallas_call(
        _norm_kernel,
        grid=(x.shape[0] // tm,),
        in_specs=[pl.BlockSpec((tm, D_MODEL), rows), _layer_block((1, D_MODEL), l, 0, 0)],
        out_specs=pl.BlockSpec((tm, D_MODEL), rows),
        out_shape=jax.ShapeDtypeStruct((out_rows, D_MODEL), BF16),
        compiler_params=_params(("arbitrary",)),
        name="norm",
    )(x, g)


def _qkv_kernel(xn_ref, w_ref, qn_ref, kn_ref, ph_ref, q_ref, k_ref, v_ref, *last_refs):
    z = _dot(xn_ref[...], w_ref[...])
    q = z[:, :BRANCH_W]
    k = z[:, BRANCH_W:BRANCH_W + KV_W]
    v = z[:, BRANCH_W + KV_W:]
    q_sq = (q * q).astype(BF16)
    q_ms = jnp.concatenate(
        [_dot(q_sq[:, c:c + KV_W], ph_ref[...]) for c in range(0, BRANCH_W, KV_W)], axis=1) * (1.0 / HEAD_DIM)
    k_ms = _dot((k * k).astype(BF16), ph_ref[...]) * (1.0 / HEAD_DIM)
    q_ref[...] = (q * lax.rsqrt(q_ms + EPS) * qn_ref[...] * (HEAD_DIM ** -0.5)).astype(BF16)
    kn = k * lax.rsqrt(k_ms + EPS) * kn_ref[...]
    k_ref[...] = kn
    v_ref[...] = v
    if last_refs:
        klast_ref, vlast_ref = last_refs
        tm = xn_ref.shape[0]
        klast_ref[...] = kn[tm - WINDOW:]
        vlast_ref[...] = v[tm - WINDOW:]


def _qkv(xn_all, w_in, l, qn, kn, ph, *, row_block0, n_rows, tm, tiles_per_seq=None):
    nt = n_rows // tm
    const = lambda i: (0, 0)
    rows = lambda i: (i, 0)
    out_specs = [
        pl.BlockSpec((tm, BRANCH_W), rows),
        pl.BlockSpec((tm, KV_W), rows),
        pl.BlockSpec((tm, KV_W), rows),
    ]
    out_shape = [
        jax.ShapeDtypeStruct((n_rows, BRANCH_W), BF16),
        jax.ShapeDtypeStruct((n_rows, KV_W), F32),
        jax.ShapeDtypeStruct((n_rows, KV_W), F32),
    ]
    if tiles_per_seq is not None:
        n_seq = nt // tiles_per_seq
        out_specs += [pl.BlockSpec((WINDOW, KV_W), lambda i: (i // tiles_per_seq, 0))] * 2
        out_shape += [jax.ShapeDtypeStruct((n_seq * WINDOW, KV_W), F32)] * 2
    return pl.pallas_call(
        _qkv_kernel,
        grid=(nt,),
        in_specs=[
            pl.BlockSpec((tm, D_MODEL), lambda i: (row_block0 + i, 0)),
            _layer_block((D_MODEL, WIDE), l, 0, 0),
            pl.BlockSpec((1, BRANCH_W), const),
            pl.BlockSpec((1, KV_W), const),
            pl.BlockSpec((KV_W, KV_W), const),
        ],
        out_specs=out_specs,
        out_shape=out_shape,
        compiler_params=_params(("arbitrary",)),
        name="qkv",
    )(xn_all, w_in, qn, kn, ph)


def _attn_prompt_kernel(sink_ref, q_ref, kp_ref, ko_ref, vp_ref, vo_ref, bias_ref, o_ref, s_scr, p_scr):
    kk = jnp.concatenate([kp_ref[...], ko_ref[...]], axis=0).astype(BF16)
    vv = jnp.concatenate([vp_ref[...], vo_ref[...]], axis=0).astype(BF16)
    group_rows = GQA_GROUP * WINDOW
    for g in range(N_KV_HEADS):
        kg = kk[:, g * HEAD_DIM:(g + 1) * HEAD_DIM]
        heads = range(g * GQA_GROUP, (g + 1) * GQA_GROUP)
        qg = jnp.concatenate([q_ref[:, h * HEAD_DIM:(h + 1) * HEAD_DIM] for h in heads], axis=0)
        s_scr[g * group_rows:(g + 1) * group_rows, :] = _dot_nt(qg, kg)
    sink_terms = []
    for h in range(N_HEADS):
        rows = slice(h * WINDOW, (h + 1) * WINDOW)
        s = s_scr[rows, :] + bias_ref[h]
        sink = sink_ref[h]
        m = jnp.maximum(jnp.max(s, axis=-1, keepdims=True), sink)
        p_scr[rows, :] = jnp.exp(s - m).astype(BF16)
        sink_terms.append(jnp.exp(sink - m))
    ones = jnp.ones((2 * WINDOW, HEAD_DIM), BF16)
    for g in range(N_KV_HEADS):
        p = p_scr[g * group_rows:(g + 1) * group_rows, :]
        o_all = _dot(p, vv[:, g * HEAD_DIM:(g + 1) * HEAD_DIM])
        den_all = _dot(p, ones)
        for i in range(GQA_GROUP):
            h = g * GQA_GROUP + i
            rows = slice(i * WINDOW, (i + 1) * WINDOW)
            o = o_all[rows] / (den_all[rows] + sink_terms[h])
            o_ref[:, h * HEAD_DIM:(h + 1) * HEAD_DIM] = o.astype(BF16)


def _attn_prompt(sinks, q, k, v, bias):
    nb = SEQ // WINDOW
    own = lambda b, j: (b * nb + j, 0)
    prev = lambda b, j: (b * nb + jnp.maximum(j - 1, 0), 0)
    return pl.pallas_call(
        _attn_prompt_kernel,
        grid=(BATCH, nb),
        in_specs=[
            pl.BlockSpec(memory_space=pltpu.SMEM),
            pl.BlockSpec((WINDOW, BRANCH_W), own),
            pl.BlockSpec((WINDOW, KV_W), prev),
            pl.BlockSpec((WINDOW, KV_W), own),
            pl.BlockSpec((WINDOW, KV_W), prev),
            pl.BlockSpec((WINDOW, KV_W), own),
            pl.BlockSpec((None, N_HEADS, WINDOW, 2 * WINDOW), lambda b, j: (jnp.minimum(j, 1), 0, 0, 0)),
        ],
        out_specs=pl.BlockSpec((WINDOW, BRANCH_W), own),
        out_shape=jax.ShapeDtypeStruct((N_ROWS, BRANCH_W), BF16),
        scratch_shapes=[
            pltpu.VMEM((N_HEADS * WINDOW, 2 * WINDOW), F32),
            pltpu.VMEM((N_HEADS * WINDOW, 2 * WINDOW), BF16),
        ],
        compiler_params=_params(("arbitrary", "arbitrary")),
        name="attn_prompt",
    )(sinks, q, k, k, v, v, bias)


def _prompt_bias():
    slopes = jnp.exp2(-8.0 * jnp.arange(1, N_HEADS + 1, dtype=F32) / N_HEADS)
    qi = jnp.arange(WINDOW, dtype=jnp.int32)[:, None]
    ki = jnp.arange(2 * WINDOW, dtype=jnp.int32)[None, :] - WINDOW
    dist = qi - ki
    valid = (dist >= 0) & (dist < WINDOW)
    bias = -slopes[:, None, None] * dist.astype(F32)[None]
    with_prev = jnp.where(valid[None], bias, NEG_INF)
    first = jnp.where((valid & (ki >= 0))[None], bias, NEG_INF)
    return jnp.stack([first, with_prev])


SAMPLE_BT = 16


def _attn_sample_kernel(q_ref, kn_ref, vn_ref, ck_ref, cv_ref, bias_ref, sink_ref, mask_ref,
                        rep_ref, o_ref):
    bt = q_ref.shape[0]
    mask = mask_ref[...][None]
    sink = sink_ref[...][None]
    qe = _dot(q_ref[...].reshape(bt * N_HEADS, HEAD_DIM), rep_ref[...])
    qe = qe.reshape(bt, N_HEADS, KV_W) * mask
    qe_bf = qe.astype(BF16)
    s = jnp.stack([_dot(qe_bf[b], ck_ref[b].astype(BF16)) for b in range(bt)])
    s = s + bias_ref[...][None]
    s_new = jnp.sum(qe * kn_ref[...], axis=-1, keepdims=True)
    m = jnp.maximum(jnp.maximum(jnp.max(s, axis=-1, keepdims=True), s_new), sink)
    p = jnp.exp(s - m)
    p_new = jnp.exp(s_new - m)
    den = jnp.sum(p, axis=-1, keepdims=True) + p_new + jnp.exp(sink - m)
    p_bf = p.astype(BF16)
    of = jnp.stack([_dot_nt(p_bf[b], cv_ref[b].astype(BF16)) for b in range(bt)])
    of = (of + p_new * vn_ref[...]) * mask / den
    o = (of[..., 0:64] + of[..., 64:128]) + (of[..., 128:192] + of[..., 192:256])
    o_ref[...] = o.astype(BF16)


def _attn_sample(q3, k_new, v_new, ck, cv, l, bias, sinks_col, mask, rep):
    nsteps = DEC_BATCH // SAMPLE_BT
    b3 = lambda i: (i, 0, 0)
    c2 = lambda i: (0, 0)
    cache = pl.BlockSpec((None, SAMPLE_BT, KV_W, WINDOW), lambda i: (l, i, 0, 0))
    return pl.pallas_call(
        _attn_sample_kernel,
        grid=(nsteps,),
        in_specs=[
            pl.BlockSpec((SAMPLE_BT, N_HEADS, HEAD_DIM), b3),
            pl.BlockSpec((SAMPLE_BT, 1, KV_W), b3),
            pl.BlockSpec((SAMPLE_BT, 1, KV_W), b3),
            cache,
            cache,
            pl.BlockSpec((N_HEADS, WINDOW), c2),
            pl.BlockSpec((N_HEADS, 1), c2),
            pl.BlockSpec((N_HEADS, KV_W), c2),
            pl.BlockSpec((HEAD_DIM, KV_W), c2),
        ],
        out_specs=pl.BlockSpec((SAMPLE_BT, N_HEADS, HEAD_DIM), b3),
        out_shape=jax.ShapeDtypeStruct((DEC_BATCH, N_HEADS, HEAD_DIM), BF16),
        compiler_params=_params(("arbitrary",)),
        name="attn_sample",
    )(q3, k_new, v_new, ck, cv, bias, sinks_col, mask, rep)


def _sample_bias(n_buf):
    slopes = jnp.exp2(-8.0 * jnp.arange(1, N_HEADS + 1, dtype=F32) / N_HEADS)
    dist = n_buf - jnp.arange(n_buf, dtype=jnp.int32)
    bias = -slopes[:, None] * dist.astype(F32)[None, :]
    return jnp.where((dist < WINDOW)[None, :], bias, NEG_INF)


def _bch_cols(wa_ref, wb_ref, seg, c):
    r = seg * D_CONV + c * TN
    return wa_ref[:, r:r + TN] if r < WIDE else wb_ref[:, r - WIDE:r - WIDE + TN]


def _conv_prompt_kernel(xn_ref, wa_ref, wb_ref, cw_ref, ob_ref, nc_ref, zbuf, carry):
    t = pl.program_id(1)
    tm = xn_ref.shape[0]
    xn = xn_ref[...]

    @pl.when(t == 0)
    def _():
        zbuf[0:8, :] = jnp.zeros((8, D_CONV), F32)

    @pl.when(t > 0)
    def _():
        zbuf[0:8, :] = carry[...]

    for c in range(D_CONV // TN):
        cols = slice(c * TN, (c + 1) * TN)
        zc = _dot(xn, _bch_cols(wa_ref, wb_ref, 1, c)) * _dot(xn, _bch_cols(wa_ref, wb_ref, 2, c))
        zbuf[8:8 + tm, cols] = zc
        cw = cw_ref[:, cols]
        y = cw[0:1] * zbuf[6:6 + tm, cols] + cw[1:2] * zbuf[7:7 + tm, cols] + cw[2:3] * zc
        ob_ref[:, cols] = (_dot(xn, _bch_cols(wa_ref, wb_ref, 0, c)) * y).astype(BF16)
    carry[...] = zbuf[tm:tm + 8, :]
    nc_ref[...] = zbuf[tm + 6:tm + 8, :]


def _conv_prompt(xn_all, w_in, l, conv_w):
    nt = SEQ // TM_SEQ
    return pl.pallas_call(
        _conv_prompt_kernel,
        grid=(BATCH, nt),
        in_specs=[
            pl.BlockSpec((TM_SEQ, D_MODEL), lambda b, t: (b * nt + t, 0)),
            _layer_block((D_MODEL, WIDE), l, 0, 1),
            _layer_block((D_MODEL, WIDE), l, 0, 2),
            _layer_block((CONV_WIDTH, D_CONV), l, 0, 0),
        ],
        out_specs=[
            pl.BlockSpec((TM_SEQ, D_CONV), lambda b, t: (b * nt + t, 0)),
            pl.BlockSpec((None, CONV_WIDTH - 1, D_CONV), lambda b, t: (b, 0, 0)),
        ],
        out_shape=[
            jax.ShapeDtypeStruct((N_ROWS, D_CONV), BF16),
            jax.ShapeDtypeStruct((BATCH, CONV_WIDTH - 1, D_CONV), F32),
        ],
        scratch_shapes=[
            pltpu.VMEM((TM_SEQ + 8, D_CONV), F32),
            pltpu.VMEM((8, D_CONV), F32),
        ],
        compiler_params=_params(("arbitrary", "arbitrary")),
        name="conv_prompt",
    )(xn_all, w_in, w_in, conv_w)


def _conv_sample_kernel(xn_ref, wa_ref, wb_ref, cw_ref, cb0_ref, cb1_ref, ob_ref, zc_ref):
    xn = xn_ref[...]
    for c in range(D_CONV // TN):
        cols = slice(c * TN, (c + 1) * TN)
        zc = _dot(xn, _bch_cols(wa_ref, wb_ref, 1, c)) * _dot(xn, _bch_cols(wa_ref, wb_ref, 2, c))
        cw = cw_ref[:, cols]
        y = cw[0:1] * cb0_ref[:, cols] + cw[1:2] * cb1_ref[:, cols] + cw[2:3] * zc
        ob_ref[:, cols] = (_dot(xn, _bch_cols(wa_ref, wb_ref, 0, c)) * y).astype(BF16)
        zc_ref[:, cols] = zc


def _conv_sample(xn_all, w_in, l, conv_w, cb0, cb1):
    c2 = lambda i: (0, 0)
    return pl.pallas_call(
        _conv_sample_kernel,
        grid=(1,),
        in_specs=[
            pl.BlockSpec((DEC_BATCH, D_MODEL), lambda i: (SAMPLE_BLOCK, 0)),
            _layer_block((D_MODEL, WIDE), l, 0, 1),
            _layer_block((D_MODEL, WIDE), l, 0, 2),
            _layer_block((CONV_WIDTH, D_CONV), l, 0, 0),
            pl.BlockSpec((DEC_BATCH, D_CONV), c2),
            pl.BlockSpec((DEC_BATCH, D_CONV), c2),
        ],
        out_specs=[pl.BlockSpec((DEC_BATCH, D_CONV), c2), pl.BlockSpec((DEC_BATCH, D_CONV), c2)],
        out_shape=[
            jax.ShapeDtypeStruct((DEC_BATCH, D_CONV), BF16),
            jax.ShapeDtypeStruct((DEC_BATCH, D_CONV), F32),
        ],
        compiler_params=_params(("arbitrary",)),
        name="conv_sample",
    )(xn_all, w_in, w_in, conv_w, cb0, cb1)


HALF_V = WIDE - D_GMLP


def _gmlp_uv(xn, wa_ref, wb_ref, vn_ref):
    u = _gelu(_dot(xn, wa_ref[:, :D_GMLP]))
    v_lo = _gelu(_dot(xn, wa_ref[:, D_GMLP:]))
    v_hi = _gelu(_dot(xn, wb_ref[:, :D_GMLP - HALF_V]))
    ms = (jnp.sum(v_lo * v_lo, axis=-1, keepdims=True)
          + jnp.sum(v_hi * v_hi, axis=-1, keepdims=True)) * (1.0 / D_GMLP)
    r = lax.rsqrt(ms + EPS)
    return u, v_lo * r * vn_ref[:, :HALF_V], v_hi * r * vn_ref[:, HALF_V:]


def _gmlp_prompt_kernel(xn_ref, wa_ref, wb_ref, vn_ref, ws_ref, bs_ref, oc_ref, u_scr, v_scr):
    tm = xn_ref.shape[0]
    u, v_lo, v_hi = _gmlp_uv(xn_ref[...], wa_ref, wb_ref, vn_ref)
    u_scr[...] = u
    v_scr[:, :HALF_V] = v_lo.astype(BF16)
    v_scr[:, HALF_V:] = v_hi.astype(BF16)
    row = lax.broadcasted_iota(jnp.int32, (CHUNK, CHUNK), 0)
    col = lax.broadcasted_iota(jnp.int32, (CHUNK, CHUNK), 1)
    for grp in range(N_SPATIAL_GROUPS):
        w = jnp.where(col <= row, ws_ref[grp], 0.0).astype(BF16)
        cols = slice(grp * SPATIAL_GROUP_W, (grp + 1) * SPATIAL_GROUP_W)
        for ch in range(tm // CHUNK):
            rows = slice(ch * CHUNK, (ch + 1) * CHUNK)
            mix = _dot(w, v_scr[rows, cols]) + bs_ref[grp]
            oc_ref[rows, cols] = (u_scr[rows, cols] * mix).astype(BF16)


def _gmlp_prompt(xn_all, w_in, l, v_norm, w_s, b_s_wide):
    nt = N_PROMPT // TM_SEQ
    return pl.pallas_call(
        _gmlp_prompt_kernel,
        grid=(nt,),
        in_specs=[
            pl.BlockSpec((TM_SEQ, D_MODEL), lambda i: (i, 0)),
            _layer_block((D_MODEL, WIDE), l, 0, 3),
            _layer_block((D_MODEL, WIDE), l, 0, 4),
            _layer_block((1, D_GMLP), l, 0, 0),
            _layer_block((N_SPATIAL_GROUPS, CHUNK, CHUNK), l, 0, 0, 0),
            _layer_block((N_SPATIAL_GROUPS, CHUNK, SPATIAL_GROUP_W), l, 0, 0, 0),
        ],
        out_specs=pl.BlockSpec((TM_SEQ, D_GMLP), lambda i: (i, 0)),
        out_shape=jax.ShapeDtypeStruct((N_ROWS, D_GMLP), BF16),
        scratch_shapes=[pltpu.VMEM((TM_SEQ, D_GMLP), F32), pltpu.VMEM((TM_SEQ, D_GMLP), BF16)],
        compiler_params=_params(("arbitrary",)),
        name="gmlp_prompt",
    )(xn_all, w_in, w_in, v_norm, w_s, b_s_wide)


def _gmlp_sample_kernel(xn_ref, wa_ref, wb_ref, vn_ref, ws0_ref, bs0_ref, oc_ref, vg_ref):
    u, v_lo, v_hi = _gmlp_uv(xn_ref[...], wa_ref, wb_ref, vn_ref)
    vg_ref[:, :HALF_V] = v_lo
    vg_ref[:, HALF_V:] = v_hi
    oc_ref[...] = (u * (ws0_ref[...] * vg_ref[...] + bs0_ref[...])).astype(BF16)


def _gmlp_sample(xn_all, w_in, l, v_norm, ws0, bs0):
    c2 = lambda i: (0, 0)
    return pl.pallas_call(
        _gmlp_sample_kernel,
        grid=(1,),
        in_specs=[
            pl.BlockSpec((DEC_BATCH, D_MODEL), lambda i: (SAMPLE_BLOCK, 0)),
            _layer_block((D_MODEL, WIDE), l, 0, 3),
            _layer_block((D_MODEL, WIDE), l, 0, 4),
            _layer_block((1, D_GMLP), l, 0, 0),
            pl.BlockSpec((1, D_GMLP), c2),
            pl.BlockSpec((1, D_GMLP), c2),
        ],
        out_specs=[pl.BlockSpec((DEC_BATCH, D_GMLP), c2), pl.BlockSpec((DEC_BATCH, D_GMLP), c2)],
        out_shape=[
            jax.ShapeDtypeStruct((DEC_BATCH, D_GMLP), BF16),
            jax.ShapeDtypeStruct((DEC_BATCH, D_GMLP), F32),
        ],
        compiler_params=_params(("arbitrary",)),
        name="gmlp_sample",
    )(xn_all, w_in, w_in, v_norm, ws0, bs0)


def _merge_kernel(xn_ref, wg0_ref, wg1_ref, wg2_ref, bg_ref, oa_ref, ob_ref, oc_ref, wb_ref, m_ref):
    xn = xn_ref[...]
    bg = bg_ref[...]
    acc = None
    for i, (wg_ref, o_ref) in enumerate(((wg0_ref, oa_ref), (wg1_ref, ob_ref), (wg2_ref, oc_ref))):
        gate = jax.nn.sigmoid(_dot(xn, wg_ref[...]) + bg[i:i + 1])
        term = gate * _dot(o_ref[...], wb_ref[i])
        acc = term if acc is None else acc + term
    m_ref[...] = acc.astype(BF16)


TM_MERGE = 832


def _merge(xn_all, w_in, l, b_gate, o_a, o_b, o_c, w_branch):
    nt = N_ROWS // TM_MERGE
    nn = D_MODEL // TN
    g0 = OFF_G // TN
    rows = lambda i, j: (i, 0)
    gate_w = lambda br: pl.BlockSpec((None, D_MODEL, TN), lambda i, j: (l, 0, g0 + br * nn + j))
    return pl.pallas_call(
        _merge_kernel,
        grid=(nt, nn),
        in_specs=[
            pl.BlockSpec((TM_MERGE, D_MODEL), rows),
            gate_w(0), gate_w(1), gate_w(2),
            pl.BlockSpec((None, N_BRANCHES, TN), lambda i, j: (l, 0, j)),
            pl.BlockSpec((TM_MERGE, BRANCH_W), rows),
            pl.BlockSpec((TM_MERGE, BRANCH_W), rows),
            pl.BlockSpec((TM_MERGE, BRANCH_W), rows),
            pl.BlockSpec((None, N_BRANCHES, BRANCH_W, TN), lambda i, j: (l, 0, 0, j)),
        ],
        out_specs=pl.BlockSpec((TM_MERGE, TN), lambda i, j: (i, j)),
        out_shape=jax.ShapeDtypeStruct((N_ROWS, D_MODEL), BF16),
        compiler_params=_params(("arbitrary", "arbitrary")),
        name="merge",
    )(xn_all, w_in, w_in, w_in, b_gate, o_a, o_b, o_c, w_branch)


OUT_SPLIT = 2


def _out_proj_kernel(m_ref, w_ref, x_ref, g_ref, x1_ref, xn_ref):
    sub = m_ref.shape[0] // OUT_SPLIT
    for s in range(OUT_SPLIT):
        rows = slice(s * sub, (s + 1) * sub)
        x1 = x_ref[rows, :] + _dot(m_ref[rows, :], w_ref[...])
        x1_ref[rows, :] = x1
        xn_ref[rows, :] = _rms_rows(x1, g_ref[...]).astype(BF16)


def _out_proj(m, w_out, l, x, g_ffn, *, m_row_block0, tm, out_rows):
    rows = lambda i: (i, 0)
    return pl.pallas_call(
        _out_proj_kernel,
        grid=(x.shape[0] // tm,),
        in_specs=[
            pl.BlockSpec((tm, D_MODEL), lambda i: (m_row_block0 + i, 0)),
            _layer_block((D_MODEL, D_MODEL), l, 0, 0),
            pl.BlockSpec((tm, D_MODEL), rows),
            _layer_block((1, D_MODEL), l, 0, 0),
        ],
        out_specs=[pl.BlockSpec((tm, D_MODEL), rows), pl.BlockSpec((tm, D_MODEL), rows)],
        out_shape=[
            jax.ShapeDtypeStruct((out_rows, D_MODEL), F32),
            jax.ShapeDtypeStruct((out_rows, D_MODEL), BF16),
        ],
        compiler_params=_params(("arbitrary",)),
        name="out_proj",
    )(m, w_out, x, g_ffn)


TM_UP = 1664
UP_SPLIT = 4


def _ffn_up_kernel(xn_ref, wgate_ref, wup_ref, h_ref, wgate_scr, wup_scr):
    @pl.when(pl.program_id(1) == 0)
    def _():
        wgate_scr[...] = wgate_ref[...].astype(BF16)
        wup_scr[...] = wup_ref[...].astype(BF16)

    sub = xn_ref.shape[0] // UP_SPLIT
    for s in range(UP_SPLIT):
        rows = slice(s * sub, (s + 1) * sub)
        xn = xn_ref[rows, :]
        gate = _dot(xn, wgate_scr[...])
        h_ref[rows, :] = (gate * jax.nn.sigmoid(gate) * _dot(xn, wup_scr[...])).astype(BF16)


def _ffn_up(xn, w_gate_up, l):
    nt = N_ROWS // TM_UP
    nn = D_FF // TN
    return pl.pallas_call(
        _ffn_up_kernel,
        grid=(nn, nt),
        in_specs=[
            pl.BlockSpec((TM_UP, D_MODEL), lambda j, i: (i, 0)),
            pl.BlockSpec((None, D_MODEL, TN), lambda j, i: (l, 0, j)),
            pl.BlockSpec((None, D_MODEL, TN), lambda j, i: (l, 0, nn + j)),
        ],
        out_specs=pl.BlockSpec((TM_UP, TN), lambda j, i: (i, j)),
        out_shape=jax.ShapeDtypeStruct((N_ROWS, D_FF), BF16),
        scratch_shapes=[pltpu.VMEM((D_MODEL, TN), BF16), pltpu.VMEM((D_MODEL, TN), BF16)],
        compiler_params=_params(("arbitrary", "arbitrary")),
        name="ffn_up",
    )(xn, w_gate_up, w_gate_up)


def _ffn_down_kernel(h_ref, w_ref, x_ref, o_ref, w_scr):
    @pl.when(pl.program_id(1) == 0)
    def _():
        w_scr[...] = w_ref[...].astype(BF16)

    o_ref[...] = x_ref[...] + _dot(h_ref[...], w_scr[...])


def _ffn_down(h, w_down, l, x1, *, row_block0, n_rows, tm):
    nt = n_rows // tm
    nn = D_MODEL // TN
    return pl.pallas_call(
        _ffn_down_kernel,
        grid=(nn, nt),
        in_specs=[
            pl.BlockSpec((tm, D_FF), lambda j, i: (row_block0 + i, 0)),
            pl.BlockSpec((None, D_FF, TN), lambda j, i: (l, 0, j)),
            pl.BlockSpec((tm, TN), lambda j, i: (row_block0 + i, j)),
        ],
        out_specs=pl.BlockSpec((tm, TN), lambda j, i: (i, j)),
        out_shape=jax.ShapeDtypeStruct((n_rows, D_MODEL), F32),
        scratch_shapes=[pltpu.VMEM((D_FF, TN), BF16)],
        compiler_params=_params(("arbitrary", "arbitrary")),
        name="ffn_down",
    )(h, w_down, x1)


def _block_diag_ones(width):
    head = np.arange(width) // HEAD_DIM
    return jnp.asarray(head[:, None] == head[None, :], dtype=BF16)


def _fill_sample_rows(o_prompt, o_sample):
    return lax.dynamic_update_slice(o_prompt, o_sample, (N_PROMPT, 0))


def _layer(x, l, cache_k, cache_v, state_conv, p, const):
    first, last = l == 0, l == DEPTH - 1
    if first:
        x_p, x_s = x
        xn = _fill_sample_rows(_norm(x_p, p["norm_mix"], l, tm=TM_SEQ, out_rows=N_ROWS),
                               _norm(x_s, p["norm_mix"], l, tm=DEC_BATCH, out_rows=DEC_BATCH))
    else:
        xn = _norm(x, p["norm_mix"], l, tm=TM_ROW, out_rows=N_ROWS)
    w_in = p["w_in"]
    qn = jnp.tile(p["q_norm"][l], N_HEADS).reshape(1, BRANCH_W)
    kn = jnp.tile(p["k_norm"][l], N_KV_HEADS).reshape(1, KV_W)
    sinks = p["sinks"][l]

    q_p, k_p, v_p, klast_p, vlast_p = _qkv(
        xn, w_in, l, qn, kn, const["ph"],
        row_block0=0, n_rows=N_PROMPT, tm=TM_SEQ, tiles_per_seq=SEQ // TM_SEQ)
    q_s, k_s, v_s = _qkv(
        xn, w_in, l, qn, kn, const["ph"],
        row_block0=SAMPLE_BLOCK, n_rows=DEC_BATCH, tm=DEC_BATCH)
    oa_p = _attn_prompt(sinks, q_p, k_p, v_p, const["bias_p"])
    oa_s = _attn_sample(
        q_s.reshape(DEC_BATCH, N_HEADS, HEAD_DIM),
        k_s.reshape(DEC_BATCH, 1, KV_W), v_s.reshape(DEC_BATCH, 1, KV_W),
        cache_k, cache_v, l,
        const["bias_s"], sinks.reshape(N_HEADS, 1), const["mask_s"], const["rep_s"],
    ).reshape(DEC_BATCH, BRANCH_W)

    ob_p, nc_p = _conv_prompt(xn, w_in, l, p["conv_w"])
    ob_s, zc_s = _conv_sample(xn, w_in, l, p["conv_w"], state_conv[l, :, 0], state_conv[l, :, 1])

    w_s = p["w_spatial"]
    b_s = p["b_spatial"]
    bs_wide = jnp.broadcast_to(b_s[:, :, :, None], (DEPTH, N_SPATIAL_GROUPS, CHUNK, SPATIAL_GROUP_W))
    oc_p = _gmlp_prompt(xn, w_in, l, p["v_norm"], w_s, bs_wide)
    ws0 = jnp.repeat(w_s[l, :, 0, 0], SPATIAL_GROUP_W).reshape(1, D_GMLP)
    bs0 = jnp.repeat(b_s[l, :, 0], SPATIAL_GROUP_W).reshape(1, D_GMLP)
    oc_s, vg_s = _gmlp_sample(xn, w_in, l, p["v_norm"], ws0, bs0)

    o_a = _fill_sample_rows(oa_p, oa_s)
    o_b = _fill_sample_rows(ob_p, ob_s)
    o_c = _fill_sample_rows(oc_p, oc_s)
    m = _merge(xn, w_in, l, p["b_gate"], o_a, o_b, o_c, p["w_branch"])
    if first:
        x1_p, xn2_p = _out_proj(m, p["w_out"], l, x_p, p["norm_ffn"], m_row_block0=0, tm=TM_SEQ, out_rows=N_ROWS)
        x1_s, xn2_s = _out_proj(m, p["w_out"], l, x_s, p["norm_ffn"], m_row_block0=SAMPLE_BLOCK,
                                tm=DEC_BATCH, out_rows=DEC_BATCH)
        x1, xn2 = _fill_sample_rows(x1_p, x1_s), _fill_sample_rows(xn2_p, xn2_s)
    else:
        x1, xn2 = _out_proj(m, p["w_out"], l, x, p["norm_ffn"], m_row_block0=0, tm=TM_ROW, out_rows=N_ROWS)
    h = _ffn_up(xn2, p["w_gate_up"], l)
    if last:
        x2 = (_ffn_down(h, p["w_down"], l, x1, row_block0=0, n_rows=N_PROMPT, tm=TM_SEQ),
              _ffn_down(h, p["w_down"], l, x1, row_block0=SAMPLE_BLOCK, n_rows=DEC_BATCH, tm=DEC_BATCH))
    else:
        x2 = _ffn_down(h, p["w_down"], l, x1, row_block0=0, n_rows=N_ROWS, tm=TM_ROW)

    new_k_p = klast_p.reshape(BATCH, WINDOW, N_KV_HEADS, HEAD_DIM)
    new_v_p = vlast_p.reshape(BATCH, WINDOW, N_KV_HEADS, HEAD_DIM)
    new_k_s = k_s.reshape(DEC_BATCH, 1, N_KV_HEADS, HEAD_DIM)
    new_v_s = v_s.reshape(DEC_BATCH, 1, N_KV_HEADS, HEAD_DIM)
    new_conv_s = jnp.stack([state_conv[l, :, 1], zc_s], axis=1)
    return x2, (new_k_p, new_v_p, nc_p, new_k_s, new_v_s, new_conv_s, vg_s.reshape(DEC_BATCH, 1, D_GMLP))


def kernel(x_prompt, x_sample, cache_k, cache_v, state_conv, norm_mix, w_in, b_gate, q_norm, k_norm,
           sinks, conv_w, v_norm, w_spatial, b_spatial, w_branch, w_out, norm_ffn, w_gate_up, w_down):
    assert min(WINDOW, SEQ) == WINDOW and SEQ % TM_SEQ == 0 and TM_SEQ >= WINDOW
    n_buf = cache_k.shape[2]
    assert n_buf == WINDOW
    cache_k = jnp.transpose(cache_k, (0, 1, 3, 4, 2)).reshape(DEPTH, DEC_BATCH, KV_W, n_buf)
    cache_v = jnp.transpose(cache_v, (0, 1, 3, 4, 2)).reshape(DEPTH, DEC_BATCH, KV_W, n_buf)
    p = {
        "norm_mix": norm_mix.reshape(DEPTH, 1, D_MODEL),
        "norm_ffn": norm_ffn.reshape(DEPTH, 1, D_MODEL),
        "v_norm": v_norm.reshape(DEPTH, 1, D_GMLP),
        "w_in": w_in.astype(BF16),
        "w_branch": w_branch.astype(BF16),
        "w_out": w_out.astype(BF16),
        "w_gate_up": w_gate_up,
        "w_down": w_down,
        "b_gate": b_gate, "q_norm": q_norm, "k_norm": k_norm, "sinks": sinks, "conv_w": conv_w,
        "w_spatial": w_spatial, "b_spatial": b_spatial,
    }
    kv_of_col = np.arange(KV_W) // HEAD_DIM
    kv_of_head = np.arange(N_HEADS) // GQA_GROUP
    const = {
        "ph": _block_diag_ones(KV_W),
        "bias_p": _prompt_bias(),
        "bias_s": _sample_bias(n_buf),
        "mask_s": jnp.asarray(kv_of_head[:, None] == kv_of_col[None, :], dtype=F32),
        "rep_s": jnp.asarray(np.tile(np.eye(HEAD_DIM), (1, N_KV_HEADS)), dtype=BF16),
    }
    x = (x_prompt.reshape(N_PROMPT, D_MODEL), x_sample.reshape(DEC_BATCH, D_MODEL))
    per_layer = []
    for l in range(DEPTH):
        x, outs = _layer(x, l, cache_k, cache_v, state_conv, p, const)
        per_layer.append(outs)
    stacked = [jnp.stack([per_layer[l][i] for l in range(DEPTH)]) for i in range(7)]
    y_prompt, y_sample = x
    return (y_prompt.reshape(BATCH, SEQ, D_MODEL), y_sample.reshape(DEC_BATCH, 1, D_MODEL), *stacked)
```

```python
import jax
import jax.numpy as jnp
import numpy as np
from jax import lax
from jax.experimental import pallas as pl
from jax.experimental.pallas import tpu as pltpu

D_MODEL = 2048
BATCH = 4
SEQ = 2048
DEPTH = 2
DEC_BATCH = 128
BRANCH_W = 1024
HEAD_DIM = 64
N_HEADS = 16
N_KV_HEADS = 4
GQA_GROUP = 4
KV_W = 256
WINDOW = 128
D_CONV = 1024
CONV_WIDTH = 3
D_GMLP = 1024
CHUNK = 128
N_SPATIAL_GROUPS = 8
SPATIAL_GROUP_W = 128
N_BRANCHES = 3
D_FF = 5632
EPS = 1e-6
NEG_INF = -1e30

N_PROMPT = BATCH * SEQ
N_ROWS = N_PROMPT + DEC_BATCH
SAMPLE_BLOCK = N_PROMPT // DEC_BATCH

WIDE = 1536
OFF_G = 6656

TM_SEQ = 512
TM_ROW = 640
TN = 512
VMEM_LIMIT = 56 * 1024 * 1024

F32 = jnp.float32
BF16 = jnp.bfloat16


def _params(sem):
    return pltpu.CompilerParams(dimension_semantics=sem, vmem_limit_bytes=VMEM_LIMIT)


def _rms_rows(x, g):
    ms = jnp.mean(x * x, axis=-1, keepdims=True)
    return x * lax.rsqrt(ms + EPS) * g


def _dot(a, b):
    return jnp.dot(a, b, preferred_element_type=F32)


def _dot_nt(a, b):
    return lax.dot_general(a, b, (((1,), (1,)), ((), ())), preferred_element_type=F32)


def _gelu(x):
    return 0.5 * x * (1.0 + jnp.tanh(np.sqrt(2.0 / np.pi).astype(np.float32) * (x + 0.044715 * (x * x * x))))


def _layer_block(shape, l, *idx):
    return pl.BlockSpec((None, *shape), lambda *_: (l, *idx))


QKV_SPLIT = 2


def _qkv_kernel(x_ref, g_ref, w_ref, qn_ref, kn_ref, ph_ref, xn_ref, q_ref, k_ref, v_ref, *last_refs):
    tm = x_ref.shape[0]
    sub = tm // QKV_SPLIT
    for s in range(QKV_SPLIT):
        rows = slice(s * sub, (s + 1) * sub)
        xn = _rms_rows(x_ref[rows, :], g_ref[...]).astype(BF16)
        xn_ref[rows, :] = xn
        z = _dot(xn, w_ref[...])
        q = z[:, :BRANCH_W]
        k = z[:, BRANCH_W:BRANCH_W + KV_W]
        v = z[:, BRANCH_W + KV_W:]
        q_sq = (q * q).astype(BF16)
        q_ms = jnp.concatenate(
            [_dot(q_sq[:, c:c + KV_W], ph_ref[...]) for c in range(0, BRANCH_W, KV_W)], axis=1) * (1.0 / HEAD_DIM)
        k_ms = _dot((k * k).astype(BF16), ph_ref[...]) * (1.0 / HEAD_DIM)
        q_ref[rows, :] = (q * lax.rsqrt(q_ms + EPS) * qn_ref[...] * (HEAD_DIM ** -0.5)).astype(BF16)
        kn = k * lax.rsqrt(k_ms + EPS) * kn_ref[...]
        k_ref[rows, :] = kn
        v_ref[rows, :] = v
        if last_refs and s == QKV_SPLIT - 1:
            klast_ref, vlast_ref = last_refs
            klast_ref[...] = kn[sub - WINDOW:]
            vlast_ref[...] = v[sub - WINDOW:]


def _qkv(x, g, w_in, l, qn, kn, ph, *, row_block0, n_rows, tm, xn_rows, tiles_per_seq=None):
    nt = n_rows // tm
    const = lambda i: (0, 0)
    rows = lambda i: (i, 0)
    out_specs = [
        pl.BlockSpec((tm, D_MODEL), rows),
        pl.BlockSpec((tm, BRANCH_W), rows),
        pl.BlockSpec((tm, KV_W), rows),
        pl.BlockSpec((tm, KV_W), rows),
    ]
    out_shape = [
        jax.ShapeDtypeStruct((xn_rows, D_MODEL), BF16),
        jax.ShapeDtypeStruct((n_rows, BRANCH_W), BF16),
        jax.ShapeDtypeStruct((n_rows, KV_W), F32),
        jax.ShapeDtypeStruct((n_rows, KV_W), F32),
    ]
    if tiles_per_seq is not None:
        n_seq = nt // tiles_per_seq
        out_specs += [pl.BlockSpec((WINDOW, KV_W), lambda i: (i // tiles_per_seq, 0))] * 2
        out_shape += [jax.ShapeDtypeStruct((n_seq * WINDOW, KV_W), F32)] * 2
    return pl.pallas_call(
        _qkv_kernel,
        grid=(nt,),
        in_specs=[
            pl.BlockSpec((tm, D_MODEL), lambda i: (row_block0 + i, 0)),
            _layer_block((1, D_MODEL), l, 0, 0),
            _layer_block((D_MODEL, WIDE), l, 0, 0),
            pl.BlockSpec((1, BRANCH_W), const),
            pl.BlockSpec((1, KV_W), const),
            pl.BlockSpec((KV_W, KV_W), const),
        ],
        out_specs=out_specs,
        out_shape=out_shape,
        compiler_params=_params(("arbitrary",)),
        name="qkv",
    )(x, g, w_in, qn, kn, ph)


def _attn_prompt_kernel(sink_ref, q_ref, kp_ref, ko_ref, vp_ref, vo_ref, bias_ref, o_ref, s_scr, p_scr):
    kk = jnp.concatenate([kp_ref[...], ko_ref[...]], axis=0).astype(BF16)
    vv = jnp.concatenate([vp_ref[...], vo_ref[...]], axis=0).astype(BF16)
    group_rows = GQA_GROUP * WINDOW
    for g in range(N_KV_HEADS):
        kg = kk[:, g * HEAD_DIM:(g + 1) * HEAD_DIM]
        heads = range(g * GQA_GROUP, (g + 1) * GQA_GROUP)
        qg = jnp.concatenate([q_ref[:, h * HEAD_DIM:(h + 1) * HEAD_DIM] for h in heads], axis=0)
        s_scr[g * group_rows:(g + 1) * group_rows, :] = _dot_nt(qg, kg)
    sink_terms = []
    for h in range(N_HEADS):
        rows = slice(h * WINDOW, (h + 1) * WINDOW)
        s = s_scr[rows, :] + bias_ref[h]
        sink = sink_ref[h]
        m = jnp.maximum(jnp.max(s, axis=-1, keepdims=True), sink)
        p_scr[rows, :] = jnp.exp(s - m).astype(BF16)
        sink_terms.append(jnp.exp(sink - m))
    ones = jnp.ones((2 * WINDOW, HEAD_DIM), BF16)
    for g in range(N_KV_HEADS):
        p = p_scr[g * group_rows:(g + 1) * group_rows, :]
        o_all = _dot(p, vv[:, g * HEAD_DIM:(g + 1) * HEAD_DIM])
        den_all = _dot(p, ones)
        for i in range(GQA_GROUP):
            h = g * GQA_GROUP + i
            rows = slice(i * WINDOW, (i + 1) * WINDOW)
            o = o_all[rows] / (den_all[rows] + sink_terms[h])
            o_ref[:, h * HEAD_DIM:(h + 1) * HEAD_DIM] = o.astype(BF16)


def _attn_prompt(sinks, q, k, v, bias):
    nb = SEQ // WINDOW
    own = lambda b, j: (b * nb + j, 0)
    prev = lambda b, j: (b * nb + jnp.maximum(j - 1, 0), 0)
    return pl.pallas_call(
        _attn_prompt_kernel,
        grid=(BATCH, nb),
        in_specs=[
            pl.BlockSpec(memory_space=pltpu.SMEM),
            pl.BlockSpec((WINDOW, BRANCH_W), own),
            pl.BlockSpec((WINDOW, KV_W), prev),
            pl.BlockSpec((WINDOW, KV_W), own),
            pl.BlockSpec((WINDOW, KV_W), prev),
            pl.BlockSpec((WINDOW, KV_W), own),
            pl.BlockSpec((None, N_HEADS, WINDOW, 2 * WINDOW), lambda b, j: (jnp.minimum(j, 1), 0, 0, 0)),
        ],
        out_specs=pl.BlockSpec((WINDOW, BRANCH_W), own),
        out_shape=jax.ShapeDtypeStruct((N_ROWS, BRANCH_W), BF16),
        scratch_shapes=[
            pltpu.VMEM((N_HEADS * WINDOW, 2 * WINDOW), F32),
            pltpu.VMEM((N_HEADS * WINDOW, 2 * WINDOW), BF16),
        ],
        compiler_params=_params(("arbitrary", "arbitrary")),
        name="attn_prompt",
    )(sinks, q, k, k, v, v, bias)


def _prompt_bias():
    slopes = jnp.exp2(-8.0 * jnp.arange(1, N_HEADS + 1, dtype=F32) / N_HEADS)
    qi = jnp.arange(WINDOW, dtype=jnp.int32)[:, None]
    ki = jnp.arange(2 * WINDOW, dtype=jnp.int32)[None, :] - WINDOW
    dist = qi - ki
    valid = (dist >= 0) & (dist < WINDOW)
    bias = -slopes[:, None, None] * dist.astype(F32)[None]
    with_prev = jnp.where(valid[None], bias, NEG_INF)
    first = jnp.where((valid & (ki >= 0))[None], bias, NEG_INF)
    return jnp.stack([first, with_prev])


SAMPLE_BT = 16


def _attn_sample_kernel(q_ref, kn_ref, vn_ref, ck_ref, cv_ref, bias_ref, sink_ref, mask_ref,
                        rep_ref, o_ref):
    bt = q_ref.shape[0]
    mask = mask_ref[...][None]
    sink = sink_ref[...][None]
    qe = _dot(q_ref[...].reshape(bt * N_HEADS, HEAD_DIM), rep_ref[...])
    qe = qe.reshape(bt, N_HEADS, KV_W) * mask
    qe_bf = qe.astype(BF16)
    s = jnp.stack([_dot(qe_bf[b], ck_ref[b].astype(BF16)) for b in range(bt)])
    s = s + bias_ref[...][None]
    s_new = jnp.sum(qe * kn_ref[...], axis=-1, keepdims=True)
    m = jnp.maximum(jnp.maximum(jnp.max(s, axis=-1, keepdims=True), s_new), sink)
    p = jnp.exp(s - m)
    p_new = jnp.exp(s_new - m)
    den = jnp.sum(p, axis=-1, keepdims=True) + p_new + jnp.exp(sink - m)
    p_bf = p.astype(BF16)
    of = jnp.stack([_dot_nt(p_bf[b], cv_ref[b].astype(BF16)) for b in range(bt)])
    of = (of + p_new * vn_ref[...]) * mask / den
    o = (of[..., 0:64] + of[..., 64:128]) + (of[..., 128:192] + of[..., 192:256])
    o_ref[...] = o.astype(BF16)


def _attn_sample(q3, k_new, v_new, ck, cv, l, bias, sinks_col, mask, rep):
    nsteps = DEC_BATCH // SAMPLE_BT
    b3 = lambda i: (i, 0, 0)
    c2 = lambda i: (0, 0)
    cache = pl.BlockSpec((None, SAMPLE_BT, KV_W, WINDOW), lambda i: (l, i, 0, 0))
    return pl.pallas_call(
        _attn_sample_kernel,
        grid=(nsteps,),
        in_specs=[
            pl.BlockSpec((SAMPLE_BT, N_HEADS, HEAD_DIM), b3),
            pl.BlockSpec((SAMPLE_BT, 1, KV_W), b3),
            pl.BlockSpec((SAMPLE_BT, 1, KV_W), b3),
            cache,
            cache,
            pl.BlockSpec((N_HEADS, WINDOW), c2),
            pl.BlockSpec((N_HEADS, 1), c2),
            pl.BlockSpec((N_HEADS, KV_W), c2),
            pl.BlockSpec((HEAD_DIM, KV_W), c2),
        ],
        out_specs=pl.BlockSpec((SAMPLE_BT, N_HEADS, HEAD_DIM), b3),
        out_shape=jax.ShapeDtypeStruct((DEC_BATCH, N_HEADS, HEAD_DIM), BF16),
        compiler_params=_params(("arbitrary",)),
        name="attn_sample",
    )(q3, k_new, v_new, ck, cv, bias, sinks_col, mask, rep)


def _sample_bias(n_buf):
    slopes = jnp.exp2(-8.0 * jnp.arange(1, N_HEADS + 1, dtype=F32) / N_HEADS)
    dist = n_buf - jnp.arange(n_buf, dtype=jnp.int32)
    bias = -slopes[:, None] * dist.astype(F32)[None, :]
    return jnp.where((dist < WINDOW)[None, :], bias, NEG_INF)


def _bch_cols(wa_ref, wb_ref, seg, c):
    r = seg * D_CONV + c * TN
    return wa_ref[:, r:r + TN] if r < WIDE else wb_ref[:, r - WIDE:r - WIDE + TN]


def _conv_prompt_kernel(xn_ref, wa_ref, wb_ref, cw_ref, ob_ref, nc_ref, zbuf, carry):
    t = pl.program_id(1)
    tm = xn_ref.shape[0]
    xn = xn_ref[...]

    @pl.when(t == 0)
    def _():
        zbuf[0:8, :] = jnp.zeros((8, D_CONV), F32)

    @pl.when(t > 0)
    def _():
        zbuf[0:8, :] = carry[...]

    for c in range(D_CONV // TN):
        cols = slice(c * TN, (c + 1) * TN)
        zc = _dot(xn, _bch_cols(wa_ref, wb_ref, 1, c)) * _dot(xn, _bch_cols(wa_ref, wb_ref, 2, c))
        zbuf[8:8 + tm, cols] = zc
        cw = cw_ref[:, cols]
        y = cw[0:1] * zbuf[6:6 + tm, cols] + cw[1:2] * zbuf[7:7 + tm, cols] + cw[2:3] * zc
        ob_ref[:, cols] = (_dot(xn, _bch_cols(wa_ref, wb_ref, 0, c)) * y).astype(BF16)
    carry[...] = zbuf[tm:tm + 8, :]
    nc_ref[...] = zbuf[tm + 6:tm + 8, :]


def _conv_prompt(xn_all, w_in, l, conv_w):
    nt = SEQ // TM_SEQ
    return pl.pallas_call(
        _conv_prompt_kernel,
        grid=(BATCH, nt),
        in_specs=[
            pl.BlockSpec((TM_SEQ, D_MODEL), lambda b, t: (b * nt + t, 0)),
            _layer_block((D_MODEL, WIDE), l, 0, 1),
            _layer_block((D_MODEL, WIDE), l, 0, 2),
            _layer_block((CONV_WIDTH, D_CONV), l, 0, 0),
        ],
        out_specs=[
            pl.BlockSpec((TM_SEQ, D_CONV), lambda b, t: (b * nt + t, 0)),
            pl.BlockSpec((None, CONV_WIDTH - 1, D_CONV), lambda b, t: (b, 0, 0)),
        ],
        out_shape=[
            jax.ShapeDtypeStruct((N_ROWS, D_CONV), BF16),
            jax.ShapeDtypeStruct((BATCH, CONV_WIDTH - 1, D_CONV), F32),
        ],
        scratch_shapes=[
            pltpu.VMEM((TM_SEQ + 8, D_CONV), F32),
            pltpu.VMEM((8, D_CONV), F32),
        ],
        compiler_params=_params(("arbitrary", "arbitrary")),
        name="conv_prompt",
    )(xn_all, w_in, w_in, conv_w)


def _conv_sample_kernel(xn_ref, wa_ref, wb_ref, cw_ref, cb0_ref, cb1_ref, ob_ref, zc_ref):
    xn = xn_ref[...]
    for c in range(D_CONV // TN):
        cols = slice(c * TN, (c + 1) * TN)
        zc = _dot(xn, _bch_cols(wa_ref, wb_ref, 1, c)) * _dot(xn, _bch_cols(wa_ref, wb_ref, 2, c))
        cw = cw_ref[:, cols]
        y = cw[0:1] * cb0_ref[:, cols] + cw[1:2] * cb1_ref[:, cols] + cw[2:3] * zc
        ob_ref[:, cols] = (_dot(xn, _bch_cols(wa_ref, wb_ref, 0, c)) * y).astype(BF16)
        zc_ref[:, cols] = zc


def _conv_sample(xn_all, w_in, l, conv_w, cb0, cb1):
    c2 = lambda i: (0, 0)
    return pl.pallas_call(
        _conv_sample_kernel,
        grid=(1,),
        in_specs=[
            pl.BlockSpec((DEC_BATCH, D_MODEL), lambda i: (SAMPLE_BLOCK, 0)),
            _layer_block((D_MODEL, WIDE), l, 0, 1),
            _layer_block((D_MODEL, WIDE), l, 0, 2),
            _layer_block((CONV_WIDTH, D_CONV), l, 0, 0),
            pl.BlockSpec((DEC_BATCH, D_CONV), c2),
            pl.BlockSpec((DEC_BATCH, D_CONV), c2),
        ],
        out_specs=[pl.BlockSpec((DEC_BATCH, D_CONV), c2), pl.BlockSpec((DEC_BATCH, D_CONV), c2)],
        out_shape=[
            jax.ShapeDtypeStruct((DEC_BATCH, D_CONV), BF16),
            jax.ShapeDtypeStruct((DEC_BATCH, D_CONV), F32),
        ],
        compiler_params=_params(("arbitrary",)),
        name="conv_sample",
    )(xn_all, w_in, w_in, conv_w, cb0, cb1)


HALF_V = WIDE - D_GMLP


def _gmlp_uv(xn, wa_ref, wb_ref, vn_ref):
    u = _gelu(_dot(xn, wa_ref[:, :D_GMLP]))
    v_lo = _gelu(_dot(xn, wa_ref[:, D_GMLP:]))
    v_hi = _gelu(_dot(xn, wb_ref[:, :D_GMLP - HALF_V]))
    ms = (jnp.sum(v_lo * v_lo, axis=-1, keepdims=True)
          + jnp.sum(v_hi * v_hi, axis=-1, keepdims=True)) * (1.0 / D_GMLP)
    r = lax.rsqrt(ms + EPS)
    return u, v_lo * r * vn_ref[:, :HALF_V], v_hi * r * vn_ref[:, HALF_V:]


def _gmlp_prompt_kernel(xn_ref, wa_ref, wb_ref, vn_ref, ws_ref, bs_ref, oc_ref, u_scr, v_scr):
    tm = xn_ref.shape[0]
    u, v_lo, v_hi = _gmlp_uv(xn_ref[...], wa_ref, wb_ref, vn_ref)
    u_scr[...] = u
    v_scr[:, :HALF_V] = v_lo.astype(BF16)
    v_scr[:, HALF_V:] = v_hi.astype(BF16)
    row = lax.broadcasted_iota(jnp.int32, (CHUNK, CHUNK), 0)
    col = lax.broadcasted_iota(jnp.int32, (CHUNK, CHUNK), 1)
    for grp in range(N_SPATIAL_GROUPS):
        w = jnp.where(col <= row, ws_ref[grp], 0.0).astype(BF16)
        cols = slice(grp * SPATIAL_GROUP_W, (grp + 1) * SPATIAL_GROUP_W)
        for ch in range(tm // CHUNK):
            rows = slice(ch * CHUNK, (ch + 1) * CHUNK)
            mix = _dot(w, v_scr[rows, cols]) + bs_ref[grp]
            oc_ref[rows, cols] = (u_scr[rows, cols] * mix).astype(BF16)


def _gmlp_prompt(xn_all, w_in, l, v_norm, w_s, b_s_wide):
    nt = N_PROMPT // TM_SEQ
    return pl.pallas_call(
        _gmlp_prompt_kernel,
        grid=(nt,),
        in_specs=[
            pl.BlockSpec((TM_SEQ, D_MODEL), lambda i: (i, 0)),
            _layer_block((D_MODEL, WIDE), l, 0, 3),
            _layer_block((D_MODEL, WIDE), l, 0, 4),
            _layer_block((1, D_GMLP), l, 0, 0),
            _layer_block((N_SPATIAL_GROUPS, CHUNK, CHUNK), l, 0, 0, 0),
            _layer_block((N_SPATIAL_GROUPS, CHUNK, SPATIAL_GROUP_W), l, 0, 0, 0),
        ],
        out_specs=pl.BlockSpec((TM_SEQ, D_GMLP), lambda i: (i, 0)),
        out_shape=jax.ShapeDtypeStruct((N_ROWS, D_GMLP), BF16),
        scratch_shapes=[pltpu.VMEM((TM_SEQ, D_GMLP), F32), pltpu.VMEM((TM_SEQ, D_GMLP), BF16)],
        compiler_params=_params(("arbitrary",)),
        name="gmlp_prompt",
    )(xn_all, w_in, w_in, v_norm, w_s, b_s_wide)


def _gmlp_sample_kernel(xn_ref, wa_ref, wb_ref, vn_ref, ws0_ref, bs0_ref, oc_ref, vg_ref):
    u, v_lo, v_hi = _gmlp_uv(xn_ref[...], wa_ref, wb_ref, vn_ref)
    vg_ref[:, :HALF_V] = v_lo
    vg_ref[:, HALF_V:] = v_hi
    oc_ref[...] = (u * (ws0_ref[...] * vg_ref[...] + bs0_ref[...])).astype(BF16)


def _gmlp_sample(xn_all, w_in, l, v_norm, ws0, bs0):
    c2 = lambda i: (0, 0)
    return pl.pallas_call(
        _gmlp_sample_kernel,
        grid=(1,),
        in_specs=[
            pl.BlockSpec((DEC_BATCH, D_MODEL), lambda i: (SAMPLE_BLOCK, 0)),
            _layer_block((D_MODEL, WIDE), l, 0, 3),
            _layer_block((D_MODEL, WIDE), l, 0, 4),
            _layer_block((1, D_GMLP), l, 0, 0),
            pl.BlockSpec((1, D_GMLP), c2),
            pl.BlockSpec((1, D_GMLP), c2),
        ],
        out_specs=[pl.BlockSpec((DEC_BATCH, D_GMLP), c2), pl.BlockSpec((DEC_BATCH, D_GMLP), c2)],
        out_shape=[
            jax.ShapeDtypeStruct((DEC_BATCH, D_GMLP), BF16),
            jax.ShapeDtypeStruct((DEC_BATCH, D_GMLP), F32),
        ],
        compiler_params=_params(("arbitrary",)),
        name="gmlp_sample",
    )(xn_all, w_in, w_in, v_norm, ws0, bs0)


def _merge_kernel(xn_ref, wg0_ref, wg1_ref, wg2_ref, bg_ref, oa_ref, ob_ref, oc_ref, wb_ref, m_ref):
    xn = xn_ref[...]
    bg = bg_ref[...]
    acc = None
    for i, (wg_ref, o_ref) in enumerate(((wg0_ref, oa_ref), (wg1_ref, ob_ref), (wg2_ref, oc_ref))):
        gate = jax.nn.sigmoid(_dot(xn, wg_ref[...]) + bg[i:i + 1])
        term = gate * _dot(o_ref[...], wb_ref[i])
        acc = term if acc is None else acc + term
    m_ref[...] = acc.astype(BF16)


TM_MERGE = 832


def _merge(xn_all, w_in, l, b_gate, o_a, o_b, o_c, w_branch):
    nt = N_ROWS // TM_MERGE
    nn = D_MODEL // TN
    g0 = OFF_G // TN
    rows = lambda i, j: (i, 0)
    gate_w = lambda br: pl.BlockSpec((None, D_MODEL, TN), lambda i, j: (l, 0, g0 + br * nn + j))
    return pl.pallas_call(
        _merge_kernel,
        grid=(nt, nn),
        in_specs=[
            pl.BlockSpec((TM_MERGE, D_MODEL), rows),
            gate_w(0), gate_w(1), gate_w(2),
            pl.BlockSpec((None, N_BRANCHES, TN), lambda i, j: (l, 0, j)),
            pl.BlockSpec((TM_MERGE, BRANCH_W), rows),
            pl.BlockSpec((TM_MERGE, BRANCH_W), rows),
            pl.BlockSpec((TM_MERGE, BRANCH_W), rows),
            pl.BlockSpec((None, N_BRANCHES, BRANCH_W, TN), lambda i, j: (l, 0, 0, j)),
        ],
        out_specs=pl.BlockSpec((TM_MERGE, TN), lambda i, j: (i, j)),
        out_shape=jax.ShapeDtypeStruct((N_ROWS, D_MODEL), BF16),
        compiler_params=_params(("arbitrary", "arbitrary")),
        name="merge",
    )(xn_all, w_in, w_in, w_in, b_gate, o_a, o_b, o_c, w_branch)


OUT_SPLIT = 2


def _out_proj_kernel(m_ref, w_ref, x_ref, g_ref, x1_ref, xn_ref):
    sub = m_ref.shape[0] // OUT_SPLIT
    for s in range(OUT_SPLIT):
        rows = slice(s * sub, (s + 1) * sub)
        x1 = x_ref[rows, :] + _dot(m_ref[rows, :], w_ref[...])
        x1_ref[rows, :] = x1
        xn_ref[rows, :] = _rms_rows(x1, g_ref[...]).astype(BF16)


def _out_proj(m, w_out, l, x, g_ffn, *, m_row_block0, tm, out_rows):
    rows = lambda i: (i, 0)
    return pl.pallas_call(
        _out_proj_kernel,
        grid=(x.shape[0] // tm,),
        in_specs=[
            pl.BlockSpec((tm, D_MODEL), lambda i: (m_row_block0 + i, 0)),
            _layer_block((D_MODEL, D_MODEL), l, 0, 0),
            pl.BlockSpec((tm, D_MODEL), rows),
            _layer_block((1, D_MODEL), l, 0, 0),
        ],
        out_specs=[pl.BlockSpec((tm, D_MODEL), rows), pl.BlockSpec((tm, D_MODEL), rows)],
        out_shape=[
            jax.ShapeDtypeStruct((out_rows, D_MODEL), F32),
            jax.ShapeDtypeStruct((out_rows, D_MODEL), BF16),
        ],
        compiler_params=_params(("arbitrary",)),
        name="out_proj",
    )(m, w_out, x, g_ffn)


TM_UP = 1664
UP_SPLIT = 4


def _ffn_up_kernel(xn_ref, wgate_ref, wup_ref, h_ref, wgate_scr, wup_scr):
    @pl.when(pl.program_id(1) == 0)
    def _():
        wgate_scr[...] = wgate_ref[...].astype(BF16)
        wup_scr[...] = wup_ref[...].astype(BF16)

    sub = xn_ref.shape[0] // UP_SPLIT
    for s in range(UP_SPLIT):
        rows = slice(s * sub, (s + 1) * sub)
        xn = xn_ref[rows, :]
        gate = _dot(xn, wgate_scr[...])
        h_ref[rows, :] = (gate * jax.nn.sigmoid(gate) * _dot(xn, wup_scr[...])).astype(BF16)


def _ffn_up(xn, w_gate_up, l):
    nt = N_ROWS // TM_UP
    nn = D_FF // TN
    return pl.pallas_call(
        _ffn_up_kernel,
        grid=(nn, nt),
        in_specs=[
            pl.BlockSpec((TM_UP, D_MODEL), lambda j, i: (i, 0)),
            pl.BlockSpec((None, D_MODEL, TN), lambda j, i: (l, 0, j)),
            pl.BlockSpec((None, D_MODEL, TN), lambda j, i: (l, 0, nn + j)),
        ],
        out_specs=pl.BlockSpec((TM_UP, TN), lambda j, i: (i, j)),
        out_shape=jax.ShapeDtypeStruct((N_ROWS, D_FF), BF16),
        scratch_shapes=[pltpu.VMEM((D_MODEL, TN), BF16), pltpu.VMEM((D_MODEL, TN), BF16)],
        compiler_params=_params(("arbitrary", "arbitrary")),
        name="ffn_up",
    )(xn, w_gate_up, w_gate_up)


def _ffn_down_kernel(h_ref, w_ref, x_ref, o_ref, w_scr):
    @pl.when(pl.program_id(1) == 0)
    def _():
        w_scr[...] = w_ref[...].astype(BF16)

    o_ref[...] = x_ref[...] + _dot(h_ref[...], w_scr[...])


def _ffn_down(h, w_down, l, x1, *, row_block0, n_rows, tm):
    nt = n_rows // tm
    nn = D_MODEL // TN
    return pl.pallas_call(
        _ffn_down_kernel,
        grid=(nn, nt),
        in_specs=[
            pl.BlockSpec((tm, D_FF), lambda j, i: (row_block0 + i, 0)),
            pl.BlockSpec((None, D_FF, TN), lambda j, i: (l, 0, j)),
            pl.BlockSpec((tm, TN), lambda j, i: (row_block0 + i, j)),
        ],
        out_specs=pl.BlockSpec((tm, TN), lambda j, i: (i, j)),
        out_shape=jax.ShapeDtypeStruct((n_rows, D_MODEL), F32),
        scratch_shapes=[pltpu.VMEM((D_FF, TN), BF16)],
        compiler_params=_params(("arbitrary", "arbitrary")),
        name="ffn_down",
    )(h, w_down, x1)


def _block_diag_ones(width):
    head = np.arange(width) // HEAD_DIM
    return jnp.asarray(head[:, None] == head[None, :], dtype=BF16)


def _fill_sample_rows(o_prompt, o_sample):
    return lax.dynamic_update_slice(o_prompt, o_sample, (N_PROMPT, 0))


def _layer(x, l, cache_k, cache_v, state_conv, p, const):
    first, last = l == 0, l == DEPTH - 1
    x_p, x_s = x if first else (x, x)
    w_in = p["w_in"]
    qn = jnp.tile(p["q_norm"][l], N_HEADS).reshape(1, BRANCH_W)
    kn = jnp.tile(p["k_norm"][l], N_KV_HEADS).reshape(1, KV_W)
    sinks = p["sinks"][l]

    xn_p, q_p, k_p, v_p, klast_p, vlast_p = _qkv(
        x_p, p["norm_mix"], w_in, l, qn, kn, const["ph"],
        row_block0=0, n_rows=N_PROMPT, tm=TM_SEQ, xn_rows=N_ROWS, tiles_per_seq=SEQ // TM_SEQ)
    xn_s, q_s, k_s, v_s = _qkv(
        x_s, p["norm_mix"], w_in, l, qn, kn, const["ph"],
        row_block0=0 if first else SAMPLE_BLOCK, n_rows=DEC_BATCH, tm=DEC_BATCH, xn_rows=DEC_BATCH)
    xn = _fill_sample_rows(xn_p, xn_s)
    oa_p = _attn_prompt(sinks, q_p, k_p, v_p, const["bias_p"])
    oa_s = _attn_sample(
        q_s.reshape(DEC_BATCH, N_HEADS, HEAD_DIM),
        k_s.reshape(DEC_BATCH, 1, KV_W), v_s.reshape(DEC_BATCH, 1, KV_W),
        cache_k, cache_v, l,
        const["bias_s"], sinks.reshape(N_HEADS, 1), const["mask_s"], const["rep_s"],
    ).reshape(DEC_BATCH, BRANCH_W)

    ob_p, nc_p = _conv_prompt(xn, w_in, l, p["conv_w"])
    ob_s, zc_s = _conv_sample(xn, w_in, l, p["conv_w"], state_conv[l, :, 0], state_conv[l, :, 1])

    w_s = p["w_spatial"]
    b_s = p["b_spatial"]
    bs_wide = jnp.broadcast_to(b_s[:, :, :, None], (DEPTH, N_SPATIAL_GROUPS, CHUNK, SPATIAL_GROUP_W))
    oc_p = _gmlp_prompt(xn, w_in, l, p["v_norm"], w_s, bs_wide)
    ws0 = jnp.repeat(w_s[l, :, 0, 0], SPATIAL_GROUP_W).reshape(1, D_GMLP)
    bs0 = jnp.repeat(b_s[l, :, 0], SPATIAL_GROUP_W).reshape(1, D_GMLP)
    oc_s, vg_s = _gmlp_sample(xn, w_in, l, p["v_norm"], ws0, bs0)

    o_a = _fill_sample_rows(oa_p, oa_s)
    o_b = _fill_sample_rows(ob_p, ob_s)
    o_c = _fill_sample_rows(oc_p, oc_s)
    m = _merge(xn, w_in, l, p["b_gate"], o_a, o_b, o_c, p["w_branch"])
    if first:
        x1_p, xn2_p = _out_proj(m, p["w_out"], l, x_p, p["norm_ffn"], m_row_block0=0, tm=TM_SEQ, out_rows=N_ROWS)
        x1_s, xn2_s = _out_proj(m, p["w_out"], l, x_s, p["norm_ffn"], m_row_block0=SAMPLE_BLOCK,
                                tm=DEC_BATCH, out_rows=DEC_BATCH)
        x1, xn2 = _fill_sample_rows(x1_p, x1_s), _fill_sample_rows(xn2_p, xn2_s)
    else:
        x1, xn2 = _out_proj(m, p["w_out"], l, x, p["norm_ffn"], m_row_block0=0, tm=TM_ROW, out_rows=N_ROWS)
    h = _ffn_up(xn2, p["w_gate_up"], l)
    if last:
        x2 = (_ffn_down(h, p["w_down"], l, x1, row_block0=0, n_rows=N_PROMPT, tm=TM_SEQ),
              _ffn_down(h, p["w_down"], l, x1, row_block0=SAMPLE_BLOCK, n_rows=DEC_BATCH, tm=DEC_BATCH))
    else:
        x2 = _ffn_down(h, p["w_down"], l, x1, row_block0=0, n_rows=N_ROWS, tm=TM_ROW)

    new_k_p = klast_p.reshape(BATCH, WINDOW, N_KV_HEADS, HEAD_DIM)
    new_v_p = vlast_p.reshape(BATCH, WINDOW, N_KV_HEADS, HEAD_DIM)
    new_k_s = k_s.reshape(DEC_BATCH, 1, N_KV_HEADS, HEAD_DIM)
    new_v_s = v_s.reshape(DEC_BATCH, 1, N_KV_HEADS, HEAD_DIM)
    new_conv_s = jnp.stack([state_conv[l, :, 1], zc_s], axis=1)
    return x2, (new_k_p, new_v_p, nc_p, new_k_s, new_v_s, new_conv_s, vg_s.reshape(DEC_BATCH, 1, D_GMLP))


def kernel(x_prompt, x_sample, cache_k, cache_v, state_conv, norm_mix, w_in, b_gate, q_norm, k_norm,
           sinks, conv_w, v_norm, w_spatial, b_spatial, w_branch, w_out, norm_ffn, w_gate_up, w_down):
    assert min(WINDOW, SEQ) == WINDOW and SEQ % TM_SEQ == 0 and TM_SEQ >= WINDOW
    n_buf = cache_k.shape[2]
    assert n_buf == WINDOW
    cache_k = jnp.transpose(cache_k, (0, 1, 3, 4, 2)).reshape(DEPTH, DEC_BATCH, KV_W, n_buf)
    cache_v = jnp.transpose(cache_v, (0, 1, 3, 4, 2)).reshape(DEPTH, DEC_BATCH, KV_W, n_buf)
    p = {
        "norm_mix": norm_mix.reshape(DEPTH, 1, D_MODEL),
        "norm_ffn": norm_ffn.reshape(DEPTH, 1, D_MODEL),
        "v_norm": v_norm.reshape(DEPTH, 1, D_GMLP),
        "w_in": w_in.astype(BF16),
        "w_branch": w_branch.astype(BF16),
        "w_out": w_out.astype(BF16),
        "w_gate_up": w_gate_up,
        "w_down": w_down,
        "b_gate": b_gate, "q_norm": q_norm, "k_norm": k_norm, "sinks": sinks, "conv_w": conv_w,
        "w_spatial": w_spatial, "b_spatial": b_spatial,
    }
    kv_of_col = np.arange(KV_W) // HEAD_DIM
    kv_of_head = np.arange(N_HEADS) // GQA_GROUP
    const = {
        "ph": _block_diag_ones(KV_W),
        "bias_p": _prompt_bias(),
        "bias_s": _sample_bias(n_buf),
        "mask_s": jnp.asarray(kv_of_head[:, None] == kv_of_col[None, :], dtype=F32),
        "rep_s": jnp.asarray(np.tile(np.eye(HEAD_DIM), (1, N_KV_HEADS)), dtype=BF16),
    }
    x = (x_prompt.reshape(N_PROMPT, D_MODEL), x_sample.reshape(DEC_BATCH, D_MODEL))
    per_layer = []
    for l in range(DEPTH):
        x, outs = _layer(x, l, cache_k, cache_v, state_conv, p, const)
        per_layer.append(outs)
    stacked = [jnp.stack([per_layer[l][i] for l in range(DEPTH)]) for i in range(7)]
    y_prompt, y_sample = x
    return (y_prompt.reshape(BATCH, SEQ, D_MODEL), y_sample.reshape(DEC_BATCH, 1, D_MODEL), *stacked)
```

```python
import functools

import jax
import jax.numpy as jnp
import numpy as np
from jax import lax
from jax.experimental import pallas as pl
from jax.experimental.pallas import tpu as pltpu

D_MODEL = 2048
BATCH = 4
SEQ = 2048
DEPTH = 2
DEC_BATCH = 128
BRANCH_W = 1024
HEAD_DIM = 64
N_HEADS = 16
N_KV_HEADS = 4
GQA_GROUP = 4
KV_W = 256
WINDOW = 128
D_CONV = 1024
CONV_WIDTH = 3
D_GMLP = 1024
CHUNK = 128
N_SPATIAL_GROUPS = 8
SPATIAL_GROUP_W = 128
N_BRANCHES = 3
D_FF = 5632
EPS = 1e-6
NEG_INF = -1e30

N_PROMPT = BATCH * SEQ
N_ROWS = N_PROMPT + DEC_BATCH
SAMPLE_BLOCK = N_PROMPT // DEC_BATCH

WIDE = 1536
N_WIDE = 5
OFF_G = 6656

TM_SEQ = 512
TM_ROW = 640
TN = 512
VMEM_LIMIT = 56 * 1024 * 1024

F32 = jnp.float32
BF16 = jnp.bfloat16


def _params(sem):
    return pltpu.CompilerParams(dimension_semantics=sem, vmem_limit_bytes=VMEM_LIMIT)


def _rms_rows(x, g):
    ms = jnp.mean(x * x, axis=-1, keepdims=True)
    return x * lax.rsqrt(ms + EPS) * g


def _dot(a, b):
    return jnp.dot(a, b, preferred_element_type=F32)


def _dot_nt(a, b):
    return lax.dot_general(a, b, (((1,), (1,)), ((), ())), preferred_element_type=F32)


def _gelu(x):
    return 0.5 * x * (1.0 + jnp.tanh(np.sqrt(2.0 / np.pi).astype(np.float32) * (x + 0.044715 * (x * x * x))))


def _layer_block(shape, l, *idx):
    return pl.BlockSpec((None, *shape), lambda *_: (l, *idx))


QKV_SPLIT = 2


def _qkv_kernel(x_ref, g_ref, w_ref, qn_ref, kn_ref, ph_ref, xn_ref, q_ref, k_ref, v_ref, *last_refs):
    tm = x_ref.shape[0]
    sub = tm // QKV_SPLIT
    for s in range(QKV_SPLIT):
        rows = slice(s * sub, (s + 1) * sub)
        xn = _rms_rows(x_ref[rows, :], g_ref[...]).astype(BF16)
        xn_ref[rows, :] = xn
        z = _dot(xn, w_ref[...])
        q = z[:, :BRANCH_W]
        k = z[:, BRANCH_W:BRANCH_W + KV_W]
        v = z[:, BRANCH_W + KV_W:]
        q_sq = (q * q).astype(BF16)
        q_ms = jnp.concatenate(
            [_dot(q_sq[:, c:c + KV_W], ph_ref[...]) for c in range(0, BRANCH_W, KV_W)], axis=1) * (1.0 / HEAD_DIM)
        k_ms = _dot((k * k).astype(BF16), ph_ref[...]) * (1.0 / HEAD_DIM)
        q_ref[rows, :] = (q * lax.rsqrt(q_ms + EPS) * qn_ref[...] * (HEAD_DIM ** -0.5)).astype(BF16)
        kn = k * lax.rsqrt(k_ms + EPS) * kn_ref[...]
        k_ref[rows, :] = kn
        v_ref[rows, :] = v
        if last_refs and s == QKV_SPLIT - 1:
            klast_ref, vlast_ref = last_refs
            klast_ref[...] = kn[sub - WINDOW:]
            vlast_ref[...] = v[sub - WINDOW:]


def _qkv(x, g, w_in, l, qn, kn, ph, *, row_block0, n_rows, tm, xn_rows, tiles_per_seq=None):
    nt = n_rows // tm
    const = lambda i: (0, 0)
    rows = lambda i: (i, 0)
    out_specs = [
        pl.BlockSpec((tm, D_MODEL), rows),
        pl.BlockSpec((tm, BRANCH_W), rows),
        pl.BlockSpec((tm, KV_W), rows),
        pl.BlockSpec((tm, KV_W), rows),
    ]
    out_shape = [
        jax.ShapeDtypeStruct((xn_rows, D_MODEL), BF16),
        jax.ShapeDtypeStruct((n_rows, BRANCH_W), BF16),
        jax.ShapeDtypeStruct((n_rows, KV_W), F32),
        jax.ShapeDtypeStruct((n_rows, KV_W), F32),
    ]
    if tiles_per_seq is not None:
        n_seq = nt // tiles_per_seq
        out_specs += [pl.BlockSpec((WINDOW, KV_W), lambda i: (i // tiles_per_seq, 0))] * 2
        out_shape += [jax.ShapeDtypeStruct((n_seq * WINDOW, KV_W), F32)] * 2
    return pl.pallas_call(
        _qkv_kernel,
        grid=(nt,),
        in_specs=[
            pl.BlockSpec((tm, D_MODEL), lambda i: (row_block0 + i, 0)),
            _layer_block((1, D_MODEL), l, 0, 0),
            _layer_block((D_MODEL, WIDE), l, 0, 0),
            pl.BlockSpec((1, BRANCH_W), const),
            pl.BlockSpec((1, KV_W), const),
            pl.BlockSpec((KV_W, KV_W), const),
        ],
        out_specs=out_specs,
        out_shape=out_shape,
        compiler_params=_params(("arbitrary",)),
        name="qkv",
    )(x, g, w_in, qn, kn, ph)


def _attn_prompt_kernel(sink_ref, q_ref, kp_ref, ko_ref, vp_ref, vo_ref, bias_ref, o_ref, s_scr, p_scr):
    kk = jnp.concatenate([kp_ref[...], ko_ref[...]], axis=0).astype(BF16)
    vv = jnp.concatenate([vp_ref[...], vo_ref[...]], axis=0).astype(BF16)
    group_rows = GQA_GROUP * WINDOW
    for g in range(N_KV_HEADS):
        kg = kk[:, g * HEAD_DIM:(g + 1) * HEAD_DIM]
        heads = range(g * GQA_GROUP, (g + 1) * GQA_GROUP)
        qg = jnp.concatenate([q_ref[:, h * HEAD_DIM:(h + 1) * HEAD_DIM] for h in heads], axis=0)
        s_scr[g * group_rows:(g + 1) * group_rows, :] = _dot_nt(qg, kg)
    sink_terms = []
    for h in range(N_HEADS):
        rows = slice(h * WINDOW, (h + 1) * WINDOW)
        s = s_scr[rows, :] + bias_ref[h]
        sink = sink_ref[h]
        m = jnp.maximum(jnp.max(s, axis=-1, keepdims=True), sink)
        p_scr[rows, :] = jnp.exp(s - m).astype(BF16)
        sink_terms.append(jnp.exp(sink - m))
    ones = jnp.ones((2 * WINDOW, HEAD_DIM), BF16)
    for g in range(N_KV_HEADS):
        p = p_scr[g * group_rows:(g + 1) * group_rows, :]
        o_all = _dot(p, vv[:, g * HEAD_DIM:(g + 1) * HEAD_DIM])
        den_all = _dot(p, ones)
        for i in range(GQA_GROUP):
            h = g * GQA_GROUP + i
            rows = slice(i * WINDOW, (i + 1) * WINDOW)
            o = o_all[rows] / (den_all[rows] + sink_terms[h])
            o_ref[:, h * HEAD_DIM:(h + 1) * HEAD_DIM] = o.astype(BF16)


def _attn_prompt(sinks, q, k, v, bias):
    nb = SEQ // WINDOW
    own = lambda b, j: (b * nb + j, 0)
    prev = lambda b, j: (b * nb + jnp.maximum(j - 1, 0), 0)
    return pl.pallas_call(
        _attn_prompt_kernel,
        grid=(BATCH, nb),
        in_specs=[
            pl.BlockSpec(memory_space=pltpu.SMEM),
            pl.BlockSpec((WINDOW, BRANCH_W), own),
            pl.BlockSpec((WINDOW, KV_W), prev),
            pl.BlockSpec((WINDOW, KV_W), own),
            pl.BlockSpec((WINDOW, KV_W), prev),
            pl.BlockSpec((WINDOW, KV_W), own),
            pl.BlockSpec((None, N_HEADS, WINDOW, 2 * WINDOW), lambda b, j: (jnp.minimum(j, 1), 0, 0, 0)),
        ],
        out_specs=pl.BlockSpec((WINDOW, BRANCH_W), own),
        out_shape=jax.ShapeDtypeStruct((N_ROWS, BRANCH_W), BF16),
        scratch_shapes=[
            pltpu.VMEM((N_HEADS * WINDOW, 2 * WINDOW), F32),
            pltpu.VMEM((N_HEADS * WINDOW, 2 * WINDOW), BF16),
        ],
        compiler_params=_params(("arbitrary", "arbitrary")),
        name="attn_prompt",
    )(sinks, q, k, k, v, v, bias)


def _prompt_bias():
    slopes = jnp.exp2(-8.0 * jnp.arange(1, N_HEADS + 1, dtype=F32) / N_HEADS)
    qi = jnp.arange(WINDOW, dtype=jnp.int32)[:, None]
    ki = jnp.arange(2 * WINDOW, dtype=jnp.int32)[None, :] - WINDOW
    dist = qi - ki
    valid = (dist >= 0) & (dist < WINDOW)
    bias = -slopes[:, None, None] * dist.astype(F32)[None]
    with_prev = jnp.where(valid[None], bias, NEG_INF)
    first = jnp.where((valid & (ki >= 0))[None], bias, NEG_INF)
    return jnp.stack([first, with_prev])


SAMPLE_BT = 16


def _attn_sample_kernel(q_ref, kn_ref, vn_ref, ck_ref, cv_ref, bias_ref, sink_ref, mask_ref,
                        rep_ref, o_ref):
    bt = q_ref.shape[0]
    mask = mask_ref[...][None]
    sink = sink_ref[...][None]
    qe = _dot(q_ref[...].reshape(bt * N_HEADS, HEAD_DIM), rep_ref[...])
    qe = qe.reshape(bt, N_HEADS, KV_W) * mask
    qe_bf = qe.astype(BF16)
    s = jnp.stack([_dot(qe_bf[b], ck_ref[b].astype(BF16)) for b in range(bt)])
    s = s + bias_ref[...][None]
    s_new = jnp.sum(qe * kn_ref[...], axis=-1, keepdims=True)
    m = jnp.maximum(jnp.maximum(jnp.max(s, axis=-1, keepdims=True), s_new), sink)
    p = jnp.exp(s - m)
    p_new = jnp.exp(s_new - m)
    den = jnp.sum(p, axis=-1, keepdims=True) + p_new + jnp.exp(sink - m)
    p_bf = p.astype(BF16)
    of = jnp.stack([_dot_nt(p_bf[b], cv_ref[b].astype(BF16)) for b in range(bt)])
    of = (of + p_new * vn_ref[...]) * mask / den
    o = (of[..., 0:64] + of[..., 64:128]) + (of[..., 128:192] + of[..., 192:256])
    o_ref[...] = o.astype(BF16)


def _attn_sample(q3, k_new, v_new, ck, cv, l, bias, sinks_col, mask, rep):
    nsteps = DEC_BATCH // SAMPLE_BT
    b3 = lambda i: (i, 0, 0)
    c2 = lambda i: (0, 0)
    cache = pl.BlockSpec((None, SAMPLE_BT, KV_W, WINDOW), lambda i: (l, i, 0, 0))
    return pl.pallas_call(
        _attn_sample_kernel,
        grid=(nsteps,),
        in_specs=[
            pl.BlockSpec((SAMPLE_BT, N_HEADS, HEAD_DIM), b3),
            pl.BlockSpec((SAMPLE_BT, 1, KV_W), b3),
            pl.BlockSpec((SAMPLE_BT, 1, KV_W), b3),
            cache,
            cache,
            pl.BlockSpec((N_HEADS, WINDOW), c2),
            pl.BlockSpec((N_HEADS, 1), c2),
            pl.BlockSpec((N_HEADS, KV_W), c2),
            pl.BlockSpec((HEAD_DIM, KV_W), c2),
        ],
        out_specs=pl.BlockSpec((SAMPLE_BT, N_HEADS, HEAD_DIM), b3),
        out_shape=jax.ShapeDtypeStruct((DEC_BATCH, N_HEADS, HEAD_DIM), BF16),
        compiler_params=_params(("arbitrary",)),
        name="attn_sample",
    )(q3, k_new, v_new, ck, cv, bias, sinks_col, mask, rep)


def _sample_bias(n_buf):
    slopes = jnp.exp2(-8.0 * jnp.arange(1, N_HEADS + 1, dtype=F32) / N_HEADS)
    dist = n_buf - jnp.arange(n_buf, dtype=jnp.int32)
    bias = -slopes[:, None] * dist.astype(F32)[None, :]
    return jnp.where((dist < WINDOW)[None, :], bias, NEG_INF)


def _bch_cols(wa_ref, wb_ref, seg, c):
    r = seg * D_CONV + c * TN
    return wa_ref[:, r:r + TN] if r < WIDE else wb_ref[:, r - WIDE:r - WIDE + TN]


def _conv_prompt_kernel(xn_ref, wa_ref, wb_ref, cw_ref, ob_ref, nc_ref, zbuf, carry):
    t = pl.program_id(1)
    tm = xn_ref.shape[0]
    xn = xn_ref[...]

    @pl.when(t == 0)
    def _():
        zbuf[0:8, :] = jnp.zeros((8, D_CONV), F32)

    @pl.when(t > 0)
    def _():
        zbuf[0:8, :] = carry[...]

    for c in range(D_CONV // TN):
        cols = slice(c * TN, (c + 1) * TN)
        zc = _dot(xn, _bch_cols(wa_ref, wb_ref, 1, c)) * _dot(xn, _bch_cols(wa_ref, wb_ref, 2, c))
        zbuf[8:8 + tm, cols] = zc
        cw = cw_ref[:, cols]
        y = cw[0:1] * zbuf[6:6 + tm, cols] + cw[1:2] * zbuf[7:7 + tm, cols] + cw[2:3] * zc
        ob_ref[:, cols] = (_dot(xn, _bch_cols(wa_ref, wb_ref, 0, c)) * y).astype(BF16)
    carry[...] = zbuf[tm:tm + 8, :]
    nc_ref[...] = zbuf[tm + 6:tm + 8, :]


def _conv_prompt(xn_all, w_in, l, conv_w):
    nt = SEQ // TM_SEQ
    return pl.pallas_call(
        _conv_prompt_kernel,
        grid=(BATCH, nt),
        in_specs=[
            pl.BlockSpec((TM_SEQ, D_MODEL), lambda b, t: (b * nt + t, 0)),
            _layer_block((D_MODEL, WIDE), l, 0, 1),
            _layer_block((D_MODEL, WIDE), l, 0, 2),
            _layer_block((CONV_WIDTH, D_CONV), l, 0, 0),
        ],
        out_specs=[
            pl.BlockSpec((TM_SEQ, D_CONV), lambda b, t: (b * nt + t, 0)),
            pl.BlockSpec((None, CONV_WIDTH - 1, D_CONV), lambda b, t: (b, 0, 0)),
        ],
        out_shape=[
            jax.ShapeDtypeStruct((N_ROWS, D_CONV), BF16),
            jax.ShapeDtypeStruct((BATCH, CONV_WIDTH - 1, D_CONV), F32),
        ],
        scratch_shapes=[
            pltpu.VMEM((TM_SEQ + 8, D_CONV), F32),
            pltpu.VMEM((8, D_CONV), F32),
        ],
        compiler_params=_params(("arbitrary", "arbitrary")),
        name="conv_prompt",
    )(xn_all, w_in, w_in, conv_w)


def _conv_sample_kernel(xn_ref, wa_ref, wb_ref, cw_ref, cb0_ref, cb1_ref, ob_ref, zc_ref):
    xn = xn_ref[...]
    for c in range(D_CONV // TN):
        cols = slice(c * TN, (c + 1) * TN)
        zc = _dot(xn, _bch_cols(wa_ref, wb_ref, 1, c)) * _dot(xn, _bch_cols(wa_ref, wb_ref, 2, c))
        cw = cw_ref[:, cols]
        y = cw[0:1] * cb0_ref[:, cols] + cw[1:2] * cb1_ref[:, cols] + cw[2:3] * zc
        ob_ref[:, cols] = (_dot(xn, _bch_cols(wa_ref, wb_ref, 0, c)) * y).astype(BF16)
        zc_ref[:, cols] = zc


def _conv_sample(xn_all, w_in, l, conv_w, cb0, cb1):
    c2 = lambda i: (0, 0)
    return pl.pallas_call(
        _conv_sample_kernel,
        grid=(1,),
        in_specs=[
            pl.BlockSpec((DEC_BATCH, D_MODEL), lambda i: (SAMPLE_BLOCK, 0)),
            _layer_block((D_MODEL, WIDE), l, 0, 1),
            _layer_block((D_MODEL, WIDE), l, 0, 2),
            _layer_block((CONV_WIDTH, D_CONV), l, 0, 0),
            pl.BlockSpec((DEC_BATCH, D_CONV), c2),
            pl.BlockSpec((DEC_BATCH, D_CONV), c2),
        ],
        out_specs=[pl.BlockSpec((DEC_BATCH, D_CONV), c2), pl.BlockSpec((DEC_BATCH, D_CONV), c2)],
        out_shape=[
            jax.ShapeDtypeStruct((DEC_BATCH, D_CONV), BF16),
            jax.ShapeDtypeStruct((DEC_BATCH, D_CONV), F32),
        ],
        compiler_params=_params(("arbitrary",)),
        name="conv_sample",
    )(xn_all, w_in, w_in, conv_w, cb0, cb1)


HALF_V = WIDE - D_GMLP


def _gmlp_uv(xn, wa_ref, wb_ref, vn_ref):
    u = _gelu(_dot(xn, wa_ref[:, :D_GMLP]))
    v_lo = _gelu(_dot(xn, wa_ref[:, D_GMLP:]))
    v_hi = _gelu(_dot(xn, wb_ref[:, :D_GMLP - HALF_V]))
    ms = (jnp.sum(v_lo * v_lo, axis=-1, keepdims=True)
          + jnp.sum(v_hi * v_hi, axis=-1, keepdims=True)) * (1.0 / D_GMLP)
    r = lax.rsqrt(ms + EPS)
    return u, v_lo * r * vn_ref[:, :HALF_V], v_hi * r * vn_ref[:, HALF_V:]


def _gmlp_prompt_kernel(xn_ref, wa_ref, wb_ref, vn_ref, ws_ref, bs_ref, oc_ref, u_scr, v_scr):
    tm = xn_ref.shape[0]
    u, v_lo, v_hi = _gmlp_uv(xn_ref[...], wa_ref, wb_ref, vn_ref)
    u_scr[...] = u
    v_scr[:, :HALF_V] = v_lo.astype(BF16)
    v_scr[:, HALF_V:] = v_hi.astype(BF16)
    row = lax.broadcasted_iota(jnp.int32, (CHUNK, CHUNK), 0)
    col = lax.broadcasted_iota(jnp.int32, (CHUNK, CHUNK), 1)
    for grp in range(N_SPATIAL_GROUPS):
        w = jnp.where(col <= row, ws_ref[grp], 0.0).astype(BF16)
        cols = slice(grp * SPATIAL_GROUP_W, (grp + 1) * SPATIAL_GROUP_W)
        for ch in range(tm // CHUNK):
            rows = slice(ch * CHUNK, (ch + 1) * CHUNK)
            mix = _dot(w, v_scr[rows, cols]) + bs_ref[grp]
            oc_ref[rows, cols] = (u_scr[rows, cols] * mix).astype(BF16)


def _gmlp_prompt(xn_all, w_in, l, v_norm, w_s, b_s_wide):
    nt = N_PROMPT // TM_SEQ
    return pl.pallas_call(
        _gmlp_prompt_kernel,
        grid=(nt,),
        in_specs=[
            pl.BlockSpec((TM_SEQ, D_MODEL), lambda i: (i, 0)),
            _layer_block((D_MODEL, WIDE), l, 0, 3),
            _layer_block((D_MODEL, WIDE), l, 0, 4),
            _layer_block((1, D_GMLP), l, 0, 0),
            _layer_block((N_SPATIAL_GROUPS, CHUNK, CHUNK), l, 0, 0, 0),
            _layer_block((N_SPATIAL_GROUPS, CHUNK, SPATIAL_GROUP_W), l, 0, 0, 0),
        ],
        out_specs=pl.BlockSpec((TM_SEQ, D_GMLP), lambda i: (i, 0)),
        out_shape=jax.ShapeDtypeStruct((N_ROWS, D_GMLP), BF16),
        scratch_shapes=[pltpu.VMEM((TM_SEQ, D_GMLP), F32), pltpu.VMEM((TM_SEQ, D_GMLP), BF16)],
        compiler_params=_params(("arbitrary",)),
        name="gmlp_prompt",
    )(xn_all, w_in, w_in, v_norm, w_s, b_s_wide)


def _gmlp_sample_kernel(xn_ref, wa_ref, wb_ref, vn_ref, ws0_ref, bs0_ref, oc_ref, vg_ref):
    u, v_lo, v_hi = _gmlp_uv(xn_ref[...], wa_ref, wb_ref, vn_ref)
    vg_ref[:, :HALF_V] = v_lo
    vg_ref[:, HALF_V:] = v_hi
    oc_ref[...] = (u * (ws0_ref[...] * vg_ref[...] + bs0_ref[...])).astype(BF16)


def _gmlp_sample(xn_all, w_in, l, v_norm, ws0, bs0):
    c2 = lambda i: (0, 0)
    return pl.pallas_call(
        _gmlp_sample_kernel,
        grid=(1,),
        in_specs=[
            pl.BlockSpec((DEC_BATCH, D_MODEL), lambda i: (SAMPLE_BLOCK, 0)),
            _layer_block((D_MODEL, WIDE), l, 0, 3),
            _layer_block((D_MODEL, WIDE), l, 0, 4),
            _layer_block((1, D_GMLP), l, 0, 0),
            pl.BlockSpec((1, D_GMLP), c2),
            pl.BlockSpec((1, D_GMLP), c2),
        ],
        out_specs=[pl.BlockSpec((DEC_BATCH, D_GMLP), c2), pl.BlockSpec((DEC_BATCH, D_GMLP), c2)],
        out_shape=[
            jax.ShapeDtypeStruct((DEC_BATCH, D_GMLP), BF16),
            jax.ShapeDtypeStruct((DEC_BATCH, D_GMLP), F32),
        ],
        compiler_params=_params(("arbitrary",)),
        name="gmlp_sample",
    )(xn_all, w_in, w_in, v_norm, ws0, bs0)


def _merge_kernel(xn_ref, bg_ref, oa_ref, ob_ref, oc_ref, wg_hbm, wb_hbm, m_ref,
                  stage_g, stage_b, wg_bf, wb_bf, sem, *, layer):
    j = pl.program_id(0)
    i = pl.program_id(1)

    def weight_copies(jj):
        copies = []
        for br in range(N_BRANCHES):
            gcol = pl.multiple_of(OFF_G + br * D_MODEL + jj * TN, TN)
            bcol = pl.multiple_of(jj * TN, TN)
            copies.append(pltpu.make_async_copy(
                wg_hbm.at[layer, :, pl.ds(gcol, TN)], stage_g.at[br], sem.at[br]))
            copies.append(pltpu.make_async_copy(
                wb_hbm.at[layer, br, :, pl.ds(bcol, TN)], stage_b.at[br], sem.at[N_BRANCHES + br]))
        return copies

    @pl.when(i == 0)
    def _():
        @pl.when(j == 0)
        def _():
            for c in weight_copies(0):
                c.start()

        for c in weight_copies(j):
            c.wait()
        def cast_rows(stage, dst):
            def body(c, carry):
                r = pl.multiple_of(c * CAST_ROWS, CAST_ROWS)
                for br in range(N_BRANCHES):
                    dst[br, pl.ds(r, CAST_ROWS), :] = stage[br, pl.ds(r, CAST_ROWS), :].astype(BF16)
                return carry
            lax.fori_loop(0, stage.shape[1] // CAST_ROWS, body, 0)

        cast_rows(stage_g, wg_bf)
        cast_rows(stage_b, wb_bf)

        @pl.when(j + 1 < pl.num_programs(0))
        def _():
            for c in weight_copies(j + 1):
                c.start()

    xn = xn_ref[...]
    bg = bg_ref[...]
    acc = None
    for br, o_ref in enumerate((oa_ref, ob_ref, oc_ref)):
        gate = jax.nn.sigmoid(_dot(xn, wg_bf[br]) + bg[br:br + 1])
        term = gate * _dot(o_ref[...], wb_bf[br])
        acc = term if acc is None else acc + term
    m_ref[...] = acc.astype(BF16)


TM_MERGE = TM_ROW
CAST_ROWS = 256


def _merge(xn_all, w_in, l, b_gate, o_a, o_b, o_c, w_branch):
    nt = N_ROWS // TM_MERGE
    nn = D_MODEL // TN
    rows = lambda j, i: (i, 0)
    return pl.pallas_call(
        functools.partial(_merge_kernel, layer=l),
        grid=(nn, nt),
        in_specs=[
            pl.BlockSpec((TM_MERGE, D_MODEL), rows),
            pl.BlockSpec((None, N_BRANCHES, TN), lambda j, i: (l, 0, j)),
            pl.BlockSpec((TM_MERGE, BRANCH_W), rows),
            pl.BlockSpec((TM_MERGE, BRANCH_W), rows),
            pl.BlockSpec((TM_MERGE, BRANCH_W), rows),
            pl.BlockSpec(memory_space=pl.ANY),
            pl.BlockSpec(memory_space=pl.ANY),
        ],
        out_specs=pl.BlockSpec((TM_MERGE, TN), lambda j, i: (i, j)),
        out_shape=jax.ShapeDtypeStruct((N_ROWS, D_MODEL), BF16),
        scratch_shapes=[
            pltpu.VMEM((N_BRANCHES, D_MODEL, TN), F32),
            pltpu.VMEM((N_BRANCHES, BRANCH_W, TN), F32),
            pltpu.VMEM((N_BRANCHES, D_MODEL, TN), BF16),
            pltpu.VMEM((N_BRANCHES, BRANCH_W, TN), BF16),
            pltpu.SemaphoreType.DMA((2 * N_BRANCHES,)),
        ],
        compiler_params=_params(("arbitrary", "arbitrary")),
        name="merge",
    )(xn_all, b_gate, o_a, o_b, o_c, w_in, w_branch)


OUT_SPLIT = 2


def _out_proj_kernel(m_ref, w_ref, x_ref, g_ref, x1_ref, xn_ref):
    sub = m_ref.shape[0] // OUT_SPLIT
    for s in range(OUT_SPLIT):
        rows = slice(s * sub, (s + 1) * sub)
        x1 = x_ref[rows, :] + _dot(m_ref[rows, :], w_ref[...])
        x1_ref[rows, :] = x1
        xn_ref[rows, :] = _rms_rows(x1, g_ref[...]).astype(BF16)


def _out_proj(m, w_out, l, x, g_ffn, *, m_row_block0, tm, out_rows):
    rows = lambda i: (i, 0)
    return pl.pallas_call(
        _out_proj_kernel,
        grid=(x.shape[0] // tm,),
        in_specs=[
            pl.BlockSpec((tm, D_MODEL), lambda i: (m_row_block0 + i, 0)),
            _layer_block((D_MODEL, D_MODEL), l, 0, 0),
            pl.BlockSpec((tm, D_MODEL), rows),
            _layer_block((1, D_MODEL), l, 0, 0),
        ],
        out_specs=[pl.BlockSpec((tm, D_MODEL), rows), pl.BlockSpec((tm, D_MODEL), rows)],
        out_shape=[
            jax.ShapeDtypeStruct((out_rows, D_MODEL), F32),
            jax.ShapeDtypeStruct((out_rows, D_MODEL), BF16),
        ],
        compiler_params=_params(("arbitrary",)),
        name="out_proj",
    )(m, w_out, x, g_ffn)


TM_UP = 1664
UP_SPLIT = 4


def _ffn_up_kernel(xn_ref, wgate_ref, wup_ref, h_ref, wgate_scr, wup_scr):
    @pl.when(pl.program_id(1) == 0)
    def _():
        wgate_scr[...] = wgate_ref[...].astype(BF16)
        wup_scr[...] = wup_ref[...].astype(BF16)

    sub = xn_ref.shape[0] // UP_SPLIT
    for s in range(UP_SPLIT):
        rows = slice(s * sub, (s + 1) * sub)
        xn = xn_ref[rows, :]
        gate = _dot(xn, wgate_scr[...])
        h_ref[rows, :] = (gate * jax.nn.sigmoid(gate) * _dot(xn, wup_scr[...])).astype(BF16)


def _ffn_up(xn, w_gate_up, l):
    nt = N_ROWS // TM_UP
    nn = D_FF // TN
    return pl.pallas_call(
        _ffn_up_kernel,
        grid=(nn, nt),
        in_specs=[
            pl.BlockSpec((TM_UP, D_MODEL), lambda j, i: (i, 0)),
            pl.BlockSpec((None, D_MODEL, TN), lambda j, i: (l, 0, j)),
            pl.BlockSpec((None, D_MODEL, TN), lambda j, i: (l, 0, nn + j)),
        ],
        out_specs=pl.BlockSpec((TM_UP, TN), lambda j, i: (i, j)),
        out_shape=jax.ShapeDtypeStruct((N_ROWS, D_FF), BF16),
        scratch_shapes=[pltpu.VMEM((D_MODEL, TN), BF16), pltpu.VMEM((D_MODEL, TN), BF16)],
        compiler_params=_params(("arbitrary", "arbitrary")),
        name="ffn_up",
    )(xn, w_gate_up, w_gate_up)


def _ffn_down_kernel(h_ref, w_ref, x_ref, o_ref, w_scr):
    @pl.when(pl.program_id(1) == 0)
    def _():
        w_scr[...] = w_ref[...].astype(BF16)

    o_ref[...] = x_ref[...] + _dot(h_ref[...], w_scr[...])


def _ffn_down(h, w_down, l, x1, *, row_block0, n_rows, tm):
    nt = n_rows // tm
    nn = D_MODEL // TN
    return pl.pallas_call(
        _ffn_down_kernel,
        grid=(nn, nt),
        in_specs=[
            pl.BlockSpec((tm, D_FF), lambda j, i: (row_block0 + i, 0)),
            pl.BlockSpec((None, D_FF, TN), lambda j, i: (l, 0, j)),
            pl.BlockSpec((tm, TN), lambda j, i: (row_block0 + i, j)),
        ],
        out_specs=pl.BlockSpec((tm, TN), lambda j, i: (i, j)),
        out_shape=jax.ShapeDtypeStruct((n_rows, D_MODEL), F32),
        scratch_shapes=[pltpu.VMEM((D_FF, TN), BF16)],
        compiler_params=_params(("arbitrary", "arbitrary")),
        name="ffn_down",
    )(h, w_down, x1)


def _block_diag_ones(width):
    head = np.arange(width) // HEAD_DIM
    return jnp.asarray(head[:, None] == head[None, :], dtype=BF16)


def _fill_sample_rows(o_prompt, o_sample):
    return lax.dynamic_update_slice(o_prompt, o_sample, (N_PROMPT, 0))


def _layer(x, l, cache_k, cache_v, state_conv, p, const):
    first, last = l == 0, l == DEPTH - 1
    x_p, x_s = x if first else (x, x)
    w_in = p["w_in"]
    qn = jnp.tile(p["q_norm"][l], N_HEADS).reshape(1, BRANCH_W)
    kn = jnp.tile(p["k_norm"][l], N_KV_HEADS).reshape(1, KV_W)
    sinks = p["sinks"][l]

    xn_p, q_p, k_p, v_p, klast_p, vlast_p = _qkv(
        x_p, p["norm_mix"], w_in, l, qn, kn, const["ph"],
        row_block0=0, n_rows=N_PROMPT, tm=TM_SEQ, xn_rows=N_ROWS, tiles_per_seq=SEQ // TM_SEQ)
    xn_s, q_s, k_s, v_s = _qkv(
        x_s, p["norm_mix"], w_in, l, qn, kn, const["ph"],
        row_block0=0 if first else SAMPLE_BLOCK, n_rows=DEC_BATCH, tm=DEC_BATCH, xn_rows=DEC_BATCH)
    xn = _fill_sample_rows(xn_p, xn_s)
    oa_p = _attn_prompt(sinks, q_p, k_p, v_p, const["bias_p"])
    oa_s = _attn_sample(
        q_s.reshape(DEC_BATCH, N_HEADS, HEAD_DIM),
        k_s.reshape(DEC_BATCH, 1, KV_W), v_s.reshape(DEC_BATCH, 1, KV_W),
        cache_k, cache_v, l,
        const["bias_s"], sinks.reshape(N_HEADS, 1), const["mask_s"], const["rep_s"],
    ).reshape(DEC_BATCH, BRANCH_W)

    ob_p, nc_p = _conv_prompt(xn, w_in, l, p["conv_w"])
    ob_s, zc_s = _conv_sample(xn, w_in, l, p["conv_w"], state_conv[l, :, 0], state_conv[l, :, 1])

    w_s = p["w_spatial"]
    b_s = p["b_spatial"]
    bs_wide = jnp.broadcast_to(b_s[:, :, :, None], (DEPTH, N_SPATIAL_GROUPS, CHUNK, SPATIAL_GROUP_W))
    oc_p = _gmlp_prompt(xn, w_in, l, p["v_norm"], w_s, bs_wide)
    ws0 = jnp.repeat(w_s[l, :, 0, 0], SPATIAL_GROUP_W).reshape(1, D_GMLP)
    bs0 = jnp.repeat(b_s[l, :, 0], SPATIAL_GROUP_W).reshape(1, D_GMLP)
    oc_s, vg_s = _gmlp_sample(xn, w_in, l, p["v_norm"], ws0, bs0)

    o_a = _fill_sample_rows(oa_p, oa_s)
    o_b = _fill_sample_rows(ob_p, ob_s)
    o_c = _fill_sample_rows(oc_p, oc_s)
    m = _merge(xn, p["w_in_f32"], l, p["b_gate"], o_a, o_b, o_c, p["w_branch"])
    if first:
        x1_p, xn2_p = _out_proj(m, p["w_out"], l, x_p, p["norm_ffn"], m_row_block0=0, tm=TM_SEQ, out_rows=N_ROWS)
        x1_s, xn2_s = _out_proj(m, p["w_out"], l, x_s, p["norm_ffn"], m_row_block0=SAMPLE_BLOCK,
                                tm=DEC_BATCH, out_rows=DEC_BATCH)
        x1, xn2 = _fill_sample_rows(x1_p, x1_s), _fill_sample_rows(xn2_p, xn2_s)
    else:
        x1, xn2 = _out_proj(m, p["w_out"], l, x, p["norm_ffn"], m_row_block0=0, tm=TM_ROW, out_rows=N_ROWS)
    h = _ffn_up(xn2, p["w_gate_up"], l)
    if last:
        x2 = (_ffn_down(h, p["w_down"], l, x1, row_block0=0, n_rows=N_PROMPT, tm=TM_SEQ),
              _ffn_down(h, p["w_down"], l, x1, row_block0=SAMPLE_BLOCK, n_rows=DEC_BATCH, tm=DEC_BATCH))
    else:
        x2 = _ffn_down(h, p["w_down"], l, x1, row_block0=0, n_rows=N_ROWS, tm=TM_ROW)

    new_k_p = klast_p.reshape(BATCH, WINDOW, N_KV_HEADS, HEAD_DIM)
    new_v_p = vlast_p.reshape(BATCH, WINDOW, N_KV_HEADS, HEAD_DIM)
    new_k_s = k_s.reshape(DEC_BATCH, 1, N_KV_HEADS, HEAD_DIM)
    new_v_s = v_s.reshape(DEC_BATCH, 1, N_KV_HEADS, HEAD_DIM)
    new_conv_s = jnp.stack([state_conv[l, :, 1], zc_s], axis=1)
    return x2, (new_k_p, new_v_p, nc_p, new_k_s, new_v_s, new_conv_s, vg_s.reshape(DEC_BATCH, 1, D_GMLP))


def kernel(x_prompt, x_sample, cache_k, cache_v, state_conv, norm_mix, w_in, b_gate, q_norm, k_norm,
           sinks, conv_w, v_norm, w_spatial, b_spatial, w_branch, w_out, norm_ffn, w_gate_up, w_down):
    assert min(WINDOW, SEQ) == WINDOW and SEQ % TM_SEQ == 0 and TM_SEQ >= WINDOW
    n_buf = cache_k.shape[2]
    assert n_buf == WINDOW
    cache_k = jnp.transpose(cache_k, (0, 1, 3, 4, 2)).reshape(DEPTH, DEC_BATCH, KV_W, n_buf)
    cache_v = jnp.transpose(cache_v, (0, 1, 3, 4, 2)).reshape(DEPTH, DEC_BATCH, KV_W, n_buf)
    p = {
        "norm_mix": norm_mix.reshape(DEPTH, 1, D_MODEL),
        "norm_ffn": norm_ffn.reshape(DEPTH, 1, D_MODEL),
        "v_norm": v_norm.reshape(DEPTH, 1, D_GMLP),
        "w_in": w_in[:, :, :N_WIDE * WIDE].astype(BF16),
        "w_in_f32": w_in,
        "w_branch": w_branch,
        "w_out": w_out.astype(BF16),
        "w_gate_up": w_gate_up,
        "w_down": w_down,
        "b_gate": b_gate, "q_norm": q_norm, "k_norm": k_norm, "sinks": sinks, "conv_w": conv_w,
        "w_spatial": w_spatial, "b_spatial": b_spatial,
    }
    kv_of_col = np.arange(KV_W) // HEAD_DIM
    kv_of_head = np.arange(N_HEADS) // GQA_GROUP
    const = {
        "ph": _block_diag_ones(KV_W),
        "bias_p": _prompt_bias(),
        "bias_s": _sample_bias(n_buf),
        "mask_s": jnp.asarray(kv_of_head[:, None] == kv_of_col[None, :], dtype=F32),
        "rep_s": jnp.asarray(np.tile(np.eye(HEAD_DIM), (1, N_KV_HEADS)), dtype=BF16),
    }
    x = (x_prompt.reshape(N_PROMPT, D_MODEL), x_sample.reshape(DEC_BATCH, D_MODEL))
    per_layer = []
    for l in range(DEPTH):
        x, outs = _layer(x, l, cache_k, cache_v, state_conv, p, const)
        per_layer.append(outs)
    stacked = [jnp.stack([per_layer[l][i] for l in range(DEPTH)]) for i in range(7)]
    y_prompt, y_sample = x
    return (y_prompt.reshape(BATCH, SEQ, D_MODEL), y_sample.reshape(DEC_BATCH, 1, D_MODEL), *stacked)
```

```python
import functools

import jax
import jax.numpy as jnp
import numpy as np
from jax import lax
from jax.experimental import pallas as pl
from jax.experimental.pallas import tpu as pltpu

D_MODEL = 2048
BATCH = 4
SEQ = 2048
DEPTH = 2
DEC_BATCH = 128
BRANCH_W = 1024
HEAD_DIM = 64
N_HEADS = 16
N_KV_HEADS = 4
GQA_GROUP = 4
KV_W = 256
WINDOW = 128
D_CONV = 1024
CONV_WIDTH = 3
D_GMLP = 1024
CHUNK = 128
N_SPATIAL_GROUPS = 8
SPATIAL_GROUP_W = 128
N_BRANCHES = 3
D_FF = 5632
EPS = 1e-6
NEG_INF = -1e30

N_PROMPT = BATCH * SEQ
N_ROWS = N_PROMPT + DEC_BATCH
SAMPLE_BLOCK = N_PROMPT // DEC_BATCH

WIDE = 1536
N_WIDE = 5
OFF_G = 6656

TM_SEQ = 512
TM_ROW = 640
TN = 512
VMEM_LIMIT = 56 * 1024 * 1024

F32 = jnp.float32
BF16 = jnp.bfloat16


def _params(sem):
    return pltpu.CompilerParams(dimension_semantics=sem, vmem_limit_bytes=VMEM_LIMIT)


def _rms_rows(x, g):
    ms = jnp.mean(x * x, axis=-1, keepdims=True)
    return x * lax.rsqrt(ms + EPS) * g


def _dot(a, b):
    return jnp.dot(a, b, preferred_element_type=F32)


def _dot_nt(a, b):
    return lax.dot_general(a, b, (((1,), (1,)), ((), ())), preferred_element_type=F32)


def _gelu(x):
    return 0.5 * x * (1.0 + jnp.tanh(np.sqrt(2.0 / np.pi).astype(np.float32) * (x + 0.044715 * (x * x * x))))


def _layer_block(shape, l, *idx):
    return pl.BlockSpec((None, *shape), lambda *_: (l, *idx))


QKV_SPLIT = 2


def _qkv_kernel(x_ref, g_ref, w_ref, qn_ref, kn_ref, ph_ref, xn_ref, q_ref, k_ref, v_ref, *last_refs):
    tm = x_ref.shape[0]
    sub = tm // QKV_SPLIT
    for s in range(QKV_SPLIT):
        rows = slice(s * sub, (s + 1) * sub)
        xn = _rms_rows(x_ref[rows, :], g_ref[...]).astype(BF16)
        xn_ref[rows, :] = xn
        z = _dot(xn, w_ref[...])
        q = z[:, :BRANCH_W]
        k = z[:, BRANCH_W:BRANCH_W + KV_W]
        v = z[:, BRANCH_W + KV_W:]
        q_sq = (q * q).astype(BF16)
        q_ms = jnp.concatenate(
            [_dot(q_sq[:, c:c + KV_W], ph_ref[...]) for c in range(0, BRANCH_W, KV_W)], axis=1) * (1.0 / HEAD_DIM)
        k_ms = _dot((k * k).astype(BF16), ph_ref[...]) * (1.0 / HEAD_DIM)
        q_ref[rows, :] = (q * lax.rsqrt(q_ms + EPS) * qn_ref[...] * (HEAD_DIM ** -0.5)).astype(BF16)
        kn = k * lax.rsqrt(k_ms + EPS) * kn_ref[...]
        k_ref[rows, :] = kn
        v_ref[rows, :] = v
        if last_refs and s == QKV_SPLIT - 1:
            klast_ref, vlast_ref = last_refs
            klast_ref[...] = kn[sub - WINDOW:]
            vlast_ref[...] = v[sub - WINDOW:]


def _qkv(x, g, w_in, l, qn, kn, ph, *, row_block0, n_rows, tm, xn_rows, tiles_per_seq=None):
    nt = n_rows // tm
    const = lambda i: (0, 0)
    rows = lambda i: (i, 0)
    out_specs = [
        pl.BlockSpec((tm, D_MODEL), rows),
        pl.BlockSpec((tm, BRANCH_W), rows),
        pl.BlockSpec((tm, KV_W), rows),
        pl.BlockSpec((tm, KV_W), rows),
    ]
    out_shape = [
        jax.ShapeDtypeStruct((xn_rows, D_MODEL), BF16),
        jax.ShapeDtypeStruct((n_rows, BRANCH_W), BF16),
        jax.ShapeDtypeStruct((n_rows, KV_W), F32),
        jax.ShapeDtypeStruct((n_rows, KV_W), F32),
    ]
    if tiles_per_seq is not None:
        n_seq = nt // tiles_per_seq
        out_specs += [pl.BlockSpec((WINDOW, KV_W), lambda i: (i // tiles_per_seq, 0))] * 2
        out_shape += [jax.ShapeDtypeStruct((n_seq * WINDOW, KV_W), F32)] * 2
    return pl.pallas_call(
        _qkv_kernel,
        grid=(nt,),
        in_specs=[
            pl.BlockSpec((tm, D_MODEL), lambda i: (row_block0 + i, 0)),
            _layer_block((1, D_MODEL), l, 0, 0),
            _layer_block((D_MODEL, WIDE), l, 0, 0),
            pl.BlockSpec((1, BRANCH_W), const),
            pl.BlockSpec((1, KV_W), const),
            pl.BlockSpec((KV_W, KV_W), const),
        ],
        out_specs=out_specs,
        out_shape=out_shape,
        compiler_params=_params(("arbitrary",)),
        name="qkv",
    )(x, g, w_in, qn, kn, ph)


def _attn_prompt_kernel(q_ref, kp_ref, ko_ref, vp_ref, vo_ref, bias_ref, sink_ref, o_ref, s_scr):
    kk = jnp.concatenate([kp_ref[...], ko_ref[...]], axis=0).astype(BF16)
    vv_t = jnp.transpose(jnp.concatenate([vp_ref[...], vo_ref[...]], axis=0)).astype(BF16)
    for g in range(N_KV_HEADS):
        kg = kk[:, g * HEAD_DIM:(g + 1) * HEAD_DIM]
        heads = range(g * GQA_GROUP, (g + 1) * GQA_GROUP)
        qg = jnp.concatenate([q_ref[:, h * HEAD_DIM:(h + 1) * HEAD_DIM] for h in heads], axis=0)
        s_scr[g] = _dot_nt(kg, qg)
    ones = jnp.ones((ONES_ROWS, 2 * WINDOW), BF16)
    for g in range(N_KV_HEADS):
        lhs = jnp.concatenate([vv_t[g * HEAD_DIM:(g + 1) * HEAD_DIM, :], ones], axis=0)
        o_t = []
        for i in range(GQA_GROUP):
            cols = slice(i * WINDOW, (i + 1) * WINDOW)
            s = s_scr[g, :, cols] + bias_ref[g, :, cols]
            sink = sink_ref[g, :, cols]
            m = jnp.maximum(jnp.max(s, axis=0, keepdims=True), sink)
            p = jnp.exp(s - m).astype(BF16)
            oe = _dot(lhs, p)
            den = oe[HEAD_DIM:HEAD_DIM + 1, :] + jnp.exp(sink - m)
            o_t.append(oe[:HEAD_DIM, :] / den)
        o_ref[:, g * KV_W:(g + 1) * KV_W] = jnp.transpose(jnp.concatenate(o_t, axis=0)).astype(BF16)


ONES_ROWS = 16


def _attn_prompt(sink_rows, q, k, v, bias):
    nb = SEQ // WINDOW
    own = lambda b, j: (b * nb + j, 0)
    prev = lambda b, j: (b * nb + jnp.maximum(j - 1, 0), 0)
    group_q = GQA_GROUP * WINDOW
    return pl.pallas_call(
        _attn_prompt_kernel,
        grid=(BATCH, nb),
        in_specs=[
            pl.BlockSpec((WINDOW, BRANCH_W), own),
            pl.BlockSpec((WINDOW, KV_W), prev),
            pl.BlockSpec((WINDOW, KV_W), own),
            pl.BlockSpec((WINDOW, KV_W), prev),
            pl.BlockSpec((WINDOW, KV_W), own),
            pl.BlockSpec((None, N_KV_HEADS, 2 * WINDOW, group_q), lambda b, j: (jnp.minimum(j, 1), 0, 0, 0)),
            pl.BlockSpec((N_KV_HEADS, 1, group_q), lambda b, j: (0, 0, 0)),
        ],
        out_specs=pl.BlockSpec((WINDOW, BRANCH_W), own),
        out_shape=jax.ShapeDtypeStruct((N_ROWS, BRANCH_W), BF16),
        scratch_shapes=[pltpu.VMEM((N_KV_HEADS, 2 * WINDOW, group_q), F32)],
        compiler_params=_params(("arbitrary", "arbitrary")),
        name="attn_prompt",
    )(q, k, k, v, v, bias, sink_rows)


def _prompt_bias():
    slopes = jnp.exp2(-8.0 * jnp.arange(1, N_HEADS + 1, dtype=F32) / N_HEADS)
    qi = jnp.arange(WINDOW, dtype=jnp.int32)[:, None]
    ki = jnp.arange(2 * WINDOW, dtype=jnp.int32)[None, :] - WINDOW
    dist = qi - ki
    valid = (dist >= 0) & (dist < WINDOW)
    bias = -slopes[:, None, None] * dist.astype(F32)[None]
    with_prev = jnp.where(valid[None], bias, NEG_INF)
    first = jnp.where((valid & (ki >= 0))[None], bias, NEG_INF)
    per_head = jnp.stack([first, with_prev])
    grouped = per_head.reshape(2, N_KV_HEADS, GQA_GROUP, WINDOW, 2 * WINDOW)
    return jnp.transpose(grouped, (0, 1, 4, 2, 3)).reshape(2, N_KV_HEADS, 2 * WINDOW, GQA_GROUP * WINDOW)


SAMPLE_BT = 16


def _attn_sample_kernel(q_ref, kn_ref, vn_ref, ck_ref, cv_ref, bias_ref, sink_ref, mask_ref,
                        rep_ref, o_ref):
    bt = q_ref.shape[0]
    mask = mask_ref[...][None]
    sink = sink_ref[...][None]
    qe = _dot(q_ref[...].reshape(bt * N_HEADS, HEAD_DIM), rep_ref[...])
    qe = qe.reshape(bt, N_HEADS, KV_W) * mask
    qe_bf = qe.astype(BF16)
    s = jnp.stack([_dot(qe_bf[b], ck_ref[b].astype(BF16)) for b in range(bt)])
    s = s + bias_ref[...][None]
    s_new = jnp.sum(qe * kn_ref[...], axis=-1, keepdims=True)
    m = jnp.maximum(jnp.maximum(jnp.max(s, axis=-1, keepdims=True), s_new), sink)
    p = jnp.exp(s - m)
    p_new = jnp.exp(s_new - m)
    den = jnp.sum(p, axis=-1, keepdims=True) + p_new + jnp.exp(sink - m)
    p_bf = p.astype(BF16)
    of = jnp.stack([_dot_nt(p_bf[b], cv_ref[b].astype(BF16)) for b in range(bt)])
    of = (of + p_new * vn_ref[...]) * mask / den
    o = (of[..., 0:64] + of[..., 64:128]) + (of[..., 128:192] + of[..., 192:256])
    o_ref[...] = o.astype(BF16)


def _attn_sample(q3, k_new, v_new, ck, cv, l, bias, sinks_col, mask, rep):
    nsteps = DEC_BATCH // SAMPLE_BT
    b3 = lambda i: (i, 0, 0)
    c2 = lambda i: (0, 0)
    cache = pl.BlockSpec((None, SAMPLE_BT, KV_W, WINDOW), lambda i: (l, i, 0, 0))
    return pl.pallas_call(
        _attn_sample_kernel,
        grid=(nsteps,),
        in_specs=[
            pl.BlockSpec((SAMPLE_BT, N_HEADS, HEAD_DIM), b3),
            pl.BlockSpec((SAMPLE_BT, 1, KV_W), b3),
            pl.BlockSpec((SAMPLE_BT, 1, KV_W), b3),
            cache,
            cache,
            pl.BlockSpec((N_HEADS, WINDOW), c2),
            pl.BlockSpec((N_HEADS, 1), c2),
            pl.BlockSpec((N_HEADS, KV_W), c2),
            pl.BlockSpec((HEAD_DIM, KV_W), c2),
        ],
        out_specs=pl.BlockSpec((SAMPLE_BT, N_HEADS, HEAD_DIM), b3),
        out_shape=jax.ShapeDtypeStruct((DEC_BATCH, N_HEADS, HEAD_DIM), BF16),
        compiler_params=_params(("arbitrary",)),
        name="attn_sample",
    )(q3, k_new, v_new, ck, cv, bias, sinks_col, mask, rep)


def _sample_bias(n_buf):
    slopes = jnp.exp2(-8.0 * jnp.arange(1, N_HEADS + 1, dtype=F32) / N_HEADS)
    dist = n_buf - jnp.arange(n_buf, dtype=jnp.int32)
    bias = -slopes[:, None] * dist.astype(F32)[None, :]
    return jnp.where((dist < WINDOW)[None, :], bias, NEG_INF)


def _bch_cols(wa_ref, wb_ref, seg, c):
    r = seg * D_CONV + c * TN
    return wa_ref[:, r:r + TN] if r < WIDE else wb_ref[:, r - WIDE:r - WIDE + TN]


def _conv_prompt_kernel(xn_ref, wa_ref, wb_ref, cw_ref, ob_ref, nc_ref, zbuf, carry):
    t = pl.program_id(1)
    tm = xn_ref.shape[0]
    xn = xn_ref[...]

    @pl.when(t == 0)
    def _():
        zbuf[0:8, :] = jnp.zeros((8, D_CONV), F32)

    @pl.when(t > 0)
    def _():
        zbuf[0:8, :] = carry[...]

    for c in range(D_CONV // TN):
        cols = slice(c * TN, (c + 1) * TN)
        zc = _dot(xn, _bch_cols(wa_ref, wb_ref, 1, c)) * _dot(xn, _bch_cols(wa_ref, wb_ref, 2, c))
        zbuf[8:8 + tm, cols] = zc
        cw = cw_ref[:, cols]
        y = cw[0:1] * zbuf[6:6 + tm, cols] + cw[1:2] * zbuf[7:7 + tm, cols] + cw[2:3] * zc
        ob_ref[:, cols] = (_dot(xn, _bch_cols(wa_ref, wb_ref, 0, c)) * y).astype(BF16)
    carry[...] = zbuf[tm:tm + 8, :]
    nc_ref[...] = zbuf[tm + 6:tm + 8, :]


def _conv_prompt(xn_all, w_in, l, conv_w):
    nt = SEQ // TM_SEQ
    return pl.pallas_call(
        _conv_prompt_kernel,
        grid=(BATCH, nt),
        in_specs=[
            pl.BlockSpec((TM_SEQ, D_MODEL), lambda b, t: (b * nt + t, 0)),
            _layer_block((D_MODEL, WIDE), l, 0, 1),
            _layer_block((D_MODEL, WIDE), l, 0, 2),
            _layer_block((CONV_WIDTH, D_CONV), l, 0, 0),
        ],
        out_specs=[
            pl.BlockSpec((TM_SEQ, D_CONV), lambda b, t: (b * nt + t, 0)),
            pl.BlockSpec((None, CONV_WIDTH - 1, D_CONV), lambda b, t: (b, 0, 0)),
        ],
        out_shape=[
            jax.ShapeDtypeStruct((N_ROWS, D_CONV), BF16),
            jax.ShapeDtypeStruct((BATCH, CONV_WIDTH - 1, D_CONV), F32),
        ],
        scratch_shapes=[
            pltpu.VMEM((TM_SEQ + 8, D_CONV), F32),
            pltpu.VMEM((8, D_CONV), F32),
        ],
        compiler_params=_params(("arbitrary", "arbitrary")),
        name="conv_prompt",
    )(xn_all, w_in, w_in, conv_w)


def _conv_sample_kernel(xn_ref, wa_ref, wb_ref, cw_ref, cb0_ref, cb1_ref, ob_ref, zc_ref):
    xn = xn_ref[...]
    for c in range(D_CONV // TN):
        cols = slice(c * TN, (c + 1) * TN)
        zc = _dot(xn, _bch_cols(wa_ref, wb_ref, 1, c)) * _dot(xn, _bch_cols(wa_ref, wb_ref, 2, c))
        cw = cw_ref[:, cols]
        y = cw[0:1] * cb0_ref[:, cols] + cw[1:2] * cb1_ref[:, cols] + cw[2:3] * zc
        ob_ref[:, cols] = (_dot(xn, _bch_cols(wa_ref, wb_ref, 0, c)) * y).astype(BF16)
        zc_ref[:, cols] = zc


def _conv_sample(xn_all, w_in, l, conv_w, cb0, cb1):
    c2 = lambda i: (0, 0)
    return pl.pallas_call(
        _conv_sample_kernel,
        grid=(1,),
        in_specs=[
            pl.BlockSpec((DEC_BATCH, D_MODEL), lambda i: (SAMPLE_BLOCK, 0)),
            _layer_block((D_MODEL, WIDE), l, 0, 1),
            _layer_block((D_MODEL, WIDE), l, 0, 2),
            _layer_block((CONV_WIDTH, D_CONV), l, 0, 0),
            pl.BlockSpec((DEC_BATCH, D_CONV), c2),
            pl.BlockSpec((DEC_BATCH, D_CONV), c2),
        ],
        out_specs=[pl.BlockSpec((DEC_BATCH, D_CONV), c2), pl.BlockSpec((DEC_BATCH, D_CONV), c2)],
        out_shape=[
            jax.ShapeDtypeStruct((DEC_BATCH, D_CONV), BF16),
            jax.ShapeDtypeStruct((DEC_BATCH, D_CONV), F32),
        ],
        compiler_params=_params(("arbitrary",)),
        name="conv_sample",
    )(xn_all, w_in, w_in, conv_w, cb0, cb1)


HALF_V = WIDE - D_GMLP


def _gmlp_uv(xn, wa_ref, wb_ref, vn_ref):
    u = _gelu(_dot(xn, wa_ref[:, :D_GMLP]))
    v_lo = _gelu(_dot(xn, wa_ref[:, D_GMLP:]))
    v_hi = _gelu(_dot(xn, wb_ref[:, :D_GMLP - HALF_V]))
    ms = (jnp.sum(v_lo * v_lo, axis=-1, keepdims=True)
          + jnp.sum(v_hi * v_hi, axis=-1, keepdims=True)) * (1.0 / D_GMLP)
    r = lax.rsqrt(ms + EPS)
    return u, v_lo * r * vn_ref[:, :HALF_V], v_hi * r * vn_ref[:, HALF_V:]


def _gmlp_prompt_kernel(xn_ref, wa_ref, wb_ref, vn_ref, ws_ref, bs_ref, oc_ref, u_scr, v_scr):
    tm = xn_ref.shape[0]
    u, v_lo, v_hi = _gmlp_uv(xn_ref[...], wa_ref, wb_ref, vn_ref)
    u_scr[...] = u
    v_scr[:, :HALF_V] = v_lo.astype(BF16)
    v_scr[:, HALF_V:] = v_hi.astype(BF16)
    row = lax.broadcasted_iota(jnp.int32, (CHUNK, CHUNK), 0)
    col = lax.broadcasted_iota(jnp.int32, (CHUNK, CHUNK), 1)
    for grp in range(N_SPATIAL_GROUPS):
        w = jnp.where(col <= row, ws_ref[grp], 0.0).astype(BF16)
        cols = slice(grp * SPATIAL_GROUP_W, (grp + 1) * SPATIAL_GROUP_W)
        for ch in range(tm // CHUNK):
            rows = slice(ch * CHUNK, (ch + 1) * CHUNK)
            mix = _dot(w, v_scr[rows, cols]) + bs_ref[grp]
            oc_ref[rows, cols] = (u_scr[rows, cols] * mix).astype(BF16)


def _gmlp_prompt(xn_all, w_in, l, v_norm, w_s, b_s_wide):
    nt = N_PROMPT // TM_SEQ
    return pl.pallas_call(
        _gmlp_prompt_kernel,
        grid=(nt,),
        in_specs=[
            pl.BlockSpec((TM_SEQ, D_MODEL), lambda i: (i, 0)),
            _layer_block((D_MODEL, WIDE), l, 0, 3),
            _layer_block((D_MODEL, WIDE), l, 0, 4),
            _layer_block((1, D_GMLP), l, 0, 0),
            _layer_block((N_SPATIAL_GROUPS, CHUNK, CHUNK), l, 0, 0, 0),
            _layer_block((N_SPATIAL_GROUPS, CHUNK, SPATIAL_GROUP_W), l, 0, 0, 0),
        ],
        out_specs=pl.BlockSpec((TM_SEQ, D_GMLP), lambda i: (i, 0)),
        out_shape=jax.ShapeDtypeStruct((N_ROWS, D_GMLP), BF16),
        scratch_shapes=[pltpu.VMEM((TM_SEQ, D_GMLP), F32), pltpu.VMEM((TM_SEQ, D_GMLP), BF16)],
        compiler_params=_params(("arbitrary",)),
        name="gmlp_prompt",
    )(xn_all, w_in, w_in, v_norm, w_s, b_s_wide)


def _gmlp_sample_kernel(xn_ref, wa_ref, wb_ref, vn_ref, ws0_ref, bs0_ref, oc_ref, vg_ref):
    u, v_lo, v_hi = _gmlp_uv(xn_ref[...], wa_ref, wb_ref, vn_ref)
    vg_ref[:, :HALF_V] = v_lo
    vg_ref[:, HALF_V:] = v_hi
    oc_ref[...] = (u * (ws0_ref[...] * vg_ref[...] + bs0_ref[...])).astype(BF16)


def _gmlp_sample(xn_all, w_in, l, v_norm, ws0, bs0):
    c2 = lambda i: (0, 0)
    return pl.pallas_call(
        _gmlp_sample_kernel,
        grid=(1,),
        in_specs=[
            pl.BlockSpec((DEC_BATCH, D_MODEL), lambda i: (SAMPLE_BLOCK, 0)),
            _layer_block((D_MODEL, WIDE), l, 0, 3),
            _layer_block((D_MODEL, WIDE), l, 0, 4),
            _layer_block((1, D_GMLP), l, 0, 0),
            pl.BlockSpec((1, D_GMLP), c2),
            pl.BlockSpec((1, D_GMLP), c2),
        ],
        out_specs=[pl.BlockSpec((DEC_BATCH, D_GMLP), c2), pl.BlockSpec((DEC_BATCH, D_GMLP), c2)],
        out_shape=[
            jax.ShapeDtypeStruct((DEC_BATCH, D_GMLP), BF16),
            jax.ShapeDtypeStruct((DEC_BATCH, D_GMLP), F32),
        ],
        compiler_params=_params(("arbitrary",)),
        name="gmlp_sample",
    )(xn_all, w_in, w_in, v_norm, ws0, bs0)


def _merge_kernel(xn_ref, bg_ref, oa_ref, ob_ref, oc_ref, wg_hbm, wb_hbm, m_ref,
                  stage_g, stage_b, wg_bf, wb_bf, sem, *, layer):
    j = pl.program_id(0)
    i = pl.program_id(1)

    def weight_copies(jj):
        copies = []
        for br in range(N_BRANCHES):
            gcol = pl.multiple_of(OFF_G + br * D_MODEL + jj * TN, TN)
            bcol = pl.multiple_of(jj * TN, TN)
            copies.append(pltpu.make_async_copy(
                wg_hbm.at[layer, :, pl.ds(gcol, TN)], stage_g.at[br], sem.at[br]))
            copies.append(pltpu.make_async_copy(
                wb_hbm.at[layer, br, :, pl.ds(bcol, TN)], stage_b.at[br], sem.at[N_BRANCHES + br]))
        return copies

    @pl.when(i == 0)
    def _():
        @pl.when(j == 0)
        def _():
            for c in weight_copies(0):
                c.start()

        for c in weight_copies(j):
            c.wait()
        def cast_rows(stage, dst):
            def body(c, carry):
                r = pl.multiple_of(c * CAST_ROWS, CAST_ROWS)
                for br in range(N_BRANCHES):
                    dst[br, pl.ds(r, CAST_ROWS), :] = stage[br, pl.ds(r, CAST_ROWS), :].astype(BF16)
                return carry
            lax.fori_loop(0, stage.shape[1] // CAST_ROWS, body, 0)

        cast_rows(stage_g, wg_bf)
        cast_rows(stage_b, wb_bf)

        @pl.when(j + 1 < pl.num_programs(0))
        def _():
            for c in weight_copies(j + 1):
                c.start()

    xn = xn_ref[...]
    bg = bg_ref[...]
    acc = None
    for br, o_ref in enumerate((oa_ref, ob_ref, oc_ref)):
        gate = jax.nn.sigmoid(_dot(xn, wg_bf[br]) + bg[br:br + 1])
        term = gate * _dot(o_ref[...], wb_bf[br])
        acc = term if acc is None else acc + term
    m_ref[...] = acc.astype(BF16)


TM_MERGE = TM_ROW
CAST_ROWS = 256


def _merge(xn_all, w_in, l, b_gate, o_a, o_b, o_c, w_branch):
    nt = N_ROWS // TM_MERGE
    nn = D_MODEL // TN
    rows = lambda j, i: (i, 0)
    return pl.pallas_call(
        functools.partial(_merge_kernel, layer=l),
        grid=(nn, nt),
        in_specs=[
            pl.BlockSpec((TM_MERGE, D_MODEL), rows),
            pl.BlockSpec((None, N_BRANCHES, TN), lambda j, i: (l, 0, j)),
            pl.BlockSpec((TM_MERGE, BRANCH_W), rows),
            pl.BlockSpec((TM_MERGE, BRANCH_W), rows),
            pl.BlockSpec((TM_MERGE, BRANCH_W), rows),
            pl.BlockSpec(memory_space=pl.ANY),
            pl.BlockSpec(memory_space=pl.ANY),
        ],
        out_specs=pl.BlockSpec((TM_MERGE, TN), lambda j, i: (i, j)),
        out_shape=jax.ShapeDtypeStruct((N_ROWS, D_MODEL), BF16),
        scratch_shapes=[
            pltpu.VMEM((N_BRANCHES, D_MODEL, TN), F32),
            pltpu.VMEM((N_BRANCHES, BRANCH_W, TN), F32),
            pltpu.VMEM((N_BRANCHES, D_MODEL, TN), BF16),
            pltpu.VMEM((N_BRANCHES, BRANCH_W, TN), BF16),
            pltpu.SemaphoreType.DMA((2 * N_BRANCHES,)),
        ],
        compiler_params=_params(("arbitrary", "arbitrary")),
        name="merge",
    )(xn_all, b_gate, o_a, o_b, o_c, w_in, w_branch)


OUT_SPLIT = 2


def _out_proj_kernel(m_ref, w_ref, x_ref, g_ref, x1_ref, xn_ref):
    sub = m_ref.shape[0] // OUT_SPLIT
    for s in range(OUT_SPLIT):
        rows = slice(s * sub, (s + 1) * sub)
        x1 = x_ref[rows, :] + _dot(m_ref[rows, :], w_ref[...])
        x1_ref[rows, :] = x1
        xn_ref[rows, :] = _rms_rows(x1, g_ref[...]).astype(BF16)


def _out_proj(m, w_out, l, x, g_ffn, *, m_row_block0, tm, out_rows):
    rows = lambda i: (i, 0)
    return pl.pallas_call(
        _out_proj_kernel,
        grid=(x.shape[0] // tm,),
        in_specs=[
            pl.BlockSpec((tm, D_MODEL), lambda i: (m_row_block0 + i, 0)),
            _layer_block((D_MODEL, D_MODEL), l, 0, 0),
            pl.BlockSpec((tm, D_MODEL), rows),
            _layer_block((1, D_MODEL), l, 0, 0),
        ],
        out_specs=[pl.BlockSpec((tm, D_MODEL), rows), pl.BlockSpec((tm, D_MODEL), rows)],
        out_shape=[
            jax.ShapeDtypeStruct((out_rows, D_MODEL), F32),
            jax.ShapeDtypeStruct((out_rows, D_MODEL), BF16),
        ],
        compiler_params=_params(("arbitrary",)),
        name="out_proj",
    )(m, w_out, x, g_ffn)


TM_UP = 1664
UP_SPLIT = 4


def _ffn_up_kernel(xn_ref, wgate_ref, wup_ref, h_ref, wgate_scr, wup_scr):
    @pl.when(pl.program_id(1) == 0)
    def _():
        wgate_scr[...] = wgate_ref[...].astype(BF16)
        wup_scr[...] = wup_ref[...].astype(BF16)

    sub = xn_ref.shape[0] // UP_SPLIT
    for s in range(UP_SPLIT):
        rows = slice(s * sub, (s + 1) * sub)
        xn = xn_ref[rows, :]
        gate = _dot(xn, wgate_scr[...])
        h_ref[rows, :] = (gate * jax.nn.sigmoid(gate) * _dot(xn, wup_scr[...])).astype(BF16)


def _ffn_up(xn, w_gate_up, l):
    nt = N_ROWS // TM_UP
    nn = D_FF // TN
    return pl.pallas_call(
        _ffn_up_kernel,
        grid=(nn, nt),
        in_specs=[
            pl.BlockSpec((TM_UP, D_MODEL), lambda j, i: (i, 0)),
            pl.BlockSpec((None, D_MODEL, TN), lambda j, i: (l, 0, j)),
            pl.BlockSpec((None, D_MODEL, TN), lambda j, i: (l, 0, nn + j)),
        ],
        out_specs=pl.BlockSpec((TM_UP, TN), lambda j, i: (i, j)),
        out_shape=jax.ShapeDtypeStruct((N_ROWS, D_FF), BF16),
        scratch_shapes=[pltpu.VMEM((D_MODEL, TN), BF16), pltpu.VMEM((D_MODEL, TN), BF16)],
        compiler_params=_params(("arbitrary", "arbitrary")),
        name="ffn_up",
    )(xn, w_gate_up, w_gate_up)


def _ffn_down_kernel(h_ref, w_ref, x_ref, o_ref, w_scr):
    @pl.when(pl.program_id(1) == 0)
    def _():
        w_scr[...] = w_ref[...].astype(BF16)

    o_ref[...] = x_ref[...] + _dot(h_ref[...], w_scr[...])


def _ffn_down(h, w_down, l, x1, *, row_block0, n_rows, tm):
    nt = n_rows // tm
    nn = D_MODEL // TN
    return pl.pallas_call(
        _ffn_down_kernel,
        grid=(nn, nt),
        in_specs=[
            pl.BlockSpec((tm, D_FF), lambda j, i: (row_block0 + i, 0)),
            pl.BlockSpec((None, D_FF, TN), lambda j, i: (l, 0, j)),
            pl.BlockSpec((tm, TN), lambda j, i: (row_block0 + i, j)),
        ],
        out_specs=pl.BlockSpec((tm, TN), lambda j, i: (i, j)),
        out_shape=jax.ShapeDtypeStruct((n_rows, D_MODEL), F32),
        scratch_shapes=[pltpu.VMEM((D_FF, TN), BF16)],
        compiler_params=_params(("arbitrary", "arbitrary")),
        name="ffn_down",
    )(h, w_down, x1)


def _block_diag_ones(width):
    head = np.arange(width) // HEAD_DIM
    return jnp.asarray(head[:, None] == head[None, :], dtype=BF16)


def _fill_sample_rows(o_prompt, o_sample):
    return lax.dynamic_update_slice(o_prompt, o_sample, (N_PROMPT, 0))


def _layer(x, l, cache_k, cache_v, state_conv, p, const):
    first, last = l == 0, l == DEPTH - 1
    x_p, x_s = x if first else (x, x)
    w_in = p["w_in"]
    qn = jnp.tile(p["q_norm"][l], N_HEADS).reshape(1, BRANCH_W)
    kn = jnp.tile(p["k_norm"][l], N_KV_HEADS).reshape(1, KV_W)
    sinks = p["sinks"][l]

    xn_p, q_p, k_p, v_p, klast_p, vlast_p = _qkv(
        x_p, p["norm_mix"], w_in, l, qn, kn, const["ph"],
        row_block0=0, n_rows=N_PROMPT, tm=TM_SEQ, xn_rows=N_ROWS, tiles_per_seq=SEQ // TM_SEQ)
    xn_s, q_s, k_s, v_s = _qkv(
        x_s, p["norm_mix"], w_in, l, qn, kn, const["ph"],
        row_block0=0 if first else SAMPLE_BLOCK, n_rows=DEC_BATCH, tm=DEC_BATCH, xn_rows=DEC_BATCH)
    xn = _fill_sample_rows(xn_p, xn_s)
    sink_rows = jnp.repeat(sinks, WINDOW).reshape(N_KV_HEADS, 1, GQA_GROUP * WINDOW)
    oa_p = _attn_prompt(sink_rows, q_p, k_p, v_p, const["bias_p"])
    oa_s = _attn_sample(
        q_s.reshape(DEC_BATCH, N_HEADS, HEAD_DIM),
        k_s.reshape(DEC_BATCH, 1, KV_W), v_s.reshape(DEC_BATCH, 1, KV_W),
        cache_k, cache_v, l,
        const["bias_s"], sinks.reshape(N_HEADS, 1), const["mask_s"], const["rep_s"],
    ).reshape(DEC_BATCH, BRANCH_W)

    ob_p, nc_p = _conv_prompt(xn, w_in, l, p["conv_w"])
    ob_s, zc_s = _conv_sample(xn, w_in, l, p["conv_w"], state_conv[l, :, 0], state_conv[l, :, 1])

    w_s = p["w_spatial"]
    b_s = p["b_spatial"]
    bs_wide = jnp.broadcast_to(b_s[:, :, :, None], (DEPTH, N_SPATIAL_GROUPS, CHUNK, SPATIAL_GROUP_W))
    oc_p = _gmlp_prompt(xn, w_in, l, p["v_norm"], w_s, bs_wide)
    ws0 = jnp.repeat(w_s[l, :, 0, 0], SPATIAL_GROUP_W).reshape(1, D_GMLP)
    bs0 = jnp.repeat(b_s[l, :, 0], SPATIAL_GROUP_W).reshape(1, D_GMLP)
    oc_s, vg_s = _gmlp_sample(xn, w_in, l, p["v_norm"], ws0, bs0)

    o_a = _fill_sample_rows(oa_p, oa_s)
    o_b = _fill_sample_rows(ob_p, ob_s)
    o_c = _fill_sample_rows(oc_p, oc_s)
    m = _merge(xn, p["w_in_f32"], l, p["b_gate"], o_a, o_b, o_c, p["w_branch"])
    if first:
        x1_p, xn2_p = _out_proj(m, p["w_out"], l, x_p, p["norm_ffn"], m_row_block0=0, tm=TM_SEQ, out_rows=N_ROWS)
        x1_s, xn2_s = _out_proj(m, p["w_out"], l, x_s, p["norm_ffn"], m_row_block0=SAMPLE_BLOCK,
                                tm=DEC_BATCH, out_rows=DEC_BATCH)
        x1, xn2 = _fill_sample_rows(x1_p, x1_s), _fill_sample_rows(xn2_p, xn2_s)
    else:
        x1, xn2 = _out_proj(m, p["w_out"], l, x, p["norm_ffn"], m_row_block0=0, tm=TM_ROW, out_rows=N_ROWS)
    h = _ffn_up(xn2, p["w_gate_up"], l)
    if last:
        x2 = (_ffn_down(h, p["w_down"], l, x1, row_block0=0, n_rows=N_PROMPT, tm=TM_SEQ),
              _ffn_down(h, p["w_down"], l, x1, row_block0=SAMPLE_BLOCK, n_rows=DEC_BATCH, tm=DEC_BATCH))
    else:
        x2 = _ffn_down(h, p["w_down"], l, x1, row_block0=0, n_rows=N_ROWS, tm=TM_ROW)

    new_k_p = klast_p.reshape(BATCH, WINDOW, N_KV_HEADS, HEAD_DIM)
    new_v_p = vlast_p.reshape(BATCH, WINDOW, N_KV_HEADS, HEAD_DIM)
    new_k_s = k_s.reshape(DEC_BATCH, 1, N_KV_HEADS, HEAD_DIM)
    new_v_s = v_s.reshape(DEC_BATCH, 1, N_KV_HEADS, HEAD_DIM)
    new_conv_s = jnp.stack([state_conv[l, :, 1], zc_s], axis=1)
    return x2, (new_k_p, new_v_p, nc_p, new_k_s, new_v_s, new_conv_s, vg_s.reshape(DEC_BATCH, 1, D_GMLP))


def kernel(x_prompt, x_sample, cache_k, cache_v, state_conv, norm_mix, w_in, b_gate, q_norm, k_norm,
           sinks, conv_w, v_norm, w_spatial, b_spatial, w_branch, w_out, norm_ffn, w_gate_up, w_down):
    assert min(WINDOW, SEQ) == WINDOW and SEQ % TM_SEQ == 0 and TM_SEQ >= WINDOW
    n_buf = cache_k.shape[2]
    assert n_buf == WINDOW
    cache_k = jnp.transpose(cache_k, (0, 1, 3, 4, 2)).reshape(DEPTH, DEC_BATCH, KV_W, n_buf)
    cache_v = jnp.transpose(cache_v, (0, 1, 3, 4, 2)).reshape(DEPTH, DEC_BATCH, KV_W, n_buf)
    p = {
        "norm_mix": norm_mix.reshape(DEPTH, 1, D_MODEL),
        "norm_ffn": norm_ffn.reshape(DEPTH, 1, D_MODEL),
        "v_norm": v_norm.reshape(DEPTH, 1, D_GMLP),
        "w_in": w_in[:, :, :N_WIDE * WIDE].astype(BF16),
        "w_in_f32": w_in,
        "w_branch": w_branch,
        "w_out": w_out.astype(BF16),
        "w_gate_up": w_gate_up,
        "w_down": w_down,
        "b_gate": b_gate, "q_norm": q_norm, "k_norm": k_norm, "sinks": sinks, "conv_w": conv_w,
        "w_spatial": w_spatial, "b_spatial": b_spatial,
    }
    kv_of_col = np.arange(KV_W) // HEAD_DIM
    kv_of_head = np.arange(N_HEADS) // GQA_GROUP
    const = {
        "ph": _block_diag_ones(KV_W),
        "bias_p": _prompt_bias(),
        "bias_s": _sample_bias(n_buf),
        "mask_s": jnp.asarray(kv_of_head[:, None] == kv_of_col[None, :], dtype=F32),
        "rep_s": jnp.asarray(np.tile(np.eye(HEAD_DIM), (1, N_KV_HEADS)), dtype=BF16),
    }
    x = (x_prompt.reshape(N_PROMPT, D_MODEL), x_sample.reshape(DEC_BATCH, D_MODEL))
    per_layer = []
    for l in range(DEPTH):
        x, outs = _layer(x, l, cache_k, cache_v, state_conv, p, const)
        per_layer.append(outs)
    stacked = [jnp.stack([per_layer[l][i] for l in range(DEPTH)]) for i in range(7)]
    y_prompt, y_sample = x
    return (y_prompt.reshape(BATCH, SEQ, D_MODEL), y_sample.reshape(DEC_BATCH, 1, D_MODEL), *stacked)
```

```python
import functools

import jax
import jax.numpy as jnp
import numpy as np
from jax import lax
from jax.experimental import pallas as pl
from jax.experimental.pallas import tpu as pltpu

D_MODEL = 2048
BATCH = 4
SEQ = 2048
DEPTH = 2
DEC_BATCH = 128
BRANCH_W = 1024
HEAD_DIM = 64
N_HEADS = 16
N_KV_HEADS = 4
GQA_GROUP = 4
KV_W = 256
WINDOW = 128
D_CONV = 1024
CONV_WIDTH = 3
D_GMLP = 1024
CHUNK = 128
N_SPATIAL_GROUPS = 8
SPATIAL_GROUP_W = 128
N_BRANCHES = 3
D_FF = 5632
EPS = 1e-6
NEG_INF = -1e30

N_PROMPT = BATCH * SEQ
N_ROWS = N_PROMPT + DEC_BATCH
SAMPLE_BLOCK = N_PROMPT // DEC_BATCH

WIDE = 1536
N_WIDE = 5
OFF_G = 6656

TM_SEQ = 512
TM_ROW = 640
TN = 512
VMEM_LIMIT = 56 * 1024 * 1024

F32 = jnp.float32
BF16 = jnp.bfloat16


def _params(sem):
    return pltpu.CompilerParams(dimension_semantics=sem, vmem_limit_bytes=VMEM_LIMIT)


def _rms_rows(x, g):
    ms = jnp.mean(x * x, axis=-1, keepdims=True)
    return x * lax.rsqrt(ms + EPS) * g


def _dot(a, b):
    return jnp.dot(a, b, preferred_element_type=F32)


def _dot_nt(a, b):
    return lax.dot_general(a, b, (((1,), (1,)), ((), ())), preferred_element_type=F32)


def _gelu(x):
    return 0.5 * x * (1.0 + jnp.tanh(np.sqrt(2.0 / np.pi).astype(np.float32) * (x + 0.044715 * (x * x * x))))


def _layer_block(shape, l, *idx):
    return pl.BlockSpec((None, *shape), lambda *_: (l, *idx))


QKV_SPLIT = 2


def _qkv_kernel(x_ref, g_ref, w_ref, qn_ref, kn_ref, ph_ref, xn_ref, q_ref, k_ref, v_ref, *last_refs):
    tm = x_ref.shape[0]
    sub = tm // QKV_SPLIT
    for s in range(QKV_SPLIT):
        rows = slice(s * sub, (s + 1) * sub)
        xn = _rms_rows(x_ref[rows, :], g_ref[...]).astype(BF16)
        xn_ref[rows, :] = xn
        z = _dot(xn, w_ref[...])
        q = z[:, :BRANCH_W]
        k = z[:, BRANCH_W:BRANCH_W + KV_W]
        v = z[:, BRANCH_W + KV_W:]
        q_sq = (q * q).astype(BF16)
        q_ms = jnp.concatenate(
            [_dot(q_sq[:, c:c + KV_W], ph_ref[...]) for c in range(0, BRANCH_W, KV_W)], axis=1) * (1.0 / HEAD_DIM)
        k_ms = _dot((k * k).astype(BF16), ph_ref[...]) * (1.0 / HEAD_DIM)
        q_ref[rows, :] = (q * lax.rsqrt(q_ms + EPS) * qn_ref[...] * (HEAD_DIM ** -0.5)).astype(BF16)
        kn = k * lax.rsqrt(k_ms + EPS) * kn_ref[...]
        k_ref[rows, :] = kn
        v_ref[rows, :] = v
        if last_refs and s == QKV_SPLIT - 1:
            klast_ref, vlast_ref = last_refs
            klast_ref[...] = kn[sub - WINDOW:]
            vlast_ref[...] = v[sub - WINDOW:]


def _qkv(x, g, w_in, l, qn, kn, ph, *, row_block0, n_rows, tm, xn_rows, tiles_per_seq=None):
    nt = n_rows // tm
    const = lambda i: (0, 0)
    rows = lambda i: (i, 0)
    out_specs = [
        pl.BlockSpec((tm, D_MODEL), rows),
        pl.BlockSpec((tm, BRANCH_W), rows),
        pl.BlockSpec((tm, KV_W), rows),
        pl.BlockSpec((tm, KV_W), rows),
    ]
    out_shape = [
        jax.ShapeDtypeStruct((xn_rows, D_MODEL), BF16),
        jax.ShapeDtypeStruct((n_rows, BRANCH_W), BF16),
        jax.ShapeDtypeStruct((n_rows, KV_W), F32),
        jax.ShapeDtypeStruct((n_rows, KV_W), F32),
    ]
    if tiles_per_seq is not None:
        n_seq = nt // tiles_per_seq
        out_specs += [pl.BlockSpec((WINDOW, KV_W), lambda i: (i // tiles_per_seq, 0))] * 2
        out_shape += [jax.ShapeDtypeStruct((n_seq * WINDOW, KV_W), F32)] * 2
    return pl.pallas_call(
        _qkv_kernel,
        grid=(nt,),
        in_specs=[
            pl.BlockSpec((tm, D_MODEL), lambda i: (row_block0 + i, 0)),
            _layer_block((1, D_MODEL), l, 0, 0),
            _layer_block((D_MODEL, WIDE), l, 0, 0),
            pl.BlockSpec((1, BRANCH_W), const),
            pl.BlockSpec((1, KV_W), const),
            pl.BlockSpec((KV_W, KV_W), const),
        ],
        out_specs=out_specs,
        out_shape=out_shape,
        compiler_params=_params(("arbitrary",)),
        name="qkv",
    )(x, g, w_in, qn, kn, ph)


def _attn_prompt_kernel(q_ref, kp_ref, ko_ref, vp_ref, vo_ref, bias0_ref, bias1_ref, sink_ref, o_ref, s_scr):
    k_all = jnp.concatenate([kp_ref[...], ko_ref[...]], axis=0).astype(BF16)
    v_all_t = jnp.transpose(jnp.concatenate([vp_ref[...], vo_ref[...]], axis=0)).astype(BF16)
    for qb in range(ATTN_BLOCKS):
        kk = k_all[qb * WINDOW:(qb + 2) * WINDOW, :]
        for g in range(N_KV_HEADS):
            kg = kk[:, g * HEAD_DIM:(g + 1) * HEAD_DIM]
            heads = range(g * GQA_GROUP, (g + 1) * GQA_GROUP)
            qg = jnp.concatenate(
                [q_ref[qb * WINDOW:(qb + 1) * WINDOW, h * HEAD_DIM:(h + 1) * HEAD_DIM] for h in heads], axis=0)
            s_scr[qb, g] = _dot_nt(kg, qg)
    ones = jnp.ones((ONES_ROWS, 2 * WINDOW), BF16)
    for qb in range(ATTN_BLOCKS):
        bias_ref = bias0_ref if qb == 0 else bias1_ref
        vv_t = v_all_t[:, qb * WINDOW:(qb + 2) * WINDOW]
        for g in range(N_KV_HEADS):
            lhs = jnp.concatenate([vv_t[g * HEAD_DIM:(g + 1) * HEAD_DIM, :], ones], axis=0)
            o_t = []
            for i in range(GQA_GROUP):
                cols = slice(i * WINDOW, (i + 1) * WINDOW)
                s = s_scr[qb, g, :, cols] + bias_ref[g, :, cols]
                sink = sink_ref[g, :, cols]
                m = jnp.maximum(jnp.max(s, axis=0, keepdims=True), sink)
                p = jnp.exp(s - m).astype(BF16)
                oe = _dot(lhs, p)
                den = oe[HEAD_DIM:HEAD_DIM + 1, :] + jnp.exp(sink - m)
                o_t.append(oe[:HEAD_DIM, :] / den)
            o_ref[qb * WINDOW:(qb + 1) * WINDOW, g * KV_W:(g + 1) * KV_W] = (
                jnp.transpose(jnp.concatenate(o_t, axis=0)).astype(BF16))


ONES_ROWS = 16
ATTN_BLOCKS = 4


def _attn_prompt(sink_rows, q, k, v, bias):
    nb = SEQ // WINDOW
    ns = nb // ATTN_BLOCKS
    rows = ATTN_BLOCKS * WINDOW
    own = lambda b, j: (b * ns + j, 0)
    prev = lambda b, j: (b * nb + jnp.maximum(ATTN_BLOCKS * j - 1, 0), 0)
    group_q = GQA_GROUP * WINDOW
    bias_block = (None, N_KV_HEADS, 2 * WINDOW, group_q)
    return pl.pallas_call(
        _attn_prompt_kernel,
        grid=(BATCH, ns),
        in_specs=[
            pl.BlockSpec((rows, BRANCH_W), own),
            pl.BlockSpec((WINDOW, KV_W), prev),
            pl.BlockSpec((rows, KV_W), own),
            pl.BlockSpec((WINDOW, KV_W), prev),
            pl.BlockSpec((rows, KV_W), own),
            pl.BlockSpec(bias_block, lambda b, j: (jnp.minimum(j, 1), 0, 0, 0)),
            pl.BlockSpec(bias_block, lambda b, j: (1, 0, 0, 0)),
            pl.BlockSpec((N_KV_HEADS, 1, group_q), lambda b, j: (0, 0, 0)),
        ],
        out_specs=pl.BlockSpec((rows, BRANCH_W), own),
        out_shape=jax.ShapeDtypeStruct((N_ROWS, BRANCH_W), BF16),
        scratch_shapes=[pltpu.VMEM((ATTN_BLOCKS, N_KV_HEADS, 2 * WINDOW, group_q), F32)],
        compiler_params=_params(("arbitrary", "arbitrary")),
        name="attn_prompt",
    )(q, k, k, v, v, bias, bias, sink_rows)


def _prompt_bias():
    slopes = jnp.exp2(-8.0 * jnp.arange(1, N_HEADS + 1, dtype=F32) / N_HEADS)
    qi = jnp.arange(WINDOW, dtype=jnp.int32)[:, None]
    ki = jnp.arange(2 * WINDOW, dtype=jnp.int32)[None, :] - WINDOW
    dist = qi - ki
    valid = (dist >= 0) & (dist < WINDOW)
    bias = -slopes[:, None, None] * dist.astype(F32)[None]
    with_prev = jnp.where(valid[None], bias, NEG_INF)
    first = jnp.where((valid & (ki >= 0))[None], bias, NEG_INF)
    per_head = jnp.stack([first, with_prev])
    grouped = per_head.reshape(2, N_KV_HEADS, GQA_GROUP, WINDOW, 2 * WINDOW)
    return jnp.transpose(grouped, (0, 1, 4, 2, 3)).reshape(2, N_KV_HEADS, 2 * WINDOW, GQA_GROUP * WINDOW)


SAMPLE_BT = 16


def _attn_sample_kernel(q_ref, kn_ref, vn_ref, ck_ref, cv_ref, bias_ref, sink_ref, mask_ref,
                        rep_ref, o_ref):
    bt = q_ref.shape[0]
    mask = mask_ref[...][None]
    sink = sink_ref[...][None]
    qe = _dot(q_ref[...].reshape(bt * N_HEADS, HEAD_DIM), rep_ref[...])
    qe = qe.reshape(bt, N_HEADS, KV_W) * mask
    qe_bf = qe.astype(BF16)
    s = jnp.stack([_dot(qe_bf[b], ck_ref[b].astype(BF16)) for b in range(bt)])
    s = s + bias_ref[...][None]
    s_new = jnp.sum(qe * kn_ref[...], axis=-1, keepdims=True)
    m = jnp.maximum(jnp.maximum(jnp.max(s, axis=-1, keepdims=True), s_new), sink)
    p = jnp.exp(s - m)
    p_new = jnp.exp(s_new - m)
    den = jnp.sum(p, axis=-1, keepdims=True) + p_new + jnp.exp(sink - m)
    p_bf = p.astype(BF16)
    of = jnp.stack([_dot_nt(p_bf[b], cv_ref[b].astype(BF16)) for b in range(bt)])
    of = (of + p_new * vn_ref[...]) * mask / den
    o = (of[..., 0:64] + of[..., 64:128]) + (of[..., 128:192] + of[..., 192:256])
    o_ref[...] = o.astype(BF16)


def _attn_sample(q3, k_new, v_new, ck, cv, l, bias, sinks_col, mask, rep):
    nsteps = DEC_BATCH // SAMPLE_BT
    b3 = lambda i: (i, 0, 0)
    c2 = lambda i: (0, 0)
    cache = pl.BlockSpec((None, SAMPLE_BT, KV_W, WINDOW), lambda i: (l, i, 0, 0))
    return pl.pallas_call(
        _attn_sample_kernel,
        grid=(nsteps,),
        in_specs=[
            pl.BlockSpec((SAMPLE_BT, N_HEADS, HEAD_DIM), b3),
            pl.BlockSpec((SAMPLE_BT, 1, KV_W), b3),
            pl.BlockSpec((SAMPLE_BT, 1, KV_W), b3),
            cache,
            cache,
            pl.BlockSpec((N_HEADS, WINDOW), c2),
            pl.BlockSpec((N_HEADS, 1), c2),
            pl.BlockSpec((N_HEADS, KV_W), c2),
            pl.BlockSpec((HEAD_DIM, KV_W), c2),
        ],
        out_specs=pl.BlockSpec((SAMPLE_BT, N_HEADS, HEAD_DIM), b3),
        out_shape=jax.ShapeDtypeStruct((DEC_BATCH, N_HEADS, HEAD_DIM), BF16),
        compiler_params=_params(("arbitrary",)),
        name="attn_sample",
    )(q3, k_new, v_new, ck, cv, bias, sinks_col, mask, rep)


def _sample_bias(n_buf):
    slopes = jnp.exp2(-8.0 * jnp.arange(1, N_HEADS + 1, dtype=F32) / N_HEADS)
    dist = n_buf - jnp.arange(n_buf, dtype=jnp.int32)
    bias = -slopes[:, None] * dist.astype(F32)[None, :]
    return jnp.where((dist < WINDOW)[None, :], bias, NEG_INF)


def _bch_cols(wa_ref, wb_ref, seg, c):
    r = seg * D_CONV + c * TN
    return wa_ref[:, r:r + TN] if r < WIDE else wb_ref[:, r - WIDE:r - WIDE + TN]


def _conv_prompt_kernel(xn_ref, wa_ref, wb_ref, cw_ref, ob_ref, nc_ref, zbuf, carry):
    t = pl.program_id(1)
    tm = xn_ref.shape[0]
    xn = xn_ref[...]

    @pl.when(t == 0)
    def _():
        zbuf[0:8, :] = jnp.zeros((8, D_CONV), F32)

    @pl.when(t > 0)
    def _():
        zbuf[0:8, :] = carry[...]

    for c in range(D_CONV // TN):
        cols = slice(c * TN, (c + 1) * TN)
        zc = _dot(xn, _bch_cols(wa_ref, wb_ref, 1, c)) * _dot(xn, _bch_cols(wa_ref, wb_ref, 2, c))
        zbuf[8:8 + tm, cols] = zc
        cw = cw_ref[:, cols]
        y = cw[0:1] * zbuf[6:6 + tm, cols] + cw[1:2] * zbuf[7:7 + tm, cols] + cw[2:3] * zc
        ob_ref[:, cols] = (_dot(xn, _bch_cols(wa_ref, wb_ref, 0, c)) * y).astype(BF16)
    carry[...] = zbuf[tm:tm + 8, :]
    nc_ref[...] = zbuf[tm + 6:tm + 8, :]


def _conv_prompt(xn_all, w_in, l, conv_w):
    nt = SEQ // TM_SEQ
    return pl.pallas_call(
        _conv_prompt_kernel,
        grid=(BATCH, nt),
        in_specs=[
            pl.BlockSpec((TM_SEQ, D_MODEL), lambda b, t: (b * nt + t, 0)),
            _layer_block((D_MODEL, WIDE), l, 0, 1),
            _layer_block((D_MODEL, WIDE), l, 0, 2),
            _layer_block((CONV_WIDTH, D_CONV), l, 0, 0),
        ],
        out_specs=[
            pl.BlockSpec((TM_SEQ, D_CONV), lambda b, t: (b * nt + t, 0)),
            pl.BlockSpec((None, CONV_WIDTH - 1, D_CONV), lambda b, t: (b, 0, 0)),
        ],
        out_shape=[
            jax.ShapeDtypeStruct((N_ROWS, D_CONV), BF16),
            jax.ShapeDtypeStruct((BATCH, CONV_WIDTH - 1, D_CONV), F32),
        ],
        scratch_shapes=[
            pltpu.VMEM((TM_SEQ + 8, D_CONV), F32),
            pltpu.VMEM((8, D_CONV), F32),
        ],
        compiler_params=_params(("arbitrary", "arbitrary")),
        name="conv_prompt",
    )(xn_all, w_in, w_in, conv_w)


def _conv_sample_kernel(xn_ref, wa_ref, wb_ref, cw_ref, cb0_ref, cb1_ref, ob_ref, zc_ref):
    xn = xn_ref[...]
    for c in range(D_CONV // TN):
        cols = slice(c * TN, (c + 1) * TN)
        zc = _dot(xn, _bch_cols(wa_ref, wb_ref, 1, c)) * _dot(xn, _bch_cols(wa_ref, wb_ref, 2, c))
        cw = cw_ref[:, cols]
        y = cw[0:1] * cb0_ref[:, cols] + cw[1:2] * cb1_ref[:, cols] + cw[2:3] * zc
        ob_ref[:, cols] = (_dot(xn, _bch_cols(wa_ref, wb_ref, 0, c)) * y).astype(BF16)
        zc_ref[:, cols] = zc


def _conv_sample(xn_all, w_in, l, conv_w, cb0, cb1):
    c2 = lambda i: (0, 0)
    return pl.pallas_call(
        _conv_sample_kernel,
        grid=(1,),
        in_specs=[
            pl.BlockSpec((DEC_BATCH, D_MODEL), lambda i: (SAMPLE_BLOCK, 0)),
            _layer_block((D_MODEL, WIDE), l, 0, 1),
            _layer_block((D_MODEL, WIDE), l, 0, 2),
            _layer_block((CONV_WIDTH, D_CONV), l, 0, 0),
            pl.BlockSpec((DEC_BATCH, D_CONV), c2),
            pl.BlockSpec((DEC_BATCH, D_CONV), c2),
        ],
        out_specs=[pl.BlockSpec((DEC_BATCH, D_CONV), c2), pl.BlockSpec((DEC_BATCH, D_CONV), c2)],
        out_shape=[
            jax.ShapeDtypeStruct((DEC_BATCH, D_CONV), BF16),
            jax.ShapeDtypeStruct((DEC_BATCH, D_CONV), F32),
        ],
        compiler_params=_params(("arbitrary",)),
        name="conv_sample",
    )(xn_all, w_in, w_in, conv_w, cb0, cb1)


HALF_V = WIDE - D_GMLP


def _gmlp_uv(xn, wa_ref, wb_ref, vn_ref):
    u = _gelu(_dot(xn, wa_ref[:, :D_GMLP]))
    v_lo = _gelu(_dot(xn, wa_ref[:, D_GMLP:]))
    v_hi = _gelu(_dot(xn, wb_ref[:, :D_GMLP - HALF_V]))
    ms = (jnp.sum(v_lo * v_lo, axis=-1, keepdims=True)
          + jnp.sum(v_hi * v_hi, axis=-1, keepdims=True)) * (1.0 / D_GMLP)
    r = lax.rsqrt(ms + EPS)
    return u, v_lo * r * vn_ref[:, :HALF_V], v_hi * r * vn_ref[:, HALF_V:]


def _gmlp_prompt_kernel(xn_ref, wa_ref, wb_ref, vn_ref, ws_ref, bs_ref, oc_ref, u_scr, v_scr):
    tm = xn_ref.shape[0]
    u, v_lo, v_hi = _gmlp_uv(xn_ref[...], wa_ref, wb_ref, vn_ref)
    u_scr[...] = u
    v_scr[:, :HALF_V] = v_lo.astype(BF16)
    v_scr[:, HALF_V:] = v_hi.astype(BF16)
    row = lax.broadcasted_iota(jnp.int32, (CHUNK, CHUNK), 0)
    col = lax.broadcasted_iota(jnp.int32, (CHUNK, CHUNK), 1)
    for grp in range(N_SPATIAL_GROUPS):
        w = jnp.where(col <= row, ws_ref[grp], 0.0).astype(BF16)
        cols = slice(grp * SPATIAL_GROUP_W, (grp + 1) * SPATIAL_GROUP_W)
        for ch in range(tm // CHUNK):
            rows = slice(ch * CHUNK, (ch + 1) * CHUNK)
            mix = _dot(w, v_scr[rows, cols]) + bs_ref[grp]
            oc_ref[rows, cols] = (u_scr[rows, cols] * mix).astype(BF16)


def _gmlp_prompt(xn_all, w_in, l, v_norm, w_s, b_s_wide):
    nt = N_PROMPT // TM_SEQ
    return pl.pallas_call(
        _gmlp_prompt_kernel,
        grid=(nt,),
        in_specs=[
            pl.BlockSpec((TM_SEQ, D_MODEL), lambda i: (i, 0)),
            _layer_block((D_MODEL, WIDE), l, 0, 3),
            _layer_block((D_MODEL, WIDE), l, 0, 4),
            _layer_block((1, D_GMLP), l, 0, 0),
            _layer_block((N_SPATIAL_GROUPS, CHUNK, CHUNK), l, 0, 0, 0),
            _layer_block((N_SPATIAL_GROUPS, CHUNK, SPATIAL_GROUP_W), l, 0, 0, 0),
        ],
        out_specs=pl.BlockSpec((TM_SEQ, D_GMLP), lambda i: (i, 0)),
        out_shape=jax.ShapeDtypeStruct((N_ROWS, D_GMLP), BF16),
        scratch_shapes=[pltpu.VMEM((TM_SEQ, D_GMLP), F32), pltpu.VMEM((TM_SEQ, D_GMLP), BF16)],
        compiler_params=_params(("arbitrary",)),
        name="gmlp_prompt",
    )(xn_all, w_in, w_in, v_norm, w_s, b_s_wide)


def _gmlp_sample_kernel(xn_ref, wa_ref, wb_ref, vn_ref, ws0_ref, bs0_ref, oc_ref, vg_ref):
    u, v_lo, v_hi = _gmlp_uv(xn_ref[...], wa_ref, wb_ref, vn_ref)
    vg_ref[:, :HALF_V] = v_lo
    vg_ref[:, HALF_V:] = v_hi
    oc_ref[...] = (u * (ws0_ref[...] * vg_ref[...] + bs0_ref[...])).astype(BF16)


def _gmlp_sample(xn_all, w_in, l, v_norm, ws0, bs0):
    c2 = lambda i: (0, 0)
    return pl.pallas_call(
        _gmlp_sample_kernel,
        grid=(1,),
        in_specs=[
            pl.BlockSpec((DEC_BATCH, D_MODEL), lambda i: (SAMPLE_BLOCK, 0)),
            _layer_block((D_MODEL, WIDE), l, 0, 3),
            _layer_block((D_MODEL, WIDE), l, 0, 4),
            _layer_block((1, D_GMLP), l, 0, 0),
            pl.BlockSpec((1, D_GMLP), c2),
            pl.BlockSpec((1, D_GMLP), c2),
        ],
        out_specs=[pl.BlockSpec((DEC_BATCH, D_GMLP), c2), pl.BlockSpec((DEC_BATCH, D_GMLP), c2)],
        out_shape=[
            jax.ShapeDtypeStruct((DEC_BATCH, D_GMLP), BF16),
            jax.ShapeDtypeStruct((DEC_BATCH, D_GMLP), F32),
        ],
        compiler_params=_params(("arbitrary",)),
        name="gmlp_sample",
    )(xn_all, w_in, w_in, v_norm, ws0, bs0)


def _merge_kernel(xn_ref, bg_ref, oa_ref, ob_ref, oc_ref, wg_hbm, wb_hbm, m_ref,
                  stage_g, stage_b, wg_bf, wb_bf, sem, *, layer):
    j = pl.program_id(0)
    i = pl.program_id(1)

    def weight_copies(jj):
        copies = []
        for br in range(N_BRANCHES):
            gcol = pl.multiple_of(OFF_G + br * D_MODEL + jj * TN, TN)
            bcol = pl.multiple_of(jj * TN, TN)
            copies.append(pltpu.make_async_copy(
                wg_hbm.at[layer, :, pl.ds(gcol, TN)], stage_g.at[br], sem.at[br]))
            copies.append(pltpu.make_async_copy(
                wb_hbm.at[layer, br, :, pl.ds(bcol, TN)], stage_b.at[br], sem.at[N_BRANCHES + br]))
        return copies

    @pl.when(i == 0)
    def _():
        @pl.when(j == 0)
        def _():
            for c in weight_copies(0):
                c.start()

        for c in weight_copies(j):
            c.wait()
        def cast_rows(stage, dst):
            def body(c, carry):
                r = pl.multiple_of(c * CAST_ROWS, CAST_ROWS)
                for br in range(N_BRANCHES):
                    dst[br, pl.ds(r, CAST_ROWS), :] = stage[br, pl.ds(r, CAST_ROWS), :].astype(BF16)
                return carry
            lax.fori_loop(0, stage.shape[1] // CAST_ROWS, body, 0)

        cast_rows(stage_g, wg_bf)
        cast_rows(stage_b, wb_bf)

        @pl.when(j + 1 < pl.num_programs(0))
        def _():
            for c in weight_copies(j + 1):
                c.start()

    xn = xn_ref[...]
    bg = bg_ref[...]
    acc = None
    for br, o_ref in enumerate((oa_ref, ob_ref, oc_ref)):
        gate = jax.nn.sigmoid(_dot(xn, wg_bf[br]) + bg[br:br + 1])
        term = gate * _dot(o_ref[...], wb_bf[br])
        acc = term if acc is None else acc + term
    m_ref[...] = acc.astype(BF16)


TM_MERGE = TM_ROW
CAST_ROWS = 256


def _merge(xn_all, w_in, l, b_gate, o_a, o_b, o_c, w_branch):
    nt = N_ROWS // TM_MERGE
    nn = D_MODEL // TN
    rows = lambda j, i: (i, 0)
    return pl.pallas_call(
        functools.partial(_merge_kernel, layer=l),
        grid=(nn, nt),
        in_specs=[
            pl.BlockSpec((TM_MERGE, D_MODEL), rows),
            pl.BlockSpec((None, N_BRANCHES, TN), lambda j, i: (l, 0, j)),
            pl.BlockSpec((TM_MERGE, BRANCH_W), rows),
            pl.BlockSpec((TM_MERGE, BRANCH_W), rows),
            pl.BlockSpec((TM_MERGE, BRANCH_W), rows),
            pl.BlockSpec(memory_space=pl.ANY),
            pl.BlockSpec(memory_space=pl.ANY),
        ],
        out_specs=pl.BlockSpec((TM_MERGE, TN), lambda j, i: (i, j)),
        out_shape=jax.ShapeDtypeStruct((N_ROWS, D_MODEL), BF16),
        scratch_shapes=[
            pltpu.VMEM((N_BRANCHES, D_MODEL, TN), F32),
            pltpu.VMEM((N_BRANCHES, BRANCH_W, TN), F32),
            pltpu.VMEM((N_BRANCHES, D_MODEL, TN), BF16),
            pltpu.VMEM((N_BRANCHES, BRANCH_W, TN), BF16),
            pltpu.SemaphoreType.DMA((2 * N_BRANCHES,)),
        ],
        compiler_params=_params(("arbitrary", "arbitrary")),
        name="merge",
    )(xn_all, b_gate, o_a, o_b, o_c, w_in, w_branch)


OUT_SPLIT = 2


def _out_proj_kernel(m_ref, w_ref, x_ref, g_ref, x1_ref, xn_ref):
    sub = m_ref.shape[0] // OUT_SPLIT
    for s in range(OUT_SPLIT):
        rows = slice(s * sub, (s + 1) * sub)
        x1 = x_ref[rows, :] + _dot(m_ref[rows, :], w_ref[...])
        x1_ref[rows, :] = x1
        xn_ref[rows, :] = _rms_rows(x1, g_ref[...]).astype(BF16)


def _out_proj(m, w_out, l, x, g_ffn, *, m_row_block0, tm, out_rows):
    rows = lambda i: (i, 0)
    return pl.pallas_call(
        _out_proj_kernel,
        grid=(x.shape[0] // tm,),
        in_specs=[
            pl.BlockSpec((tm, D_MODEL), lambda i: (m_row_block0 + i, 0)),
            _layer_block((D_MODEL, D_MODEL), l, 0, 0),
            pl.BlockSpec((tm, D_MODEL), rows),
            _layer_block((1, D_MODEL), l, 0, 0),
        ],
        out_specs=[pl.BlockSpec((tm, D_MODEL), rows), pl.BlockSpec((tm, D_MODEL), rows)],
        out_shape=[
            jax.ShapeDtypeStruct((out_rows, D_MODEL), F32),
            jax.ShapeDtypeStruct((out_rows, D_MODEL), BF16),
        ],
        compiler_params=_params(("arbitrary",)),
        name="out_proj",
    )(m, w_out, x, g_ffn)


TM_UP = 1664
UP_SPLIT = 4


def _ffn_up_kernel(xn_ref, wgate_ref, wup_ref, h_ref, wgate_scr, wup_scr):
    @pl.when(pl.program_id(1) == 0)
    def _():
        wgate_scr[...] = wgate_ref[...].astype(BF16)
        wup_scr[...] = wup_ref[...].astype(BF16)

    sub = xn_ref.shape[0] // UP_SPLIT
    for s in range(UP_SPLIT):
        rows = slice(s * sub, (s + 1) * sub)
        xn = xn_ref[rows, :]
        gate = _dot(xn, wgate_scr[...])
        h_ref[rows, :] = (gate * jax.nn.sigmoid(gate) * _dot(xn, wup_scr[...])).astype(BF16)


def _ffn_up(xn, w_gate_up, l):
    nt = N_ROWS // TM_UP
    nn = D_FF // TN
    return pl.pallas_call(
        _ffn_up_kernel,
        grid=(nn, nt),
        in_specs=[
            pl.BlockSpec((TM_UP, D_MODEL), lambda j, i: (i, 0)),
            pl.BlockSpec((None, D_MODEL, TN), lambda j, i: (l, 0, j)),
            pl.BlockSpec((None, D_MODEL, TN), lambda j, i: (l, 0, nn + j)),
        ],
        out_specs=pl.BlockSpec((TM_UP, TN), lambda j, i: (i, j)),
        out_shape=jax.ShapeDtypeStruct((N_ROWS, D_FF), BF16),
        scratch_shapes=[pltpu.VMEM((D_MODEL, TN), BF16), pltpu.VMEM((D_MODEL, TN), BF16)],
        compiler_params=_params(("arbitrary", "arbitrary")),
        name="ffn_up",
    )(xn, w_gate_up, w_gate_up)


def _ffn_down_kernel(h_ref, w_ref, x_ref, o_ref, w_scr):
    @pl.when(pl.program_id(1) == 0)
    def _():
        w_scr[...] = w_ref[...].astype(BF16)

    o_ref[...] = x_ref[...] + _dot(h_ref[...], w_scr[...])


def _ffn_down(h, w_down, l, x1, *, row_block0, n_rows, tm):
    nt = n_rows // tm
    nn = D_MODEL // TN
    return pl.pallas_call(
        _ffn_down_kernel,
        grid=(nn, nt),
        in_specs=[
            pl.BlockSpec((tm, D_FF), lambda j, i: (row_block0 + i, 0)),
            pl.BlockSpec((None, D_FF, TN), lambda j, i: (l, 0, j)),
            pl.BlockSpec((tm, TN), lambda j, i: (row_block0 + i, j)),
        ],
        out_specs=pl.BlockSpec((tm, TN), lambda j, i: (i, j)),
        out_shape=jax.ShapeDtypeStruct((n_rows, D_MODEL), F32),
        scratch_shapes=[pltpu.VMEM((D_FF, TN), BF16)],
        compiler_params=_params(("arbitrary", "arbitrary")),
        name="ffn_down",
    )(h, w_down, x1)


def _block_diag_ones(width):
    head = np.arange(width) // HEAD_DIM
    return jnp.asarray(head[:, None] == head[None, :], dtype=BF16)


def _fill_sample_rows(o_prompt, o_sample):
    return lax.dynamic_update_slice(o_prompt, o_sample, (N_PROMPT, 0))


def _layer(x, l, cache_k, cache_v, state_conv, p, const):
    first, last = l == 0, l == DEPTH - 1
    x_p, x_s = x if first else (x, x)
    w_in = p["w_in"]
    qn = jnp.tile(p["q_norm"][l], N_HEADS).reshape(1, BRANCH_W)
    kn = jnp.tile(p["k_norm"][l], N_KV_HEADS).reshape(1, KV_W)
    sinks = p["sinks"][l]

    xn_p, q_p, k_p, v_p, klast_p, vlast_p = _qkv(
        x_p, p["norm_mix"], w_in, l, qn, kn, const["ph"],
        row_block0=0, n_rows=N_PROMPT, tm=TM_SEQ, xn_rows=N_ROWS, tiles_per_seq=SEQ // TM_SEQ)
    xn_s, q_s, k_s, v_s = _qkv(
        x_s, p["norm_mix"], w_in, l, qn, kn, const["ph"],
        row_block0=0 if first else SAMPLE_BLOCK, n_rows=DEC_BATCH, tm=DEC_BATCH, xn_rows=DEC_BATCH)
    xn = _fill_sample_rows(xn_p, xn_s)
    sink_rows = jnp.repeat(sinks, WINDOW).reshape(N_KV_HEADS, 1, GQA_GROUP * WINDOW)
    oa_p = _attn_prompt(sink_rows, q_p, k_p, v_p, const["bias_p"])
    oa_s = _attn_sample(
        q_s.reshape(DEC_BATCH, N_HEADS, HEAD_DIM),
        k_s.reshape(DEC_BATCH, 1, KV_W), v_s.reshape(DEC_BATCH, 1, KV_W),
        cache_k, cache_v, l,
        const["bias_s"], sinks.reshape(N_HEADS, 1), const["mask_s"], const["rep_s"],
    ).reshape(DEC_BATCH, BRANCH_W)

    ob_p, nc_p = _conv_prompt(xn, w_in, l, p["conv_w"])
    ob_s, zc_s = _conv_sample(xn, w_in, l, p["conv_w"], state_conv[l, :, 0], state_conv[l, :, 1])

    w_s = p["w_spatial"]
    b_s = p["b_spatial"]
    bs_wide = jnp.broadcast_to(b_s[:, :, :, None], (DEPTH, N_SPATIAL_GROUPS, CHUNK, SPATIAL_GROUP_W))
    oc_p = _gmlp_prompt(xn, w_in, l, p["v_norm"], w_s, bs_wide)
    ws0 = jnp.repeat(w_s[l, :, 0, 0], SPATIAL_GROUP_W).reshape(1, D_GMLP)
    bs0 = jnp.repeat(b_s[l, :, 0], SPATIAL_GROUP_W).reshape(1, D_GMLP)
    oc_s, vg_s = _gmlp_sample(xn, w_in, l, p["v_norm"], ws0, bs0)

    o_a = _fill_sample_rows(oa_p, oa_s)
    o_b = _fill_sample_rows(ob_p, ob_s)
    o_c = _fill_sample_rows(oc_p, oc_s)
    m = _merge(xn, p["w_in_f32"], l, p["b_gate"], o_a, o_b, o_c, p["w_branch"])
    if first:
        x1_p, xn2_p = _out_proj(m, p["w_out"], l, x_p, p["norm_ffn"], m_row_block0=0, tm=TM_SEQ, out_rows=N_ROWS)
        x1_s, xn2_s = _out_proj(m, p["w_out"], l, x_s, p["norm_ffn"], m_row_block0=SAMPLE_BLOCK,
                                tm=DEC_BATCH, out_rows=DEC_BATCH)
        x1, xn2 = _fill_sample_rows(x1_p, x1_s), _fill_sample_rows(xn2_p, xn2_s)
    else:
        x1, xn2 = _out_proj(m, p["w_out"], l, x, p["norm_ffn"], m_row_block0=0, tm=TM_ROW, out_rows=N_ROWS)
    h = _ffn_up(xn2, p["w_gate_up"], l)
    if last:
        x2 = (_ffn_down(h, p["w_down"], l, x1, row_block0=0, n_rows=N_PROMPT, tm=TM_SEQ),
              _ffn_down(h, p["w_down"], l, x1, row_block0=SAMPLE_BLOCK, n_rows=DEC_BATCH, tm=DEC_BATCH))
    else:
        x2 = _ffn_down(h, p["w_down"], l, x1, row_block0=0, n_rows=N_ROWS, tm=TM_ROW)

    new_k_p = klast_p.reshape(BATCH, WINDOW, N_KV_HEADS, HEAD_DIM)
    new_v_p = vlast_p.reshape(BATCH, WINDOW, N_KV_HEADS, HEAD_DIM)
    new_k_s = k_s.reshape(DEC_BATCH, 1, N_KV_HEADS, HEAD_DIM)
    new_v_s = v_s.reshape(DEC_BATCH, 1, N_KV_HEADS, HEAD_DIM)
    new_conv_s = jnp.stack([state_conv[l, :, 1], zc_s], axis=1)
    return x2, (new_k_p, new_v_p, nc_p, new_k_s, new_v_s, new_conv_s, vg_s.reshape(DEC_BATCH, 1, D_GMLP))


def kernel(x_prompt, x_sample, cache_k, cache_v, state_conv, norm_mix, w_in, b_gate, q_norm, k_norm,
           sinks, conv_w, v_norm, w_spatial, b_spatial, w_branch, w_out, norm_ffn, w_gate_up, w_down):
    assert min(WINDOW, SEQ) == WINDOW and SEQ % TM_SEQ == 0 and TM_SEQ >= WINDOW
    n_buf = cache_k.shape[2]
    assert n_buf == WINDOW
    cache_k = jnp.transpose(cache_k, (0, 1, 3, 4, 2)).reshape(DEPTH, DEC_BATCH, KV_W, n_buf)
    cache_v = jnp.transpose(cache_v, (0, 1, 3, 4, 2)).reshape(DEPTH, DEC_BATCH, KV_W, n_buf)
    p = {
        "norm_mix": norm_mix.reshape(DEPTH, 1, D_MODEL),
        "norm_ffn": norm_ffn.reshape(DEPTH, 1, D_MODEL),
        "v_norm": v_norm.reshape(DEPTH, 1, D_GMLP),
        "w_in": w_in[:, :, :N_WIDE * WIDE].astype(BF16),
        "w_in_f32": w_in,
        "w_branch": w_branch,
        "w_out": w_out.astype(BF16),
        "w_gate_up": w_gate_up,
        "w_down": w_down,
        "b_gate": b_gate, "q_norm": q_norm, "k_norm": k_norm, "sinks": sinks, "conv_w": conv_w,
        "w_spatial": w_spatial, "b_spatial": b_spatial,
    }
    kv_of_col = np.arange(KV_W) // HEAD_DIM
    kv_of_head = np.arange(N_HEADS) // GQA_GROUP
    const = {
        "ph": _block_diag_ones(KV_W),
        "bias_p": _prompt_bias(),
        "bias_s": _sample_bias(n_buf),
        "mask_s": jnp.asarray(kv_of_head[:, None] == kv_of_col[None, :], dtype=F32),
        "rep_s": jnp.asarray(np.tile(np.eye(HEAD_DIM), (1, N_KV_HEADS)), dtype=BF16),
    }
    x = (x_prompt.reshape(N_PROMPT, D_MODEL), x_sample.reshape(DEC_BATCH, D_MODEL))
    per_layer = []
    for l in range(DEPTH):
        x, outs = _layer(x, l, cache_k, cache_v, state_conv, p, const)
        per_layer.append(outs)
    stacked = [jnp.stack([per_layer[l][i] for l in range(DEPTH)]) for i in range(7)]
    y_prompt, y_sample = x
    return (y_prompt.reshape(BATCH, SEQ, D_MODEL), y_sample.reshape(DEC_BATCH, 1, D_MODEL), *stacked)
```

```python
import functools

import jax
import jax.numpy as jnp
import numpy as np
from jax import lax
from jax.experimental import pallas as pl
from jax.experimental.pallas import tpu as pltpu

D_MODEL = 2048
BATCH = 4
SEQ = 2048
DEPTH = 2
DEC_BATCH = 128
BRANCH_W = 1024
HEAD_DIM = 64
N_HEADS = 16
N_KV_HEADS = 4
GQA_GROUP = 4
KV_W = 256
WINDOW = 128
D_CONV = 1024
CONV_WIDTH = 3
D_GMLP = 1024
CHUNK = 128
N_SPATIAL_GROUPS = 8
SPATIAL_GROUP_W = 128
N_BRANCHES = 3
D_FF = 5632
EPS = 1e-6
NEG_INF = -1e30

N_PROMPT = BATCH * SEQ
N_ROWS = N_PROMPT + DEC_BATCH
SAMPLE_BLOCK = N_PROMPT // DEC_BATCH

WIDE = 1536
N_WIDE = 5
OFF_G = 6656

TM_SEQ = 512
TM_ROW = 640
TN = 512
VMEM_LIMIT = 56 * 1024 * 1024

F32 = jnp.float32
BF16 = jnp.bfloat16


def _params(sem):
    return pltpu.CompilerParams(dimension_semantics=sem, vmem_limit_bytes=VMEM_LIMIT)


def _rms_rows(x, g):
    ms = jnp.mean(x * x, axis=-1, keepdims=True)
    return x * lax.rsqrt(ms + EPS) * g


def _dot(a, b):
    return jnp.dot(a, b, preferred_element_type=F32)


def _dot_nt(a, b):
    return lax.dot_general(a, b, (((1,), (1,)), ((), ())), preferred_element_type=F32)


def _gelu(x):
    return 0.5 * x * (1.0 + jnp.tanh(np.sqrt(2.0 / np.pi).astype(np.float32) * (x + 0.044715 * (x * x * x))))


def _layer_block(shape, l, *idx):
    return pl.BlockSpec((None, *shape), lambda *_: (l, *idx))


QKV_SPLIT = 2


def _qkv_kernel(x_ref, g_ref, w_ref, qn_ref, kn_ref, ph_ref, xn_ref, q_ref, k_ref, v_ref, *last_refs):
    tm = x_ref.shape[0]
    sub = tm // QKV_SPLIT
    for s in range(QKV_SPLIT):
        rows = slice(s * sub, (s + 1) * sub)
        xn = _rms_rows(x_ref[rows, :], g_ref[...]).astype(BF16)
        xn_ref[rows, :] = xn
        z = _dot(xn, w_ref[...])
        q = z[:, :BRANCH_W]
        k = z[:, BRANCH_W:BRANCH_W + KV_W]
        v = z[:, BRANCH_W + KV_W:]
        q_sq = (q * q).astype(BF16)
        q_ms = jnp.concatenate(
            [_dot(q_sq[:, c:c + KV_W], ph_ref[...]) for c in range(0, BRANCH_W, KV_W)], axis=1) * (1.0 / HEAD_DIM)
        k_ms = _dot((k * k).astype(BF16), ph_ref[...]) * (1.0 / HEAD_DIM)
        q_ref[rows, :] = (q * lax.rsqrt(q_ms + EPS) * qn_ref[...] * (HEAD_DIM ** -0.5)).astype(BF16)
        kn = k * lax.rsqrt(k_ms + EPS) * kn_ref[...]
        k_ref[rows, :] = kn
        v_ref[rows, :] = v
        if last_refs and s == QKV_SPLIT - 1:
            klast_ref, vlast_ref = last_refs
            klast_ref[...] = kn[sub - WINDOW:]
            vlast_ref[...] = v[sub - WINDOW:]


def _qkv(x, g, w_in, l, qn, kn, ph, *, row_block0, n_rows, tm, tiles_per_seq=None):
    nt = n_rows // tm
    const = lambda i: (0, 0)
    rows = lambda i: (i, 0)
    out_specs = [
        pl.BlockSpec((tm, D_MODEL), rows),
        pl.BlockSpec((tm, BRANCH_W), rows),
        pl.BlockSpec((tm, KV_W), rows),
        pl.BlockSpec((tm, KV_W), rows),
    ]
    out_shape = [
        jax.ShapeDtypeStruct((n_rows, D_MODEL), BF16),
        jax.ShapeDtypeStruct((n_rows, BRANCH_W), BF16),
        jax.ShapeDtypeStruct((n_rows, KV_W), F32),
        jax.ShapeDtypeStruct((n_rows, KV_W), F32),
    ]
    if tiles_per_seq is not None:
        n_seq = nt // tiles_per_seq
        out_specs += [pl.BlockSpec((WINDOW, KV_W), lambda i: (i // tiles_per_seq, 0))] * 2
        out_shape += [jax.ShapeDtypeStruct((n_seq * WINDOW, KV_W), F32)] * 2
    return pl.pallas_call(
        _qkv_kernel,
        grid=(nt,),
        in_specs=[
            pl.BlockSpec((tm, D_MODEL), lambda i: (row_block0 + i, 0)),
            _layer_block((1, D_MODEL), l, 0, 0),
            _layer_block((D_MODEL, WIDE), l, 0, 0),
            pl.BlockSpec((1, BRANCH_W), const),
            pl.BlockSpec((1, KV_W), const),
            pl.BlockSpec((KV_W, KV_W), const),
        ],
        out_specs=out_specs,
        out_shape=out_shape,
        compiler_params=_params(("arbitrary",)),
        name="qkv",
    )(x, g, w_in, qn, kn, ph)


def _attn_prompt_kernel(q_ref, kp_ref, ko_ref, vp_ref, vo_ref, bias0_ref, bias1_ref, sink_ref, o_ref, s_scr):
    k_all = jnp.concatenate([kp_ref[...], ko_ref[...]], axis=0).astype(BF16)
    v_all_t = jnp.transpose(jnp.concatenate([vp_ref[...], vo_ref[...]], axis=0)).astype(BF16)
    for qb in range(ATTN_BLOCKS):
        kk = k_all[qb * WINDOW:(qb + 2) * WINDOW, :]
        for g in range(N_KV_HEADS):
            kg = kk[:, g * HEAD_DIM:(g + 1) * HEAD_DIM]
            heads = range(g * GQA_GROUP, (g + 1) * GQA_GROUP)
            qg = jnp.concatenate(
                [q_ref[qb * WINDOW:(qb + 1) * WINDOW, h * HEAD_DIM:(h + 1) * HEAD_DIM] for h in heads], axis=0)
            s_scr[qb, g] = _dot_nt(kg, qg)
    ones = jnp.ones((ONES_ROWS, 2 * WINDOW), BF16)
    for qb in range(ATTN_BLOCKS):
        bias_ref = bias0_ref if qb == 0 else bias1_ref
        vv_t = v_all_t[:, qb * WINDOW:(qb + 2) * WINDOW]
        for g in range(N_KV_HEADS):
            lhs = jnp.concatenate([vv_t[g * HEAD_DIM:(g + 1) * HEAD_DIM, :], ones], axis=0)
            o_t = []
            for i in range(GQA_GROUP):
                cols = slice(i * WINDOW, (i + 1) * WINDOW)
                s = s_scr[qb, g, :, cols] + bias_ref[g, :, cols]
                sink = sink_ref[g, :, cols]
                m = jnp.maximum(jnp.max(s, axis=0, keepdims=True), sink)
                p = jnp.exp(s - m).astype(BF16)
                oe = _dot(lhs, p)
                den = oe[HEAD_DIM:HEAD_DIM + 1, :] + jnp.exp(sink - m)
                o_t.append(oe[:HEAD_DIM, :] / den)
            o_ref[qb * WINDOW:(qb + 1) * WINDOW, g * KV_W:(g + 1) * KV_W] = (
                jnp.transpose(jnp.concatenate(o_t, axis=0)).astype(BF16))


ONES_ROWS = 16
ATTN_BLOCKS = 4


def _attn_prompt(sink_rows, q, k, v, bias):
    nb = SEQ // WINDOW
    ns = nb // ATTN_BLOCKS
    rows = ATTN_BLOCKS * WINDOW
    own = lambda b, j: (b * ns + j, 0)
    prev = lambda b, j: (b * nb + jnp.maximum(ATTN_BLOCKS * j - 1, 0), 0)
    group_q = GQA_GROUP * WINDOW
    bias_block = (None, N_KV_HEADS, 2 * WINDOW, group_q)
    return pl.pallas_call(
        _attn_prompt_kernel,
        grid=(BATCH, ns),
        in_specs=[
            pl.BlockSpec((rows, BRANCH_W), own),
            pl.BlockSpec((WINDOW, KV_W), prev),
            pl.BlockSpec((rows, KV_W), own),
            pl.BlockSpec((WINDOW, KV_W), prev),
            pl.BlockSpec((rows, KV_W), own),
            pl.BlockSpec(bias_block, lambda b, j: (jnp.minimum(j, 1), 0, 0, 0)),
            pl.BlockSpec(bias_block, lambda b, j: (1, 0, 0, 0)),
            pl.BlockSpec((N_KV_HEADS, 1, group_q), lambda b, j: (0, 0, 0)),
        ],
        out_specs=pl.BlockSpec((rows, BRANCH_W), own),
        out_shape=jax.ShapeDtypeStruct((N_PROMPT, BRANCH_W), BF16),
        scratch_shapes=[pltpu.VMEM((ATTN_BLOCKS, N_KV_HEADS, 2 * WINDOW, group_q), F32)],
        compiler_params=_params(("arbitrary", "arbitrary")),
        name="attn_prompt",
    )(q, k, k, v, v, bias, bias, sink_rows)


def _prompt_bias():
    slopes = jnp.exp2(-8.0 * jnp.arange(1, N_HEADS + 1, dtype=F32) / N_HEADS)
    qi = jnp.arange(WINDOW, dtype=jnp.int32)[:, None]
    ki = jnp.arange(2 * WINDOW, dtype=jnp.int32)[None, :] - WINDOW
    dist = qi - ki
    valid = (dist >= 0) & (dist < WINDOW)
    bias = -slopes[:, None, None] * dist.astype(F32)[None]
    with_prev = jnp.where(valid[None], bias, NEG_INF)
    first = jnp.where((valid & (ki >= 0))[None], bias, NEG_INF)
    per_head = jnp.stack([first, with_prev])
    grouped = per_head.reshape(2, N_KV_HEADS, GQA_GROUP, WINDOW, 2 * WINDOW)
    return jnp.transpose(grouped, (0, 1, 4, 2, 3)).reshape(2, N_KV_HEADS, 2 * WINDOW, GQA_GROUP * WINDOW)


SAMPLE_BT = 32


def _attn_sample_kernel(q_ref, kn_ref, vn_ref, ck_ref, cv_ref, bias_ref, sink_ref, mask_ref,
                        rep_ref, o_ref):
    bt = q_ref.shape[0]
    mask = mask_ref[...][None]
    sink = sink_ref[...][None]
    qe = _dot(q_ref[...].reshape(bt * N_HEADS, HEAD_DIM), rep_ref[...])
    qe = qe.reshape(bt, N_HEADS, KV_W) * mask
    qe_bf = qe.astype(BF16)
    s = jnp.stack([_dot(qe_bf[b], ck_ref[b].astype(BF16)) for b in range(bt)])
    s = s + bias_ref[...][None]
    s_new = jnp.sum(qe * kn_ref[...], axis=-1, keepdims=True)
    m = jnp.maximum(jnp.maximum(jnp.max(s, axis=-1, keepdims=True), s_new), sink)
    p = jnp.exp(s - m)
    p_new = jnp.exp(s_new - m)
    den = jnp.sum(p, axis=-1, keepdims=True) + p_new + jnp.exp(sink - m)
    p_bf = p.astype(BF16)
    of = jnp.stack([_dot_nt(p_bf[b], cv_ref[b].astype(BF16)) for b in range(bt)])
    of = (of + p_new * vn_ref[...]) * mask / den
    o = (of[..., 0:64] + of[..., 64:128]) + (of[..., 128:192] + of[..., 192:256])
    o_ref[...] = o.astype(BF16)


def _attn_sample(q3, k_new, v_new, ck, cv, l, bias, sinks_col, mask, rep):
    nsteps = DEC_BATCH // SAMPLE_BT
    b3 = lambda i: (i, 0, 0)
    c2 = lambda i: (0, 0)
    cache = pl.BlockSpec((None, SAMPLE_BT, KV_W, WINDOW), lambda i: (l, i, 0, 0))
    return pl.pallas_call(
        _attn_sample_kernel,
        grid=(nsteps,),
        in_specs=[
            pl.BlockSpec((SAMPLE_BT, N_HEADS, HEAD_DIM), b3),
            pl.BlockSpec((SAMPLE_BT, 1, KV_W), b3),
            pl.BlockSpec((SAMPLE_BT, 1, KV_W), b3),
            cache,
            cache,
            pl.BlockSpec((N_HEADS, WINDOW), c2),
            pl.BlockSpec((N_HEADS, 1), c2),
            pl.BlockSpec((N_HEADS, KV_W), c2),
            pl.BlockSpec((HEAD_DIM, KV_W), c2),
        ],
        out_specs=pl.BlockSpec((SAMPLE_BT, N_HEADS, HEAD_DIM), b3),
        out_shape=jax.ShapeDtypeStruct((DEC_BATCH, N_HEADS, HEAD_DIM), BF16),
        compiler_params=_params(("arbitrary",)),
        name="attn_sample",
    )(q3, k_new, v_new, ck, cv, bias, sinks_col, mask, rep)


def _sample_bias(n_buf):
    slopes = jnp.exp2(-8.0 * jnp.arange(1, N_HEADS + 1, dtype=F32) / N_HEADS)
    dist = n_buf - jnp.arange(n_buf, dtype=jnp.int32)
    bias = -slopes[:, None] * dist.astype(F32)[None, :]
    return jnp.where((dist < WINDOW)[None, :], bias, NEG_INF)


def _bch_cols(wa_ref, wb_ref, seg, c):
    r = seg * D_CONV + c * TN
    return wa_ref[:, r:r + TN] if r < WIDE else wb_ref[:, r - WIDE:r - WIDE + TN]


def _conv_prompt_kernel(xn_ref, wa_ref, wb_ref, cw_ref, ob_ref, nc_ref, zbuf, carry):
    t = pl.program_id(1)
    tm = xn_ref.shape[0]
    xn = xn_ref[...]

    @pl.when(t == 0)
    def _():
        zbuf[0:8, :] = jnp.zeros((8, D_CONV), F32)

    @pl.when(t > 0)
    def _():
        zbuf[0:8, :] = carry[...]

    for c in range(D_CONV // TN):
        cols = slice(c * TN, (c + 1) * TN)
        zc = _dot(xn, _bch_cols(wa_ref, wb_ref, 1, c)) * _dot(xn, _bch_cols(wa_ref, wb_ref, 2, c))
        zbuf[8:8 + tm, cols] = zc
        cw = cw_ref[:, cols]
        y = cw[0:1] * zbuf[6:6 + tm, cols] + cw[1:2] * zbuf[7:7 + tm, cols] + cw[2:3] * zc
        ob_ref[:, cols] = (_dot(xn, _bch_cols(wa_ref, wb_ref, 0, c)) * y).astype(BF16)
    carry[...] = zbuf[tm:tm + 8, :]
    nc_ref[...] = zbuf[tm + 6:tm + 8, :]


def _conv_prompt(xn_all, w_in, l, conv_w):
    nt = SEQ // TM_SEQ
    return pl.pallas_call(
        _conv_prompt_kernel,
        grid=(BATCH, nt),
        in_specs=[
            pl.BlockSpec((TM_SEQ, D_MODEL), lambda b, t: (b * nt + t, 0)),
            _layer_block((D_MODEL, WIDE), l, 0, 1),
            _layer_block((D_MODEL, WIDE), l, 0, 2),
            _layer_block((CONV_WIDTH, D_CONV), l, 0, 0),
        ],
        out_specs=[
            pl.BlockSpec((TM_SEQ, D_CONV), lambda b, t: (b * nt + t, 0)),
            pl.BlockSpec((None, CONV_WIDTH - 1, D_CONV), lambda b, t: (b, 0, 0)),
        ],
        out_shape=[
            jax.ShapeDtypeStruct((N_PROMPT, D_CONV), BF16),
            jax.ShapeDtypeStruct((BATCH, CONV_WIDTH - 1, D_CONV), F32),
        ],
        scratch_shapes=[
            pltpu.VMEM((TM_SEQ + 8, D_CONV), F32),
            pltpu.VMEM((8, D_CONV), F32),
        ],
        compiler_params=_params(("arbitrary", "arbitrary")),
        name="conv_prompt",
    )(xn_all, w_in, w_in, conv_w)


def _conv_sample_kernel(xn_ref, wa_ref, wb_ref, cw_ref, cb0_ref, cb1_ref, ob_ref, zc_ref):
    xn = xn_ref[...]
    for c in range(D_CONV // TN):
        cols = slice(c * TN, (c + 1) * TN)
        zc = _dot(xn, _bch_cols(wa_ref, wb_ref, 1, c)) * _dot(xn, _bch_cols(wa_ref, wb_ref, 2, c))
        cw = cw_ref[:, cols]
        y = cw[0:1] * cb0_ref[:, cols] + cw[1:2] * cb1_ref[:, cols] + cw[2:3] * zc
        ob_ref[:, cols] = (_dot(xn, _bch_cols(wa_ref, wb_ref, 0, c)) * y).astype(BF16)
        zc_ref[:, cols] = zc


def _conv_sample(xn_all, w_in, l, conv_w, cb0, cb1):
    c2 = lambda i: (0, 0)
    return pl.pallas_call(
        _conv_sample_kernel,
        grid=(1,),
        in_specs=[
            pl.BlockSpec((DEC_BATCH, D_MODEL), lambda i: (0, 0)),
            _layer_block((D_MODEL, WIDE), l, 0, 1),
            _layer_block((D_MODEL, WIDE), l, 0, 2),
            _layer_block((CONV_WIDTH, D_CONV), l, 0, 0),
            pl.BlockSpec((DEC_BATCH, D_CONV), c2),
            pl.BlockSpec((DEC_BATCH, D_CONV), c2),
        ],
        out_specs=[pl.BlockSpec((DEC_BATCH, D_CONV), c2), pl.BlockSpec((DEC_BATCH, D_CONV), c2)],
        out_shape=[
            jax.ShapeDtypeStruct((DEC_BATCH, D_CONV), BF16),
            jax.ShapeDtypeStruct((DEC_BATCH, D_CONV), F32),
        ],
        compiler_params=_params(("arbitrary",)),
        name="conv_sample",
    )(xn_all, w_in, w_in, conv_w, cb0, cb1)


HALF_V = WIDE - D_GMLP


def _gmlp_uv(xn, wa_ref, wb_ref, vn_ref):
    v_lo = _gelu(_dot(xn, wa_ref[:, D_GMLP:]))
    v_hi = _gelu(_dot(xn, wb_ref[:, :D_GMLP - HALF_V]))
    ms = (jnp.sum(v_lo * v_lo, axis=-1, keepdims=True)
          + jnp.sum(v_hi * v_hi, axis=-1, keepdims=True)) * (1.0 / D_GMLP)
    r = lax.rsqrt(ms + EPS)
    v_lo, v_hi = v_lo * r * vn_ref[:, :HALF_V], v_hi * r * vn_ref[:, HALF_V:]
    u = _gelu(_dot(xn, wa_ref[:, :D_GMLP]))
    return u, v_lo, v_hi


def _gmlp_prompt_kernel(xn_ref, wa_ref, wb_ref, vn_ref, ws_ref, bs_ref, oc_ref, u_scr, v_scr):
    tm = xn_ref.shape[0]
    u, v_lo, v_hi = _gmlp_uv(xn_ref[...], wa_ref, wb_ref, vn_ref)
    v_scr[:, :HALF_V] = v_lo.astype(BF16)
    v_scr[:, HALF_V:] = v_hi.astype(BF16)
    u_scr[...] = u
    row = lax.broadcasted_iota(jnp.int32, (CHUNK, CHUNK), 0)
    col = lax.broadcasted_iota(jnp.int32, (CHUNK, CHUNK), 1)
    for grp in range(N_SPATIAL_GROUPS):
        w = jnp.where(col <= row, ws_ref[grp], 0.0).astype(BF16)
        cols = slice(grp * SPATIAL_GROUP_W, (grp + 1) * SPATIAL_GROUP_W)
        for ch in range(tm // CHUNK):
            rows = slice(ch * CHUNK, (ch + 1) * CHUNK)
            mix = _dot(w, v_scr[rows, cols]) + bs_ref[grp]
            oc_ref[rows, cols] = (u_scr[rows, cols] * mix).astype(BF16)


def _gmlp_prompt(xn_all, w_in, l, v_norm, w_s, b_s_wide):
    nt = N_PROMPT // TM_SEQ
    return pl.pallas_call(
        _gmlp_prompt_kernel,
        grid=(nt,),
        in_specs=[
            pl.BlockSpec((TM_SEQ, D_MODEL), lambda i: (i, 0)),
            _layer_block((D_MODEL, WIDE), l, 0, 3),
            _layer_block((D_MODEL, WIDE), l, 0, 4),
            _layer_block((1, D_GMLP), l, 0, 0),
            _layer_block((N_SPATIAL_GROUPS, CHUNK, CHUNK), l, 0, 0, 0),
            _layer_block((N_SPATIAL_GROUPS, CHUNK, SPATIAL_GROUP_W), l, 0, 0, 0),
        ],
        out_specs=pl.BlockSpec((TM_SEQ, D_GMLP), lambda i: (i, 0)),
        out_shape=jax.ShapeDtypeStruct((N_PROMPT, D_GMLP), BF16),
        scratch_shapes=[pltpu.VMEM((TM_SEQ, D_GMLP), F32), pltpu.VMEM((TM_SEQ, D_GMLP), BF16)],
        compiler_params=_params(("arbitrary",)),
        name="gmlp_prompt",
    )(xn_all, w_in, w_in, v_norm, w_s, b_s_wide)


def _gmlp_sample_kernel(xn_ref, wa_ref, wb_ref, vn_ref, ws0_ref, bs0_ref, oc_ref, vg_ref):
    u, v_lo, v_hi = _gmlp_uv(xn_ref[...], wa_ref, wb_ref, vn_ref)
    vg_ref[:, :HALF_V] = v_lo
    vg_ref[:, HALF_V:] = v_hi
    oc_ref[...] = (u * (ws0_ref[...] * vg_ref[...] + bs0_ref[...])).astype(BF16)


def _gmlp_sample(xn_all, w_in, l, v_norm, ws0, bs0):
    c2 = lambda i: (0, 0)
    return pl.pallas_call(
        _gmlp_sample_kernel,
        grid=(1,),
        in_specs=[
            pl.BlockSpec((DEC_BATCH, D_MODEL), lambda i: (0, 0)),
            _layer_block((D_MODEL, WIDE), l, 0, 3),
            _layer_block((D_MODEL, WIDE), l, 0, 4),
            _layer_block((1, D_GMLP), l, 0, 0),
            pl.BlockSpec((1, D_GMLP), c2),
            pl.BlockSpec((1, D_GMLP), c2),
        ],
        out_specs=[pl.BlockSpec((DEC_BATCH, D_GMLP), c2), pl.BlockSpec((DEC_BATCH, D_GMLP), c2)],
        out_shape=[
            jax.ShapeDtypeStruct((DEC_BATCH, D_GMLP), BF16),
            jax.ShapeDtypeStruct((DEC_BATCH, D_GMLP), F32),
        ],
        compiler_params=_params(("arbitrary",)),
        name="gmlp_sample",
    )(xn_all, w_in, w_in, v_norm, ws0, bs0)


def _merge_kernel(xn_ref, oa_ref, ob_ref, oc_ref, xns_ref, oas_ref, obs_ref, ocs_ref, bg_ref, wg_hbm, wb_hbm,
                  m_ref, stage_g, stage_b, wg_bf, wb_bf, xn_tail, oa_tail, ob_tail, oc_tail, sem, *, layer):
    j = pl.program_id(0)
    i = pl.program_id(1)

    def weight_copies(jj):
        copies = []
        for br in range(N_BRANCHES):
            gcol = pl.multiple_of(OFF_G + br * D_MODEL + jj * TN, TN)
            bcol = pl.multiple_of(jj * TN, TN)
            copies.append(pltpu.make_async_copy(
                wg_hbm.at[layer, :, pl.ds(gcol, TN)], stage_g.at[br], sem.at[br]))
            copies.append(pltpu.make_async_copy(
                wb_hbm.at[layer, br, :, pl.ds(bcol, TN)], stage_b.at[br], sem.at[N_BRANCHES + br]))
        return copies

    @pl.when(i == 0)
    def _():
        @pl.when(j == 0)
        def _():
            for c in weight_copies(0):
                c.start()

        for c in weight_copies(j):
            c.wait()
        def cast_rows(stage, dst):
            def body(c, carry):
                r = pl.multiple_of(c * CAST_ROWS, CAST_ROWS)
                for br in range(N_BRANCHES):
                    dst[br, pl.ds(r, CAST_ROWS), :] = stage[br, pl.ds(r, CAST_ROWS), :].astype(BF16)
                return carry
            lax.fori_loop(0, stage.shape[1] // CAST_ROWS, body, 0)

        cast_rows(stage_g, wg_bf)
        cast_rows(stage_b, wb_bf)

        @pl.when(j + 1 < pl.num_programs(0))
        def _():
            for c in weight_copies(j + 1):
                c.start()

    bg = bg_ref[...]

    def merged(xn, branch_outs):
        acc = None
        for br, o in enumerate(branch_outs):
            gate = jax.nn.sigmoid(_dot(xn, wg_bf[br]) + bg[br:br + 1])
            term = gate * _dot(o, wb_bf[br])
            acc = term if acc is None else acc + term
        return acc.astype(BF16)

    last = pl.num_programs(1) - 1

    @pl.when(i < last)
    def _():
        m_ref[...] = merged(xn_ref[...], (oa_ref[...], ob_ref[...], oc_ref[...]))

    @pl.when(i == last)
    def _():
        for src, src_s, tail in ((xn_ref, xns_ref, xn_tail), (oa_ref, oas_ref, oa_tail),
                                 (ob_ref, obs_ref, ob_tail), (oc_ref, ocs_ref, oc_tail)):
            tail[:PROMPT_TAIL, :] = src[:PROMPT_TAIL, :]
            tail[PROMPT_TAIL:, :] = src_s[...]
        m_ref[...] = merged(xn_tail[...], (oa_tail[...], ob_tail[...], oc_tail[...]))


CAST_ROWS = 256
PROMPT_TAIL = N_PROMPT % TM_ROW
assert PROMPT_TAIL + DEC_BATCH == TM_ROW


def _row_tile_specs(width, index_of, tm):
    n_prompt_tiles = N_PROMPT // tm
    prompt = pl.BlockSpec((tm, width), lambda *g: (jnp.minimum(index_of(*g), n_prompt_tiles - 1), 0))
    sample = pl.BlockSpec((DEC_BATCH, width), lambda *g: (0, 0))
    return prompt, sample


def _tail_tile_specs(width):
    prompt = pl.BlockSpec((TM_ROW, width), lambda j, i: (i, 0))
    sample = pl.BlockSpec((DEC_BATCH, width), lambda j, i: (0, 0))
    return prompt, sample


def _merge(xn, w_in, l, b_gate, o_a, o_b, o_c, w_branch):
    nt = N_ROWS // TM_ROW
    nn = D_MODEL // TN
    wide_p, wide_s = _tail_tile_specs(D_MODEL)
    br_p, br_s = _tail_tile_specs(BRANCH_W)
    return pl.pallas_call(
        functools.partial(_merge_kernel, layer=l),
        grid=(nn, nt),
        in_specs=[
            wide_p, br_p, br_p, br_p,
            wide_s, br_s, br_s, br_s,
            pl.BlockSpec((None, N_BRANCHES, TN), lambda j, i: (l, 0, j)),
            pl.BlockSpec(memory_space=pl.ANY),
            pl.BlockSpec(memory_space=pl.ANY),
        ],
        out_specs=pl.BlockSpec((TM_ROW, TN), lambda j, i: (i, j)),
        out_shape=jax.ShapeDtypeStruct((N_ROWS, D_MODEL), BF16),
        scratch_shapes=[
            pltpu.VMEM((N_BRANCHES, D_MODEL, TN), F32),
            pltpu.VMEM((N_BRANCHES, BRANCH_W, TN), F32),
            pltpu.VMEM((N_BRANCHES, D_MODEL, TN), BF16),
            pltpu.VMEM((N_BRANCHES, BRANCH_W, TN), BF16),
            pltpu.VMEM((TM_ROW, D_MODEL), BF16),
            pltpu.VMEM((TM_ROW, BRANCH_W), BF16),
            pltpu.VMEM((TM_ROW, BRANCH_W), BF16),
            pltpu.VMEM((TM_ROW, BRANCH_W), BF16),
            pltpu.SemaphoreType.DMA((2 * N_BRANCHES,)),
        ],
        compiler_params=_params(("arbitrary", "arbitrary")),
        name="merge",
    )(xn[0], o_a[0], o_b[0], o_c[0], xn[1], o_a[1], o_b[1], o_c[1], b_gate, w_in, w_branch)


OUT_SPLIT = 2


def _out_proj_kernel(m_ref, w_ref, xp_ref, xs_ref, g_ref, x1_ref, xn_ref):
    i = pl.program_id(0)
    sample_tile = pl.num_programs(0) - 1

    def project(rows, x):
        x1 = x + _dot(m_ref[rows, :], w_ref[...])
        x1_ref[rows, :] = x1
        xn_ref[rows, :] = _rms_rows(x1, g_ref[...]).astype(BF16)

    @pl.when(i < sample_tile)
    def _():
        sub = m_ref.shape[0] // OUT_SPLIT
        for s in range(OUT_SPLIT):
            rows = slice(s * sub, (s + 1) * sub)
            project(rows, xp_ref[rows, :])

    @pl.when(i == sample_tile)
    def _():
        project(slice(0, DEC_BATCH), xs_ref[...])


def _out_proj(m, w_out, l, x_p, x_s, sample_block, g_ffn):
    nt = N_PROMPT // TM_SEQ + 1
    rows = lambda i: (i, 0)
    x_prompt_spec, _ = _row_tile_specs(D_MODEL, lambda i: i, TM_SEQ)
    return pl.pallas_call(
        _out_proj_kernel,
        grid=(nt,),
        in_specs=[
            pl.BlockSpec((TM_SEQ, D_MODEL), rows),
            _layer_block((D_MODEL, D_MODEL), l, 0, 0),
            x_prompt_spec,
            pl.BlockSpec((DEC_BATCH, D_MODEL), lambda i: (sample_block, 0)),
            _layer_block((1, D_MODEL), l, 0, 0),
        ],
        out_specs=[pl.BlockSpec((TM_SEQ, D_MODEL), rows), pl.BlockSpec((TM_SEQ, D_MODEL), rows)],
        out_shape=[
            jax.ShapeDtypeStruct((N_ROWS, D_MODEL), F32),
            jax.ShapeDtypeStruct((N_ROWS, D_MODEL), BF16),
        ],
        compiler_params=_params(("arbitrary",)),
        name="out_proj",
    )(m, w_out, x_p, x_s, g_ffn)


TM_UP = 1664
UP_SPLIT = 4


def _ffn_up_kernel(xn_ref, wgate_ref, wup_ref, h_ref, wgate_scr, wup_scr):
    @pl.when(pl.program_id(1) == 0)
    def _():
        wgate_scr[...] = wgate_ref[...].astype(BF16)
        wup_scr[...] = wup_ref[...].astype(BF16)

    sub = xn_ref.shape[0] // UP_SPLIT
    for s in range(UP_SPLIT):
        rows = slice(s * sub, (s + 1) * sub)
        xn = xn_ref[rows, :]
        gate = _dot(xn, wgate_scr[...])
        h_ref[rows, :] = (gate * jax.nn.sigmoid(gate) * _dot(xn, wup_scr[...])).astype(BF16)


def _ffn_up(xn, w_gate_up, l):
    nt = N_ROWS // TM_UP
    nn = D_FF // TN
    return pl.pallas_call(
        _ffn_up_kernel,
        grid=(nn, nt),
        in_specs=[
            pl.BlockSpec((TM_UP, D_MODEL), lambda j, i: (i, 0)),
            pl.BlockSpec((None, D_MODEL, TN), lambda j, i: (l, 0, j)),
            pl.BlockSpec((None, D_MODEL, TN), lambda j, i: (l, 0, nn + j)),
        ],
        out_specs=pl.BlockSpec((TM_UP, TN), lambda j, i: (i, j)),
        out_shape=jax.ShapeDtypeStruct((N_ROWS, D_FF), BF16),
        scratch_shapes=[pltpu.VMEM((D_MODEL, TN), BF16), pltpu.VMEM((D_MODEL, TN), BF16)],
        compiler_params=_params(("arbitrary", "arbitrary")),
        name="ffn_up",
    )(xn, w_gate_up, w_gate_up)


def _ffn_down_kernel(h_ref, w_ref, x_ref, o_ref, w_scr):
    @pl.when(pl.program_id(1) == 0)
    def _():
        w_scr[...] = w_ref[...].astype(BF16)

    o_ref[...] = x_ref[...] + _dot(h_ref[...], w_scr[...])


def _ffn_down(h, w_down, l, x1):
    nt = N_ROWS // TM_ROW
    nn = D_MODEL // TN
    return pl.pallas_call(
        _ffn_down_kernel,
        grid=(nn, nt),
        in_specs=[
            pl.BlockSpec((TM_ROW, D_FF), lambda j, i: (i, 0)),
            pl.BlockSpec((None, D_FF, TN), lambda j, i: (l, 0, j)),
            pl.BlockSpec((TM_ROW, TN), lambda j, i: (i, j)),
        ],
        out_specs=pl.BlockSpec((TM_ROW, TN), lambda j, i: (i, j)),
        out_shape=jax.ShapeDtypeStruct((N_ROWS, D_MODEL), F32),
        scratch_shapes=[pltpu.VMEM((D_FF, TN), BF16)],
        compiler_params=_params(("arbitrary", "arbitrary")),
        name="ffn_down",
    )(h, w_down, x1)


def _ffn_down_split_kernel(h_ref, w_ref, x_ref, yp_ref, ys_ref, w_scr):
    i = pl.program_id(1)

    @pl.when(i == 0)
    def _():
        w_scr[...] = w_ref[...].astype(BF16)

    y = x_ref[...] + _dot(h_ref[...], w_scr[...])
    yp_ref[...] = y

    @pl.when(i == pl.num_programs(1) - 1)
    def _():
        ys_ref[...] = y[PROMPT_TAIL:, :]


def _ffn_down_split(h, w_down, l, x1):
    nt = N_ROWS // TM_ROW
    nn = D_MODEL // TN
    return pl.pallas_call(
        _ffn_down_split_kernel,
        grid=(nn, nt),
        in_specs=[
            pl.BlockSpec((TM_ROW, D_FF), lambda j, i: (i, 0)),
            pl.BlockSpec((None, D_FF, TN), lambda j, i: (l, 0, j)),
            pl.BlockSpec((TM_ROW, TN), lambda j, i: (i, j)),
        ],
        out_specs=[
            pl.BlockSpec((TM_ROW, TN), lambda j, i: (i, j)),
            pl.BlockSpec((DEC_BATCH, TN), lambda j, i: (0, j)),
        ],
        out_shape=[
            jax.ShapeDtypeStruct((N_PROMPT, D_MODEL), F32),
            jax.ShapeDtypeStruct((DEC_BATCH, D_MODEL), F32),
        ],
        scratch_shapes=[pltpu.VMEM((D_FF, TN), BF16)],
        compiler_params=_params(("arbitrary", "arbitrary")),
        name="ffn_down_split",
    )(h, w_down, x1)


def _block_diag_ones(width):
    head = np.arange(width) // HEAD_DIM
    return jnp.asarray(head[:, None] == head[None, :], dtype=BF16)


def _layer(x, l, cache_k, cache_v, state_conv, p, const):
    first, last = l == 0, l == DEPTH - 1
    x_p, x_s = x if first else (x, x)
    sample_block = 0 if first else SAMPLE_BLOCK
    w_in = p["w_in"]
    qn = jnp.tile(p["q_norm"][l], N_HEADS).reshape(1, BRANCH_W)
    kn = jnp.tile(p["k_norm"][l], N_KV_HEADS).reshape(1, KV_W)
    sinks = p["sinks"][l]

    xn_p, q_p, k_p, v_p, klast_p, vlast_p = _qkv(
        x_p, p["norm_mix"], w_in, l, qn, kn, const["ph"],
        row_block0=0, n_rows=N_PROMPT, tm=TM_SEQ, tiles_per_seq=SEQ // TM_SEQ)
    xn_s, q_s, k_s, v_s = _qkv(
        x_s, p["norm_mix"], w_in, l, qn, kn, const["ph"],
        row_block0=sample_block, n_rows=DEC_BATCH, tm=DEC_BATCH)
    sink_rows = jnp.repeat(sinks, WINDOW).reshape(N_KV_HEADS, 1, GQA_GROUP * WINDOW)
    oa_p = _attn_prompt(sink_rows, q_p, k_p, v_p, const["bias_p"])
    oa_s = _attn_sample(
        q_s.reshape(DEC_BATCH, N_HEADS, HEAD_DIM),
        k_s.reshape(DEC_BATCH, 1, KV_W), v_s.reshape(DEC_BATCH, 1, KV_W),
        cache_k, cache_v, l,
        const["bias_s"], sinks.reshape(N_HEADS, 1), const["mask_s"], const["rep_s"],
    ).reshape(DEC_BATCH, BRANCH_W)

    ob_p, nc_p = _conv_prompt(xn_p, w_in, l, p["conv_w"])
    ob_s, zc_s = _conv_sample(xn_s, w_in, l, p["conv_w"], state_conv[l, :, 0], state_conv[l, :, 1])

    w_s = p["w_spatial"]
    b_s = p["b_spatial"]
    bs_wide = jnp.broadcast_to(b_s[:, :, :, None], (DEPTH, N_SPATIAL_GROUPS, CHUNK, SPATIAL_GROUP_W))
    oc_p = _gmlp_prompt(xn_p, w_in, l, p["v_norm"], w_s, bs_wide)
    ws0 = jnp.repeat(w_s[l, :, 0, 0], SPATIAL_GROUP_W).reshape(1, D_GMLP)
    bs0 = jnp.repeat(b_s[l, :, 0], SPATIAL_GROUP_W).reshape(1, D_GMLP)
    oc_s, vg_s = _gmlp_sample(xn_s, w_in, l, p["v_norm"], ws0, bs0)

    m = _merge((xn_p, xn_s), p["w_in_f32"], l, p["b_gate"], (oa_p, oa_s), (ob_p, ob_s), (oc_p, oc_s),
               p["w_branch"])
    x1, xn2 = _out_proj(m, p["w_out"], l, x_p, x_s, sample_block, p["norm_ffn"])
    h = _ffn_up(xn2, p["w_gate_up"], l)
    x2 = _ffn_down_split(h, p["w_down"], l, x1) if last else _ffn_down(h, p["w_down"], l, x1)

    new_k_p = klast_p.reshape(BATCH, WINDOW, N_KV_HEADS, HEAD_DIM)
    new_v_p = vlast_p.reshape(BATCH, WINDOW, N_KV_HEADS, HEAD_DIM)
    new_k_s = k_s.reshape(DEC_BATCH, 1, N_KV_HEADS, HEAD_DIM)
    new_v_s = v_s.reshape(DEC_BATCH, 1, N_KV_HEADS, HEAD_DIM)
    new_conv_s = jnp.stack([state_conv[l, :, 1], zc_s], axis=1)
    return x2, (new_k_p, new_v_p, nc_p, new_k_s, new_v_s, new_conv_s, vg_s.reshape(DEC_BATCH, 1, D_GMLP))


def kernel(x_prompt, x_sample, cache_k, cache_v, state_conv, norm_mix, w_in, b_gate, q_norm, k_norm,
           sinks, conv_w, v_norm, w_spatial, b_spatial, w_branch, w_out, norm_ffn, w_gate_up, w_down):
    assert min(WINDOW, SEQ) == WINDOW and SEQ % TM_SEQ == 0 and TM_SEQ >= WINDOW
    n_buf = cache_k.shape[2]
    assert n_buf == WINDOW
    cache_k = jnp.transpose(cache_k, (0, 1, 3, 4, 2)).reshape(DEPTH, DEC_BATCH, KV_W, n_buf)
    cache_v = jnp.transpose(cache_v, (0, 1, 3, 4, 2)).reshape(DEPTH, DEC_BATCH, KV_W, n_buf)
    p = {
        "norm_mix": norm_mix.reshape(DEPTH, 1, D_MODEL),
        "norm_ffn": norm_ffn.reshape(DEPTH, 1, D_MODEL),
        "v_norm": v_norm.reshape(DEPTH, 1, D_GMLP),
        "w_in": w_in[:, :, :N_WIDE * WIDE].astype(BF16),
        "w_in_f32": w_in,
        "w_branch": w_branch,
        "w_out": w_out.astype(BF16),
        "w_gate_up": w_gate_up,
        "w_down": w_down,
        "b_gate": b_gate, "q_norm": q_norm, "k_norm": k_norm, "sinks": sinks, "conv_w": conv_w,
        "w_spatial": w_spatial, "b_spatial": b_spatial,
    }
    kv_of_col = np.arange(KV_W) // HEAD_DIM
    kv_of_head = np.arange(N_HEADS) // GQA_GROUP
    const = {
        "ph": _block_diag_ones(KV_W),
        "bias_p": _prompt_bias(),
        "bias_s": _sample_bias(n_buf),
        "mask_s": jnp.asarray(kv_of_head[:, None] == kv_of_col[None, :], dtype=F32),
        "rep_s": jnp.asarray(np.tile(np.eye(HEAD_DIM), (1, N_KV_HEADS)), dtype=BF16),
    }
    x = (x_prompt.reshape(N_PROMPT, D_MODEL), x_sample.reshape(DEC_BATCH, D_MODEL))
    per_layer = []
    for l in range(DEPTH):
        x, outs = _layer(x, l, cache_k, cache_v, state_conv, p, const)
        per_layer.append(outs)
    stacked = [jnp.stack([per_layer[l][i] for l in range(DEPTH)]) for i in range(7)]
    y_prompt, y_sample = x
    return (y_prompt.reshape(BATCH, SEQ, D_MODEL), y_sample.reshape(DEC_BATCH, 1, D_MODEL), *stacked)
```

```python
import functools

import jax
import jax.numpy as jnp
import numpy as np
from jax import lax
from jax.experimental import pallas as pl
from jax.experimental.pallas import tpu as pltpu

D_MODEL = 2048
BATCH = 4
SEQ = 2048
DEPTH = 2
DEC_BATCH = 128
BRANCH_W = 1024
HEAD_DIM = 64
N_HEADS = 16
N_KV_HEADS = 4
GQA_GROUP = 4
KV_W = 256
WINDOW = 128
D_CONV = 1024
CONV_WIDTH = 3
D_GMLP = 1024
CHUNK = 128
N_SPATIAL_GROUPS = 8
SPATIAL_GROUP_W = 128
N_BRANCHES = 3
D_FF = 5632
EPS = 1e-6
NEG_INF = -1e30

N_PROMPT = BATCH * SEQ
N_ROWS = N_PROMPT + DEC_BATCH
SAMPLE_BLOCK = N_PROMPT // DEC_BATCH

WIDE = 1536
OFF_G = 6656

TM_SEQ = 512
TM_ROW = 640
TN = 512
VMEM_LIMIT = 56 * 1024 * 1024

F32 = jnp.float32
BF16 = jnp.bfloat16


def _params(sem):
    return pltpu.CompilerParams(dimension_semantics=sem, vmem_limit_bytes=VMEM_LIMIT)


def _rms_rows(x, g):
    ms = jnp.mean(x * x, axis=-1, keepdims=True)
    return x * lax.rsqrt(ms + EPS) * g


def _dot(a, b):
    return jnp.dot(a, b, preferred_element_type=F32)


def _dot_nt(a, b):
    return lax.dot_general(a, b, (((1,), (1,)), ((), ())), preferred_element_type=F32)


def _gelu(x):
    return 0.5 * x * (1.0 + jnp.tanh(np.sqrt(2.0 / np.pi).astype(np.float32) * (x + 0.044715 * (x * x * x))))


def _layer_block(shape, l, *idx):
    return pl.BlockSpec((None, *shape), lambda *_: (l, *idx))


QKV_SPLIT = 2


def _qkv_kernel(x_ref, g_ref, w_ref, qn_ref, kn_ref, ph_ref, xn_ref, q_ref, k_ref, v_ref, *last_refs):
    tm = x_ref.shape[0]
    sub = tm // QKV_SPLIT
    for s in range(QKV_SPLIT):
        rows = slice(s * sub, (s + 1) * sub)
        xn = _rms_rows(x_ref[rows, :], g_ref[...]).astype(BF16)
        xn_ref[rows, :] = xn
        z = _dot(xn, w_ref[...])
        q = z[:, :BRANCH_W]
        k = z[:, BRANCH_W:BRANCH_W + KV_W]
        v = z[:, BRANCH_W + KV_W:]
        q_sq = (q * q).astype(BF16)
        q_ms = jnp.concatenate(
            [_dot(q_sq[:, c:c + KV_W], ph_ref[...]) for c in range(0, BRANCH_W, KV_W)], axis=1) * (1.0 / HEAD_DIM)
        k_ms = _dot((k * k).astype(BF16), ph_ref[...]) * (1.0 / HEAD_DIM)
        q_ref[rows, :] = (q * lax.rsqrt(q_ms + EPS) * qn_ref[...] * (HEAD_DIM ** -0.5)).astype(BF16)
        kn = k * lax.rsqrt(k_ms + EPS) * kn_ref[...]
        k_ref[rows, :] = kn
        v_ref[rows, :] = v
        if last_refs and s == QKV_SPLIT - 1:
            klast_ref, vlast_ref = last_refs
            klast_ref[...] = kn[sub - WINDOW:]
            vlast_ref[...] = v[sub - WINDOW:]


def _qkv(x, g, w_in, l, qn, kn, ph, *, row_block0, n_rows, tm, tiles_per_seq=None):
    nt = n_rows // tm
    const = lambda i: (0, 0)
    rows = lambda i: (i, 0)
    out_specs = [
        pl.BlockSpec((tm, D_MODEL), rows),
        pl.BlockSpec((tm, BRANCH_W), rows),
        pl.BlockSpec((tm, KV_W), rows),
        pl.BlockSpec((tm, KV_W), rows),
    ]
    out_shape = [
        jax.ShapeDtypeStruct((n_rows, D_MODEL), BF16),
        jax.ShapeDtypeStruct((n_rows, BRANCH_W), BF16),
        jax.ShapeDtypeStruct((n_rows, KV_W), F32),
        jax.ShapeDtypeStruct((n_rows, KV_W), F32),
    ]
    if tiles_per_seq is not None:
        n_seq = nt // tiles_per_seq
        out_specs += [pl.BlockSpec((WINDOW, KV_W), lambda i: (i // tiles_per_seq, 0))] * 2
        out_shape += [jax.ShapeDtypeStruct((n_seq * WINDOW, KV_W), F32)] * 2
    return pl.pallas_call(
        _qkv_kernel,
        grid=(nt,),
        in_specs=[
            pl.BlockSpec((tm, D_MODEL), lambda i: (row_block0 + i, 0)),
            _layer_block((1, D_MODEL), l, 0, 0),
            _layer_block((D_MODEL, WIDE), l, 0, 0),
            pl.BlockSpec((1, BRANCH_W), const),
            pl.BlockSpec((1, KV_W), const),
            pl.BlockSpec((KV_W, KV_W), const),
        ],
        out_specs=out_specs,
        out_shape=out_shape,
        compiler_params=_params(("arbitrary",)),
        name="qkv",
    )(x, g, w_in, qn, kn, ph)


def _attn_prompt_kernel(q_ref, kp_ref, ko_ref, vp_ref, vo_ref, bias0_ref, bias1_ref, sink_ref, o_ref, s_scr):
    k_all = jnp.concatenate([kp_ref[...], ko_ref[...]], axis=0).astype(BF16)
    v_all_t = jnp.transpose(jnp.concatenate([vp_ref[...], vo_ref[...]], axis=0)).astype(BF16)
    for qb in range(ATTN_BLOCKS):
        kk = k_all[qb * WINDOW:(qb + 2) * WINDOW, :]
        for g in range(N_KV_HEADS):
            kg = kk[:, g * HEAD_DIM:(g + 1) * HEAD_DIM]
            heads = range(g * GQA_GROUP, (g + 1) * GQA_GROUP)
            qg = jnp.concatenate(
                [q_ref[qb * WINDOW:(qb + 1) * WINDOW, h * HEAD_DIM:(h + 1) * HEAD_DIM] for h in heads], axis=0)
            s_scr[qb, g] = _dot_nt(kg, qg)
    ones = jnp.ones((ONES_ROWS, 2 * WINDOW), BF16)
    for qb in range(ATTN_BLOCKS):
        bias_ref = bias0_ref if qb == 0 else bias1_ref
        vv_t = v_all_t[:, qb * WINDOW:(qb + 2) * WINDOW]
        for g in range(N_KV_HEADS):
            lhs = jnp.concatenate([vv_t[g * HEAD_DIM:(g + 1) * HEAD_DIM, :], ones], axis=0)
            o_t = []
            for i in range(GQA_GROUP):
                cols = slice(i * WINDOW, (i + 1) * WINDOW)
                s = s_scr[qb, g, :, cols] + bias_ref[g, :, cols]
                sink = sink_ref[g, :, cols]
                m = jnp.maximum(jnp.max(s, axis=0, keepdims=True), sink)
                p = jnp.exp(s - m).astype(BF16)
                oe = _dot(lhs, p)
                den = oe[HEAD_DIM:HEAD_DIM + 1, :] + jnp.exp(sink - m)
                o_t.append(oe[:HEAD_DIM, :] / den)
            o_ref[qb * WINDOW:(qb + 1) * WINDOW, g * KV_W:(g + 1) * KV_W] = (
                jnp.transpose(jnp.concatenate(o_t, axis=0)).astype(BF16))


ONES_ROWS = 16
ATTN_BLOCKS = 4


def _attn_prompt(sink_rows, q, k, v, bias):
    nb = SEQ // WINDOW
    ns = nb // ATTN_BLOCKS
    rows = ATTN_BLOCKS * WINDOW
    own = lambda b, j: (b * ns + j, 0)
    prev = lambda b, j: (b * nb + jnp.maximum(ATTN_BLOCKS * j - 1, 0), 0)
    group_q = GQA_GROUP * WINDOW
    bias_block = (None, N_KV_HEADS, 2 * WINDOW, group_q)
    return pl.pallas_call(
        _attn_prompt_kernel,
        grid=(BATCH, ns),
        in_specs=[
            pl.BlockSpec((rows, BRANCH_W), own),
            pl.BlockSpec((WINDOW, KV_W), prev),
            pl.BlockSpec((rows, KV_W), own),
            pl.BlockSpec((WINDOW, KV_W), prev),
            pl.BlockSpec((rows, KV_W), own),
            pl.BlockSpec(bias_block, lambda b, j: (jnp.minimum(j, 1), 0, 0, 0)),
            pl.BlockSpec(bias_block, lambda b, j: (1, 0, 0, 0)),
            pl.BlockSpec((N_KV_HEADS, 1, group_q), lambda b, j: (0, 0, 0)),
        ],
        out_specs=pl.BlockSpec((rows, BRANCH_W), own),
        out_shape=jax.ShapeDtypeStruct((N_PROMPT, BRANCH_W), BF16),
        scratch_shapes=[pltpu.VMEM((ATTN_BLOCKS, N_KV_HEADS, 2 * WINDOW, group_q), F32)],
        compiler_params=_params(("arbitrary", "arbitrary")),
        name="attn_prompt",
    )(q, k, k, v, v, bias, bias, sink_rows)


def _prompt_bias():
    slopes = jnp.exp2(-8.0 * jnp.arange(1, N_HEADS + 1, dtype=F32) / N_HEADS)
    qi = jnp.arange(WINDOW, dtype=jnp.int32)[:, None]
    ki = jnp.arange(2 * WINDOW, dtype=jnp.int32)[None, :] - WINDOW
    dist = qi - ki
    valid = (dist >= 0) & (dist < WINDOW)
    bias = -slopes[:, None, None] * dist.astype(F32)[None]
    with_prev = jnp.where(valid[None], bias, NEG_INF)
    first = jnp.where((valid & (ki >= 0))[None], bias, NEG_INF)
    per_head = jnp.stack([first, with_prev])
    grouped = per_head.reshape(2, N_KV_HEADS, GQA_GROUP, WINDOW, 2 * WINDOW)
    return jnp.transpose(grouped, (0, 1, 4, 2, 3)).reshape(2, N_KV_HEADS, 2 * WINDOW, GQA_GROUP * WINDOW)


SAMPLE_BT = 32


def _attn_sample_kernel(q_ref, kn_ref, vn_ref, ck_ref, cv_ref, bias_ref, sink_ref, mask_ref,
                        rep_ref, o_ref):
    bt = q_ref.shape[0]
    mask = mask_ref[...][None]
    sink = sink_ref[...][None]
    qe = _dot(q_ref[...].reshape(bt * N_HEADS, HEAD_DIM), rep_ref[...])
    qe = qe.reshape(bt, N_HEADS, KV_W) * mask
    qe_bf = qe.astype(BF16)
    s = jnp.stack([_dot(qe_bf[b], ck_ref[b].astype(BF16)) for b in range(bt)])
    s = s + bias_ref[...][None]
    s_new = jnp.sum(qe * kn_ref[...], axis=-1, keepdims=True)
    m = jnp.maximum(jnp.maximum(jnp.max(s, axis=-1, keepdims=True), s_new), sink)
    p = jnp.exp(s - m)
    p_new = jnp.exp(s_new - m)
    den = jnp.sum(p, axis=-1, keepdims=True) + p_new + jnp.exp(sink - m)
    p_bf = p.astype(BF16)
    of = jnp.stack([_dot_nt(p_bf[b], cv_ref[b].astype(BF16)) for b in range(bt)])
    of = (of + p_new * vn_ref[...]) * mask / den
    o = (of[..., 0:64] + of[..., 64:128]) + (of[..., 128:192] + of[..., 192:256])
    o_ref[...] = o.astype(BF16)


def _attn_sample(q3, k_new, v_new, ck, cv, l, bias, sinks_col, mask, rep):
    nsteps = DEC_BATCH // SAMPLE_BT
    b3 = lambda i: (i, 0, 0)
    c2 = lambda i: (0, 0)
    cache = pl.BlockSpec((None, SAMPLE_BT, KV_W, WINDOW), lambda i: (l, i, 0, 0))
    return pl.pallas_call(
        _attn_sample_kernel,
        grid=(nsteps,),
        in_specs=[
            pl.BlockSpec((SAMPLE_BT, N_HEADS, HEAD_DIM), b3),
            pl.BlockSpec((SAMPLE_BT, 1, KV_W), b3),
            pl.BlockSpec((SAMPLE_BT, 1, KV_W), b3),
            cache,
            cache,
            pl.BlockSpec((N_HEADS, WINDOW), c2),
            pl.BlockSpec((N_HEADS, 1), c2),
            pl.BlockSpec((N_HEADS, KV_W), c2),
            pl.BlockSpec((HEAD_DIM, KV_W), c2),
        ],
        out_specs=pl.BlockSpec((SAMPLE_BT, N_HEADS, HEAD_DIM), b3),
        out_shape=jax.ShapeDtypeStruct((DEC_BATCH, N_HEADS, HEAD_DIM), BF16),
        compiler_params=_params(("arbitrary",)),
        name="attn_sample",
    )(q3, k_new, v_new, ck, cv, bias, sinks_col, mask, rep)


def _sample_bias(n_buf):
    slopes = jnp.exp2(-8.0 * jnp.arange(1, N_HEADS + 1, dtype=F32) / N_HEADS)
    dist = n_buf - jnp.arange(n_buf, dtype=jnp.int32)
    bias = -slopes[:, None] * dist.astype(F32)[None, :]
    return jnp.where((dist < WINDOW)[None, :], bias, NEG_INF)


def _bch_cols(wa_ref, wb_ref, seg, c):
    r = seg * D_CONV + c * TN
    return wa_ref[:, r:r + TN] if r < WIDE else wb_ref[:, r - WIDE:r - WIDE + TN]


def _conv_prompt_kernel(xn_ref, wa_ref, wb_ref, cw_ref, ob_ref, nc_ref, zbuf, carry):
    t = pl.program_id(1)
    tm = xn_ref.shape[0]
    xn = xn_ref[...]

    @pl.when(t == 0)
    def _():
        zbuf[0:8, :] = jnp.zeros((8, D_CONV), F32)

    @pl.when(t > 0)
    def _():
        zbuf[0:8, :] = carry[...]

    for c in range(D_CONV // TN):
        cols = slice(c * TN, (c + 1) * TN)
        zc = _dot(xn, _bch_cols(wa_ref, wb_ref, 1, c)) * _dot(xn, _bch_cols(wa_ref, wb_ref, 2, c))
        zbuf[8:8 + tm, cols] = zc
        cw = cw_ref[:, cols]
        y = cw[0:1] * zbuf[6:6 + tm, cols] + cw[1:2] * zbuf[7:7 + tm, cols] + cw[2:3] * zc
        ob_ref[:, cols] = (_dot(xn, _bch_cols(wa_ref, wb_ref, 0, c)) * y).astype(BF16)
    carry[...] = zbuf[tm:tm + 8, :]
    nc_ref[...] = zbuf[tm + 6:tm + 8, :]


def _conv_prompt(xn_all, w_in, l, conv_w):
    nt = SEQ // TM_SEQ
    return pl.pallas_call(
        _conv_prompt_kernel,
        grid=(BATCH, nt),
        in_specs=[
            pl.BlockSpec((TM_SEQ, D_MODEL), lambda b, t: (b * nt + t, 0)),
            _layer_block((D_MODEL, WIDE), l, 0, 1),
            _layer_block((D_MODEL, WIDE), l, 0, 2),
            _layer_block((CONV_WIDTH, D_CONV), l, 0, 0),
        ],
        out_specs=[
            pl.BlockSpec((TM_SEQ, D_CONV), lambda b, t: (b * nt + t, 0)),
            pl.BlockSpec((None, CONV_WIDTH - 1, D_CONV), lambda b, t: (b, 0, 0)),
        ],
        out_shape=[
            jax.ShapeDtypeStruct((N_PROMPT, D_CONV), BF16),
            jax.ShapeDtypeStruct((BATCH, CONV_WIDTH - 1, D_CONV), F32),
        ],
        scratch_shapes=[
            pltpu.VMEM((TM_SEQ + 8, D_CONV), F32),
            pltpu.VMEM((8, D_CONV), F32),
        ],
        compiler_params=_params(("arbitrary", "arbitrary")),
        name="conv_prompt",
    )(xn_all, w_in, w_in, conv_w)


def _conv_sample_kernel(xn_ref, wa_ref, wb_ref, cw_ref, cb0_ref, cb1_ref, ob_ref, zc_ref):
    xn = xn_ref[...]
    for c in range(D_CONV // TN):
        cols = slice(c * TN, (c + 1) * TN)
        zc = _dot(xn, _bch_cols(wa_ref, wb_ref, 1, c)) * _dot(xn, _bch_cols(wa_ref, wb_ref, 2, c))
        cw = cw_ref[:, cols]
        y = cw[0:1] * cb0_ref[:, cols] + cw[1:2] * cb1_ref[:, cols] + cw[2:3] * zc
        ob_ref[:, cols] = (_dot(xn, _bch_cols(wa_ref, wb_ref, 0, c)) * y).astype(BF16)
        zc_ref[:, cols] = zc


def _conv_sample(xn_all, w_in, l, conv_w, cb0, cb1):
    c2 = lambda i: (0, 0)
    return pl.pallas_call(
        _conv_sample_kernel,
        grid=(1,),
        in_specs=[
            pl.BlockSpec((DEC_BATCH, D_MODEL), lambda i: (0, 0)),
            _layer_block((D_MODEL, WIDE), l, 0, 1),
            _layer_block((D_MODEL, WIDE), l, 0, 2),
            _layer_block((CONV_WIDTH, D_CONV), l, 0, 0),
            pl.BlockSpec((DEC_BATCH, D_CONV), c2),
            pl.BlockSpec((DEC_BATCH, D_CONV), c2),
        ],
        out_specs=[pl.BlockSpec((DEC_BATCH, D_CONV), c2), pl.BlockSpec((DEC_BATCH, D_CONV), c2)],
        out_shape=[
            jax.ShapeDtypeStruct((DEC_BATCH, D_CONV), BF16),
            jax.ShapeDtypeStruct((DEC_BATCH, D_CONV), F32),
        ],
        compiler_params=_params(("arbitrary",)),
        name="conv_sample",
    )(xn_all, w_in, w_in, conv_w, cb0, cb1)


HALF_V = WIDE - D_GMLP
assert D_GMLP - HALF_V == TN


def _gmlp_uv(xn, wa_ref, wb_ref, vn_ref):
    v_lo = _gelu(_dot(xn, wa_ref[:, D_GMLP:]))
    v_hi = _gelu(_dot(xn, wb_ref[...]))
    ms = (jnp.sum(v_lo * v_lo, axis=-1, keepdims=True)
          + jnp.sum(v_hi * v_hi, axis=-1, keepdims=True)) * (1.0 / D_GMLP)
    r = lax.rsqrt(ms + EPS)
    v_lo, v_hi = v_lo * r * vn_ref[:, :HALF_V], v_hi * r * vn_ref[:, HALF_V:]
    u = _gelu(_dot(xn, wa_ref[:, :D_GMLP]))
    return u, v_lo, v_hi


def _gmlp_prompt_kernel(xn_ref, wa_ref, wb_ref, vn_ref, ws_ref, bs_ref, oc_ref, u_scr, v_scr):
    tm = xn_ref.shape[0]
    u, v_lo, v_hi = _gmlp_uv(xn_ref[...], wa_ref, wb_ref, vn_ref)
    v_scr[:, :HALF_V] = v_lo.astype(BF16)
    v_scr[:, HALF_V:] = v_hi.astype(BF16)
    u_scr[...] = u
    row = lax.broadcasted_iota(jnp.int32, (CHUNK, CHUNK), 0)
    col = lax.broadcasted_iota(jnp.int32, (CHUNK, CHUNK), 1)
    for grp in range(N_SPATIAL_GROUPS):
        w = jnp.where(col <= row, ws_ref[grp], 0.0).astype(BF16)
        cols = slice(grp * SPATIAL_GROUP_W, (grp + 1) * SPATIAL_GROUP_W)
        for ch in range(tm // CHUNK):
            rows = slice(ch * CHUNK, (ch + 1) * CHUNK)
            mix = _dot(w, v_scr[rows, cols]) + bs_ref[grp]
            oc_ref[rows, cols] = (u_scr[rows, cols] * mix).astype(BF16)


def _gmlp_prompt(xn_all, w_in, l, v_norm, w_s, b_s_wide):
    nt = N_PROMPT // TM_SEQ
    return pl.pallas_call(
        _gmlp_prompt_kernel,
        grid=(nt,),
        in_specs=[
            pl.BlockSpec((TM_SEQ, D_MODEL), lambda i: (i, 0)),
            _layer_block((D_MODEL, WIDE), l, 0, 3),
            _layer_block((D_MODEL, TN), l, 0, OFF_G // TN - 1),
            _layer_block((1, D_GMLP), l, 0, 0),
            _layer_block((N_SPATIAL_GROUPS, CHUNK, CHUNK), l, 0, 0, 0),
            _layer_block((N_SPATIAL_GROUPS, CHUNK, SPATIAL_GROUP_W), l, 0, 0, 0),
        ],
        out_specs=pl.BlockSpec((TM_SEQ, D_GMLP), lambda i: (i, 0)),
        out_shape=jax.ShapeDtypeStruct((N_PROMPT, D_GMLP), BF16),
        scratch_shapes=[pltpu.VMEM((TM_SEQ, D_GMLP), F32), pltpu.VMEM((TM_SEQ, D_GMLP), BF16)],
        compiler_params=_params(("arbitrary",)),
        name="gmlp_prompt",
    )(xn_all, w_in, w_in, v_norm, w_s, b_s_wide)


def _gmlp_sample_kernel(xn_ref, wa_ref, wb_ref, vn_ref, ws0_ref, bs0_ref, oc_ref, vg_ref):
    u, v_lo, v_hi = _gmlp_uv(xn_ref[...], wa_ref, wb_ref, vn_ref)
    vg_ref[:, :HALF_V] = v_lo
    vg_ref[:, HALF_V:] = v_hi
    oc_ref[...] = (u * (ws0_ref[...] * vg_ref[...] + bs0_ref[...])).astype(BF16)


def _gmlp_sample(xn_all, w_in, l, v_norm, ws0, bs0):
    c2 = lambda i: (0, 0)
    return pl.pallas_call(
        _gmlp_sample_kernel,
        grid=(1,),
        in_specs=[
            pl.BlockSpec((DEC_BATCH, D_MODEL), lambda i: (0, 0)),
            _layer_block((D_MODEL, WIDE), l, 0, 3),
            _layer_block((D_MODEL, TN), l, 0, OFF_G // TN - 1),
            _layer_block((1, D_GMLP), l, 0, 0),
            pl.BlockSpec((1, D_GMLP), c2),
            pl.BlockSpec((1, D_GMLP), c2),
        ],
        out_specs=[pl.BlockSpec((DEC_BATCH, D_GMLP), c2), pl.BlockSpec((DEC_BATCH, D_GMLP), c2)],
        out_shape=[
            jax.ShapeDtypeStruct((DEC_BATCH, D_GMLP), BF16),
            jax.ShapeDtypeStruct((DEC_BATCH, D_GMLP), F32),
        ],
        compiler_params=_params(("arbitrary",)),
        name="gmlp_sample",
    )(xn_all, w_in, w_in, v_norm, ws0, bs0)


def _merge_kernel(xn_ref, oa_ref, ob_ref, oc_ref, xns_ref, oas_ref, obs_ref, ocs_ref, bg_ref, wg_hbm, wb_hbm,
                  m_ref, stage_g, stage_b, wg_bf, wb_bf, xn_tail, oa_tail, ob_tail, oc_tail, sem, *, layer):
    j = pl.program_id(0)
    i = pl.program_id(1)

    def weight_copies(jj):
        copies = []
        for br in range(N_BRANCHES):
            gcol = pl.multiple_of(OFF_G + br * D_MODEL + jj * TN, TN)
            bcol = pl.multiple_of(jj * TN, TN)
            copies.append(pltpu.make_async_copy(
                wg_hbm.at[layer, :, pl.ds(gcol, TN)], stage_g.at[br], sem.at[br]))
            copies.append(pltpu.make_async_copy(
                wb_hbm.at[layer, br, :, pl.ds(bcol, TN)], stage_b.at[br], sem.at[N_BRANCHES + br]))
        return copies

    @pl.when(i == 0)
    def _():
        @pl.when(j == 0)
        def _():
            for c in weight_copies(0):
                c.start()

        for c in weight_copies(j):
            c.wait()
        def cast_rows(stage, dst):
            def body(c, carry):
                r = pl.multiple_of(c * CAST_ROWS, CAST_ROWS)
                for br in range(N_BRANCHES):
                    dst[br, pl.ds(r, CAST_ROWS), :] = stage[br, pl.ds(r, CAST_ROWS), :].astype(BF16)
                return carry
            lax.fori_loop(0, stage.shape[1] // CAST_ROWS, body, 0)

        cast_rows(stage_g, wg_bf)
        cast_rows(stage_b, wb_bf)

        @pl.when(j + 1 < pl.num_programs(0))
        def _():
            for c in weight_copies(j + 1):
                c.start()

    bg = bg_ref[...]

    def merged(xn, branch_outs):
        acc = None
        for br, o in enumerate(branch_outs):
            gate = jax.nn.sigmoid(_dot(xn, wg_bf[br]) + bg[br:br + 1])
            term = gate * _dot(o, wb_bf[br])
            acc = term if acc is None else acc + term
        return acc.astype(BF16)

    last = pl.num_programs(1) - 1

    @pl.when(i < last)
    def _():
        m_ref[...] = merged(xn_ref[...], (oa_ref[...], ob_ref[...], oc_ref[...]))

    @pl.when(i == last)
    def _():
        for src, src_s, tail in ((xn_ref, xns_ref, xn_tail), (oa_ref, oas_ref, oa_tail),
                                 (ob_ref, obs_ref, ob_tail), (oc_ref, ocs_ref, oc_tail)):
            tail[:PROMPT_TAIL, :] = src[:PROMPT_TAIL, :]
            tail[PROMPT_TAIL:, :] = src_s[...]
        m_ref[...] = merged(xn_tail[...], (oa_tail[...], ob_tail[...], oc_tail[...]))


CAST_ROWS = 256
PROMPT_TAIL = N_PROMPT % TM_ROW
assert PROMPT_TAIL + DEC_BATCH == TM_ROW


def _tail_tile_specs(width):
    prompt = pl.BlockSpec((TM_ROW, width), lambda j, i: (i, 0))
    sample = pl.BlockSpec((DEC_BATCH, width), lambda j, i: (0, 0))
    return prompt, sample


def _merge(xn, w_in, l, b_gate, o_a, o_b, o_c, w_branch):
    nt = N_ROWS // TM_ROW
    nn = D_MODEL // TN
    wide_p, wide_s = _tail_tile_specs(D_MODEL)
    br_p, br_s = _tail_tile_specs(BRANCH_W)
    return pl.pallas_call(
        functools.partial(_merge_kernel, layer=l),
        grid=(nn, nt),
        in_specs=[
            wide_p, br_p, br_p, br_p,
            wide_s, br_s, br_s, br_s,
            pl.BlockSpec((None, N_BRANCHES, TN), lambda j, i: (l, 0, j)),
            pl.BlockSpec(memory_space=pl.ANY),
            pl.BlockSpec(memory_space=pl.ANY),
        ],
        out_specs=pl.BlockSpec((TM_ROW, TN), lambda j, i: (i, j)),
        out_shape=jax.ShapeDtypeStruct((N_ROWS, D_MODEL), BF16),
        scratch_shapes=[
            pltpu.VMEM((N_BRANCHES, D_MODEL, TN), F32),
            pltpu.VMEM((N_BRANCHES, BRANCH_W, TN), F32),
            pltpu.VMEM((N_BRANCHES, D_MODEL, TN), BF16),
            pltpu.VMEM((N_BRANCHES, BRANCH_W, TN), BF16),
            pltpu.VMEM((TM_ROW, D_MODEL), BF16),
            pltpu.VMEM((TM_ROW, BRANCH_W), BF16),
            pltpu.VMEM((TM_ROW, BRANCH_W), BF16),
            pltpu.VMEM((TM_ROW, BRANCH_W), BF16),
            pltpu.SemaphoreType.DMA((2 * N_BRANCHES,)),
        ],
        compiler_params=_params(("arbitrary", "arbitrary")),
        name="merge",
    )(xn[0], o_a[0], o_b[0], o_c[0], xn[1], o_a[1], o_b[1], o_c[1], b_gate, w_in, w_branch)


OUT_SPLIT = 2


def _out_proj_kernel(m_ref, w_ref, xp_ref, xs_ref, g_ref, x1_ref, xn_ref):
    i = pl.program_id(0)
    last = pl.num_programs(0) - 1
    sub = m_ref.shape[0] // OUT_SPLIT

    def project(rows, x):
        x1 = x + _dot(m_ref[rows, :], w_ref[...])
        x1_ref[rows, :] = x1
        xn_ref[rows, :] = _rms_rows(x1, g_ref[...]).astype(BF16)

    @pl.when(i < last)
    def _():
        for s in range(OUT_SPLIT):
            rows = slice(s * sub, (s + 1) * sub)
            project(rows, xp_ref[rows, :])

    @pl.when(i == last)
    def _():
        for s in range(OUT_SPLIT):
            lo, hi = s * sub, (s + 1) * sub
            parts = []
            if lo < PROMPT_TAIL:
                parts.append(xp_ref[lo:min(hi, PROMPT_TAIL), :])
            if hi > PROMPT_TAIL:
                parts.append(xs_ref[max(lo, PROMPT_TAIL) - PROMPT_TAIL:hi - PROMPT_TAIL, :])
            project(slice(lo, hi), parts[0] if len(parts) == 1 else jnp.concatenate(parts, axis=0))


def _out_proj(m, w_out, l, x_p, x_s, sample_block, g_ffn):
    rows = lambda i: (i, 0)
    return pl.pallas_call(
        _out_proj_kernel,
        grid=(N_ROWS // TM_ROW,),
        in_specs=[
            pl.BlockSpec((TM_ROW, D_MODEL), rows),
            _layer_block((D_MODEL, D_MODEL), l, 0, 0),
            pl.BlockSpec((TM_ROW, D_MODEL), rows),
            pl.BlockSpec((DEC_BATCH, D_MODEL), lambda i: (sample_block, 0)),
            _layer_block((1, D_MODEL), l, 0, 0),
        ],
        out_specs=[pl.BlockSpec((TM_ROW, D_MODEL), rows), pl.BlockSpec((TM_ROW, D_MODEL), rows)],
        out_shape=[
            jax.ShapeDtypeStruct((N_ROWS, D_MODEL), F32),
            jax.ShapeDtypeStruct((N_ROWS, D_MODEL), BF16),
        ],
        compiler_params=_params(("arbitrary",)),
        name="out_proj",
    )(m, w_out, x_p, x_s, g_ffn)


TM_UP = 1664
UP_SPLIT = 4


def _ffn_up_kernel(xn_ref, wgate_ref, wup_ref, h_ref, wgate_scr, wup_scr):
    @pl.when(pl.program_id(1) == 0)
    def _():
        wgate_scr[...] = wgate_ref[...].astype(BF16)
        wup_scr[...] = wup_ref[...].astype(BF16)

    sub = xn_ref.shape[0] // UP_SPLIT
    for s in range(UP_SPLIT):
        rows = slice(s * sub, (s + 1) * sub)
        xn = xn_ref[rows, :]
        gate = _dot(xn, wgate_scr[...])
        h_ref[rows, :] = (gate * jax.nn.sigmoid(gate) * _dot(xn, wup_scr[...])).astype(BF16)


def _ffn_up(xn, w_gate_up, l):
    nt = N_ROWS // TM_UP
    nn = D_FF // TN
    return pl.pallas_call(
        _ffn_up_kernel,
        grid=(nn, nt),
        in_specs=[
            pl.BlockSpec((TM_UP, D_MODEL), lambda j, i: (i, 0)),
            pl.BlockSpec((None, D_MODEL, TN), lambda j, i: (l, 0, j)),
            pl.BlockSpec((None, D_MODEL, TN), lambda j, i: (l, 0, nn + j)),
        ],
        out_specs=pl.BlockSpec((TM_UP, TN), lambda j, i: (i, j)),
        out_shape=jax.ShapeDtypeStruct((N_ROWS, D_FF), BF16),
        scratch_shapes=[pltpu.VMEM((D_MODEL, TN), BF16), pltpu.VMEM((D_MODEL, TN), BF16)],
        compiler_params=_params(("arbitrary", "arbitrary")),
        name="ffn_up",
    )(xn, w_gate_up, w_gate_up)


def _ffn_down_kernel(h_ref, w_ref, x_ref, o_ref, w_scr):
    @pl.when(pl.program_id(1) == 0)
    def _():
        w_scr[...] = w_ref[...].astype(BF16)

    o_ref[...] = x_ref[...] + _dot(h_ref[...], w_scr[...])


def _ffn_down(h, w_down, l, x1):
    nt = N_ROWS // TM_ROW
    nn = D_MODEL // TN
    return pl.pallas_call(
        _ffn_down_kernel,
        grid=(nn, nt),
        in_specs=[
            pl.BlockSpec((TM_ROW, D_FF), lambda j, i: (i, 0)),
            pl.BlockSpec((None, D_FF, TN), lambda j, i: (l, 0, j)),
            pl.BlockSpec((TM_ROW, TN), lambda j, i: (i, j)),
        ],
        out_specs=pl.BlockSpec((TM_ROW, TN), lambda j, i: (i, j)),
        out_shape=jax.ShapeDtypeStruct((N_ROWS, D_MODEL), F32),
        scratch_shapes=[pltpu.VMEM((D_FF, TN), BF16)],
        compiler_params=_params(("arbitrary", "arbitrary")),
        name="ffn_down",
    )(h, w_down, x1)


def _ffn_down_split_kernel(h_ref, w_ref, x_ref, yp_ref, ys_ref, w_scr):
    i = pl.program_id(1)

    @pl.when(i == 0)
    def _():
        w_scr[...] = w_ref[...].astype(BF16)

    y = x_ref[...] + _dot(h_ref[...], w_scr[...])
    yp_ref[...] = y

    @pl.when(i == pl.num_programs(1) - 1)
    def _():
        ys_ref[...] = y[PROMPT_TAIL:, :]


def _ffn_down_split(h, w_down, l, x1):
    nt = N_ROWS // TM_ROW
    nn = D_MODEL // TN
    return pl.pallas_call(
        _ffn_down_split_kernel,
        grid=(nn, nt),
        in_specs=[
            pl.BlockSpec((TM_ROW, D_FF), lambda j, i: (i, 0)),
            pl.BlockSpec((None, D_FF, TN), lambda j, i: (l, 0, j)),
            pl.BlockSpec((TM_ROW, TN), lambda j, i: (i, j)),
        ],
        out_specs=[
            pl.BlockSpec((TM_ROW, TN), lambda j, i: (i, j)),
            pl.BlockSpec((DEC_BATCH, TN), lambda j, i: (0, j)),
        ],
        out_shape=[
            jax.ShapeDtypeStruct((N_PROMPT, D_MODEL), F32),
            jax.ShapeDtypeStruct((DEC_BATCH, D_MODEL), F32),
        ],
        scratch_shapes=[pltpu.VMEM((D_FF, TN), BF16)],
        compiler_params=_params(("arbitrary", "arbitrary")),
        name="ffn_down_split",
    )(h, w_down, x1)


def _block_diag_ones(width):
    head = np.arange(width) // HEAD_DIM
    return jnp.asarray(head[:, None] == head[None, :], dtype=BF16)


def _layer(x, l, cache_k, cache_v, state_conv, p, const):
    first, last = l == 0, l == DEPTH - 1
    x_p, x_s = x if first else (x, x)
    sample_block = 0 if first else SAMPLE_BLOCK
    w_in = p["w_in"]
    qn = jnp.tile(p["q_norm"][l], N_HEADS).reshape(1, BRANCH_W)
    kn = jnp.tile(p["k_norm"][l], N_KV_HEADS).reshape(1, KV_W)
    sinks = p["sinks"][l]

    xn_p, q_p, k_p, v_p, klast_p, vlast_p = _qkv(
        x_p, p["norm_mix"], w_in, l, qn, kn, const["ph"],
        row_block0=0, n_rows=N_PROMPT, tm=TM_SEQ, tiles_per_seq=SEQ // TM_SEQ)
    xn_s, q_s, k_s, v_s = _qkv(
        x_s, p["norm_mix"], w_in, l, qn, kn, const["ph"],
        row_block0=sample_block, n_rows=DEC_BATCH, tm=DEC_BATCH)
    sink_rows = jnp.repeat(sinks, WINDOW).reshape(N_KV_HEADS, 1, GQA_GROUP * WINDOW)
    oa_p = _attn_prompt(sink_rows, q_p, k_p, v_p, const["bias_p"])
    oa_s = _attn_sample(
        q_s.reshape(DEC_BATCH, N_HEADS, HEAD_DIM),
        k_s.reshape(DEC_BATCH, 1, KV_W), v_s.reshape(DEC_BATCH, 1, KV_W),
        cache_k, cache_v, l,
        const["bias_s"], sinks.reshape(N_HEADS, 1), const["mask_s"], const["rep_s"],
    ).reshape(DEC_BATCH, BRANCH_W)

    ob_p, nc_p = _conv_prompt(xn_p, w_in, l, p["conv_w"])
    ob_s, zc_s = _conv_sample(xn_s, w_in, l, p["conv_w"], state_conv[l, :, 0], state_conv[l, :, 1])

    w_s = p["w_spatial"]
    b_s = p["b_spatial"]
    bs_wide = jnp.broadcast_to(b_s[:, :, :, None], (DEPTH, N_SPATIAL_GROUPS, CHUNK, SPATIAL_GROUP_W))
    oc_p = _gmlp_prompt(xn_p, w_in, l, p["v_norm"], w_s, bs_wide)
    ws0 = jnp.repeat(w_s[l, :, 0, 0], SPATIAL_GROUP_W).reshape(1, D_GMLP)
    bs0 = jnp.repeat(b_s[l, :, 0], SPATIAL_GROUP_W).reshape(1, D_GMLP)
    oc_s, vg_s = _gmlp_sample(xn_s, w_in, l, p["v_norm"], ws0, bs0)

    m = _merge((xn_p, xn_s), p["w_in_f32"], l, p["b_gate"], (oa_p, oa_s), (ob_p, ob_s), (oc_p, oc_s),
               p["w_branch"])
    x1, xn2 = _out_proj(m, p["w_out"], l, x_p, x_s, sample_block, p["norm_ffn"])
    h = _ffn_up(xn2, p["w_gate_up"], l)
    x2 = _ffn_down_split(h, p["w_down"], l, x1) if last else _ffn_down(h, p["w_down"], l, x1)

    new_k_p = klast_p.reshape(BATCH, WINDOW, N_KV_HEADS, HEAD_DIM)
    new_v_p = vlast_p.reshape(BATCH, WINDOW, N_KV_HEADS, HEAD_DIM)
    new_k_s = k_s.reshape(DEC_BATCH, 1, N_KV_HEADS, HEAD_DIM)
    new_v_s = v_s.reshape(DEC_BATCH, 1, N_KV_HEADS, HEAD_DIM)
    new_conv_s = jnp.stack([state_conv[l, :, 1], zc_s], axis=1)
    return x2, (new_k_p, new_v_p, nc_p, new_k_s, new_v_s, new_conv_s, vg_s.reshape(DEC_BATCH, 1, D_GMLP))


def kernel(x_prompt, x_sample, cache_k, cache_v, state_conv, norm_mix, w_in, b_gate, q_norm, k_norm,
           sinks, conv_w, v_norm, w_spatial, b_spatial, w_branch, w_out, norm_ffn, w_gate_up, w_down):
    assert min(WINDOW, SEQ) == WINDOW and SEQ % TM_SEQ == 0 and TM_SEQ >= WINDOW
    n_buf = cache_k.shape[2]
    assert n_buf == WINDOW
    cache_k = jnp.transpose(cache_k, (0, 1, 3, 4, 2)).reshape(DEPTH, DEC_BATCH, KV_W, n_buf)
    cache_v = jnp.transpose(cache_v, (0, 1, 3, 4, 2)).reshape(DEPTH, DEC_BATCH, KV_W, n_buf)
    p = {
        "norm_mix": norm_mix.reshape(DEPTH, 1, D_MODEL),
        "norm_ffn": norm_ffn.reshape(DEPTH, 1, D_MODEL),
        "v_norm": v_norm.reshape(DEPTH, 1, D_GMLP),
        "w_in": w_in[:, :, :OFF_G].astype(BF16),
        "w_in_f32": w_in,
        "w_branch": w_branch,
        "w_out": w_out.astype(BF16),
        "w_gate_up": w_gate_up,
        "w_down": w_down,
        "b_gate": b_gate, "q_norm": q_norm, "k_norm": k_norm, "sinks": sinks, "conv_w": conv_w,
        "w_spatial": w_spatial, "b_spatial": b_spatial,
    }
    kv_of_col = np.arange(KV_W) // HEAD_DIM
    kv_of_head = np.arange(N_HEADS) // GQA_GROUP
    const = {
        "ph": _block_diag_ones(KV_W),
        "bias_p": _prompt_bias(),
        "bias_s": _sample_bias(n_buf),
        "mask_s": jnp.asarray(kv_of_head[:, None] == kv_of_col[None, :], dtype=F32),
        "rep_s": jnp.asarray(np.tile(np.eye(HEAD_DIM), (1, N_KV_HEADS)), dtype=BF16),
    }
    x = (x_prompt.reshape(N_PROMPT, D_MODEL), x_sample.reshape(DEC_BATCH, D_MODEL))
    per_layer = []
    for l in range(DEPTH):
        x, outs = _layer(x, l, cache_k, cache_v, state_conv, p, const)
        per_layer.append(outs)
    stacked = [jnp.stack([per_layer[l][i] for l in range(DEPTH)]) for i in range(7)]
    y_prompt, y_sample = x
    return (y_prompt.reshape(BATCH, SEQ, D_MODEL), y_sample.reshape(DEC_BATCH, 1, D_MODEL), *stacked)
```

```python
import functools

import jax
import jax.numpy as jnp
import numpy as np
from jax import lax
from jax.experimental import pallas as pl
from jax.experimental.pallas import tpu as pltpu

D_MODEL = 2048
BATCH = 4
SEQ = 2048
DEPTH = 2
DEC_BATCH = 128
BRANCH_W = 1024
HEAD_DIM = 64
N_HEADS = 16
N_KV_HEADS = 4
GQA_GROUP = 4
KV_W = 256
WINDOW = 128
D_CONV = 1024
CONV_WIDTH = 3
D_GMLP = 1024
CHUNK = 128
N_SPATIAL_GROUPS = 8
SPATIAL_GROUP_W = 128
N_BRANCHES = 3
D_FF = 5632
EPS = 1e-6
NEG_INF = -1e30

N_PROMPT = BATCH * SEQ
N_ROWS = N_PROMPT + DEC_BATCH
SAMPLE_BLOCK = N_PROMPT // DEC_BATCH

WIDE = 1536
OFF_G = 6656

TM_SEQ = 512
TM_ROW = 640
TN = 512
VMEM_LIMIT = 56 * 1024 * 1024

F32 = jnp.float32
BF16 = jnp.bfloat16


def _params(sem):
    return pltpu.CompilerParams(dimension_semantics=sem, vmem_limit_bytes=VMEM_LIMIT)


def _rms_rows(x, g):
    ms = jnp.mean(x * x, axis=-1, keepdims=True)
    return x * lax.rsqrt(ms + EPS) * g


def _dot(a, b):
    return jnp.dot(a, b, preferred_element_type=F32)


def _dot_nt(a, b):
    return lax.dot_general(a, b, (((1,), (1,)), ((), ())), preferred_element_type=F32)


def _gelu(x):
    return 0.5 * x * (1.0 + jnp.tanh(np.sqrt(2.0 / np.pi).astype(np.float32) * (x + 0.044715 * (x * x * x))))


def _layer_block(shape, l, *idx):
    return pl.BlockSpec((None, *shape), lambda *_: (l, *idx))


QKV_SPLIT = 2


def _qkv_kernel(x_ref, g_ref, w_ref, qn_ref, kn_ref, ph_ref, xn_ref, q_ref, k_ref, v_ref, *last_refs):
    tm = x_ref.shape[0]
    sub = tm // QKV_SPLIT
    for s in range(QKV_SPLIT):
        rows = slice(s * sub, (s + 1) * sub)
        xn = _rms_rows(x_ref[rows, :], g_ref[...]).astype(BF16)
        xn_ref[rows, :] = xn
        z = _dot(xn, w_ref[...])
        q = z[:, :BRANCH_W]
        k = z[:, BRANCH_W:BRANCH_W + KV_W]
        v = z[:, BRANCH_W + KV_W:]
        q_sq = (q * q).astype(BF16)
        q_ms = jnp.concatenate(
            [_dot(q_sq[:, c:c + KV_W], ph_ref[...]) for c in range(0, BRANCH_W, KV_W)], axis=1) * (1.0 / HEAD_DIM)
        k_ms = _dot((k * k).astype(BF16), ph_ref[...]) * (1.0 / HEAD_DIM)
        q_ref[rows, :] = (q * lax.rsqrt(q_ms + EPS) * qn_ref[...] * (HEAD_DIM ** -0.5)).astype(BF16)
        kn = k * lax.rsqrt(k_ms + EPS) * kn_ref[...]
        k_ref[rows, :] = kn
        v_ref[rows, :] = v
        if last_refs and s == QKV_SPLIT - 1:
            klast_ref, vlast_ref = last_refs
            klast_ref[...] = kn[sub - WINDOW:]
            vlast_ref[...] = v[sub - WINDOW:]


def _qkv(x, g, w_in, l, qn, kn, ph, *, row_block0, n_rows, tm, tiles_per_seq=None):
    nt = n_rows // tm
    const = lambda i: (0, 0)
    rows = lambda i: (i, 0)
    out_specs = [
        pl.BlockSpec((tm, D_MODEL), rows),
        pl.BlockSpec((tm, BRANCH_W), rows),
        pl.BlockSpec((tm, KV_W), rows),
        pl.BlockSpec((tm, KV_W), rows),
    ]
    out_shape = [
        jax.ShapeDtypeStruct((n_rows, D_MODEL), BF16),
        jax.ShapeDtypeStruct((n_rows, BRANCH_W), BF16),
        jax.ShapeDtypeStruct((n_rows, KV_W), F32),
        jax.ShapeDtypeStruct((n_rows, KV_W), F32),
    ]
    if tiles_per_seq is not None:
        n_seq = nt // tiles_per_seq
        out_specs += [pl.BlockSpec((WINDOW, KV_W), lambda i: (i // tiles_per_seq, 0))] * 2
        out_shape += [jax.ShapeDtypeStruct((n_seq * WINDOW, KV_W), F32)] * 2
    return pl.pallas_call(
        _qkv_kernel,
        grid=(nt,),
        in_specs=[
            pl.BlockSpec((tm, D_MODEL), lambda i: (row_block0 + i, 0)),
            _layer_block((1, D_MODEL), l, 0, 0),
            _layer_block((D_MODEL, WIDE), l, 0, 0),
            pl.BlockSpec((1, BRANCH_W), const),
            pl.BlockSpec((1, KV_W), const),
            pl.BlockSpec((KV_W, KV_W), const),
        ],
        out_specs=out_specs,
        out_shape=out_shape,
        compiler_params=_params(("arbitrary",)),
        name="qkv",
    )(x, g, w_in, qn, kn, ph)


def _attn_prompt_kernel(q_ref, kp_ref, ko_ref, vp_ref, vo_ref, bias0_ref, bias1_ref, sink_ref, o_ref, s_scr):
    k_all = jnp.concatenate([kp_ref[...], ko_ref[...]], axis=0).astype(BF16)
    v_all_t = jnp.transpose(jnp.concatenate([vp_ref[...], vo_ref[...]], axis=0)).astype(BF16)
    for qb in range(ATTN_BLOCKS):
        kk = k_all[qb * WINDOW:(qb + 2) * WINDOW, :]
        for g in range(N_KV_HEADS):
            kg = kk[:, g * HEAD_DIM:(g + 1) * HEAD_DIM]
            heads = range(g * GQA_GROUP, (g + 1) * GQA_GROUP)
            qg = jnp.concatenate(
                [q_ref[qb * WINDOW:(qb + 1) * WINDOW, h * HEAD_DIM:(h + 1) * HEAD_DIM] for h in heads], axis=0)
            s_scr[qb, g] = _dot_nt(kg, qg)
    ones = jnp.ones((ONES_ROWS, 2 * WINDOW), BF16)
    for qb in range(ATTN_BLOCKS):
        bias_ref = bias0_ref if qb == 0 else bias1_ref
        vv_t = v_all_t[:, qb * WINDOW:(qb + 2) * WINDOW]
        for g in range(N_KV_HEADS):
            lhs = jnp.concatenate([vv_t[g * HEAD_DIM:(g + 1) * HEAD_DIM, :], ones], axis=0)
            o_t = []
            for i in range(GQA_GROUP):
                cols = slice(i * WINDOW, (i + 1) * WINDOW)
                s = s_scr[qb, g, :, cols] + bias_ref[g, :, cols]
                sink = sink_ref[g, :, cols]
                m = jnp.maximum(jnp.max(s, axis=0, keepdims=True), sink)
                p = jnp.exp(s - m).astype(BF16)
                oe = _dot(lhs, p)
                den = oe[HEAD_DIM:HEAD_DIM + 1, :] + jnp.exp(sink - m)
                o_t.append(oe[:HEAD_DIM, :] / den)
            o_ref[qb * WINDOW:(qb + 1) * WINDOW, g * KV_W:(g + 1) * KV_W] = (
                jnp.transpose(jnp.concatenate(o_t, axis=0)).astype(BF16))


ONES_ROWS = 16
ATTN_BLOCKS = 8


def _attn_prompt(sink_rows, q, k, v, bias):
    nb = SEQ // WINDOW
    ns = nb // ATTN_BLOCKS
    rows = ATTN_BLOCKS * WINDOW
    own = lambda b, j: (b * ns + j, 0)
    prev = lambda b, j: (b * nb + jnp.maximum(ATTN_BLOCKS * j - 1, 0), 0)
    group_q = GQA_GROUP * WINDOW
    bias_block = (None, N_KV_HEADS, 2 * WINDOW, group_q)
    return pl.pallas_call(
        _attn_prompt_kernel,
        grid=(BATCH, ns),
        in_specs=[
            pl.BlockSpec((rows, BRANCH_W), own),
            pl.BlockSpec((WINDOW, KV_W), prev),
            pl.BlockSpec((rows, KV_W), own),
            pl.BlockSpec((WINDOW, KV_W), prev),
            pl.BlockSpec((rows, KV_W), own),
            pl.BlockSpec(bias_block, lambda b, j: (jnp.minimum(j, 1), 0, 0, 0)),
            pl.BlockSpec(bias_block, lambda b, j: (1, 0, 0, 0)),
            pl.BlockSpec((N_KV_HEADS, 1, group_q), lambda b, j: (0, 0, 0)),
        ],
        out_specs=pl.BlockSpec((rows, BRANCH_W), own),
        out_shape=jax.ShapeDtypeStruct((N_PROMPT, BRANCH_W), BF16),
        scratch_shapes=[pltpu.VMEM((ATTN_BLOCKS, N_KV_HEADS, 2 * WINDOW, group_q), F32)],
        compiler_params=_params(("arbitrary", "arbitrary")),
        name="attn_prompt",
    )(q, k, k, v, v, bias, bias, sink_rows)


def _prompt_bias():
    slopes = jnp.exp2(-8.0 * jnp.arange(1, N_HEADS + 1, dtype=F32) / N_HEADS)
    qi = jnp.arange(WINDOW, dtype=jnp.int32)[:, None]
    ki = jnp.arange(2 * WINDOW, dtype=jnp.int32)[None, :] - WINDOW
    dist = qi - ki
    valid = (dist >= 0) & (dist < WINDOW)
    bias = -slopes[:, None, None] * dist.astype(F32)[None]
    with_prev = jnp.where(valid[None], bias, NEG_INF)
    first = jnp.where((valid & (ki >= 0))[None], bias, NEG_INF)
    per_head = jnp.stack([first, with_prev])
    grouped = per_head.reshape(2, N_KV_HEADS, GQA_GROUP, WINDOW, 2 * WINDOW)
    return jnp.transpose(grouped, (0, 1, 4, 2, 3)).reshape(2, N_KV_HEADS, 2 * WINDOW, GQA_GROUP * WINDOW)


SAMPLE_BT = 32


def _attn_sample_kernel(q_ref, kn_ref, vn_ref, ck_ref, cv_ref, bias_ref, sink_ref, mask_ref,
                        rep_ref, o_ref):
    bt = q_ref.shape[0]
    mask = mask_ref[...][None]
    sink = sink_ref[...][None]
    qe = _dot(q_ref[...].reshape(bt * N_HEADS, HEAD_DIM), rep_ref[...])
    qe = qe.reshape(bt, N_HEADS, KV_W) * mask
    qe_bf = qe.astype(BF16)
    s = jnp.stack([_dot(qe_bf[b], ck_ref[b].astype(BF16)) for b in range(bt)])
    s = s + bias_ref[...][None]
    s_new = jnp.sum(qe * kn_ref[...], axis=-1, keepdims=True)
    m = jnp.maximum(jnp.maximum(jnp.max(s, axis=-1, keepdims=True), s_new), sink)
    p = jnp.exp(s - m)
    p_new = jnp.exp(s_new - m)
    den = jnp.sum(p, axis=-1, keepdims=True) + p_new + jnp.exp(sink - m)
    p_bf = p.astype(BF16)
    of = jnp.stack([_dot_nt(p_bf[b], cv_ref[b].astype(BF16)) for b in range(bt)])
    of = (of + p_new * vn_ref[...]) * mask / den
    o = (of[..., 0:64] + of[..., 64:128]) + (of[..., 128:192] + of[..., 192:256])
    o_ref[...] = o.astype(BF16)


def _attn_sample(q3, k_new, v_new, ck, cv, l, bias, sinks_col, mask, rep):
    nsteps = DEC_BATCH // SAMPLE_BT
    b3 = lambda i: (i, 0, 0)
    c2 = lambda i: (0, 0)
    cache = pl.BlockSpec((None, SAMPLE_BT, KV_W, WINDOW), lambda i: (l, i, 0, 0))
    return pl.pallas_call(
        _attn_sample_kernel,
        grid=(nsteps,),
        in_specs=[
            pl.BlockSpec((SAMPLE_BT, N_HEADS, HEAD_DIM), b3),
            pl.BlockSpec((SAMPLE_BT, 1, KV_W), b3),
            pl.BlockSpec((SAMPLE_BT, 1, KV_W), b3),
            cache,
            cache,
            pl.BlockSpec((N_HEADS, WINDOW), c2),
            pl.BlockSpec((N_HEADS, 1), c2),
            pl.BlockSpec((N_HEADS, KV_W), c2),
            pl.BlockSpec((HEAD_DIM, KV_W), c2),
        ],
        out_specs=pl.BlockSpec((SAMPLE_BT, N_HEADS, HEAD_DIM), b3),
        out_shape=jax.ShapeDtypeStruct((DEC_BATCH, N_HEADS, HEAD_DIM), BF16),
        compiler_params=_params(("arbitrary",)),
        name="attn_sample",
    )(q3, k_new, v_new, ck, cv, bias, sinks_col, mask, rep)


def _sample_bias(n_buf):
    slopes = jnp.exp2(-8.0 * jnp.arange(1, N_HEADS + 1, dtype=F32) / N_HEADS)
    dist = n_buf - jnp.arange(n_buf, dtype=jnp.int32)
    bias = -slopes[:, None] * dist.astype(F32)[None, :]
    return jnp.where((dist < WINDOW)[None, :], bias, NEG_INF)


def _bch_cols(wa_ref, wb_ref, seg, c):
    r = seg * D_CONV + c * TN
    return wa_ref[:, r:r + TN] if r < WIDE else wb_ref[:, r - WIDE:r - WIDE + TN]


def _conv_prompt_kernel(xn_ref, wa_ref, wb_ref, cw_ref, ob_ref, nc_ref, zbuf, carry):
    t = pl.program_id(1)
    tm = xn_ref.shape[0]
    xn = xn_ref[...]

    @pl.when(t == 0)
    def _():
        zbuf[0:8, :] = jnp.zeros((8, D_CONV), F32)

    @pl.when(t > 0)
    def _():
        zbuf[0:8, :] = carry[...]

    for c in range(D_CONV // TN):
        cols = slice(c * TN, (c + 1) * TN)
        zc = _dot(xn, _bch_cols(wa_ref, wb_ref, 1, c)) * _dot(xn, _bch_cols(wa_ref, wb_ref, 2, c))
        zbuf[8:8 + tm, cols] = zc
        cw = cw_ref[:, cols]
        y = cw[0:1] * zbuf[6:6 + tm, cols] + cw[1:2] * zbuf[7:7 + tm, cols] + cw[2:3] * zc
        ob_ref[:, cols] = (_dot(xn, _bch_cols(wa_ref, wb_ref, 0, c)) * y).astype(BF16)
    carry[...] = zbuf[tm:tm + 8, :]
    nc_ref[...] = zbuf[tm + 6:tm + 8, :]


def _conv_prompt(xn_all, w_in, l, conv_w):
    nt = SEQ // TM_SEQ
    return pl.pallas_call(
        _conv_prompt_kernel,
        grid=(BATCH, nt),
        in_specs=[
            pl.BlockSpec((TM_SEQ, D_MODEL), lambda b, t: (b * nt + t, 0)),
            _layer_block((D_MODEL, WIDE), l, 0, 1),
            _layer_block((D_MODEL, WIDE), l, 0, 2),
            _layer_block((CONV_WIDTH, D_CONV), l, 0, 0),
        ],
        out_specs=[
            pl.BlockSpec((TM_SEQ, D_CONV), lambda b, t: (b * nt + t, 0)),
            pl.BlockSpec((None, CONV_WIDTH - 1, D_CONV), lambda b, t: (b, 0, 0)),
        ],
        out_shape=[
            jax.ShapeDtypeStruct((N_PROMPT, D_CONV), BF16),
            jax.ShapeDtypeStruct((BATCH, CONV_WIDTH - 1, D_CONV), F32),
        ],
        scratch_shapes=[
            pltpu.VMEM((TM_SEQ + 8, D_CONV), F32),
            pltpu.VMEM((8, D_CONV), F32),
        ],
        compiler_params=_params(("arbitrary", "arbitrary")),
        name="conv_prompt",
    )(xn_all, w_in, w_in, conv_w)


def _conv_sample_kernel(xn_ref, wa_ref, wb_ref, cw_ref, cb0_ref, cb1_ref, ob_ref, zc_ref):
    xn = xn_ref[...]
    for c in range(D_CONV // TN):
        cols = slice(c * TN, (c + 1) * TN)
        zc = _dot(xn, _bch_cols(wa_ref, wb_ref, 1, c)) * _dot(xn, _bch_cols(wa_ref, wb_ref, 2, c))
        cw = cw_ref[:, cols]
        y = cw[0:1] * cb0_ref[:, cols] + cw[1:2] * cb1_ref[:, cols] + cw[2:3] * zc
        ob_ref[:, cols] = (_dot(xn, _bch_cols(wa_ref, wb_ref, 0, c)) * y).astype(BF16)
        zc_ref[:, cols] = zc


def _conv_sample(xn_all, w_in, l, conv_w, cb0, cb1):
    c2 = lambda i: (0, 0)
    return pl.pallas_call(
        _conv_sample_kernel,
        grid=(1,),
        in_specs=[
            pl.BlockSpec((DEC_BATCH, D_MODEL), lambda i: (0, 0)),
            _layer_block((D_MODEL, WIDE), l, 0, 1),
            _layer_block((D_MODEL, WIDE), l, 0, 2),
            _layer_block((CONV_WIDTH, D_CONV), l, 0, 0),
            pl.BlockSpec((DEC_BATCH, D_CONV), c2),
            pl.BlockSpec((DEC_BATCH, D_CONV), c2),
        ],
        out_specs=[pl.BlockSpec((DEC_BATCH, D_CONV), c2), pl.BlockSpec((DEC_BATCH, D_CONV), c2)],
        out_shape=[
            jax.ShapeDtypeStruct((DEC_BATCH, D_CONV), BF16),
            jax.ShapeDtypeStruct((DEC_BATCH, D_CONV), F32),
        ],
        compiler_params=_params(("arbitrary",)),
        name="conv_sample",
    )(xn_all, w_in, w_in, conv_w, cb0, cb1)


HALF_V = WIDE - D_GMLP
assert D_GMLP - HALF_V == TN


def _gmlp_uv(xn, wa_ref, wb_ref, vn_ref):
    v_lo = _gelu(_dot(xn, wa_ref[:, D_GMLP:]))
    v_hi = _gelu(_dot(xn, wb_ref[...]))
    ms = (jnp.sum(v_lo * v_lo, axis=-1, keepdims=True)
          + jnp.sum(v_hi * v_hi, axis=-1, keepdims=True)) * (1.0 / D_GMLP)
    r = lax.rsqrt(ms + EPS)
    v_lo, v_hi = v_lo * r * vn_ref[:, :HALF_V], v_hi * r * vn_ref[:, HALF_V:]
    u = _gelu(_dot(xn, wa_ref[:, :D_GMLP]))
    return u, v_lo, v_hi


def _gmlp_prompt_kernel(xn_ref, wa_ref, wb_ref, vn_ref, ws_ref, bs_ref, oc_ref, u_scr, v_scr):
    tm = xn_ref.shape[0]
    u, v_lo, v_hi = _gmlp_uv(xn_ref[...], wa_ref, wb_ref, vn_ref)
    v_scr[:, :HALF_V] = v_lo.astype(BF16)
    v_scr[:, HALF_V:] = v_hi.astype(BF16)
    u_scr[...] = u
    row = lax.broadcasted_iota(jnp.int32, (CHUNK, CHUNK), 0)
    col = lax.broadcasted_iota(jnp.int32, (CHUNK, CHUNK), 1)
    for grp in range(N_SPATIAL_GROUPS):
        w = jnp.where(col <= row, ws_ref[grp], 0.0).astype(BF16)
        cols = slice(grp * SPATIAL_GROUP_W, (grp + 1) * SPATIAL_GROUP_W)
        for ch in range(tm // CHUNK):
            rows = slice(ch * CHUNK, (ch + 1) * CHUNK)
            mix = _dot(w, v_scr[rows, cols]) + bs_ref[grp]
            oc_ref[rows, cols] = (u_scr[rows, cols] * mix).astype(BF16)


def _gmlp_prompt(xn_all, w_in, l, v_norm, w_s, b_s_wide):
    nt = N_PROMPT // TM_SEQ
    return pl.pallas_call(
        _gmlp_prompt_kernel,
        grid=(nt,),
        in_specs=[
            pl.BlockSpec((TM_SEQ, D_MODEL), lambda i: (i, 0)),
            _layer_block((D_MODEL, WIDE), l, 0, 3),
            _layer_block((D_MODEL, TN), l, 0, OFF_G // TN - 1),
            _layer_block((1, D_GMLP), l, 0, 0),
            _layer_block((N_SPATIAL_GROUPS, CHUNK, CHUNK), l, 0, 0, 0),
            _layer_block((N_SPATIAL_GROUPS, CHUNK, SPATIAL_GROUP_W), l, 0, 0, 0),
        ],
        out_specs=pl.BlockSpec((TM_SEQ, D_GMLP), lambda i: (i, 0)),
        out_shape=jax.ShapeDtypeStruct((N_PROMPT, D_GMLP), BF16),
        scratch_shapes=[pltpu.VMEM((TM_SEQ, D_GMLP), F32), pltpu.VMEM((TM_SEQ, D_GMLP), BF16)],
        compiler_params=_params(("arbitrary",)),
        name="gmlp_prompt",
    )(xn_all, w_in, w_in, v_norm, w_s, b_s_wide)


def _gmlp_sample_kernel(xn_ref, wa_ref, wb_ref, vn_ref, ws0_ref, bs0_ref, oc_ref, vg_ref):
    u, v_lo, v_hi = _gmlp_uv(xn_ref[...], wa_ref, wb_ref, vn_ref)
    vg_ref[:, :HALF_V] = v_lo
    vg_ref[:, HALF_V:] = v_hi
    oc_ref[...] = (u * (ws0_ref[...] * vg_ref[...] + bs0_ref[...])).astype(BF16)


def _gmlp_sample(xn_all, w_in, l, v_norm, ws0, bs0):
    c2 = lambda i: (0, 0)
    return pl.pallas_call(
        _gmlp_sample_kernel,
        grid=(1,),
        in_specs=[
            pl.BlockSpec((DEC_BATCH, D_MODEL), lambda i: (0, 0)),
            _layer_block((D_MODEL, WIDE), l, 0, 3),
            _layer_block((D_MODEL, TN), l, 0, OFF_G // TN - 1),
            _layer_block((1, D_GMLP), l, 0, 0),
            pl.BlockSpec((1, D_GMLP), c2),
            pl.BlockSpec((1, D_GMLP), c2),
        ],
        out_specs=[pl.BlockSpec((DEC_BATCH, D_GMLP), c2), pl.BlockSpec((DEC_BATCH, D_GMLP), c2)],
        out_shape=[
            jax.ShapeDtypeStruct((DEC_BATCH, D_GMLP), BF16),
            jax.ShapeDtypeStruct((DEC_BATCH, D_GMLP), F32),
        ],
        compiler_params=_params(("arbitrary",)),
        name="gmlp_sample",
    )(xn_all, w_in, w_in, v_norm, ws0, bs0)


def _merge_kernel(xn_ref, oa_ref, ob_ref, oc_ref, xns_ref, oas_ref, obs_ref, ocs_ref, bg_ref, wg_hbm, wb_hbm,
                  m_ref, stage_g, stage_b, wg_bf, wb_bf, xn_tail, oa_tail, ob_tail, oc_tail, sem, *, layer):
    j = pl.program_id(0)
    i = pl.program_id(1)

    def weight_copies(jj):
        copies = []
        for br in range(N_BRANCHES):
            gcol = pl.multiple_of(OFF_G + br * D_MODEL + jj * TN, TN)
            bcol = pl.multiple_of(jj * TN, TN)
            copies.append(pltpu.make_async_copy(
                wg_hbm.at[layer, :, pl.ds(gcol, TN)], stage_g.at[br], sem.at[br]))
            copies.append(pltpu.make_async_copy(
                wb_hbm.at[layer, br, :, pl.ds(bcol, TN)], stage_b.at[br], sem.at[N_BRANCHES + br]))
        return copies

    @pl.when(i == 0)
    def _():
        @pl.when(j == 0)
        def _():
            for c in weight_copies(0):
                c.start()

        for c in weight_copies(j):
            c.wait()
        def cast_rows(stage, dst):
            def body(c, carry):
                r = pl.multiple_of(c * CAST_ROWS, CAST_ROWS)
                for br in range(N_BRANCHES):
                    dst[br, pl.ds(r, CAST_ROWS), :] = stage[br, pl.ds(r, CAST_ROWS), :].astype(BF16)
                return carry
            lax.fori_loop(0, stage.shape[1] // CAST_ROWS, body, 0)

        cast_rows(stage_g, wg_bf)
        cast_rows(stage_b, wb_bf)

        @pl.when(j + 1 < pl.num_programs(0))
        def _():
            for c in weight_copies(j + 1):
                c.start()

    bg = bg_ref[...]

    def merged(xn, branch_outs):
        acc = None
        for br, o in enumerate(branch_outs):
            gate = jax.nn.sigmoid(_dot(xn, wg_bf[br]) + bg[br:br + 1])
            term = gate * _dot(o, wb_bf[br])
            acc = term if acc is None else acc + term
        return acc.astype(BF16)

    last = pl.num_programs(1) - 1

    @pl.when(i < last)
    def _():
        m_ref[...] = merged(xn_ref[...], (oa_ref[...], ob_ref[...], oc_ref[...]))

    @pl.when(i == last)
    def _():
        for src, src_s, tail in ((xn_ref, xns_ref, xn_tail), (oa_ref, oas_ref, oa_tail),
                                 (ob_ref, obs_ref, ob_tail), (oc_ref, ocs_ref, oc_tail)):
            tail[:PROMPT_TAIL, :] = src[:PROMPT_TAIL, :]
            tail[PROMPT_TAIL:, :] = src_s[...]
        m_ref[...] = merged(xn_tail[...], (oa_tail[...], ob_tail[...], oc_tail[...]))


CAST_ROWS = 256
PROMPT_TAIL = N_PROMPT % TM_ROW
assert PROMPT_TAIL + DEC_BATCH == TM_ROW


def _tail_tile_specs(width):
    prompt = pl.BlockSpec((TM_ROW, width), lambda j, i: (i, 0))
    sample = pl.BlockSpec((DEC_BATCH, width), lambda j, i: (0, 0))
    return prompt, sample


def _merge(xn, w_in, l, b_gate, o_a, o_b, o_c, w_branch):
    nt = N_ROWS // TM_ROW
    nn = D_MODEL // TN
    wide_p, wide_s = _tail_tile_specs(D_MODEL)
    br_p, br_s = _tail_tile_specs(BRANCH_W)
    return pl.pallas_call(
        functools.partial(_merge_kernel, layer=l),
        grid=(nn, nt),
        in_specs=[
            wide_p, br_p, br_p, br_p,
            wide_s, br_s, br_s, br_s,
            pl.BlockSpec((None, N_BRANCHES, TN), lambda j, i: (l, 0, j)),
            pl.BlockSpec(memory_space=pl.ANY),
            pl.BlockSpec(memory_space=pl.ANY),
        ],
        out_specs=pl.BlockSpec((TM_ROW, TN), lambda j, i: (i, j)),
        out_shape=jax.ShapeDtypeStruct((N_ROWS, D_MODEL), BF16),
        scratch_shapes=[
            pltpu.VMEM((N_BRANCHES, D_MODEL, TN), F32),
            pltpu.VMEM((N_BRANCHES, BRANCH_W, TN), F32),
            pltpu.VMEM((N_BRANCHES, D_MODEL, TN), BF16),
            pltpu.VMEM((N_BRANCHES, BRANCH_W, TN), BF16),
            pltpu.VMEM((TM_ROW, D_MODEL), BF16),
            pltpu.VMEM((TM_ROW, BRANCH_W), BF16),
            pltpu.VMEM((TM_ROW, BRANCH_W), BF16),
            pltpu.VMEM((TM_ROW, BRANCH_W), BF16),
            pltpu.SemaphoreType.DMA((2 * N_BRANCHES,)),
        ],
        compiler_params=_params(("arbitrary", "arbitrary")),
        name="merge",
    )(xn[0], o_a[0], o_b[0], o_c[0], xn[1], o_a[1], o_b[1], o_c[1], b_gate, w_in, w_branch)


OUT_SPLIT = 2


def _out_proj_kernel(m_ref, w_ref, xp_ref, xs_ref, g_ref, x1_ref, xn_ref):
    i = pl.program_id(0)
    last = pl.num_programs(0) - 1
    sub = m_ref.shape[0] // OUT_SPLIT

    def project(rows, x):
        x1 = x + _dot(m_ref[rows, :], w_ref[...])
        x1_ref[rows, :] = x1
        xn_ref[rows, :] = _rms_rows(x1, g_ref[...]).astype(BF16)

    @pl.when(i < last)
    def _():
        for s in range(OUT_SPLIT):
            rows = slice(s * sub, (s + 1) * sub)
            project(rows, xp_ref[rows, :])

    @pl.when(i == last)
    def _():
        for s in range(OUT_SPLIT):
            lo, hi = s * sub, (s + 1) * sub
            parts = []
            if lo < PROMPT_TAIL:
                parts.append(xp_ref[lo:min(hi, PROMPT_TAIL), :])
            if hi > PROMPT_TAIL:
                parts.append(xs_ref[max(lo, PROMPT_TAIL) - PROMPT_TAIL:hi - PROMPT_TAIL, :])
            project(slice(lo, hi), parts[0] if len(parts) == 1 else jnp.concatenate(parts, axis=0))


def _out_proj(m, w_out, l, x_p, x_s, sample_block, g_ffn):
    rows = lambda i: (i, 0)
    return pl.pallas_call(
        _out_proj_kernel,
        grid=(N_ROWS // TM_ROW,),
        in_specs=[
            pl.BlockSpec((TM_ROW, D_MODEL), rows),
            _layer_block((D_MODEL, D_MODEL), l, 0, 0),
            pl.BlockSpec((TM_ROW, D_MODEL), rows),
            pl.BlockSpec((DEC_BATCH, D_MODEL), lambda i: (sample_block, 0)),
            _layer_block((1, D_MODEL), l, 0, 0),
        ],
        out_specs=[pl.BlockSpec((TM_ROW, D_MODEL), rows), pl.BlockSpec((TM_ROW, D_MODEL), rows)],
        out_shape=[
            jax.ShapeDtypeStruct((N_ROWS, D_MODEL), F32),
            jax.ShapeDtypeStruct((N_ROWS, D_MODEL), BF16),
        ],
        compiler_params=_params(("arbitrary",)),
        name="out_proj",
    )(m, w_out, x_p, x_s, g_ffn)


TM_UP = 2080
UP_SPLIT = 5


def _ffn_up_kernel(xn_ref, wgate_ref, wup_ref, h_ref, wgate_scr, wup_scr):
    @pl.when(pl.program_id(1) == 0)
    def _():
        wgate_scr[...] = wgate_ref[...].astype(BF16)
        wup_scr[...] = wup_ref[...].astype(BF16)

    sub = xn_ref.shape[0] // UP_SPLIT
    for s in range(UP_SPLIT):
        rows = slice(s * sub, (s + 1) * sub)
        xn = xn_ref[rows, :]
        gate = _dot(xn, wgate_scr[...])
        h_ref[rows, :] = (gate * jax.nn.sigmoid(gate) * _dot(xn, wup_scr[...])).astype(BF16)


def _ffn_up(xn, w_gate_up, l):
    nt = N_ROWS // TM_UP
    nn = D_FF // TN
    return pl.pallas_call(
        _ffn_up_kernel,
        grid=(nn, nt),
        in_specs=[
            pl.BlockSpec((TM_UP, D_MODEL), lambda j, i: (i, 0)),
            pl.BlockSpec((None, D_MODEL, TN), lambda j, i: (l, 0, j)),
            pl.BlockSpec((None, D_MODEL, TN), lambda j, i: (l, 0, nn + j)),
        ],
        out_specs=pl.BlockSpec((TM_UP, TN), lambda j, i: (i, j)),
        out_shape=jax.ShapeDtypeStruct((N_ROWS, D_FF), BF16),
        scratch_shapes=[pltpu.VMEM((D_MODEL, TN), BF16), pltpu.VMEM((D_MODEL, TN), BF16)],
        compiler_params=_params(("arbitrary", "arbitrary")),
        name="ffn_up",
    )(xn, w_gate_up, w_gate_up)


def _ffn_down_kernel(h_ref, w_ref, x_ref, o_ref, w_scr):
    @pl.when(pl.program_id(1) == 0)
    def _():
        w_scr[...] = w_ref[...].astype(BF16)

    o_ref[...] = x_ref[...] + _dot(h_ref[...], w_scr[...])


def _ffn_down(h, w_down, l, x1):
    nt = N_ROWS // TM_ROW
    nn = D_MODEL // TN
    return pl.pallas_call(
        _ffn_down_kernel,
        grid=(nn, nt),
        in_specs=[
            pl.BlockSpec((TM_ROW, D_FF), lambda j, i: (i, 0)),
            pl.BlockSpec((None, D_FF, TN), lambda j, i: (l, 0, j)),
            pl.BlockSpec((TM_ROW, TN), lambda j, i: (i, j)),
        ],
        out_specs=pl.BlockSpec((TM_ROW, TN), lambda j, i: (i, j)),
        out_shape=jax.ShapeDtypeStruct((N_ROWS, D_MODEL), F32),
        scratch_shapes=[pltpu.VMEM((D_FF, TN), BF16)],
        compiler_params=_params(("arbitrary", "arbitrary")),
        name="ffn_down",
    )(h, w_down, x1)


def _ffn_down_split_kernel(h_ref, w_ref, x_ref, yp_ref, ys_ref, w_scr):
    i = pl.program_id(1)

    @pl.when(i == 0)
    def _():
        w_scr[...] = w_ref[...].astype(BF16)

    y = x_ref[...] + _dot(h_ref[...], w_scr[...])
    yp_ref[...] = y

    @pl.when(i == pl.num_programs(1) - 1)
    def _():
        ys_ref[...] = y[PROMPT_TAIL:, :]


def _ffn_down_split(h, w_down, l, x1):
    nt = N_ROWS // TM_ROW
    nn = D_MODEL // TN
    return pl.pallas_call(
        _ffn_down_split_kernel,
        grid=(nn, nt),
        in_specs=[
            pl.BlockSpec((TM_ROW, D_FF), lambda j, i: (i, 0)),
            pl.BlockSpec((None, D_FF, TN), lambda j, i: (l, 0, j)),
            pl.BlockSpec((TM_ROW, TN), lambda j, i: (i, j)),
        ],
        out_specs=[
            pl.BlockSpec((TM_ROW, TN), lambda j, i: (i, j)),
            pl.BlockSpec((DEC_BATCH, TN), lambda j, i: (0, j)),
        ],
        out_shape=[
            jax.ShapeDtypeStruct((N_PROMPT, D_MODEL), F32),
            jax.ShapeDtypeStruct((DEC_BATCH, D_MODEL), F32),
        ],
        scratch_shapes=[pltpu.VMEM((D_FF, TN), BF16)],
        compiler_params=_params(("arbitrary", "arbitrary")),
        name="ffn_down_split",
    )(h, w_down, x1)


def _block_diag_ones(width):
    head = np.arange(width) // HEAD_DIM
    return jnp.asarray(head[:, None] == head[None, :], dtype=BF16)


def _layer(x, l, cache_k, cache_v, state_conv, p, const):
    first, last = l == 0, l == DEPTH - 1
    x_p, x_s = x if first else (x, x)
    sample_block = 0 if first else SAMPLE_BLOCK
    w_in = p["w_in"]
    qn = jnp.tile(p["q_norm"][l], N_HEADS).reshape(1, BRANCH_W)
    kn = jnp.tile(p["k_norm"][l], N_KV_HEADS).reshape(1, KV_W)
    sinks = p["sinks"][l]

    xn_p, q_p, k_p, v_p, klast_p, vlast_p = _qkv(
        x_p, p["norm_mix"], w_in, l, qn, kn, const["ph"],
        row_block0=0, n_rows=N_PROMPT, tm=TM_SEQ, tiles_per_seq=SEQ // TM_SEQ)
    xn_s, q_s, k_s, v_s = _qkv(
        x_s, p["norm_mix"], w_in, l, qn, kn, const["ph"],
        row_block0=sample_block, n_rows=DEC_BATCH, tm=DEC_BATCH)
    sink_rows = jnp.repeat(sinks, WINDOW).reshape(N_KV_HEADS, 1, GQA_GROUP * WINDOW)
    oa_p = _attn_prompt(sink_rows, q_p, k_p, v_p, const["bias_p"])
    oa_s = _attn_sample(
        q_s.reshape(DEC_BATCH, N_HEADS, HEAD_DIM),
        k_s.reshape(DEC_BATCH, 1, KV_W), v_s.reshape(DEC_BATCH, 1, KV_W),
        cache_k, cache_v, l,
        const["bias_s"], sinks.reshape(N_HEADS, 1), const["mask_s"], const["rep_s"],
    ).reshape(DEC_BATCH, BRANCH_W)

    ob_p, nc_p = _conv_prompt(xn_p, w_in, l, p["conv_w"])
    ob_s, zc_s = _conv_sample(xn_s, w_in, l, p["conv_w"], state_conv[l, :, 0], state_conv[l, :, 1])

    w_s = p["w_spatial"]
    b_s = p["b_spatial"]
    bs_wide = jnp.broadcast_to(b_s[:, :, :, None], (DEPTH, N_SPATIAL_GROUPS, CHUNK, SPATIAL_GROUP_W))
    oc_p = _gmlp_prompt(xn_p, w_in, l, p["v_norm"], w_s, bs_wide)
    ws0 = jnp.repeat(w_s[l, :, 0, 0], SPATIAL_GROUP_W).reshape(1, D_GMLP)
    bs0 = jnp.repeat(b_s[l, :, 0], SPATIAL_GROUP_W).reshape(1, D_GMLP)
    oc_s, vg_s = _gmlp_sample(xn_s, w_in, l, p["v_norm"], ws0, bs0)

    m = _merge((xn_p, xn_s), p["w_in_f32"], l, p["b_gate"], (oa_p, oa_s), (ob_p, ob_s), (oc_p, oc_s),
               p["w_branch"])
    x1, xn2 = _out_proj(m, p["w_out"], l, x_p, x_s, sample_block, p["norm_ffn"])
    h = _ffn_up(xn2, p["w_gate_up"], l)
    x2 = _ffn_down_split(h, p["w_down"], l, x1) if last else _ffn_down(h, p["w_down"], l, x1)

    new_k_p = klast_p.reshape(BATCH, WINDOW, N_KV_HEADS, HEAD_DIM)
    new_v_p = vlast_p.reshape(BATCH, WINDOW, N_KV_HEADS, HEAD_DIM)
    new_k_s = k_s.reshape(DEC_BATCH, 1, N_KV_HEADS, HEAD_DIM)
    new_v_s = v_s.reshape(DEC_BATCH, 1, N_KV_HEADS, HEAD_DIM)
    new_conv_s = jnp.stack([state_conv[l, :, 1], zc_s], axis=1)
    return x2, (new_k_p, new_v_p, nc_p, new_k_s, new_v_s, new_conv_s, vg_s.reshape(DEC_BATCH, 1, D_GMLP))


def kernel(x_prompt, x_sample, cache_k, cache_v, state_conv, norm_mix, w_in, b_gate, q_norm, k_norm,
           sinks, conv_w, v_norm, w_spatial, b_spatial, w_branch, w_out, norm_ffn, w_gate_up, w_down):
    assert min(WINDOW, SEQ) == WINDOW and SEQ % TM_SEQ == 0 and TM_SEQ >= WINDOW
    n_buf = cache_k.shape[2]
    assert n_buf == WINDOW
    cache_k = jnp.transpose(cache_k, (0, 1, 3, 4, 2)).reshape(DEPTH, DEC_BATCH, KV_W, n_buf)
    cache_v = jnp.transpose(cache_v, (0, 1, 3, 4, 2)).reshape(DEPTH, DEC_BATCH, KV_W, n_buf)
    p = {
        "norm_mix": norm_mix.reshape(DEPTH, 1, D_MODEL),
        "norm_ffn": norm_ffn.reshape(DEPTH, 1, D_MODEL),
        "v_norm": v_norm.reshape(DEPTH, 1, D_GMLP),
        "w_in": w_in[:, :, :OFF_G].astype(BF16),
        "w_in_f32": w_in,
        "w_branch": w_branch,
        "w_out": w_out.astype(BF16),
        "w_gate_up": w_gate_up,
        "w_down": w_down,
        "b_gate": b_gate, "q_norm": q_norm, "k_norm": k_norm, "sinks": sinks, "conv_w": conv_w,
        "w_spatial": w_spatial, "b_spatial": b_spatial,
    }
    kv_of_col = np.arange(KV_W) // HEAD_DIM
    kv_of_head = np.arange(N_HEADS) // GQA_GROUP
    const = {
        "ph": _block_diag_ones(KV_W),
        "bias_p": _prompt_bias(),
        "bias_s": _sample_bias(n_buf),
        "mask_s": jnp.asarray(kv_of_head[:, None] == kv_of_col[None, :], dtype=F32),
        "rep_s": jnp.asarray(np.tile(np.eye(HEAD_DIM), (1, N_KV_HEADS)), dtype=BF16),
    }
    x = (x_prompt.reshape(N_PROMPT, D_MODEL), x_sample.reshape(DEC_BATCH, D_MODEL))
    per_layer = []
    for l in range(DEPTH):
        x, outs = _layer(x, l, cache_k, cache_v, state_conv, p, const)
        per_layer.append(outs)
    stacked = [jnp.stack([per_layer[l][i] for l in range(DEPTH)]) for i in range(7)]
    y_prompt, y_sample = x
    return (y_prompt.reshape(BATCH, SEQ, D_MODEL), y_sample.reshape(DEC_BATCH, 1, D_MODEL), *stacked)
```

```python
import functools

import jax
import jax.numpy as jnp
import numpy as np
from jax import lax
from jax.experimental import pallas as pl
from jax.experimental.pallas import tpu as pltpu

D_MODEL = 2048
BATCH = 4
SEQ = 2048
DEPTH = 2
DEC_BATCH = 128
BRANCH_W = 1024
HEAD_DIM = 64
N_HEADS = 16
N_KV_HEADS = 4
GQA_GROUP = 4
KV_W = 256
WINDOW = 128
D_CONV = 1024
CONV_WIDTH = 3
D_GMLP = 1024
CHUNK = 128
N_SPATIAL_GROUPS = 8
SPATIAL_GROUP_W = 128
N_BRANCHES = 3
D_FF = 5632
EPS = 1e-6
NEG_INF = -1e30

N_PROMPT = BATCH * SEQ
N_ROWS = N_PROMPT + DEC_BATCH
SAMPLE_BLOCK = N_PROMPT // DEC_BATCH

WIDE = 1536
OFF_G = 6656

TM_SEQ = 1024
TM_ROW = 640
TN = 512
VMEM_LIMIT = 56 * 1024 * 1024

F32 = jnp.float32
BF16 = jnp.bfloat16


def _params(sem):
    return pltpu.CompilerParams(dimension_semantics=sem, vmem_limit_bytes=VMEM_LIMIT)


def _rms_rows(x, g):
    ms = jnp.mean(x * x, axis=-1, keepdims=True)
    return x * lax.rsqrt(ms + EPS) * g


def _dot(a, b):
    return jnp.dot(a, b, preferred_element_type=F32)


def _dot_nt(a, b):
    return lax.dot_general(a, b, (((1,), (1,)), ((), ())), preferred_element_type=F32)


def _gelu(x):
    return 0.5 * x * (1.0 + jnp.tanh(np.sqrt(2.0 / np.pi).astype(np.float32) * (x + 0.044715 * (x * x * x))))


def _layer_block(shape, l, *idx):
    return pl.BlockSpec((None, *shape), lambda *_: (l, *idx))


QKV_SPLIT = 2


def _qkv_kernel(x_ref, g_ref, w_ref, qn_ref, kn_ref, ph_ref, xn_ref, q_ref, k_ref, v_ref, *last_refs):
    tm = x_ref.shape[0]
    sub = tm // QKV_SPLIT
    for s in range(QKV_SPLIT):
        rows = slice(s * sub, (s + 1) * sub)
        xn = _rms_rows(x_ref[rows, :], g_ref[...]).astype(BF16)
        xn_ref[rows, :] = xn
        z = _dot(xn, w_ref[...])
        q = z[:, :BRANCH_W]
        k = z[:, BRANCH_W:BRANCH_W + KV_W]
        v = z[:, BRANCH_W + KV_W:]
        q_sq = (q * q).astype(BF16)
        q_ms = jnp.concatenate(
            [_dot(q_sq[:, c:c + KV_W], ph_ref[...]) for c in range(0, BRANCH_W, KV_W)], axis=1) * (1.0 / HEAD_DIM)
        k_ms = _dot((k * k).astype(BF16), ph_ref[...]) * (1.0 / HEAD_DIM)
        q_ref[rows, :] = (q * lax.rsqrt(q_ms + EPS) * qn_ref[...] * (HEAD_DIM ** -0.5)).astype(BF16)
        kn = k * lax.rsqrt(k_ms + EPS) * kn_ref[...]
        k_ref[rows, :] = kn
        v_ref[rows, :] = v
        if last_refs and s == QKV_SPLIT - 1:
            klast_ref, vlast_ref = last_refs
            klast_ref[...] = kn[sub - WINDOW:]
            vlast_ref[...] = v[sub - WINDOW:]


def _qkv(x, g, w_in, l, qn, kn, ph, *, row_block0, n_rows, tm, tiles_per_seq=None):
    nt = n_rows // tm
    const = lambda i: (0, 0)
    rows = lambda i: (i, 0)
    out_specs = [
        pl.BlockSpec((tm, D_MODEL), rows),
        pl.BlockSpec((tm, BRANCH_W), rows),
        pl.BlockSpec((tm, KV_W), rows),
        pl.BlockSpec((tm, KV_W), rows),
    ]
    out_shape = [
        jax.ShapeDtypeStruct((n_rows, D_MODEL), BF16),
        jax.ShapeDtypeStruct((n_rows, BRANCH_W), BF16),
        jax.ShapeDtypeStruct((n_rows, KV_W), F32),
        jax.ShapeDtypeStruct((n_rows, KV_W), F32),
    ]
    if tiles_per_seq is not None:
        n_seq = nt // tiles_per_seq
        out_specs += [pl.BlockSpec((WINDOW, KV_W), lambda i: (i // tiles_per_seq, 0))] * 2
        out_shape += [jax.ShapeDtypeStruct((n_seq * WINDOW, KV_W), F32)] * 2
    return pl.pallas_call(
        _qkv_kernel,
        grid=(nt,),
        in_specs=[
            pl.BlockSpec((tm, D_MODEL), lambda i: (row_block0 + i, 0)),
            _layer_block((1, D_MODEL), l, 0, 0),
            _layer_block((D_MODEL, WIDE), l, 0, 0),
            pl.BlockSpec((1, BRANCH_W), const),
            pl.BlockSpec((1, KV_W), const),
            pl.BlockSpec((KV_W, KV_W), const),
        ],
        out_specs=out_specs,
        out_shape=out_shape,
        compiler_params=_params(("arbitrary",)),
        name="qkv",
    )(x, g, w_in, qn, kn, ph)


def _attn_prompt_kernel(q_ref, kp_ref, ko_ref, vp_ref, vo_ref, bias0_ref, bias1_ref, sink_ref, o_ref, s_scr):
    k_all = jnp.concatenate([kp_ref[...], ko_ref[...]], axis=0).astype(BF16)
    v_all_t = jnp.transpose(jnp.concatenate([vp_ref[...], vo_ref[...]], axis=0)).astype(BF16)
    for qb in range(ATTN_BLOCKS):
        kk = k_all[qb * WINDOW:(qb + 2) * WINDOW, :]
        for g in range(N_KV_HEADS):
            kg = kk[:, g * HEAD_DIM:(g + 1) * HEAD_DIM]
            heads = range(g * GQA_GROUP, (g + 1) * GQA_GROUP)
            qg = jnp.concatenate(
                [q_ref[qb * WINDOW:(qb + 1) * WINDOW, h * HEAD_DIM:(h + 1) * HEAD_DIM] for h in heads], axis=0)
            s_scr[qb, g] = _dot_nt(kg, qg)
    ones = jnp.ones((ONES_ROWS, 2 * WINDOW), BF16)
    for qb in range(ATTN_BLOCKS):
        bias_ref = bias0_ref if qb == 0 else bias1_ref
        vv_t = v_all_t[:, qb * WINDOW:(qb + 2) * WINDOW]
        for g in range(N_KV_HEADS):
            lhs = jnp.concatenate([vv_t[g * HEAD_DIM:(g + 1) * HEAD_DIM, :], ones], axis=0)
            o_t = []
            for i in range(GQA_GROUP):
                cols = slice(i * WINDOW, (i + 1) * WINDOW)
                s = s_scr[qb, g, :, cols] + bias_ref[g, :, cols]
                sink = sink_ref[g, :, cols]
                m = jnp.maximum(jnp.max(s, axis=0, keepdims=True), sink)
                p = jnp.exp(s - m).astype(BF16)
                oe = _dot(lhs, p)
                den = oe[HEAD_DIM:HEAD_DIM + 1, :] + jnp.exp(sink - m)
                o_t.append(oe[:HEAD_DIM, :] / den)
            o_ref[qb * WINDOW:(qb + 1) * WINDOW, g * KV_W:(g + 1) * KV_W] = (
                jnp.transpose(jnp.concatenate(o_t, axis=0)).astype(BF16))


ONES_ROWS = 16
ATTN_BLOCKS = 8


def _attn_prompt(sink_rows, q, k, v, bias):
    nb = SEQ // WINDOW
    ns = nb // ATTN_BLOCKS
    rows = ATTN_BLOCKS * WINDOW
    own = lambda b, j: (b * ns + j, 0)
    prev = lambda b, j: (b * nb + jnp.maximum(ATTN_BLOCKS * j - 1, 0), 0)
    group_q = GQA_GROUP * WINDOW
    bias_block = (None, N_KV_HEADS, 2 * WINDOW, group_q)
    return pl.pallas_call(
        _attn_prompt_kernel,
        grid=(BATCH, ns),
        in_specs=[
            pl.BlockSpec((rows, BRANCH_W), own),
            pl.BlockSpec((WINDOW, KV_W), prev),
            pl.BlockSpec((rows, KV_W), own),
            pl.BlockSpec((WINDOW, KV_W), prev),
            pl.BlockSpec((rows, KV_W), own),
            pl.BlockSpec(bias_block, lambda b, j: (jnp.minimum(j, 1), 0, 0, 0)),
            pl.BlockSpec(bias_block, lambda b, j: (1, 0, 0, 0)),
            pl.BlockSpec((N_KV_HEADS, 1, group_q), lambda b, j: (0, 0, 0)),
        ],
        out_specs=pl.BlockSpec((rows, BRANCH_W), own),
        out_shape=jax.ShapeDtypeStruct((N_PROMPT, BRANCH_W), BF16),
        scratch_shapes=[pltpu.VMEM((ATTN_BLOCKS, N_KV_HEADS, 2 * WINDOW, group_q), F32)],
        compiler_params=_params(("arbitrary", "arbitrary")),
        name="attn_prompt",
    )(q, k, k, v, v, bias, bias, sink_rows)


def _prompt_bias():
    slopes = jnp.exp2(-8.0 * jnp.arange(1, N_HEADS + 1, dtype=F32) / N_HEADS)
    qi = jnp.arange(WINDOW, dtype=jnp.int32)[:, None]
    ki = jnp.arange(2 * WINDOW, dtype=jnp.int32)[None, :] - WINDOW
    dist = qi - ki
    valid = (dist >= 0) & (dist < WINDOW)
    bias = -slopes[:, None, None] * dist.astype(F32)[None]
    with_prev = jnp.where(valid[None], bias, NEG_INF)
    first = jnp.where((valid & (ki >= 0))[None], bias, NEG_INF)
    per_head = jnp.stack([first, with_prev])
    grouped = per_head.reshape(2, N_KV_HEADS, GQA_GROUP, WINDOW, 2 * WINDOW)
    return jnp.transpose(grouped, (0, 1, 4, 2, 3)).reshape(2, N_KV_HEADS, 2 * WINDOW, GQA_GROUP * WINDOW)


SAMPLE_BT = 32


def _attn_sample_kernel(q_ref, kn_ref, vn_ref, ck_ref, cv_ref, bias_ref, sink_ref, mask_ref,
                        rep_ref, o_ref):
    bt = q_ref.shape[0]
    mask = mask_ref[...][None]
    sink = sink_ref[...][None]
    qe = _dot(q_ref[...].reshape(bt * N_HEADS, HEAD_DIM), rep_ref[...])
    qe = qe.reshape(bt, N_HEADS, KV_W) * mask
    qe_bf = qe.astype(BF16)
    s = jnp.stack([_dot(qe_bf[b], ck_ref[b].astype(BF16)) for b in range(bt)])
    s = s + bias_ref[...][None]
    s_new = jnp.sum(qe * kn_ref[...], axis=-1, keepdims=True)
    m = jnp.maximum(jnp.maximum(jnp.max(s, axis=-1, keepdims=True), s_new), sink)
    p = jnp.exp(s - m)
    p_new = jnp.exp(s_new - m)
    den = jnp.sum(p, axis=-1, keepdims=True) + p_new + jnp.exp(sink - m)
    p_bf = p.astype(BF16)
    of = jnp.stack([_dot_nt(p_bf[b], cv_ref[b].astype(BF16)) for b in range(bt)])
    of = (of + p_new * vn_ref[...]) * mask / den
    o = (of[..., 0:64] + of[..., 64:128]) + (of[..., 128:192] + of[..., 192:256])
    o_ref[...] = o.astype(BF16)


def _attn_sample(q3, k_new, v_new, ck, cv, l, bias, sinks_col, mask, rep):
    nsteps = DEC_BATCH // SAMPLE_BT
    b3 = lambda i: (i, 0, 0)
    c2 = lambda i: (0, 0)
    cache = pl.BlockSpec((None, SAMPLE_BT, KV_W, WINDOW), lambda i: (l, i, 0, 0))
    return pl.pallas_call(
        _attn_sample_kernel,
        grid=(nsteps,),
        in_specs=[
            pl.BlockSpec((SAMPLE_BT, N_HEADS, HEAD_DIM), b3),
            pl.BlockSpec((SAMPLE_BT, 1, KV_W), b3),
            pl.BlockSpec((SAMPLE_BT, 1, KV_W), b3),
            cache,
            cache,
            pl.BlockSpec((N_HEADS, WINDOW), c2),
            pl.BlockSpec((N_HEADS, 1), c2),
            pl.BlockSpec((N_HEADS, KV_W), c2),
            pl.BlockSpec((HEAD_DIM, KV_W), c2),
        ],
        out_specs=pl.BlockSpec((SAMPLE_BT, N_HEADS, HEAD_DIM), b3),
        out_shape=jax.ShapeDtypeStruct((DEC_BATCH, N_HEADS, HEAD_DIM), BF16),
        compiler_params=_params(("arbitrary",)),
        name="attn_sample",
    )(q3, k_new, v_new, ck, cv, bias, sinks_col, mask, rep)


def _sample_bias(n_buf):
    slopes = jnp.exp2(-8.0 * jnp.arange(1, N_HEADS + 1, dtype=F32) / N_HEADS)
    dist = n_buf - jnp.arange(n_buf, dtype=jnp.int32)
    bias = -slopes[:, None] * dist.astype(F32)[None, :]
    return jnp.where((dist < WINDOW)[None, :], bias, NEG_INF)


def _bch_cols(wa_ref, wb_ref, seg, c):
    r = seg * D_CONV + c * TN
    return wa_ref[:, r:r + TN] if r < WIDE else wb_ref[:, r - WIDE:r - WIDE + TN]


def _conv_prompt_kernel(xn_ref, wa_ref, wb_ref, cw_ref, ob_ref, nc_ref, zbuf, carry):
    t = pl.program_id(1)
    tm = xn_ref.shape[0]
    xn = xn_ref[...]

    @pl.when(t == 0)
    def _():
        zbuf[0:8, :] = jnp.zeros((8, D_CONV), F32)

    @pl.when(t > 0)
    def _():
        zbuf[0:8, :] = carry[...]

    for c in range(D_CONV // TN):
        cols = slice(c * TN, (c + 1) * TN)
        zc = _dot(xn, _bch_cols(wa_ref, wb_ref, 1, c)) * _dot(xn, _bch_cols(wa_ref, wb_ref, 2, c))
        zbuf[8:8 + tm, cols] = zc
        cw = cw_ref[:, cols]
        y = cw[0:1] * zbuf[6:6 + tm, cols] + cw[1:2] * zbuf[7:7 + tm, cols] + cw[2:3] * zc
        ob_ref[:, cols] = (_dot(xn, _bch_cols(wa_ref, wb_ref, 0, c)) * y).astype(BF16)
    carry[...] = zbuf[tm:tm + 8, :]
    nc_ref[...] = zbuf[tm + 6:tm + 8, :]


def _conv_prompt(xn_all, w_in, l, conv_w):
    nt = SEQ // TM_SEQ
    return pl.pallas_call(
        _conv_prompt_kernel,
        grid=(BATCH, nt),
        in_specs=[
            pl.BlockSpec((TM_SEQ, D_MODEL), lambda b, t: (b * nt + t, 0)),
            _layer_block((D_MODEL, WIDE), l, 0, 1),
            _layer_block((D_MODEL, WIDE), l, 0, 2),
            _layer_block((CONV_WIDTH, D_CONV), l, 0, 0),
        ],
        out_specs=[
            pl.BlockSpec((TM_SEQ, D_CONV), lambda b, t: (b * nt + t, 0)),
            pl.BlockSpec((None, CONV_WIDTH - 1, D_CONV), lambda b, t: (b, 0, 0)),
        ],
        out_shape=[
            jax.ShapeDtypeStruct((N_PROMPT, D_CONV), BF16),
            jax.ShapeDtypeStruct((BATCH, CONV_WIDTH - 1, D_CONV), F32),
        ],
        scratch_shapes=[
            pltpu.VMEM((TM_SEQ + 8, D_CONV), F32),
            pltpu.VMEM((8, D_CONV), F32),
        ],
        compiler_params=_params(("arbitrary", "arbitrary")),
        name="conv_prompt",
    )(xn_all, w_in, w_in, conv_w)


def _conv_sample_kernel(xn_ref, wa_ref, wb_ref, cw_ref, cb0_ref, cb1_ref, ob_ref, zc_ref):
    xn = xn_ref[...]
    for c in range(D_CONV // TN):
        cols = slice(c * TN, (c + 1) * TN)
        zc = _dot(xn, _bch_cols(wa_ref, wb_ref, 1, c)) * _dot(xn, _bch_cols(wa_ref, wb_ref, 2, c))
        cw = cw_ref[:, cols]
        y = cw[0:1] * cb0_ref[:, cols] + cw[1:2] * cb1_ref[:, cols] + cw[2:3] * zc
        ob_ref[:, cols] = (_dot(xn, _bch_cols(wa_ref, wb_ref, 0, c)) * y).astype(BF16)
        zc_ref[:, cols] = zc


def _conv_sample(xn_all, w_in, l, conv_w, cb0, cb1):
    c2 = lambda i: (0, 0)
    return pl.pallas_call(
        _conv_sample_kernel,
        grid=(1,),
        in_specs=[
            pl.BlockSpec((DEC_BATCH, D_MODEL), lambda i: (0, 0)),
            _layer_block((D_MODEL, WIDE), l, 0, 1),
            _layer_block((D_MODEL, WIDE), l, 0, 2),
            _layer_block((CONV_WIDTH, D_CONV), l, 0, 0),
            pl.BlockSpec((DEC_BATCH, D_CONV), c2),
            pl.BlockSpec((DEC_BATCH, D_CONV), c2),
        ],
        out_specs=[pl.BlockSpec((DEC_BATCH, D_CONV), c2), pl.BlockSpec((DEC_BATCH, D_CONV), c2)],
        out_shape=[
            jax.ShapeDtypeStruct((DEC_BATCH, D_CONV), BF16),
            jax.ShapeDtypeStruct((DEC_BATCH, D_CONV), F32),
        ],
        compiler_params=_params(("arbitrary",)),
        name="conv_sample",
    )(xn_all, w_in, w_in, conv_w, cb0, cb1)


HALF_V = WIDE - D_GMLP
assert D_GMLP - HALF_V == TN


def _gmlp_uv(xn, wa_ref, wb_ref, vn_ref):
    v_lo = _gelu(_dot(xn, wa_ref[:, D_GMLP:]))
    v_hi = _gelu(_dot(xn, wb_ref[...]))
    ms = (jnp.sum(v_lo * v_lo, axis=-1, keepdims=True)
          + jnp.sum(v_hi * v_hi, axis=-1, keepdims=True)) * (1.0 / D_GMLP)
    r = lax.rsqrt(ms + EPS)
    v_lo, v_hi = v_lo * r * vn_ref[:, :HALF_V], v_hi * r * vn_ref[:, HALF_V:]
    u = _gelu(_dot(xn, wa_ref[:, :D_GMLP]))
    return u, v_lo, v_hi


def _gmlp_prompt_kernel(xn_ref, wa_ref, wb_ref, vn_ref, ws_ref, bs_ref, oc_ref, u_scr, v_scr):
    tm = xn_ref.shape[0]
    u, v_lo, v_hi = _gmlp_uv(xn_ref[...], wa_ref, wb_ref, vn_ref)
    v_scr[:, :HALF_V] = v_lo.astype(BF16)
    v_scr[:, HALF_V:] = v_hi.astype(BF16)
    u_scr[...] = u
    row = lax.broadcasted_iota(jnp.int32, (CHUNK, CHUNK), 0)
    col = lax.broadcasted_iota(jnp.int32, (CHUNK, CHUNK), 1)
    for grp in range(N_SPATIAL_GROUPS):
        w = jnp.where(col <= row, ws_ref[grp], 0.0).astype(BF16)
        cols = slice(grp * SPATIAL_GROUP_W, (grp + 1) * SPATIAL_GROUP_W)
        for ch in range(tm // CHUNK):
            rows = slice(ch * CHUNK, (ch + 1) * CHUNK)
            mix = _dot(w, v_scr[rows, cols]) + bs_ref[grp]
            oc_ref[rows, cols] = (u_scr[rows, cols] * mix).astype(BF16)


def _gmlp_prompt(xn_all, w_in, l, v_norm, w_s, b_s_wide):
    nt = N_PROMPT // TM_SEQ
    return pl.pallas_call(
        _gmlp_prompt_kernel,
        grid=(nt,),
        in_specs=[
            pl.BlockSpec((TM_SEQ, D_MODEL), lambda i: (i, 0)),
            _layer_block((D_MODEL, WIDE), l, 0, 3),
            _layer_block((D_MODEL, TN), l, 0, OFF_G // TN - 1),
            _layer_block((1, D_GMLP), l, 0, 0),
            _layer_block((N_SPATIAL_GROUPS, CHUNK, CHUNK), l, 0, 0, 0),
            _layer_block((N_SPATIAL_GROUPS, CHUNK, SPATIAL_GROUP_W), l, 0, 0, 0),
        ],
        out_specs=pl.BlockSpec((TM_SEQ, D_GMLP), lambda i: (i, 0)),
        out_shape=jax.ShapeDtypeStruct((N_PROMPT, D_GMLP), BF16),
        scratch_shapes=[pltpu.VMEM((TM_SEQ, D_GMLP), F32), pltpu.VMEM((TM_SEQ, D_GMLP), BF16)],
        compiler_params=_params(("arbitrary",)),
        name="gmlp_prompt",
    )(xn_all, w_in, w_in, v_norm, w_s, b_s_wide)


def _gmlp_sample_kernel(xn_ref, wa_ref, wb_ref, vn_ref, ws0_ref, bs0_ref, oc_ref, vg_ref):
    u, v_lo, v_hi = _gmlp_uv(xn_ref[...], wa_ref, wb_ref, vn_ref)
    vg_ref[:, :HALF_V] = v_lo
    vg_ref[:, HALF_V:] = v_hi
    oc_ref[...] = (u * (ws0_ref[...] * vg_ref[...] + bs0_ref[...])).astype(BF16)


def _gmlp_sample(xn_all, w_in, l, v_norm, ws0, bs0):
    c2 = lambda i: (0, 0)
    return pl.pallas_call(
        _gmlp_sample_kernel,
        grid=(1,),
        in_specs=[
            pl.BlockSpec((DEC_BATCH, D_MODEL), lambda i: (0, 0)),
            _layer_block((D_MODEL, WIDE), l, 0, 3),
            _layer_block((D_MODEL, TN), l, 0, OFF_G // TN - 1),
            _layer_block((1, D_GMLP), l, 0, 0),
            pl.BlockSpec((1, D_GMLP), c2),
            pl.BlockSpec((1, D_GMLP), c2),
        ],
        out_specs=[pl.BlockSpec((DEC_BATCH, D_GMLP), c2), pl.BlockSpec((DEC_BATCH, D_GMLP), c2)],
        out_shape=[
            jax.ShapeDtypeStruct((DEC_BATCH, D_GMLP), BF16),
            jax.ShapeDtypeStruct((DEC_BATCH, D_GMLP), F32),
        ],
        compiler_params=_params(("arbitrary",)),
        name="gmlp_sample",
    )(xn_all, w_in, w_in, v_norm, ws0, bs0)


def _merge_kernel(xn_ref, oa_ref, ob_ref, oc_ref, xns_ref, oas_ref, obs_ref, ocs_ref, bg_ref, wg_hbm, wb_hbm,
                  m_ref, stage_g, stage_b, wg_bf, wb_bf, xn_tail, oa_tail, ob_tail, oc_tail, sem, *, layer):
    j = pl.program_id(0)
    i = pl.program_id(1)

    def weight_copies(jj):
        copies = []
        for br in range(N_BRANCHES):
            gcol = pl.multiple_of(OFF_G + br * D_MODEL + jj * TN, TN)
            bcol = pl.multiple_of(jj * TN, TN)
            copies.append(pltpu.make_async_copy(
                wg_hbm.at[layer, :, pl.ds(gcol, TN)], stage_g.at[br], sem.at[br]))
            copies.append(pltpu.make_async_copy(
                wb_hbm.at[layer, br, :, pl.ds(bcol, TN)], stage_b.at[br], sem.at[N_BRANCHES + br]))
        return copies

    @pl.when(i == 0)
    def _():
        @pl.when(j == 0)
        def _():
            for c in weight_copies(0):
                c.start()

        for c in weight_copies(j):
            c.wait()
        def cast_rows(stage, dst):
            def body(c, carry):
                r = pl.multiple_of(c * CAST_ROWS, CAST_ROWS)
                for br in range(N_BRANCHES):
                    dst[br, pl.ds(r, CAST_ROWS), :] = stage[br, pl.ds(r, CAST_ROWS), :].astype(BF16)
                return carry
            lax.fori_loop(0, stage.shape[1] // CAST_ROWS, body, 0)

        cast_rows(stage_g, wg_bf)
        cast_rows(stage_b, wb_bf)

        @pl.when(j + 1 < pl.num_programs(0))
        def _():
            for c in weight_copies(j + 1):
                c.start()

    bg = bg_ref[...]

    def merged(xn, branch_outs):
        acc = None
        for br, o in enumerate(branch_outs):
            gate = jax.nn.sigmoid(_dot(xn, wg_bf[br]) + bg[br:br + 1])
            term = gate * _dot(o, wb_bf[br])
            acc = term if acc is None else acc + term
        return acc.astype(BF16)

    last = pl.num_programs(1) - 1

    @pl.when(i < last)
    def _():
        m_ref[...] = merged(xn_ref[...], (oa_ref[...], ob_ref[...], oc_ref[...]))

    @pl.when(i == last)
    def _():
        for src, src_s, tail in ((xn_ref, xns_ref, xn_tail), (oa_ref, oas_ref, oa_tail),
                                 (ob_ref, obs_ref, ob_tail), (oc_ref, ocs_ref, oc_tail)):
            tail[:PROMPT_TAIL, :] = src[:PROMPT_TAIL, :]
            tail[PROMPT_TAIL:, :] = src_s[...]
        m_ref[...] = merged(xn_tail[...], (oa_tail[...], ob_tail[...], oc_tail[...]))


CAST_ROWS = 256
PROMPT_TAIL = N_PROMPT % TM_ROW
assert PROMPT_TAIL + DEC_BATCH == TM_ROW


def _tail_tile_specs(width):
    prompt = pl.BlockSpec((TM_ROW, width), lambda j, i: (i, 0))
    sample = pl.BlockSpec((DEC_BATCH, width), lambda j, i: (0, 0))
    return prompt, sample


def _merge(xn, w_in, l, b_gate, o_a, o_b, o_c, w_branch):
    nt = N_ROWS // TM_ROW
    nn = D_MODEL // TN
    wide_p, wide_s = _tail_tile_specs(D_MODEL)
    br_p, br_s = _tail_tile_specs(BRANCH_W)
    return pl.pallas_call(
        functools.partial(_merge_kernel, layer=l),
        grid=(nn, nt),
        in_specs=[
            wide_p, br_p, br_p, br_p,
            wide_s, br_s, br_s, br_s,
            pl.BlockSpec((None, N_BRANCHES, TN), lambda j, i: (l, 0, j)),
            pl.BlockSpec(memory_space=pl.ANY),
            pl.BlockSpec(memory_space=pl.ANY),
        ],
        out_specs=pl.BlockSpec((TM_ROW, TN), lambda j, i: (i, j)),
        out_shape=jax.ShapeDtypeStruct((N_ROWS, D_MODEL), BF16),
        scratch_shapes=[
            pltpu.VMEM((N_BRANCHES, D_MODEL, TN), F32),
            pltpu.VMEM((N_BRANCHES, BRANCH_W, TN), F32),
            pltpu.VMEM((N_BRANCHES, D_MODEL, TN), BF16),
            pltpu.VMEM((N_BRANCHES, BRANCH_W, TN), BF16),
            pltpu.VMEM((TM_ROW, D_MODEL), BF16),
            pltpu.VMEM((TM_ROW, BRANCH_W), BF16),
            pltpu.VMEM((TM_ROW, BRANCH_W), BF16),
            pltpu.VMEM((TM_ROW, BRANCH_W), BF16),
            pltpu.SemaphoreType.DMA((2 * N_BRANCHES,)),
        ],
        compiler_params=_params(("arbitrary", "arbitrary")),
        name="merge",
    )(xn[0], o_a[0], o_b[0], o_c[0], xn[1], o_a[1], o_b[1], o_c[1], b_gate, w_in, w_branch)


OUT_SPLIT = 2


def _out_proj_kernel(m_ref, w_ref, xp_ref, xs_ref, g_ref, x1_ref, xn_ref):
    i = pl.program_id(0)
    last = pl.num_programs(0) - 1
    sub = m_ref.shape[0] // OUT_SPLIT

    def project(rows, x):
        x1 = x + _dot(m_ref[rows, :], w_ref[...])
        x1_ref[rows, :] = x1
        xn_ref[rows, :] = _rms_rows(x1, g_ref[...]).astype(BF16)

    @pl.when(i < last)
    def _():
        for s in range(OUT_SPLIT):
            rows = slice(s * sub, (s + 1) * sub)
            project(rows, xp_ref[rows, :])

    @pl.when(i == last)
    def _():
        for s in range(OUT_SPLIT):
            lo, hi = s * sub, (s + 1) * sub
            parts = []
            if lo < PROMPT_TAIL:
                parts.append(xp_ref[lo:min(hi, PROMPT_TAIL), :])
            if hi > PROMPT_TAIL:
                parts.append(xs_ref[max(lo, PROMPT_TAIL) - PROMPT_TAIL:hi - PROMPT_TAIL, :])
            project(slice(lo, hi), parts[0] if len(parts) == 1 else jnp.concatenate(parts, axis=0))


def _out_proj(m, w_out, l, x_p, x_s, sample_block, g_ffn):
    rows = lambda i: (i, 0)
    return pl.pallas_call(
        _out_proj_kernel,
        grid=(N_ROWS // TM_ROW,),
        in_specs=[
            pl.BlockSpec((TM_ROW, D_MODEL), rows),
            _layer_block((D_MODEL, D_MODEL), l, 0, 0),
            pl.BlockSpec((TM_ROW, D_MODEL), rows),
            pl.BlockSpec((DEC_BATCH, D_MODEL), lambda i: (sample_block, 0)),
            _layer_block((1, D_MODEL), l, 0, 0),
        ],
        out_specs=[pl.BlockSpec((TM_ROW, D_MODEL), rows), pl.BlockSpec((TM_ROW, D_MODEL), rows)],
        out_shape=[
            jax.ShapeDtypeStruct((N_ROWS, D_MODEL), F32),
            jax.ShapeDtypeStruct((N_ROWS, D_MODEL), BF16),
        ],
        compiler_params=_params(("arbitrary",)),
        name="out_proj",
    )(m, w_out, x_p, x_s, g_ffn)


TM_UP = 2080
UP_SPLIT = 5


def _ffn_up_kernel(xn_ref, wgate_ref, wup_ref, h_ref, wgate_scr, wup_scr):
    @pl.when(pl.program_id(1) == 0)
    def _():
        wgate_scr[...] = wgate_ref[...].astype(BF16)
        wup_scr[...] = wup_ref[...].astype(BF16)

    sub = xn_ref.shape[0] // UP_SPLIT
    for s in range(UP_SPLIT):
        rows = slice(s * sub, (s + 1) * sub)
        xn = xn_ref[rows, :]
        gate = _dot(xn, wgate_scr[...])
        h_ref[rows, :] = (gate * jax.nn.sigmoid(gate) * _dot(xn, wup_scr[...])).astype(BF16)


def _ffn_up(xn, w_gate_up, l):
    nt = N_ROWS // TM_UP
    nn = D_FF // TN
    return pl.pallas_call(
        _ffn_up_kernel,
        grid=(nn, nt),
        in_specs=[
            pl.BlockSpec((TM_UP, D_MODEL), lambda j, i: (i, 0)),
            pl.BlockSpec((None, D_MODEL, TN), lambda j, i: (l, 0, j)),
            pl.BlockSpec((None, D_MODEL, TN), lambda j, i: (l, 0, nn + j)),
        ],
        out_specs=pl.BlockSpec((TM_UP, TN), lambda j, i: (i, j)),
        out_shape=jax.ShapeDtypeStruct((N_ROWS, D_FF), BF16),
        scratch_shapes=[pltpu.VMEM((D_MODEL, TN), BF16), pltpu.VMEM((D_MODEL, TN), BF16)],
        compiler_params=_params(("arbitrary", "arbitrary")),
        name="ffn_up",
    )(xn, w_gate_up, w_gate_up)


def _ffn_down_kernel(h_ref, w_ref, x_ref, o_ref, w_scr):
    @pl.when(pl.program_id(1) == 0)
    def _():
        w_scr[...] = w_ref[...].astype(BF16)

    o_ref[...] = x_ref[...] + _dot(h_ref[...], w_scr[...])


def _ffn_down(h, w_down, l, x1):
    nt = N_ROWS // TM_ROW
    nn = D_MODEL // TN
    return pl.pallas_call(
        _ffn_down_kernel,
        grid=(nn, nt),
        in_specs=[
            pl.BlockSpec((TM_ROW, D_FF), lambda j, i: (i, 0)),
            pl.BlockSpec((None, D_FF, TN), lambda j, i: (l, 0, j)),
            pl.BlockSpec((TM_ROW, TN), lambda j, i: (i, j)),
        ],
        out_specs=pl.BlockSpec((TM_ROW, TN), lambda j, i: (i, j)),
        out_shape=jax.ShapeDtypeStruct((N_ROWS, D_MODEL), F32),
        scratch_shapes=[pltpu.VMEM((D_FF, TN), BF16)],
        compiler_params=_params(("arbitrary", "arbitrary")),
        name="ffn_down",
    )(h, w_down, x1)


def _ffn_down_split_kernel(h_ref, w_ref, x_ref, yp_ref, ys_ref, w_scr):
    i = pl.program_id(1)

    @pl.when(i == 0)
    def _():
        w_scr[...] = w_ref[...].astype(BF16)

    y = x_ref[...] + _dot(h_ref[...], w_scr[...])
    yp_ref[...] = y

    @pl.when(i == pl.num_programs(1) - 1)
    def _():
        ys_ref[...] = y[PROMPT_TAIL:, :]


def _ffn_down_split(h, w_down, l, x1):
    nt = N_ROWS // TM_ROW
    nn = D_MODEL // TN
    return pl.pallas_call(
        _ffn_down_split_kernel,
        grid=(nn, nt),
        in_specs=[
            pl.BlockSpec((TM_ROW, D_FF), lambda j, i: (i, 0)),
            pl.BlockSpec((None, D_FF, TN), lambda j, i: (l, 0, j)),
            pl.BlockSpec((TM_ROW, TN), lambda j, i: (i, j)),
        ],
        out_specs=[
            pl.BlockSpec((TM_ROW, TN), lambda j, i: (i, j)),
            pl.BlockSpec((DEC_BATCH, TN), lambda j, i: (0, j)),
        ],
        out_shape=[
            jax.ShapeDtypeStruct((N_PROMPT, D_MODEL), F32),
            jax.ShapeDtypeStruct((DEC_BATCH, D_MODEL), F32),
        ],
        scratch_shapes=[pltpu.VMEM((D_FF, TN), BF16)],
        compiler_params=_params(("arbitrary", "arbitrary")),
        name="ffn_down_split",
    )(h, w_down, x1)


def _block_diag_ones(width):
    head = np.arange(width) // HEAD_DIM
    return jnp.asarray(head[:, None] == head[None, :], dtype=BF16)


def _layer(x, l, cache_k, cache_v, state_conv, p, const):
    first, last = l == 0, l == DEPTH - 1
    x_p, x_s = x if first else (x, x)
    sample_block = 0 if first else SAMPLE_BLOCK
    w_in = p["w_in"]
    qn = jnp.tile(p["q_norm"][l], N_HEADS).reshape(1, BRANCH_W)
    kn = jnp.tile(p["k_norm"][l], N_KV_HEADS).reshape(1, KV_W)
    sinks = p["sinks"][l]

    xn_p, q_p, k_p, v_p, klast_p, vlast_p = _qkv(
        x_p, p["norm_mix"], w_in, l, qn, kn, const["ph"],
        row_block0=0, n_rows=N_PROMPT, tm=TM_SEQ, tiles_per_seq=SEQ // TM_SEQ)
    xn_s, q_s, k_s, v_s = _qkv(
        x_s, p["norm_mix"], w_in, l, qn, kn, const["ph"],
        row_block0=sample_block, n_rows=DEC_BATCH, tm=DEC_BATCH)
    sink_rows = jnp.repeat(sinks, WINDOW).reshape(N_KV_HEADS, 1, GQA_GROUP * WINDOW)
    oa_p = _attn_prompt(sink_rows, q_p, k_p, v_p, const["bias_p"])
    oa_s = _attn_sample(
        q_s.reshape(DEC_BATCH, N_HEADS, HEAD_DIM),
        k_s.reshape(DEC_BATCH, 1, KV_W), v_s.reshape(DEC_BATCH, 1, KV_W),
        cache_k, cache_v, l,
        const["bias_s"], sinks.reshape(N_HEADS, 1), const["mask_s"], const["rep_s"],
    ).reshape(DEC_BATCH, BRANCH_W)

    ob_p, nc_p = _conv_prompt(xn_p, w_in, l, p["conv_w"])
    ob_s, zc_s = _conv_sample(xn_s, w_in, l, p["conv_w"], state_conv[l, :, 0], state_conv[l, :, 1])

    w_s = p["w_spatial"]
    b_s = p["b_spatial"]
    bs_wide = jnp.broadcast_to(b_s[:, :, :, None], (DEPTH, N_SPATIAL_GROUPS, CHUNK, SPATIAL_GROUP_W))
    oc_p = _gmlp_prompt(xn_p, w_in, l, p["v_norm"], w_s, bs_wide)
    ws0 = jnp.repeat(w_s[l, :, 0, 0], SPATIAL_GROUP_W).reshape(1, D_GMLP)
    bs0 = jnp.repeat(b_s[l, :, 0], SPATIAL_GROUP_W).reshape(1, D_GMLP)
    oc_s, vg_s = _gmlp_sample(xn_s, w_in, l, p["v_norm"], ws0, bs0)

    m = _merge((xn_p, xn_s), p["w_in_f32"], l, p["b_gate"], (oa_p, oa_s), (ob_p, ob_s), (oc_p, oc_s),
               p["w_branch"])
    x1, xn2 = _out_proj(m, p["w_out"], l, x_p, x_s, sample_block, p["norm_ffn"])
    h = _ffn_up(xn2, p["w_gate_up"], l)
    x2 = _ffn_down_split(h, p["w_down"], l, x1) if last else _ffn_down(h, p["w_down"], l, x1)

    new_k_p = klast_p.reshape(BATCH, WINDOW, N_KV_HEADS, HEAD_DIM)
    new_v_p = vlast_p.reshape(BATCH, WINDOW, N_KV_HEADS, HEAD_DIM)
    new_k_s = k_s.reshape(DEC_BATCH, 1, N_KV_HEADS, HEAD_DIM)
    new_v_s = v_s.reshape(DEC_BATCH, 1, N_KV_HEADS, HEAD_DIM)
    new_conv_s = jnp.stack([state_conv[l, :, 1], zc_s], axis=1)
    return x2, (new_k_p, new_v_p, nc_p, new_k_s, new_v_s, new_conv_s, vg_s.reshape(DEC_BATCH, 1, D_GMLP))


def kernel(x_prompt, x_sample, cache_k, cache_v, state_conv, norm_mix, w_in, b_gate, q_norm, k_norm,
           sinks, conv_w, v_norm, w_spatial, b_spatial, w_branch, w_out, norm_ffn, w_gate_up, w_down):
    assert min(WINDOW, SEQ) == WINDOW and SEQ % TM_SEQ == 0 and TM_SEQ >= WINDOW
    n_buf = cache_k.shape[2]
    assert n_buf == WINDOW
    cache_k = jnp.transpose(cache_k, (0, 1, 3, 4, 2)).reshape(DEPTH, DEC_BATCH, KV_W, n_buf)
    cache_v = jnp.transpose(cache_v, (0, 1, 3, 4, 2)).reshape(DEPTH, DEC_BATCH, KV_W, n_buf)
    p = {
        "norm_mix": norm_mix.reshape(DEPTH, 1, D_MODEL),
        "norm_ffn": norm_ffn.reshape(DEPTH, 1, D_MODEL),
        "v_norm": v_norm.reshape(DEPTH, 1, D_GMLP),
        "w_in": w_in[:, :, :OFF_G].astype(BF16),
        "w_in_f32": w_in,
        "w_branch": w_branch,
        "w_out": w_out.astype(BF16),
        "w_gate_up": w_gate_up,
        "w_down": w_down,
        "b_gate": b_gate, "q_norm": q_norm, "k_norm": k_norm, "sinks": sinks, "conv_w": conv_w,
        "w_spatial": w_spatial, "b_spatial": b_spatial,
    }
    kv_of_col = np.arange(KV_W) // HEAD_DIM
    kv_of_head = np.arange(N_HEADS) // GQA_GROUP
    const = {
        "ph": _block_diag_ones(KV_W),
        "bias_p": _prompt_bias(),
        "bias_s": _sample_bias(n_buf),
        "mask_s": jnp.asarray(kv_of_head[:, None] == kv_of_col[None, :], dtype=F32),
        "rep_s": jnp.asarray(np.tile(np.eye(HEAD_DIM), (1, N_KV_HEADS)), dtype=BF16),
    }
    x = (x_prompt.reshape(N_PROMPT, D_MODEL), x_sample.reshape(DEC_BATCH, D_MODEL))
    per_layer = []
    for l in range(DEPTH):
        x, outs = _layer(x, l, cache_k, cache_v, state_conv, p, const)
        per_layer.append(outs)
    stacked = [jnp.stack([per_layer[l][i] for l in range(DEPTH)]) for i in range(7)]
    y_prompt, y_sample = x
    return (y_prompt.reshape(BATCH, SEQ, D_MODEL), y_sample.reshape(DEC_BATCH, 1, D_MODEL), *stacked)
```

```python
import functools

import jax
import jax.numpy as jnp
import numpy as np
from jax import lax
from jax.experimental import pallas as pl
from jax.experimental.pallas import tpu as pltpu

D_MODEL = 2048
BATCH = 4
SEQ = 2048
DEPTH = 2
DEC_BATCH = 128
BRANCH_W = 1024
HEAD_DIM = 64
N_HEADS = 16
N_KV_HEADS = 4
GQA_GROUP = 4
KV_W = 256
WINDOW = 128
D_CONV = 1024
CONV_WIDTH = 3
D_GMLP = 1024
CHUNK = 128
N_SPATIAL_GROUPS = 8
SPATIAL_GROUP_W = 128
N_BRANCHES = 3
D_FF = 5632
EPS = 1e-6
NEG_INF = -1e30

N_PROMPT = BATCH * SEQ
N_ROWS = N_PROMPT + DEC_BATCH
SAMPLE_BLOCK = N_PROMPT // DEC_BATCH

WIDE = 1536
OFF_G = 6656

TM_SEQ = 512
TM_ROW = 640
TN = 512
VMEM_LIMIT = 56 * 1024 * 1024

F32 = jnp.float32
BF16 = jnp.bfloat16


def _params(sem):
    return pltpu.CompilerParams(dimension_semantics=sem, vmem_limit_bytes=VMEM_LIMIT)


def _rms_rows(x, g):
    ms = jnp.mean(x * x, axis=-1, keepdims=True)
    return x * lax.rsqrt(ms + EPS) * g


def _dot(a, b):
    return jnp.dot(a, b, preferred_element_type=F32)


def _dot_nt(a, b):
    return lax.dot_general(a, b, (((1,), (1,)), ((), ())), preferred_element_type=F32)


def _gelu(x):
    return 0.5 * x * (1.0 + jnp.tanh(np.sqrt(2.0 / np.pi).astype(np.float32) * (x + 0.044715 * (x * x * x))))


def _layer_block(shape, l, *idx):
    return pl.BlockSpec((None, *shape), lambda *_: (l, *idx))


QKV_SPLIT = 2


def _qkv_kernel(x_ref, g_ref, w_ref, qn_ref, kn_ref, ph_ref, xn_ref, q_ref, k_ref, v_ref, *last_refs):
    tm = x_ref.shape[0]
    sub = tm // QKV_SPLIT
    for s in range(QKV_SPLIT):
        rows = slice(s * sub, (s + 1) * sub)
        xn = _rms_rows(x_ref[rows, :], g_ref[...]).astype(BF16)
        xn_ref[rows, :] = xn
        z = _dot(xn, w_ref[...])
        q = z[:, :BRANCH_W]
        k = z[:, BRANCH_W:BRANCH_W + KV_W]
        v = z[:, BRANCH_W + KV_W:]
        q_sq = (q * q).astype(BF16)
        q_ms = jnp.concatenate(
            [_dot(q_sq[:, c:c + KV_W], ph_ref[...]) for c in range(0, BRANCH_W, KV_W)], axis=1) * (1.0 / HEAD_DIM)
        k_ms = _dot((k * k).astype(BF16), ph_ref[...]) * (1.0 / HEAD_DIM)
        q_ref[rows, :] = (q * lax.rsqrt(q_ms + EPS) * qn_ref[...] * (HEAD_DIM ** -0.5)).astype(BF16)
        kn = k * lax.rsqrt(k_ms + EPS) * kn_ref[...]
        k_ref[rows, :] = kn
        v_ref[rows, :] = v
        if last_refs and s == QKV_SPLIT - 1:
            klast_ref, vlast_ref = last_refs
            klast_ref[...] = kn[sub - WINDOW:]
            vlast_ref[...] = v[sub - WINDOW:]


def _qkv(x, g, w_in, l, qn, kn, ph, *, row_block0, n_rows, tm, tiles_per_seq=None):
    nt = n_rows // tm
    const = lambda i: (0, 0)
    rows = lambda i: (i, 0)
    out_specs = [
        pl.BlockSpec((tm, D_MODEL), rows),
        pl.BlockSpec((tm, BRANCH_W), rows),
        pl.BlockSpec((tm, KV_W), rows),
        pl.BlockSpec((tm, KV_W), rows),
    ]
    out_shape = [
        jax.ShapeDtypeStruct((n_rows, D_MODEL), BF16),
        jax.ShapeDtypeStruct((n_rows, BRANCH_W), BF16),
        jax.ShapeDtypeStruct((n_rows, KV_W), F32),
        jax.ShapeDtypeStruct((n_rows, KV_W), F32),
    ]
    if tiles_per_seq is not None:
        n_seq = nt // tiles_per_seq
        out_specs += [pl.BlockSpec((WINDOW, KV_W), lambda i: (i // tiles_per_seq, 0))] * 2
        out_shape += [jax.ShapeDtypeStruct((n_seq * WINDOW, KV_W), F32)] * 2
    return pl.pallas_call(
        _qkv_kernel,
        grid=(nt,),
        in_specs=[
            pl.BlockSpec((tm, D_MODEL), lambda i: (row_block0 + i, 0)),
            _layer_block((1, D_MODEL), l, 0, 0),
            _layer_block((D_MODEL, WIDE), l, 0, 0),
            pl.BlockSpec((1, BRANCH_W), const),
            pl.BlockSpec((1, KV_W), const),
            pl.BlockSpec((KV_W, KV_W), const),
        ],
        out_specs=out_specs,
        out_shape=out_shape,
        compiler_params=_params(("arbitrary",)),
        name="qkv",
    )(x, g, w_in, qn, kn, ph)


def _attn_prompt_kernel(q_ref, kp_ref, ko_ref, vp_ref, vo_ref, bias0_ref, bias1_ref, sink_ref, o_ref, s_scr):
    k_all = jnp.concatenate([kp_ref[...], ko_ref[...]], axis=0).astype(BF16)
    v_all_t = jnp.transpose(jnp.concatenate([vp_ref[...], vo_ref[...]], axis=0)).astype(BF16)
    for qb in range(ATTN_BLOCKS):
        kk = k_all[qb * WINDOW:(qb + 2) * WINDOW, :]
        for g in range(N_KV_HEADS):
            kg = kk[:, g * HEAD_DIM:(g + 1) * HEAD_DIM]
            heads = range(g * GQA_GROUP, (g + 1) * GQA_GROUP)
            qg = jnp.concatenate(
                [q_ref[qb * WINDOW:(qb + 1) * WINDOW, h * HEAD_DIM:(h + 1) * HEAD_DIM] for h in heads], axis=0)
            s_scr[qb, g] = _dot_nt(kg, qg)
    ones = jnp.ones((ONES_ROWS, 2 * WINDOW), BF16)
    for qb in range(ATTN_BLOCKS):
        bias_ref = bias0_ref if qb == 0 else bias1_ref
        vv_t = v_all_t[:, qb * WINDOW:(qb + 2) * WINDOW]
        for g in range(N_KV_HEADS):
            lhs = jnp.concatenate([vv_t[g * HEAD_DIM:(g + 1) * HEAD_DIM, :], ones], axis=0)
            o_t = []
            for i in range(GQA_GROUP):
                cols = slice(i * WINDOW, (i + 1) * WINDOW)
                s = s_scr[qb, g, :, cols] + bias_ref[g, :, cols]
                sink = sink_ref[g, :, cols]
                m = jnp.maximum(jnp.max(s, axis=0, keepdims=True), sink)
                p = jnp.exp(s - m).astype(BF16)
                oe = _dot(lhs, p)
                den = oe[HEAD_DIM:HEAD_DIM + 1, :] + jnp.exp(sink - m)
                o_t.append(oe[:HEAD_DIM, :] / den)
            o_ref[qb * WINDOW:(qb + 1) * WINDOW, g * KV_W:(g + 1) * KV_W] = (
                jnp.transpose(jnp.concatenate(o_t, axis=0)).astype(BF16))


ONES_ROWS = 16
ATTN_BLOCKS = 8


def _attn_prompt(sink_rows, q, k, v, bias):
    nb = SEQ // WINDOW
    ns = nb // ATTN_BLOCKS
    rows = ATTN_BLOCKS * WINDOW
    own = lambda b, j: (b * ns + j, 0)
    prev = lambda b, j: (b * nb + jnp.maximum(ATTN_BLOCKS * j - 1, 0), 0)
    group_q = GQA_GROUP * WINDOW
    bias_block = (None, N_KV_HEADS, 2 * WINDOW, group_q)
    return pl.pallas_call(
        _attn_prompt_kernel,
        grid=(BATCH, ns),
        in_specs=[
            pl.BlockSpec((rows, BRANCH_W), own),
            pl.BlockSpec((WINDOW, KV_W), prev),
            pl.BlockSpec((rows, KV_W), own),
            pl.BlockSpec((WINDOW, KV_W), prev),
            pl.BlockSpec((rows, KV_W), own),
            pl.BlockSpec(bias_block, lambda b, j: (jnp.minimum(j, 1), 0, 0, 0)),
            pl.BlockSpec(bias_block, lambda b, j: (1, 0, 0, 0)),
            pl.BlockSpec((N_KV_HEADS, 1, group_q), lambda b, j: (0, 0, 0)),
        ],
        out_specs=pl.BlockSpec((rows, BRANCH_W), own),
        out_shape=jax.ShapeDtypeStruct((N_PROMPT, BRANCH_W), BF16),
        scratch_shapes=[pltpu.VMEM((ATTN_BLOCKS, N_KV_HEADS, 2 * WINDOW, group_q), F32)],
        compiler_params=_params(("arbitrary", "arbitrary")),
        name="attn_prompt",
    )(q, k, k, v, v, bias, bias, sink_rows)


def _prompt_bias():
    slopes = jnp.exp2(-8.0 * jnp.arange(1, N_HEADS + 1, dtype=F32) / N_HEADS)
    qi = jnp.arange(WINDOW, dtype=jnp.int32)[:, None]
    ki = jnp.arange(2 * WINDOW, dtype=jnp.int32)[None, :] - WINDOW
    dist = qi - ki
    valid = (dist >= 0) & (dist < WINDOW)
    bias = -slopes[:, None, None] * dist.astype(F32)[None]
    with_prev = jnp.where(valid[None], bias, NEG_INF)
    first = jnp.where((valid & (ki >= 0))[None], bias, NEG_INF)
    per_head = jnp.stack([first, with_prev])
    grouped = per_head.reshape(2, N_KV_HEADS, GQA_GROUP, WINDOW, 2 * WINDOW)
    return jnp.transpose(grouped, (0, 1, 4, 2, 3)).reshape(2, N_KV_HEADS, 2 * WINDOW, GQA_GROUP * WINDOW)


SAMPLE_BT = 32


def _attn_sample_kernel(q_ref, kn_ref, vn_ref, ck_ref, cv_ref, bias_ref, sink_ref, mask_ref,
                        rep_ref, o_ref):
    bt = q_ref.shape[0]
    mask = mask_ref[...][None]
    sink = sink_ref[...][None]
    qe = _dot(q_ref[...].reshape(bt * N_HEADS, HEAD_DIM), rep_ref[...])
    qe = qe.reshape(bt, N_HEADS, KV_W) * mask
    qe_bf = qe.astype(BF16)
    s = jnp.stack([_dot(qe_bf[b], ck_ref[b].astype(BF16)) for b in range(bt)])
    s = s + bias_ref[...][None]
    s_new = jnp.sum(qe * kn_ref[...], axis=-1, keepdims=True)
    m = jnp.maximum(jnp.maximum(jnp.max(s, axis=-1, keepdims=True), s_new), sink)
    p = jnp.exp(s - m)
    p_new = jnp.exp(s_new - m)
    den = jnp.sum(p, axis=-1, keepdims=True) + p_new + jnp.exp(sink - m)
    p_bf = p.astype(BF16)
    of = jnp.stack([_dot_nt(p_bf[b], cv_ref[b].astype(BF16)) for b in range(bt)])
    of = (of + p_new * vn_ref[...]) * mask / den
    o = (of[..., 0:64] + of[..., 64:128]) + (of[..., 128:192] + of[..., 192:256])
    o_ref[...] = o.astype(BF16)


def _attn_sample(q3, k_new, v_new, ck, cv, l, bias, sinks_col, mask, rep):
    nsteps = DEC_BATCH // SAMPLE_BT
    b3 = lambda i: (i, 0, 0)
    c2 = lambda i: (0, 0)
    cache = pl.BlockSpec((None, SAMPLE_BT, KV_W, WINDOW), lambda i: (l, i, 0, 0))
    return pl.pallas_call(
        _attn_sample_kernel,
        grid=(nsteps,),
        in_specs=[
            pl.BlockSpec((SAMPLE_BT, N_HEADS, HEAD_DIM), b3),
            pl.BlockSpec((SAMPLE_BT, 1, KV_W), b3),
            pl.BlockSpec((SAMPLE_BT, 1, KV_W), b3),
            cache,
            cache,
            pl.BlockSpec((N_HEADS, WINDOW), c2),
            pl.BlockSpec((N_HEADS, 1), c2),
            pl.BlockSpec((N_HEADS, KV_W), c2),
            pl.BlockSpec((HEAD_DIM, KV_W), c2),
        ],
        out_specs=pl.BlockSpec((SAMPLE_BT, N_HEADS, HEAD_DIM), b3),
        out_shape=jax.ShapeDtypeStruct((DEC_BATCH, N_HEADS, HEAD_DIM), BF16),
        compiler_params=_params(("arbitrary",)),
        name="attn_sample",
    )(q3, k_new, v_new, ck, cv, bias, sinks_col, mask, rep)


def _sample_bias(n_buf):
    slopes = jnp.exp2(-8.0 * jnp.arange(1, N_HEADS + 1, dtype=F32) / N_HEADS)
    dist = n_buf - jnp.arange(n_buf, dtype=jnp.int32)
    bias = -slopes[:, None] * dist.astype(F32)[None, :]
    return jnp.where((dist < WINDOW)[None, :], bias, NEG_INF)


def _bch_cols(wa_ref, wb_ref, seg, c):
    r = seg * D_CONV + c * TN
    return wa_ref[:, r:r + TN] if r < WIDE else wb_ref[:, r - WIDE:r - WIDE + TN]


def _conv_prompt_kernel(xn_ref, wa_ref, wb_ref, cw_ref, ob_ref, nc_ref, zbuf, carry):
    t = pl.program_id(1)
    tm = xn_ref.shape[0]
    xn = xn_ref[...]

    @pl.when(t == 0)
    def _():
        zbuf[0:8, :] = jnp.zeros((8, D_CONV), F32)

    @pl.when(t > 0)
    def _():
        zbuf[0:8, :] = carry[...]

    for c in range(D_CONV // TN):
        cols = slice(c * TN, (c + 1) * TN)
        zc = _dot(xn, _bch_cols(wa_ref, wb_ref, 1, c)) * _dot(xn, _bch_cols(wa_ref, wb_ref, 2, c))
        zbuf[8:8 + tm, cols] = zc
        cw = cw_ref[:, cols]
        y = cw[0:1] * zbuf[6:6 + tm, cols] + cw[1:2] * zbuf[7:7 + tm, cols] + cw[2:3] * zc
        ob_ref[:, cols] = (_dot(xn, _bch_cols(wa_ref, wb_ref, 0, c)) * y).astype(BF16)
    carry[...] = zbuf[tm:tm + 8, :]
    nc_ref[...] = zbuf[tm + 6:tm + 8, :]


def _conv_gmlp_prompt_kernel(xn_ref, wa_ref, wb_ref, cw_ref, ua_ref, ub_ref, vn_ref, ws_ref, bs_ref,
                             ob_ref, nc_ref, oc_ref, zbuf, carry, u_scr, v_scr):
    _conv_prompt_kernel(xn_ref, wa_ref, wb_ref, cw_ref, ob_ref, nc_ref, zbuf, carry)
    _gmlp_prompt_kernel(xn_ref, ua_ref, ub_ref, vn_ref, ws_ref, bs_ref, oc_ref, u_scr, v_scr)


def _conv_gmlp_prompt(xn_all, w_in, l, conv_w, v_norm, w_s, b_s_wide):
    nt = SEQ // TM_SEQ
    rows = lambda b, t: (b * nt + t, 0)
    resident = lambda shape, *idx: pl.BlockSpec((None, *shape), lambda *_: (l, *idx), pipeline_mode=pl.Buffered(1))
    return pl.pallas_call(
        _conv_gmlp_prompt_kernel,
        grid=(BATCH, nt),
        in_specs=[
            pl.BlockSpec((TM_SEQ, D_MODEL), rows),
            resident((D_MODEL, WIDE), 0, 1),
            resident((D_MODEL, WIDE), 0, 2),
            _layer_block((CONV_WIDTH, D_CONV), l, 0, 0),
            resident((D_MODEL, WIDE), 0, 3),
            resident((D_MODEL, TN), 0, OFF_G // TN - 1),
            _layer_block((1, D_GMLP), l, 0, 0),
            _layer_block((N_SPATIAL_GROUPS, CHUNK, CHUNK), l, 0, 0, 0),
            _layer_block((N_SPATIAL_GROUPS, CHUNK, SPATIAL_GROUP_W), l, 0, 0, 0),
        ],
        out_specs=[
            pl.BlockSpec((TM_SEQ, D_CONV), rows),
            pl.BlockSpec((None, CONV_WIDTH - 1, D_CONV), lambda b, t: (b, 0, 0)),
            pl.BlockSpec((TM_SEQ, D_GMLP), rows),
        ],
        out_shape=[
            jax.ShapeDtypeStruct((N_PROMPT, D_CONV), BF16),
            jax.ShapeDtypeStruct((BATCH, CONV_WIDTH - 1, D_CONV), F32),
            jax.ShapeDtypeStruct((N_PROMPT, D_GMLP), BF16),
        ],
        scratch_shapes=[
            pltpu.VMEM((TM_SEQ + 8, D_CONV), F32),
            pltpu.VMEM((8, D_CONV), F32),
            pltpu.VMEM((TM_SEQ, D_GMLP), F32),
            pltpu.VMEM((TM_SEQ, D_GMLP), BF16),
        ],
        compiler_params=_params(("arbitrary", "arbitrary")),
        name="conv_gmlp_prompt",
    )(xn_all, w_in, w_in, conv_w, w_in, w_in, v_norm, w_s, b_s_wide)


def _conv_sample_kernel(xn_ref, wa_ref, wb_ref, cw_ref, cb0_ref, cb1_ref, ob_ref, zc_ref):
    xn = xn_ref[...]
    for c in range(D_CONV // TN):
        cols = slice(c * TN, (c + 1) * TN)
        zc = _dot(xn, _bch_cols(wa_ref, wb_ref, 1, c)) * _dot(xn, _bch_cols(wa_ref, wb_ref, 2, c))
        cw = cw_ref[:, cols]
        y = cw[0:1] * cb0_ref[:, cols] + cw[1:2] * cb1_ref[:, cols] + cw[2:3] * zc
        ob_ref[:, cols] = (_dot(xn, _bch_cols(wa_ref, wb_ref, 0, c)) * y).astype(BF16)
        zc_ref[:, cols] = zc


def _conv_sample(xn_all, w_in, l, conv_w, cb0, cb1):
    c2 = lambda i: (0, 0)
    return pl.pallas_call(
        _conv_sample_kernel,
        grid=(1,),
        in_specs=[
            pl.BlockSpec((DEC_BATCH, D_MODEL), lambda i: (0, 0)),
            _layer_block((D_MODEL, WIDE), l, 0, 1),
            _layer_block((D_MODEL, WIDE), l, 0, 2),
            _layer_block((CONV_WIDTH, D_CONV), l, 0, 0),
            pl.BlockSpec((DEC_BATCH, D_CONV), c2),
            pl.BlockSpec((DEC_BATCH, D_CONV), c2),
        ],
        out_specs=[pl.BlockSpec((DEC_BATCH, D_CONV), c2), pl.BlockSpec((DEC_BATCH, D_CONV), c2)],
        out_shape=[
            jax.ShapeDtypeStruct((DEC_BATCH, D_CONV), BF16),
            jax.ShapeDtypeStruct((DEC_BATCH, D_CONV), F32),
        ],
        compiler_params=_params(("arbitrary",)),
        name="conv_sample",
    )(xn_all, w_in, w_in, conv_w, cb0, cb1)


HALF_V = WIDE - D_GMLP
assert D_GMLP - HALF_V == TN


def _gmlp_uv(xn, wa_ref, wb_ref, vn_ref):
    v_lo = _gelu(_dot(xn, wa_ref[:, D_GMLP:]))
    v_hi = _gelu(_dot(xn, wb_ref[...]))
    ms = (jnp.sum(v_lo * v_lo, axis=-1, keepdims=True)
          + jnp.sum(v_hi * v_hi, axis=-1, keepdims=True)) * (1.0 / D_GMLP)
    r = lax.rsqrt(ms + EPS)
    v_lo, v_hi = v_lo * r * vn_ref[:, :HALF_V], v_hi * r * vn_ref[:, HALF_V:]
    u = _gelu(_dot(xn, wa_ref[:, :D_GMLP]))
    return u, v_lo, v_hi


def _gmlp_prompt_kernel(xn_ref, wa_ref, wb_ref, vn_ref, ws_ref, bs_ref, oc_ref, u_scr, v_scr):
    tm = xn_ref.shape[0]
    u, v_lo, v_hi = _gmlp_uv(xn_ref[...], wa_ref, wb_ref, vn_ref)
    v_scr[:, :HALF_V] = v_lo.astype(BF16)
    v_scr[:, HALF_V:] = v_hi.astype(BF16)
    u_scr[...] = u
    row = lax.broadcasted_iota(jnp.int32, (CHUNK, CHUNK), 0)
    col = lax.broadcasted_iota(jnp.int32, (CHUNK, CHUNK), 1)
    for grp in range(N_SPATIAL_GROUPS):
        w = jnp.where(col <= row, ws_ref[grp], 0.0).astype(BF16)
        cols = slice(grp * SPATIAL_GROUP_W, (grp + 1) * SPATIAL_GROUP_W)
        for ch in range(tm // CHUNK):
            rows = slice(ch * CHUNK, (ch + 1) * CHUNK)
            mix = _dot(w, v_scr[rows, cols]) + bs_ref[grp]
            oc_ref[rows, cols] = (u_scr[rows, cols] * mix).astype(BF16)


def _gmlp_sample_kernel(xn_ref, wa_ref, wb_ref, vn_ref, ws0_ref, bs0_ref, oc_ref, vg_ref):
    u, v_lo, v_hi = _gmlp_uv(xn_ref[...], wa_ref, wb_ref, vn_ref)
    vg_ref[:, :HALF_V] = v_lo
    vg_ref[:, HALF_V:] = v_hi
    oc_ref[...] = (u * (ws0_ref[...] * vg_ref[...] + bs0_ref[...])).astype(BF16)


def _gmlp_sample(xn_all, w_in, l, v_norm, ws0, bs0):
    c2 = lambda i: (0, 0)
    return pl.pallas_call(
        _gmlp_sample_kernel,
        grid=(1,),
        in_specs=[
            pl.BlockSpec((DEC_BATCH, D_MODEL), lambda i: (0, 0)),
            _layer_block((D_MODEL, WIDE), l, 0, 3),
            _layer_block((D_MODEL, TN), l, 0, OFF_G // TN - 1),
            _layer_block((1, D_GMLP), l, 0, 0),
            pl.BlockSpec((1, D_GMLP), c2),
            pl.BlockSpec((1, D_GMLP), c2),
        ],
        out_specs=[pl.BlockSpec((DEC_BATCH, D_GMLP), c2), pl.BlockSpec((DEC_BATCH, D_GMLP), c2)],
        out_shape=[
            jax.ShapeDtypeStruct((DEC_BATCH, D_GMLP), BF16),
            jax.ShapeDtypeStruct((DEC_BATCH, D_GMLP), F32),
        ],
        compiler_params=_params(("arbitrary",)),
        name="gmlp_sample",
    )(xn_all, w_in, w_in, v_norm, ws0, bs0)


def _merge_kernel(xn_ref, oa_ref, ob_ref, oc_ref, xns_ref, oas_ref, obs_ref, ocs_ref, bg_ref, wg_hbm, wb_hbm,
                  m_ref, stage_g, stage_b, wg_bf, wb_bf, xn_tail, oa_tail, ob_tail, oc_tail, sem, *, layer):
    j = pl.program_id(0)
    i = pl.program_id(1)

    def weight_copies(jj):
        copies = []
        for br in range(N_BRANCHES):
            gcol = pl.multiple_of(OFF_G + br * D_MODEL + jj * TN, TN)
            bcol = pl.multiple_of(jj * TN, TN)
            copies.append(pltpu.make_async_copy(
                wg_hbm.at[layer, :, pl.ds(gcol, TN)], stage_g.at[br], sem.at[br]))
            copies.append(pltpu.make_async_copy(
                wb_hbm.at[layer, br, :, pl.ds(bcol, TN)], stage_b.at[br], sem.at[N_BRANCHES + br]))
        return copies

    @pl.when(i == 0)
    def _():
        @pl.when(j == 0)
        def _():
            for c in weight_copies(0):
                c.start()

        for c in weight_copies(j):
            c.wait()
        def cast_rows(stage, dst):
            def body(c, carry):
                r = pl.multiple_of(c * CAST_ROWS, CAST_ROWS)
                for br in range(N_BRANCHES):
                    dst[br, pl.ds(r, CAST_ROWS), :] = stage[br, pl.ds(r, CAST_ROWS), :].astype(BF16)
                return carry
            lax.fori_loop(0, stage.shape[1] // CAST_ROWS, body, 0)

        cast_rows(stage_g, wg_bf)
        cast_rows(stage_b, wb_bf)

        @pl.when(j + 1 < pl.num_programs(0))
        def _():
            for c in weight_copies(j + 1):
                c.start()

    bg = bg_ref[...]

    def merged(xn, branch_outs):
        acc = None
        for br, o in enumerate(branch_outs):
            gate = jax.nn.sigmoid(_dot(xn, wg_bf[br]) + bg[br:br + 1])
            term = gate * _dot(o, wb_bf[br])
            acc = term if acc is None else acc + term
        return acc.astype(BF16)

    last = pl.num_programs(1) - 1

    @pl.when(i < last)
    def _():
        m_ref[...] = merged(xn_ref[...], (oa_ref[...], ob_ref[...], oc_ref[...]))

    @pl.when(i == last)
    def _():
        for src, src_s, tail in ((xn_ref, xns_ref, xn_tail), (oa_ref, oas_ref, oa_tail),
                                 (ob_ref, obs_ref, ob_tail), (oc_ref, ocs_ref, oc_tail)):
            tail[:PROMPT_TAIL, :] = src[:PROMPT_TAIL, :]
            tail[PROMPT_TAIL:, :] = src_s[...]
        m_ref[...] = merged(xn_tail[...], (oa_tail[...], ob_tail[...], oc_tail[...]))


CAST_ROWS = 256
PROMPT_TAIL = N_PROMPT % TM_ROW
assert PROMPT_TAIL + DEC_BATCH == TM_ROW


def _tail_tile_specs(width):
    prompt = pl.BlockSpec((TM_ROW, width), lambda j, i: (i, 0))
    sample = pl.BlockSpec((DEC_BATCH, width), lambda j, i: (0, 0))
    return prompt, sample


def _merge(xn, w_in, l, b_gate, o_a, o_b, o_c, w_branch):
    nt = N_ROWS // TM_ROW
    nn = D_MODEL // TN
    wide_p, wide_s = _tail_tile_specs(D_MODEL)
    br_p, br_s = _tail_tile_specs(BRANCH_W)
    return pl.pallas_call(
        functools.partial(_merge_kernel, layer=l),
        grid=(nn, nt),
        in_specs=[
            wide_p, br_p, br_p, br_p,
            wide_s, br_s, br_s, br_s,
            pl.BlockSpec((None, N_BRANCHES, TN), lambda j, i: (l, 0, j)),
            pl.BlockSpec(memory_space=pl.ANY),
            pl.BlockSpec(memory_space=pl.ANY),
        ],
        out_specs=pl.BlockSpec((TM_ROW, TN), lambda j, i: (i, j)),
        out_shape=jax.ShapeDtypeStruct((N_ROWS, D_MODEL), BF16),
        scratch_shapes=[
            pltpu.VMEM((N_BRANCHES, D_MODEL, TN), F32),
            pltpu.VMEM((N_BRANCHES, BRANCH_W, TN), F32),
            pltpu.VMEM((N_BRANCHES, D_MODEL, TN), BF16),
            pltpu.VMEM((N_BRANCHES, BRANCH_W, TN), BF16),
            pltpu.VMEM((TM_ROW, D_MODEL), BF16),
            pltpu.VMEM((TM_ROW, BRANCH_W), BF16),
            pltpu.VMEM((TM_ROW, BRANCH_W), BF16),
            pltpu.VMEM((TM_ROW, BRANCH_W), BF16),
            pltpu.SemaphoreType.DMA((2 * N_BRANCHES,)),
        ],
        compiler_params=_params(("arbitrary", "arbitrary")),
        name="merge",
    )(xn[0], o_a[0], o_b[0], o_c[0], xn[1], o_a[1], o_b[1], o_c[1], b_gate, w_in, w_branch)


OUT_SPLIT = 2


def _out_proj_kernel(m_ref, w_ref, xp_ref, xs_ref, g_ref, x1_ref, xn_ref):
    i = pl.program_id(0)
    last = pl.num_programs(0) - 1
    sub = m_ref.shape[0] // OUT_SPLIT

    def project(rows, x):
        x1 = x + _dot(m_ref[rows, :], w_ref[...])
        x1_ref[rows, :] = x1
        xn_ref[rows, :] = _rms_rows(x1, g_ref[...]).astype(BF16)

    @pl.when(i < last)
    def _():
        for s in range(OUT_SPLIT):
            rows = slice(s * sub, (s + 1) * sub)
            project(rows, xp_ref[rows, :])

    @pl.when(i == last)
    def _():
        for s in range(OUT_SPLIT):
            lo, hi = s * sub, (s + 1) * sub
            parts = []
            if lo < PROMPT_TAIL:
                parts.append(xp_ref[lo:min(hi, PROMPT_TAIL), :])
            if hi > PROMPT_TAIL:
                parts.append(xs_ref[max(lo, PROMPT_TAIL) - PROMPT_TAIL:hi - PROMPT_TAIL, :])
            project(slice(lo, hi), parts[0] if len(parts) == 1 else jnp.concatenate(parts, axis=0))


def _out_proj(m, w_out, l, x_p, x_s, sample_block, g_ffn):
    rows = lambda i: (i, 0)
    return pl.pallas_call(
        _out_proj_kernel,
        grid=(N_ROWS // TM_ROW,),
        in_specs=[
            pl.BlockSpec((TM_ROW, D_MODEL), rows),
            _layer_block((D_MODEL, D_MODEL), l, 0, 0),
            pl.BlockSpec((TM_ROW, D_MODEL), rows),
            pl.BlockSpec((DEC_BATCH, D_MODEL), lambda i: (sample_block, 0)),
            _layer_block((1, D_MODEL), l, 0, 0),
        ],
        out_specs=[pl.BlockSpec((TM_ROW, D_MODEL), rows), pl.BlockSpec((TM_ROW, D_MODEL), rows)],
        out_shape=[
            jax.ShapeDtypeStruct((N_ROWS, D_MODEL), F32),
            jax.ShapeDtypeStruct((N_ROWS, D_MODEL), BF16),
        ],
        compiler_params=_params(("arbitrary",)),
        name="out_proj",
    )(m, w_out, x_p, x_s, g_ffn)


TM_UP = 2080
UP_SPLIT = 5


def _ffn_up_kernel(xn_ref, wgate_ref, wup_ref, h_ref, wgate_scr, wup_scr):
    @pl.when(pl.program_id(1) == 0)
    def _():
        wgate_scr[...] = wgate_ref[...].astype(BF16)
        wup_scr[...] = wup_ref[...].astype(BF16)

    sub = xn_ref.shape[0] // UP_SPLIT
    for s in range(UP_SPLIT):
        rows = slice(s * sub, (s + 1) * sub)
        xn = xn_ref[rows, :]
        gate = _dot(xn, wgate_scr[...])
        h_ref[rows, :] = (gate * jax.nn.sigmoid(gate) * _dot(xn, wup_scr[...])).astype(BF16)


def _ffn_up(xn, w_gate_up, l):
    nt = N_ROWS // TM_UP
    nn = D_FF // TN
    return pl.pallas_call(
        _ffn_up_kernel,
        grid=(nn, nt),
        in_specs=[
            pl.BlockSpec((TM_UP, D_MODEL), lambda j, i: (i, 0)),
            pl.BlockSpec((None, D_MODEL, TN), lambda j, i: (l, 0, j)),
            pl.BlockSpec((None, D_MODEL, TN), lambda j, i: (l, 0, nn + j)),
        ],
        out_specs=pl.BlockSpec((TM_UP, TN), lambda j, i: (i, j)),
        out_shape=jax.ShapeDtypeStruct((N_ROWS, D_FF), BF16),
        scratch_shapes=[pltpu.VMEM((D_MODEL, TN), BF16), pltpu.VMEM((D_MODEL, TN), BF16)],
        compiler_params=_params(("arbitrary", "arbitrary")),
        name="ffn_up",
    )(xn, w_gate_up, w_gate_up)


def _ffn_down_kernel(h_ref, w_ref, x_ref, o_ref, w_scr):
    @pl.when(pl.program_id(1) == 0)
    def _():
        w_scr[...] = w_ref[...].astype(BF16)

    o_ref[...] = x_ref[...] + _dot(h_ref[...], w_scr[...])


def _ffn_down(h, w_down, l, x1):
    nt = N_ROWS // TM_ROW
    nn = D_MODEL // TN
    return pl.pallas_call(
        _ffn_down_kernel,
        grid=(nn, nt),
        in_specs=[
            pl.BlockSpec((TM_ROW, D_FF), lambda j, i: (i, 0)),
            pl.BlockSpec((None, D_FF, TN), lambda j, i: (l, 0, j)),
            pl.BlockSpec((TM_ROW, TN), lambda j, i: (i, j)),
        ],
        out_specs=pl.BlockSpec((TM_ROW, TN), lambda j, i: (i, j)),
        out_shape=jax.ShapeDtypeStruct((N_ROWS, D_MODEL), F32),
        scratch_shapes=[pltpu.VMEM((D_FF, TN), BF16)],
        compiler_params=_params(("arbitrary", "arbitrary")),
        name="ffn_down",
    )(h, w_down, x1)


def _ffn_down_split_kernel(h_ref, w_ref, x_ref, yp_ref, ys_ref, w_scr):
    i = pl.program_id(1)

    @pl.when(i == 0)
    def _():
        w_scr[...] = w_ref[...].astype(BF16)

    y = x_ref[...] + _dot(h_ref[...], w_scr[...])
    yp_ref[...] = y

    @pl.when(i == pl.num_programs(1) - 1)
    def _():
        ys_ref[...] = y[PROMPT_TAIL:, :]


def _ffn_down_split(h, w_down, l, x1):
    nt = N_ROWS // TM_ROW
    nn = D_MODEL // TN
    return pl.pallas_call(
        _ffn_down_split_kernel,
        grid=(nn, nt),
        in_specs=[
            pl.BlockSpec((TM_ROW, D_FF), lambda j, i: (i, 0)),
            pl.BlockSpec((None, D_FF, TN), lambda j, i: (l, 0, j)),
            pl.BlockSpec((TM_ROW, TN), lambda j, i: (i, j)),
        ],
        out_specs=[
            pl.BlockSpec((TM_ROW, TN), lambda j, i: (i, j)),
            pl.BlockSpec((DEC_BATCH, TN), lambda j, i: (0, j)),
        ],
        out_shape=[
            jax.ShapeDtypeStruct((N_PROMPT, D_MODEL), F32),
            jax.ShapeDtypeStruct((DEC_BATCH, D_MODEL), F32),
        ],
        scratch_shapes=[pltpu.VMEM((D_FF, TN), BF16)],
        compiler_params=_params(("arbitrary", "arbitrary")),
        name="ffn_down_split",
    )(h, w_down, x1)


def _block_diag_ones(width):
    head = np.arange(width) // HEAD_DIM
    return jnp.asarray(head[:, None] == head[None, :], dtype=BF16)


def _layer(x, l, cache_k, cache_v, state_conv, p, const):
    first, last = l == 0, l == DEPTH - 1
    x_p, x_s = x if first else (x, x)
    sample_block = 0 if first else SAMPLE_BLOCK
    w_in = p["w_in"]
    qn = jnp.tile(p["q_norm"][l], N_HEADS).reshape(1, BRANCH_W)
    kn = jnp.tile(p["k_norm"][l], N_KV_HEADS).reshape(1, KV_W)
    sinks = p["sinks"][l]

    xn_p, q_p, k_p, v_p, klast_p, vlast_p = _qkv(
        x_p, p["norm_mix"], w_in, l, qn, kn, const["ph"],
        row_block0=0, n_rows=N_PROMPT, tm=TM_SEQ, tiles_per_seq=SEQ // TM_SEQ)
    xn_s, q_s, k_s, v_s = _qkv(
        x_s, p["norm_mix"], w_in, l, qn, kn, const["ph"],
        row_block0=sample_block, n_rows=DEC_BATCH, tm=DEC_BATCH)
    sink_rows = jnp.repeat(sinks, WINDOW).reshape(N_KV_HEADS, 1, GQA_GROUP * WINDOW)
    oa_p = _attn_prompt(sink_rows, q_p, k_p, v_p, const["bias_p"])
    oa_s = _attn_sample(
        q_s.reshape(DEC_BATCH, N_HEADS, HEAD_DIM),
        k_s.reshape(DEC_BATCH, 1, KV_W), v_s.reshape(DEC_BATCH, 1, KV_W),
        cache_k, cache_v, l,
        const["bias_s"], sinks.reshape(N_HEADS, 1), const["mask_s"], const["rep_s"],
    ).reshape(DEC_BATCH, BRANCH_W)

    w_s = p["w_spatial"]
    b_s = p["b_spatial"]
    bs_wide = jnp.broadcast_to(b_s[:, :, :, None], (DEPTH, N_SPATIAL_GROUPS, CHUNK, SPATIAL_GROUP_W))
    ob_p, nc_p, oc_p = _conv_gmlp_prompt(xn_p, w_in, l, p["conv_w"], p["v_norm"], w_s, bs_wide)
    ob_s, zc_s = _conv_sample(xn_s, w_in, l, p["conv_w"], state_conv[l, :, 0], state_conv[l, :, 1])

    ws0 = jnp.repeat(w_s[l, :, 0, 0], SPATIAL_GROUP_W).reshape(1, D_GMLP)
    bs0 = jnp.repeat(b_s[l, :, 0], SPATIAL_GROUP_W).reshape(1, D_GMLP)
    oc_s, vg_s = _gmlp_sample(xn_s, w_in, l, p["v_norm"], ws0, bs0)

    m = _merge((xn_p, xn_s), p["w_in_f32"], l, p["b_gate"], (oa_p, oa_s), (ob_p, ob_s), (oc_p, oc_s),
               p["w_branch"])
    x1, xn2 = _out_proj(m, p["w_out"], l, x_p, x_s, sample_block, p["norm_ffn"])
    h = _ffn_up(xn2, p["w_gate_up"], l)
    x2 = _ffn_down_split(h, p["w_down"], l, x1) if last else _ffn_down(h, p["w_down"], l, x1)

    new_k_p = klast_p.reshape(BATCH, WINDOW, N_KV_HEADS, HEAD_DIM)
    new_v_p = vlast_p.reshape(BATCH, WINDOW, N_KV_HEADS, HEAD_DIM)
    new_k_s = k_s.reshape(DEC_BATCH, 1, N_KV_HEADS, HEAD_DIM)
    new_v_s = v_s.reshape(DEC_BATCH, 1, N_KV_HEADS, HEAD_DIM)
    new_conv_s = jnp.stack([state_conv[l, :, 1], zc_s], axis=1)
    return x2, (new_k_p, new_v_p, nc_p, new_k_s, new_v_s, new_conv_s, vg_s.reshape(DEC_BATCH, 1, D_GMLP))


def kernel(x_prompt, x_sample, cache_k, cache_v, state_conv, norm_mix, w_in, b_gate, q_norm, k_norm,
           sinks, conv_w, v_norm, w_spatial, b_spatial, w_branch, w_out, norm_ffn, w_gate_up, w_down):
    assert min(WINDOW, SEQ) == WINDOW and SEQ % TM_SEQ == 0 and TM_SEQ >= WINDOW
    n_buf = cache_k.shape[2]
    assert n_buf == WINDOW
    cache_k = jnp.transpose(cache_k, (0, 1, 3, 4, 2)).reshape(DEPTH, DEC_BATCH, KV_W, n_buf)
    cache_v = jnp.transpose(cache_v, (0, 1, 3, 4, 2)).reshape(DEPTH, DEC_BATCH, KV_W, n_buf)
    p = {
        "norm_mix": norm_mix.reshape(DEPTH, 1, D_MODEL),
        "norm_ffn": norm_ffn.reshape(DEPTH, 1, D_MODEL),
        "v_norm": v_norm.reshape(DEPTH, 1, D_GMLP),
        "w_in": w_in[:, :, :OFF_G].astype(BF16),
        "w_in_f32": w_in,
        "w_branch": w_branch,
        "w_out": w_out.astype(BF16),
        "w_gate_up": w_gate_up,
        "w_down": w_down,
        "b_gate": b_gate, "q_norm": q_norm, "k_norm": k_norm, "sinks": sinks, "conv_w": conv_w,
        "w_spatial": w_spatial, "b_spatial": b_spatial,
    }
    kv_of_col = np.arange(KV_W) // HEAD_DIM
    kv_of_head = np.arange(N_HEADS) // GQA_GROUP
    const = {
        "ph": _block_diag_ones(KV_W),
        "bias_p": _prompt_bias(),
        "bias_s": _sample_bias(n_buf),
        "mask_s": jnp.asarray(kv_of_head[:, None] == kv_of_col[None, :], dtype=F32),
        "rep_s": jnp.asarray(np.tile(np.eye(HEAD_DIM), (1, N_KV_HEADS)), dtype=BF16),
    }
    x = (x_prompt.reshape(N_PROMPT, D_MODEL), x_sample.reshape(DEC_BATCH, D_MODEL))
    per_layer = []
    for l in range(DEPTH):
        x, outs = _layer(x, l, cache_k, cache_v, state_conv, p, const)
        per_layer.append(outs)
    stacked = [jnp.stack([per_layer[l][i] for l in range(DEPTH)]) for i in range(7)]
    y_prompt, y_sample = x
    return (y_prompt.reshape(BATCH, SEQ, D_MODEL), y_sample.reshape(DEC_BATCH, 1, D_MODEL), *stacked)
```
